```python
import functools
import jax
import jax.numpy as jnp
from jax import lax
import numpy as np

D_MODEL = 2048
BATCH = 4
SEQ = 2048
DEPTH = 1
DEC_BATCH = 32
DEC_SEQ = 4
PAST_LEN = 8192
PAGE_SIZE = 128

GLA_HEADS = 4
GLA_DK = D_MODEL // 16
GLA_DV = D_MODEL // 8
GLA_GATE_RANK = 16
GLA_TAU = 16.0
GLA_CHUNK = 32
FOX_HEADS = 8
FOX_DH = D_MODEL // 16
Q_BLOCK = 128
N_GROUPS = 4
EXPERTS_PER_GROUP = 4
N_EXPERTS = N_GROUPS * EXPERTS_PER_GROUP
TOP_K_IN_GROUP = 2
D_EXPERT = D_MODEL // 4
RMS_EPS = 1e-6

GLA_KW = GLA_HEADS * GLA_DK
GLA_VW = GLA_HEADS * GLA_DV
FOX_W = FOX_HEADS * FOX_DH
IN_SPLIT = (GLA_KW, GLA_KW, GLA_VW, GLA_VW, GLA_GATE_RANK, FOX_W, FOX_W, FOX_W, FOX_HEADS, D_MODEL, D_MODEL)
D_IN = sum(IN_SPLIT)

kernel_name = 'hybrid_gla_fox_hmoe_step'


def rmsnorm(x, g):
    x32 = x.astype(jnp.float32)
    y = x32 * lax.rsqrt(jnp.mean(x32 * x32, axis=-1, keepdims=True) + RMS_EPS)
    return (y * g.astype(jnp.float32)).astype(x.dtype)


def adaln(c, w_ada, b_ada):
    mod = jax.nn.silu(c) @ w_ada + b_ada
    return jnp.split(mod[:, None, :], 6, axis=-1)


def modulate(h, shift, scale):
    return h * (1.0 + scale) + shift


def mixer_inputs(h, w_in, w_a2, b_a, b_f):
    B, L, _ = h.shape
    offs = [int(o) for o in np.cumsum(IN_SPLIT)[:-1]]
    qa, ka, va, ra, lra, qb, kb, vb, fb, ga, gb = jnp.split(h @ w_in, offs, axis=-1)
    qa = qa.reshape(B, L, GLA_HEADS, GLA_DK) * (GLA_DK ** -0.5)
    ka = ka.reshape(B, L, GLA_HEADS, GLA_DK)
    va = va.reshape(B, L, GLA_HEADS, GLA_DV)
    ra = ra.reshape(B, L, GLA_HEADS, GLA_DV)
    log_a = (jax.nn.log_sigmoid((lra @ w_a2 + b_a).astype(jnp.float32)) / GLA_TAU).reshape(B, L, GLA_HEADS, GLA_DK)
    qb = qb.reshape(B, L, FOX_HEADS, FOX_DH)
    kb = kb.reshape(B, L, FOX_HEADS, FOX_DH)
    vb = vb.reshape(B, L, FOX_HEADS, FOX_DH)
    logf = jax.nn.log_sigmoid((fb + b_f).astype(jnp.float32))
    return qa, ka, va, ra, log_a, qb, kb, vb, logf, ga, gb


def gla_chunked(q, k, v, log_a, s0, chunk):
    B, L, H, DK = q.shape
    DV = v.shape[-1]
    n = L // chunk
    f32 = jnp.float32
    qc = q.astype(f32).reshape(B, n, chunk, H, DK)
    kc = k.astype(f32).reshape(B, n, chunk, H, DK)
    vc = v.astype(f32).reshape(B, n, chunk, H, DV)
    cum = jnp.cumsum(log_a.astype(f32).reshape(B, n, chunk, H, DK), axis=2)
    causal = jnp.tril(jnp.ones((chunk, chunk), dtype=bool))[None, None, :, :, None, None]
    rel = jnp.where(causal, cum[:, :, :, None] - cum[:, :, None, :], -jnp.inf)
    scores = jnp.einsum('bnthd,bnshd,bntshd->bntsh', qc, kc, jnp.exp(rel))
    o_intra = jnp.einsum('bntsh,bnshv->bnthv', scores, vc)
    last = cum[:, :, -1]
    upd = jnp.einsum('bnshd,bnshv->bnhdv', kc * jnp.exp(last[:, :, None] - cum), vc)

    def step(s, inp):
        decay, u = inp
        return decay[..., None] * s + u, s

    s_final, s_prev = lax.scan(step, s0.astype(f32),
                               (jnp.moveaxis(jnp.exp(last), 1, 0), jnp.moveaxis(upd, 1, 0)))
    o_inter = jnp.einsum('bnthd,nbhdv->bnthv', qc * jnp.exp(cum), s_prev)
    return (o_intra + o_inter).reshape(B, L, H, DV), s_final


def fox_prompt(q, k, v, logf):
    B, L, H, Dh = q.shape
    nb = L // Q_BLOCK
    Ft = jnp.cumsum(logf.astype(jnp.float32), axis=1).transpose(0, 2, 1)
    q_blocks = jnp.moveaxis(q.reshape(B, nb, Q_BLOCK, H, Dh), 1, 0)
    f_blocks = jnp.moveaxis(Ft.reshape(B, H, nb, Q_BLOCK), 2, 0)
    pos_blocks = jnp.arange(L).reshape(nb, Q_BLOCK)
    key_pos = jnp.arange(L)
    scale = FOX_DH ** -0.5

    def block(args):
        qi, fi, pi = args
        s = jnp.einsum('bqhd,bkhd->bhqk', qi, k).astype(jnp.float32) * scale
        s = s + fi[..., None] - Ft[:, :, None, :]
        s = jnp.where((pi[:, None] >= key_pos[None, :])[None, None], s, -jnp.inf)
        p = jax.nn.softmax(s, axis=-1).astype(v.dtype)
        return jnp.einsum('bhqk,bkhd->bqhd', p, v)

    o = lax.map(block, (q_blocks, f_blocks, pos_blocks))
    return jnp.moveaxis(o, 0, 1).reshape(B, L, H, Dh)


def fox_sample(q, k_new, v_new, logf_new, cache_k, cache_v, cache_logf, page_table):
    DB, T, H, Dh = q.shape
    k_past = cache_k[page_table].reshape(DB, -1, H, Dh)
    v_past = cache_v[page_table].reshape(DB, -1, H, Dh)
    lf_past = cache_logf[page_table].reshape(DB, -1, H).astype(jnp.float32)
    P = k_past.shape[1]
    R = (lax.cumsum(lf_past, axis=1, reverse=True) - lf_past).transpose(0, 2, 1)
    Fn = jnp.cumsum(logf_new.astype(jnp.float32), axis=1).transpose(0, 2, 1)
    scale = FOX_DH ** -0.5
    s_past = jnp.einsum('bqhd,bkhd->bhqk', q, k_past).astype(jnp.float32) * scale
    s_past = s_past + Fn[..., None] + R[:, :, None, :]
    s_new = jnp.einsum('bqhd,bkhd->bhqk', q, k_new).astype(jnp.float32) * scale
    s_new = s_new + Fn[..., :, None] - Fn[..., None, :]
    s_new = jnp.where(jnp.tril(jnp.ones((T, T), dtype=bool))[None, None], s_new, -jnp.inf)
    p = jax.nn.softmax(jnp.concatenate([s_past, s_new], axis=-1), axis=-1).astype(v_new.dtype)
    return (jnp.einsum('bhqk,bkhd->bqhd', p[..., :P], v_past)
            + jnp.einsum('bhqk,bkhd->bqhd', p[..., P:], v_new))


def mixer_output(o_gla, ra, o_fox, ga, gb, g_gla_norm, w_up_a, w_up_b, w_out):
    B, L = o_gla.shape[:2]
    o_gla = rmsnorm(o_gla.astype(ra.dtype), g_gla_norm) * jax.nn.silu(ra)
    u_a = o_gla.reshape(B, L, GLA_VW) @ w_up_a
    u_b = o_fox.reshape(B, L, FOX_W) @ w_up_b
    merged = jax.nn.sigmoid(ga) * u_a + jax.nn.sigmoid(gb) * u_b
    return merged @ w_out


def hier_moe(h, w_grp, b_grp, w_exp, b_exp, w_gate_e, w_up_e, w_down_e):
    B, L, D = h.shape
    t = h.reshape(-1, D)
    g_logits = (t @ w_grp + b_grp).astype(jnp.float32)
    g_prob = jax.nn.softmax(g_logits, axis=-1)
    _, g_idx = lax.top_k(g_logits, 1)
    g_onehot = jax.nn.one_hot(g_idx[:, 0], N_GROUPS, dtype=jnp.float32)
    g_w = jnp.sum(g_prob * g_onehot, axis=-1, keepdims=True)
    e_logits = (t @ w_exp + b_exp).astype(jnp.float32).reshape(-1, N_GROUPS, EXPERTS_PER_GROUP)
    e_in_group = jnp.einsum('tge,tg->te', e_logits, g_onehot)
    top_v, top_i = lax.top_k(e_in_group, TOP_K_IN_GROUP)
    top_w = jax.nn.softmax(top_v, axis=-1) * g_w
    expert_id = g_idx * EXPERTS_PER_GROUP + top_i
    combine = jnp.sum(jax.nn.one_hot(expert_id, N_EXPERTS, dtype=jnp.float32) * top_w[..., None], axis=1)
    a = jnp.einsum('td,edf->tef', t, w_gate_e)
    u = jnp.einsum('td,edf->tef', t, w_up_e)
    hid = jax.nn.silu(a) * u * combine[..., None].astype(t.dtype)
    return jnp.einsum('tef,efd->td', hid, w_down_e).reshape(B, L, D)


def trunk_layer(x, c, gla_s0, gla_chunk, fox_attend, w_ada, b_ada, g_norm1, g_norm2, w_in, w_a2, b_a, b_f,
                g_gla_norm, w_up_a, w_up_b, w_out, w_grp, b_grp, w_exp, b_exp, w_gate_e, w_up_e, w_down_e):
    sh1, sc1, gt1, sh2, sc2, gt2 = adaln(c, w_ada, b_ada)
    h = modulate(rmsnorm(x, g_norm1), sh1, sc1)
    qa, ka, va, ra, log_a, qb, kb, vb, logf, ga, gb = mixer_inputs(h, w_in, w_a2, b_a, b_f)
    o_a, s_new = gla_chunked(qa, ka, va, log_a, gla_s0, gla_chunk)
    o_b = fox_attend(qb, kb, vb, logf)
    x = x + gt1 * mixer_output(o_a, ra, o_b, ga, gb, g_gla_norm, w_up_a, w_up_b, w_out)
    h2 = modulate(rmsnorm(x, g_norm2), sh2, sc2)
    x = x + gt2 * hier_moe(h2, w_grp, b_grp, w_exp, b_exp, w_gate_e, w_up_e, w_down_e)
    return x, kb, vb, logf, s_new


def setup_inputs(seed: int = 0) -> dict:
    key = jax.random.key(seed)
    ks = jax.random.split(key, 40)
    f32 = jnp.float32

    def nrm(i, shape, scale=1.0):
        return scale * jax.random.normal(ks[i], shape, f32)

    n_pages = PAST_LEN // PAGE_SIZE
    n_used = DEC_BATCH * n_pages
    n_pool = n_used + max(1, n_used // 4)
    page_table = jax.random.permutation(ks[0], n_pool)[:n_used].reshape(DEC_BATCH, n_pages).astype(jnp.int32)
    D = D_MODEL
    return {
        'x_prompt': nrm(1, (BATCH, SEQ, D)),
        'x_sample': nrm(2, (DEC_BATCH, DEC_SEQ, D)),
        'cache_k': nrm(3, (DEPTH, n_pool, PAGE_SIZE, FOX_HEADS, FOX_DH)),
        'cache_v': nrm(4, (DEPTH, n_pool, PAGE_SIZE, FOX_HEADS, FOX_DH)),
        'cache_logf': jax.nn.log_sigmoid(6.0 + nrm(5, (DEPTH, n_pool, PAGE_SIZE, FOX_HEADS), 0.5)),
        'state_gla': nrm(6, (DEPTH, DEC_BATCH, GLA_HEADS, GLA_DK, GLA_DV), 2.0),
        'page_table': page_table,
        'c_prompt': nrm(7, (BATCH, D)),
        'c_sample': nrm(8, (DEC_BATCH, D)),
        'w_ada': nrm(9, (DEPTH, D, 6 * D), D ** -0.5),
        'b_ada': nrm(10, (DEPTH, 6 * D), 0.02),
        'g_norm1': 1.0 + nrm(11, (DEPTH, D), 0.02),
        'g_norm2': 1.0 + nrm(12, (DEPTH, D), 0.02),
        'g_final': 1.0 + nrm(13, (D,), 0.02),
        'w_in': nrm(14, (DEPTH, D, D_IN), D ** -0.5),
        'w_a2': nrm(15, (DEPTH, GLA_GATE_RANK, GLA_KW), GLA_GATE_RANK ** -0.5),
        'b_a': nrm(16, (DEPTH, GLA_KW), 0.02),
        'b_f': 5.0 + nrm(17, (DEPTH, FOX_HEADS), 0.1),
        'g_gla_norm': 1.0 + nrm(18, (DEPTH, GLA_DV), 0.02),
        'w_up_a': nrm(19, (DEPTH, GLA_VW, D), GLA_VW ** -0.5),
        'w_up_b': nrm(20, (DEPTH, FOX_W, D), FOX_W ** -0.5),
        'w_out': nrm(21, (DEPTH, D, D), D ** -0.5),
        'w_grp': nrm(22, (DEPTH, D, N_GROUPS), D ** -0.5),
        'b_grp': nrm(23, (DEPTH, N_GROUPS), 0.01),
        'w_exp': nrm(24, (DEPTH, D, N_EXPERTS), D ** -0.5),
        'b_exp': nrm(25, (DEPTH, N_EXPERTS), 0.01),
        'w_gate_e': nrm(26, (DEPTH, N_EXPERTS, D, D_EXPERT), D ** -0.5),
        'w_up_e': nrm(27, (DEPTH, N_EXPERTS, D, D_EXPERT), D ** -0.5),
        'w_down_e': nrm(28, (DEPTH, N_EXPERTS, D_EXPERT, D), D_EXPERT ** -0.5),
    }


def reference(x_prompt, x_sample, cache_k, cache_v, cache_logf, state_gla, page_table, c_prompt, c_sample,
              w_ada, b_ada, g_norm1, g_norm2, g_final, w_in, w_a2, b_a, b_f, g_gla_norm, w_up_a, w_up_b, w_out,
              w_grp, b_grp, w_exp, b_exp, w_gate_e, w_up_e, w_down_e):
    xp, xs = x_prompt, x_sample
    kp_l, vp_l, fp_l, sp_l = [], [], [], []
    ks_l, vs_l, fs_l, ss_l = [], [], [], []
    for l in range(DEPTH):
        weights = (w_ada[l], b_ada[l], g_norm1[l], g_norm2[l], w_in[l], w_a2[l], b_a[l], b_f[l], g_gla_norm[l],
                   w_up_a[l], w_up_b[l], w_out[l], w_grp[l], b_grp[l], w_exp[l], b_exp[l],
                   w_gate_e[l], w_up_e[l], w_down_e[l])
        s0 = jnp.zeros((xp.shape[0], GLA_HEADS, GLA_DK, GLA_DV), jnp.float32)
        xp, kp, vp, fp, sp = trunk_layer(xp, c_prompt, s0, GLA_CHUNK, fox_prompt, *weights)
        attend = functools.partial(fox_sample, cache_k=cache_k[l], cache_v=cache_v[l],
                                   cache_logf=cache_logf[l], page_table=page_table)
        xs, ks_, vs_, fs_, ss_ = trunk_layer(xs, c_sample, state_gla[l], xs.shape[1], attend, *weights)
        kp_l.append(kp); vp_l.append(vp); fp_l.append(fp); sp_l.append(sp)
        ks_l.append(ks_); vs_l.append(vs_); fs_l.append(fs_); ss_l.append(ss_)
    y_prompt = rmsnorm(xp, g_final)
    y_sample = rmsnorm(xs, g_final)
    return (y_prompt, y_sample, jnp.stack(kp_l), jnp.stack(vp_l), jnp.stack(fp_l), jnp.stack(sp_l),
            jnp.stack(ks_l), jnp.stack(vs_l), jnp.stack(fs_l), jnp.stack(ss_l))
```

```python
import functools

import numpy as np
import jax
import jax.numpy as jnp
from jax import lax
from jax.experimental import pallas as pl
from jax.experimental.pallas import tpu as pltpu

F32 = jnp.float32
BF16 = jnp.bfloat16

D_MODEL = 2048
GLA_HEADS = 4
GLA_DK = 128
GLA_DV = 256
GLA_RANK = 16
GLA_TAU = 16.0
FOX_HEADS = 8
FOX_DH = 128
PAGE = 128
N_GROUPS = 4
EXP_PER_GROUP = 4
N_EXPERTS = 16
D_EXPERT = 512
RMS_EPS = 1e-6
GLA_KW = GLA_HEADS * GLA_DK
GLA_VW = GLA_HEADS * GLA_DV
FOX_W = FOX_HEADS * FOX_DH
GLA_CHUNK = 128
GLA_LEVELS = 7
LANES = 128
NEG = -1e30
VMEM_LIMIT = 56 * 1024 * 1024


def _cparams(sem):
    return pltpu.CompilerParams(dimension_semantics=sem, vmem_limit_bytes=VMEM_LIMIT)


def _dot(a, b):
    return jnp.dot(a, b, preferred_element_type=F32)


def _dot_nt(a, b):
    return lax.dot_general(a, b, (((1,), (1,)), ((), ())), preferred_element_type=F32)


def _dot_tn(a, b):
    return lax.dot_general(a, b, (((0,), (0,)), ((), ())), preferred_element_type=F32)


def _split2(x):
    hi = x.astype(BF16)
    lo = (x - hi.astype(F32)).astype(BF16)
    return hi, lo


def _split3(x):
    hi = x.astype(BF16)
    r = x - hi.astype(F32)
    mid = r.astype(BF16)
    lo = (r - mid.astype(F32)).astype(BF16)
    return hi, mid, lo


def _dot3(a, b):
    ah, al = _split2(a)
    bh, bl = _split2(b)
    return _dot(ah, bh) + _dot(ah, bl) + _dot(al, bh)


def _dot_sel(w01, x):
    hi, mid, lo = _split3(x)
    return _dot(w01, hi) + _dot(w01, mid) + _dot(w01, lo)


def _log_sigmoid(x):
    return jnp.minimum(x, 0.0) - jnp.log1p(jnp.exp(-jnp.abs(x)))


def _silu(x):
    return x * jax.nn.sigmoid(x)


def _adaln_kernel(c_ref, w_ref, b_ref, o_ref):
    o_ref[...] = _dot3(_silu(c_ref[...]), w_ref[...]) + b_ref[...]


def _adaln(c, w, b, tn=512):
    nb, d = c.shape
    n = w.shape[1]
    return pl.pallas_call(
        _adaln_kernel,
        grid=(n // tn,),
        in_specs=[pl.BlockSpec((nb, d), lambda j: (0, 0)),
                  pl.BlockSpec((d, tn), lambda j: (0, j)),
                  pl.BlockSpec((1, tn), lambda j: (0, j))],
        out_specs=pl.BlockSpec((nb, tn), lambda j: (0, j)),
        out_shape=jax.ShapeDtypeStruct((nb, n), F32),
        compiler_params=_cparams(("arbitrary",)),
        name="adaln",
    )(c, w, b)


def _rms(x):
    return x * lax.rsqrt(jnp.mean(x * x, axis=-1, keepdims=True) + RMS_EPS)


def _normmod_kernel(x_ref, g_ref, sh_ref, sc_ref, o_ref):
    y = _rms(x_ref[...]) * g_ref[...]
    o_ref[...] = (y * (1.0 + sc_ref[...]) + sh_ref[...]).astype(o_ref.dtype)


def _mod_spec(k, rb, d, tiles_per_batch):
    return pl.BlockSpec((None, None, rb, d), lambda i: (k, i // tiles_per_batch, 0, 0))


def _normmod(x, g, mod, tm, tiles_per_batch):
    m, d = x.shape
    rb = mod.shape[2]
    return pl.pallas_call(
        _normmod_kernel,
        grid=(m // tm,),
        in_specs=[pl.BlockSpec((tm, d), lambda i: (i, 0)),
                  pl.BlockSpec((1, d), lambda i: (0, 0)),
                  _mod_spec(0, rb, d, tiles_per_batch),
                  _mod_spec(1, rb, d, tiles_per_batch)],
        out_specs=pl.BlockSpec((tm, d), lambda i: (i, 0)),
        out_shape=jax.ShapeDtypeStruct((m, d), BF16),
        compiler_params=_cparams(("arbitrary",)),
        name="normmod",
    )(x, g, mod, mod)


def _mm_kernel(x_ref, w_ref, *o_refs):
    r = _dot(x_ref[...], w_ref[...])
    for o_ref in o_refs:
        o_ref[...] = r.astype(o_ref.dtype)


def _mm(x, w, tm, tn, out_dtypes=(F32,)):
    m, k = x.shape
    n = w.shape[1]
    outs = pl.pallas_call(
        _mm_kernel,
        grid=(n // tn, m // tm),
        in_specs=[pl.BlockSpec((tm, k), lambda j, i: (i, 0)),
                  pl.BlockSpec((k, tn), lambda j, i: (0, j))],
        out_specs=[pl.BlockSpec((tm, tn), lambda j, i: (i, j)) for _ in out_dtypes],
        out_shape=[jax.ShapeDtypeStruct((m, n), dt) for dt in out_dtypes],
        compiler_params=_cparams(("arbitrary", "arbitrary")),
        name="proj_mm",
    )(x, w)
    return outs


def _gla_tables():
    c, p = GLA_CHUNK, GLA_LEVELS
    t = np.arange(c)[:, None]
    m = np.arange(c)[None, :]
    wall = np.zeros((p + 2, c, c), np.float32)
    mask = np.zeros((p + 1, c, c), np.float32)
    for l in range(p):
        half = 1 << l
        pos = t % (2 * half)
        mid = t - pos + half
        right = pos >= half
        wall[l] = np.where(right, (m >= mid) & (m <= t), (m > t) & (m < mid))
        s = m
        mask[l] = ((t >> (l + 1)) == (s >> (l + 1))) & (((t >> l) & 1) == 1) & (((s >> l) & 1) == 0)
    wall[p] = m <= t
    wall[p + 1] = m > t
    mask[p] = t == m
    return wall.reshape((p + 2) * c, c), mask


def _gla_body(q_ref, k_ref, v_ref, ra_ref, sm_ref, wall_ref, mask_ref, wa2_ref, ba_ref, g_ref,
              og_ref, st_ref, *, nchunk, valid, store_rows):
    c, p = GLA_CHUNK, GLA_LEVELS
    row = lax.broadcasted_iota(jnp.int32, (c, GLA_DK), 0)
    wall = wall_ref[...]
    for ci in range(nchunk):
        rows = pl.ds(ci * c, c)
        x = _dot3(sm_ref[rows, :], wa2_ref[...]) + ba_ref[...]
        la = _log_sigmoid(x) * (1.0 / GLA_TAU)
        if valid < c:
            rowh = lax.broadcasted_iota(jnp.int32, la.shape, 0)
            la = jnp.where(rowh < valid, la, 0.0)
        e_all = jnp.exp(_dot_sel(wall, la))
        for h in range(GLA_HEADS):
            ks = slice(h * GLA_DK, (h + 1) * GLA_DK)
            vs = slice(h * GLA_DV, (h + 1) * GLA_DV)
            q = q_ref[rows, ks] * (GLA_DK ** -0.5)
            k = k_ref[rows, ks]
            vb = v_ref[rows, vs].astype(BF16)
            a = mask_ref[p] * _dot_nt(q.astype(BF16), k.astype(BF16))
            for l in range(p):
                el = e_all[l * c:(l + 1) * c, ks]
                xl = (jnp.where(((row >> l) & 1) == 1, q, k) * el).astype(BF16)
                a = a + mask_ref[l] * _dot_nt(xl, xl)
            st = st_ref[h]
            qc = (q * e_all[p * c:(p + 1) * c, ks]).astype(BF16)
            o = _dot(a.astype(BF16), vb) + _dot_nt(qc, st.astype(BF16))
            kr = (k * e_all[(p + 1) * c:(p + 2) * c, ks]).astype(BF16)
            dec = e_all[(p + 1) * c - 1:(p + 1) * c, ks]
            st_ref[h] = dec * st + _dot_tn(vb, kr)
            og = _rms(o) * g_ref[...] * _silu(ra_ref[rows, vs])
            if store_rows < c:
                og_ref[:, vs] = og[:store_rows].astype(og_ref.dtype)
            else:
                og_ref[rows, vs] = og.astype(og_ref.dtype)


def _gla_prompt_kernel(q_ref, k_ref, v_ref, ra_ref, sm_ref, wall_ref, mask_ref, wa2_ref, ba_ref, g_ref, s0_ref,
                       og_ref, s_ref, st_ref, *, nchunk):
    n = pl.program_id(1)

    @pl.when(n == 0)
    def _():
        for h in range(GLA_HEADS):
            st_ref[h] = s0_ref[h].T

    _gla_body(q_ref, k_ref, v_ref, ra_ref, sm_ref, wall_ref, mask_ref, wa2_ref, ba_ref, g_ref, og_ref, st_ref,
              nchunk=nchunk, valid=GLA_CHUNK, store_rows=GLA_CHUNK)

    @pl.when(n == pl.num_programs(1) - 1)
    def _():
        for h in range(GLA_HEADS):
            s_ref[h] = st_ref[h].T


def _gla_prompt(pg, small, s0, wall, mask, wa2, ba, g, batch, seq, tb=256):
    nblk = seq // tb
    rowmap = lambda cb: (lambda b, n: (b * nblk + n, cb))
    const2 = lambda b, n: (0, 0)
    return pl.pallas_call(
        functools.partial(_gla_prompt_kernel, nchunk=tb // GLA_CHUNK),
        grid=(batch, nblk),
        in_specs=[pl.BlockSpec((tb, GLA_KW), rowmap(0)),
                  pl.BlockSpec((tb, GLA_KW), rowmap(1)),
                  pl.BlockSpec((tb, GLA_VW), rowmap(1)),
                  pl.BlockSpec((tb, GLA_VW), rowmap(2)),
                  pl.BlockSpec((tb, LANES), rowmap(0)),
                  pl.BlockSpec(wall.shape, const2),
                  pl.BlockSpec(mask.shape, lambda b, n: (0, 0, 0)),
                  pl.BlockSpec(wa2.shape, const2),
                  pl.BlockSpec(ba.shape, const2),
                  pl.BlockSpec(g.shape, const2),
                  pl.BlockSpec((None, GLA_HEADS, GLA_DK, GLA_DV), lambda b, n: (b, 0, 0, 0))],
        out_specs=[pl.BlockSpec((tb, GLA_VW), rowmap(0)),
                   pl.BlockSpec((None, GLA_HEADS, GLA_DK, GLA_DV), lambda b, n: (b, 0, 0, 0))],
        out_shape=[jax.ShapeDtypeStruct((batch * seq, GLA_VW), BF16),
                   jax.ShapeDtypeStruct((batch, GLA_HEADS, GLA_DK, GLA_DV), F32)],
        scratch_shapes=[pltpu.VMEM((GLA_HEADS, GLA_DV, GLA_DK), F32)],
        compiler_params=_cparams(("arbitrary", "arbitrary")),
        name="gla_prompt",
    )(pg, pg, pg, pg, small, wall, mask, wa2, ba, g, s0)


def _gla_sample_kernel(pg_ref, sm_ref, wall_ref, mask_ref, wa2_ref, ba_ref, g_ref, s0_ref,
                       og_ref, s_ref, pad_ref, smpad_ref, st_ref, *, t):
    @pl.when(pl.program_id(0) == 0)
    def _():
        pad_ref[...] = jnp.zeros(pad_ref.shape, F32)
        smpad_ref[...] = jnp.zeros(smpad_ref.shape, F32)

    pad_ref[0:t, :] = pg_ref[...]
    smpad_ref[0:t, :] = sm_ref[...]
    for h in range(GLA_HEADS):
        st_ref[h] = s0_ref[h].T
    q_ref = pad_ref.at[:, 0:GLA_KW]
    k_ref = pad_ref.at[:, GLA_KW:2 * GLA_KW]
    v_ref = pad_ref.at[:, 2 * GLA_KW:2 * GLA_KW + GLA_VW]
    ra_ref = pad_ref.at[:, 2 * GLA_KW + GLA_VW:2 * GLA_KW + 2 * GLA_VW]
    _gla_body(q_ref, k_ref, v_ref, ra_ref, smpad_ref, wall_ref, mask_ref, wa2_ref, ba_ref, g_ref, og_ref, st_ref,
              nchunk=1, valid=t, store_rows=t)
    for h in range(GLA_HEADS):
        s_ref[h] = st_ref[h].T


def _gla_sample(pg, small, s0, wall, mask, wa2, ba, g, batch, t):
    width = pg.shape[-1]
    const2 = lambda b: (0, 0)
    return pl.pallas_call(
        functools.partial(_gla_sample_kernel, t=t),
        grid=(batch,),
        in_specs=[pl.BlockSpec((None, t, width), lambda b: (b, 0, 0)),
                  pl.BlockSpec((None, t, LANES), lambda b: (b, 0, 0)),
                  pl.BlockSpec(wall.shape, const2),
                  pl.BlockSpec(mask.shape, lambda b: (0, 0, 0)),
                  pl.BlockSpec(wa2.shape, const2),
                  pl.BlockSpec(ba.shape, const2),
                  pl.BlockSpec(g.shape, const2),
                  pl.BlockSpec((None, GLA_HEADS, GLA_DK, GLA_DV), lambda b: (b, 0, 0, 0))],
        out_specs=[pl.BlockSpec((None, t, GLA_VW), lambda b: (b, 0, 0)),
                   pl.BlockSpec((None, GLA_HEADS, GLA_DK, GLA_DV), lambda b: (b, 0, 0, 0))],
        out_shape=[jax.ShapeDtypeStruct((batch, t, GLA_VW), F32),
                   jax.ShapeDtypeStruct((batch, GLA_HEADS, GLA_DK, GLA_DV), F32)],
        scratch_shapes=[pltpu.VMEM((GLA_CHUNK, width), F32),
                        pltpu.VMEM((GLA_CHUNK, LANES), F32),
                        pltpu.VMEM((GLA_HEADS, GLA_DV, GLA_DK), F32)],
        compiler_params=_cparams(("arbitrary",)),
        name="gla_sample",
    )(pg, small, wall, mask, wa2, ba, g, s0)


FB_LANE = GLA_RANK


def _fox_bias_prompt_kernel(sm_ref, bf_ref, tri_ref, lf_ref, fc_ref, ft_ref, carry_ref):
    @pl.when(pl.program_id(1) == 0)
    def _():
        carry_ref[...] = jnp.zeros(carry_ref.shape, F32)

    lf = _log_sigmoid(sm_ref[...] + bf_ref[...])
    lf_ref[...] = lf[:, FB_LANE:FB_LANE + FOX_HEADS]
    cum = _dot_sel(tri_ref[...], lf) + carry_ref[...]
    carry_ref[...] = cum[cum.shape[0] - 1:, :]
    fc_ref[...] = cum
    ft_ref[...] = cum.T[FB_LANE:FB_LANE + FOX_HEADS, :]


def _fox_bias_prompt(small, bf_row, batch, seq, tb=256):
    nblk = seq // tb
    tri = jnp.asarray(np.tril(np.ones((tb, tb), np.float32)), BF16)
    return pl.pallas_call(
        _fox_bias_prompt_kernel,
        grid=(batch, nblk),
        in_specs=[pl.BlockSpec((tb, LANES), lambda b, n: (b * nblk + n, 0)),
                  pl.BlockSpec((1, LANES), lambda b, n: (0, 0)),
                  pl.BlockSpec((tb, tb), lambda b, n: (0, 0))],
        out_specs=[pl.BlockSpec((tb, FOX_HEADS), lambda b, n: (b * nblk + n, 0)),
                   pl.BlockSpec((tb, LANES), lambda b, n: (b * nblk + n, 0)),
                   pl.BlockSpec((None, FOX_HEADS, tb), lambda b, n: (b, 0, n))],
        out_shape=[jax.ShapeDtypeStruct((batch * seq, FOX_HEADS), F32),
                   jax.ShapeDtypeStruct((batch * seq, LANES), F32),
                   jax.ShapeDtypeStruct((batch, FOX_HEADS, seq), F32)],
        scratch_shapes=[pltpu.VMEM((1, LANES), F32)],
        compiler_params=_cparams(("arbitrary", "arbitrary")),
        name="fox_bias_prompt",
    )(small, bf_row, tri)


def _fox_bias_sample_kernel(sm_ref, bf_ref, sel_ref, lf_ref, fc_ref, ft_ref, *, t):
    lf = _log_sigmoid(sm_ref[...] + bf_ref[...])
    lf_ref[...] = lf[:, FB_LANE:FB_LANE + FOX_HEADS]
    cum = _dot_sel(sel_ref[...], lf)
    fc_ref[...] = cum
    ft_ref[...] = cum.T[FB_LANE:FB_LANE + FOX_HEADS, :]


def _fox_bias_sample(small, bf_row, t):
    rows = small.shape[0]
    r = np.arange(rows)
    sel = ((r[:, None] // t) == (r[None, :] // t)) & (r[None, :] <= r[:, None])
    sel = jnp.asarray(sel.astype(np.float32), BF16)
    full = lambda shape: pl.BlockSpec(shape, lambda i: tuple(0 for _ in shape))
    return pl.pallas_call(
        functools.partial(_fox_bias_sample_kernel, t=t),
        grid=(1,),
        in_specs=[full((rows, LANES)), full((1, LANES)), full((rows, rows))],
        out_specs=[full((rows, FOX_HEADS)), full((rows, LANES)), full((FOX_HEADS, rows))],
        out_shape=[jax.ShapeDtypeStruct((rows, FOX_HEADS), F32),
                   jax.ShapeDtypeStruct((rows, LANES), F32),
                   jax.ShapeDtypeStruct((FOX_HEADS, rows), F32)],
        compiler_params=_cparams(("arbitrary",)),
        name="fox_bias_sample",
    )(small, bf_row, sel)


def _fox_prompt_kernel(q_ref, k_ref, v_ref, fc_ref, ft_ref, o_ref, *, tq, tk):
    h = pl.program_id(1)
    i = pl.program_id(2)
    q = (q_ref[...].astype(F32) * (FOX_DH ** -0.5)).astype(BF16)
    lane = lax.broadcasted_iota(jnp.int32, (tq, LANES), 1)
    f_t = jnp.sum(jnp.where(lane == h + FB_LANE, fc_ref[...], 0.0), axis=-1, keepdims=True)
    rowi = lax.broadcasted_iota(jnp.int32, (tq, tk), 0)
    coli = lax.broadcasted_iota(jnp.int32, (tq, tk), 1)

    def step(j, carry, masked):
        m, l, acc = carry
        ks = pl.ds(pl.multiple_of(j * tk, tk), tk)
        s = _dot_nt(q, k_ref[ks, :])
        f_s = ft_ref[:, ks]
        s = s + f_t - f_s
        if masked:
            s = jnp.where(rowi >= coli, s, NEG)
        m_new = jnp.maximum(m, jnp.max(s, axis=-1, keepdims=True))
        p = jnp.exp(s - m_new)
        alpha = jnp.exp(m - m_new)
        l = alpha * l + jnp.sum(p, axis=-1, keepdims=True)
        acc = alpha * acc + _dot(p.astype(BF16), v_ref[ks, :])
        return m_new, l, acc

    init = (jnp.full((tq, 1), NEG, F32), jnp.zeros((tq, 1), F32), jnp.zeros((tq, FOX_DH), F32))
    nfull = i * (tq // tk)
    carry = lax.fori_loop(0, nfull, lambda j, c: step(j, c, False), init)
    m, l, acc = step(nfull, carry, True)
    o_ref[...] = (acc / l).astype(o_ref.dtype)


def _fox_prompt(qb, kb, vb, fcol, ft, batch, seq, tq=256):
    nq = seq // tq
    return pl.pallas_call(
        functools.partial(_fox_prompt_kernel, tq=tq, tk=tq),
        grid=(batch, FOX_HEADS, nq),
        in_specs=[pl.BlockSpec((tq, FOX_DH), lambda b, h, i: (b * nq + i, h)),
                  pl.BlockSpec((seq, FOX_DH), lambda b, h, i: (b, h)),
                  pl.BlockSpec((seq, FOX_DH), lambda b, h, i: (b, h)),
                  pl.BlockSpec((tq, LANES), lambda b, h, i: (b * nq + i, 0)),
                  pl.BlockSpec((None, None, 1, seq), lambda b, h, i: (b, h, 0, 0))],
        out_specs=pl.BlockSpec((tq, FOX_DH), lambda b, h, i: (b * nq + i, h)),
        out_shape=jax.ShapeDtypeStruct((batch * seq, FOX_W), BF16),
        compiler_params=_cparams(("arbitrary", "arbitrary", "arbitrary")),
        name="fox_prompt",
    )(qb, kb, vb, fcol, ft.reshape(batch, FOX_HEADS, 1, seq))


PAGES_PER_STEP = 4
ROWS8 = 8


def _fox_sample_kernel(pt_ref, q_ref, kn_ref, vn_ref, fc_ref, ft_ref, *refs, t, npages):
    g = PAGES_PER_STEP
    k_refs, v_refs, lf_refs = refs[0:g], refs[g:2 * g], refs[2 * g:3 * g]
    ustrict_ref = refs[3 * g]
    o_ref = refs[3 * g + 1]
    q8_ref, kpad_ref, vpad_ref, m_ref, l_ref, acc_ref, carry_ref = refs[3 * g + 2:]
    j = pl.program_id(1)
    lane = lax.broadcasted_iota(jnp.int32, (ROWS8, LANES), 1)
    rowi = lax.broadcasted_iota(jnp.int32, (ROWS8, LANES), 0)

    def f_col(h):
        return jnp.sum(jnp.where(lane == h + FB_LANE, fc_ref[...], 0.0), axis=-1, keepdims=True)

    @pl.when(j == 0)
    def _():
        q8_ref[...] = jnp.zeros(q8_ref.shape, F32)
        q8_ref[0:t, :] = q_ref[...] * (FOX_DH ** -0.5)
        kpad_ref[...] = jnp.zeros(kpad_ref.shape, F32)
        vpad_ref[...] = jnp.zeros(vpad_ref.shape, F32)
        kpad_ref[0:t, :] = kn_ref[...]
        vpad_ref[0:t, :] = vn_ref[...]
        carry_ref[...] = jnp.zeros(carry_ref.shape, F32)
        for h in range(FOX_HEADS):
            hs = slice(h * FOX_DH, (h + 1) * FOX_DH)
            s = _dot_nt(q8_ref[:, hs], kpad_ref[:, hs])
            s = s + f_col(h) - ft_ref[h:h + 1, :]
            s = jnp.where((lane <= rowi) & (lane < t), s, NEG)
            m = jnp.max(s, axis=-1, keepdims=True)
            p = jnp.exp(s - m)
            m_ref[h] = jnp.broadcast_to(m, (ROWS8, LANES))
            l_ref[h] = jnp.broadcast_to(jnp.sum(p, axis=-1, keepdims=True), (ROWS8, LANES))
            acc_ref[h] = _dot(p, vpad_ref[:, hs])

    r_pages = []
    carry = carry_ref[...]
    for gi in range(g):
        lf = lf_refs[gi][...]
        r_pages.append(_dot_sel_rhs(lf, ustrict_ref[...]) + carry)
        carry = carry + jnp.sum(lf, axis=-1, keepdims=True)
    carry_ref[...] = carry
    for h in range(FOX_HEADS):
        hs = slice(h * FOX_DH, (h + 1) * FOX_DH)
        qh = q8_ref[:, hs]
        fc = f_col(h)
        s_list = [_dot_nt(qh, k_refs[gi][:, hs]) + fc + r_pages[gi][h:h + 1, :] for gi in range(g)]
        m_old = m_ref[h]
        m_new = m_old
        for s in s_list:
            m_new = jnp.maximum(m_new, jnp.max(s, axis=-1, keepdims=True))
        alpha = jnp.exp(m_old - m_new)
        l = alpha * l_ref[h]
        acc = alpha * acc_ref[h]
        for gi in range(g):
            p = jnp.exp(s_list[gi] - m_new)
            l = l + jnp.sum(p, axis=-1, keepdims=True)
            acc = acc + _dot(p, v_refs[gi][:, hs])
        m_ref[h] = m_new
        l_ref[h] = l
        acc_ref[h] = acc

    @pl.when(j == pl.num_programs(1) - 1)
    def _():
        for h in range(FOX_HEADS):
            hs = slice(h * FOX_DH, (h + 1) * FOX_DH)
            o_ref[:, hs] = (acc_ref[h] / l_ref[h])[0:t, :].astype(o_ref.dtype)


def _dot_sel_rhs(x, w01):
    hi, mid, lo = _split3(x)
    return _dot(hi, w01) + _dot(mid, w01) + _dot(lo, w01)


def _fox_sample(page_table, q, kn, vn, fcol8, ft, cache_k, cache_v, cache_lft, t):
    batch, npages = page_table.shape
    g = PAGES_PER_STEP
    nsteps = npages // g
    ustrict = jnp.asarray(np.triu(np.ones((PAGE, PAGE), np.float32), 0).T * (1 - np.eye(PAGE, dtype=np.float32)), BF16)

    def page_map(gi):
        return lambda b, j, pt: (pt[b, npages - 1 - (j * g + gi)], 0, 0)

    seq_map = lambda b, j, pt: (b, 0, 0)
    in_specs = [pl.BlockSpec((None, t, FOX_W), seq_map),
                pl.BlockSpec((None, t, FOX_W), seq_map),
                pl.BlockSpec((None, t, FOX_W), seq_map),
                pl.BlockSpec((None, ROWS8, LANES), seq_map),
                pl.BlockSpec((None, FOX_HEADS, LANES), seq_map)]
    in_specs += [pl.BlockSpec((None, PAGE, FOX_W), page_map(gi)) for gi in range(g)]
    in_specs += [pl.BlockSpec((None, PAGE, FOX_W), page_map(gi)) for gi in range(g)]
    in_specs += [pl.BlockSpec((None, FOX_HEADS, PAGE), page_map(gi)) for gi in range(g)]
    in_specs += [pl.BlockSpec((PAGE, PAGE), lambda b, j, pt: (0, 0))]
    grid_spec = pltpu.PrefetchScalarGridSpec(
        num_scalar_prefetch=1,
        grid=(batch, nsteps),
        in_specs=in_specs,
        out_specs=pl.BlockSpec((None, t, FOX_W), seq_map),
        scratch_shapes=[pltpu.VMEM((ROWS8, FOX_W), F32),
                        pltpu.VMEM((PAGE, FOX_W), F32),
                        pltpu.VMEM((PAGE, FOX_W), F32),
                        pltpu.VMEM((FOX_HEADS, ROWS8, LANES), F32),
                        pltpu.VMEM((FOX_HEADS, ROWS8, LANES), F32),
                        pltpu.VMEM((FOX_HEADS, ROWS8, FOX_DH), F32),
                        pltpu.VMEM((FOX_HEADS, LANES), F32)],
    )
    return pl.pallas_call(
        functools.partial(_fox_sample_kernel, t=t, npages=npages),
        grid_spec=grid_spec,
        out_shape=jax.ShapeDtypeStruct((batch, t, FOX_W), F32),
        compiler_params=_cparams(("arbitrary", "arbitrary")),
        name="fox_sample",
    )(page_table, q, kn, vn, fcol8, ft, *([cache_k] * g), *([cache_v] * g), *([cache_lft] * g), ustrict)


def _merge_kernel(oa_ref, ob_ref, wa_ref, wb_ref, ga_ref, gb_ref, o_ref):
    ua = _dot(oa_ref[...], wa_ref[...])
    ub = _dot(ob_ref[...], wb_ref[...])
    o_ref[...] = (jax.nn.sigmoid(ga_ref[...]) * ua + jax.nn.sigmoid(gb_ref[...]) * ub).astype(o_ref.dtype)


def _merge(oa, ob, wa, wb, gates, tm, tn=1024):
    m = oa.shape[0]
    d = wa.shape[1]
    nj = d // tn
    return pl.pallas_call(
        _merge_kernel,
        grid=(nj, m // tm),
        in_specs=[pl.BlockSpec((tm, GLA_VW), lambda j, i: (i, 0)),
                  pl.BlockSpec((tm, FOX_W), lambda j, i: (i, 0)),
                  pl.BlockSpec((GLA_VW, tn), lambda j, i: (0, j)),
                  pl.BlockSpec((FOX_W, tn), lambda j, i: (0, j)),
                  pl.BlockSpec((tm, tn), lambda j, i: (i, j)),
                  pl.BlockSpec((tm, tn), lambda j, i: (i, nj + j))],
        out_specs=pl.BlockSpec((tm, tn), lambda j, i: (i, j)),
        out_shape=jax.ShapeDtypeStruct((m, d), BF16),
        compiler_params=_cparams(("arbitrary", "arbitrary")),
        name="merge",
    )(oa, ob, wa, wb, gates, gates)


ROUTER_GROUP_LANE = N_EXPERTS


def _route(logits):
    lane_i = lax.broadcasted_iota(jnp.int32, logits.shape, 1)
    lane = lane_i.astype(F32)
    grp_of_lane = (lane_i >> 2).astype(F32)
    big = float(LANES)
    is_grp = (lane_i >= ROUTER_GROUP_LANE) & (lane_i < ROUTER_GROUP_LANE + N_GROUPS)
    gl = jnp.where(is_grp, logits, NEG)
    gmax = jnp.max(gl, axis=-1, keepdims=True)
    g_idx = jnp.min(jnp.where(is_grp & (gl == gmax), lane - ROUTER_GROUP_LANE, big), axis=-1, keepdims=True)
    g_w = 1.0 / jnp.sum(jnp.where(is_grp, jnp.exp(gl - gmax), 0.0), axis=-1, keepdims=True)
    in_grp = (lane_i < N_EXPERTS) & (grp_of_lane == g_idx)
    e1 = jnp.where(in_grp, logits, NEG)
    v1 = jnp.max(e1, axis=-1, keepdims=True)
    i1 = jnp.min(jnp.where(in_grp & (e1 == v1), lane, big), axis=-1, keepdims=True)
    rest = in_grp & (lane != i1)
    e2 = jnp.where(rest, logits, NEG)
    v2 = jnp.max(e2, axis=-1, keepdims=True)
    i2 = jnp.min(jnp.where(rest & (e2 == v2), lane, big), axis=-1, keepdims=True)
    r = jnp.exp(v2 - v1)
    w1 = g_w / (1.0 + r)
    w2 = g_w * r / (1.0 + r)
    return jnp.where(lane == i1, w1, 0.0) + jnp.where(lane == i2, w2, 0.0)


def _outproj_kernel(mg_ref, w_ref, x_ref, gt_ref, g2_ref, sh_ref, sc_ref, wr_ref, br_ref,
                    x1_ref, h2_ref, cmb_ref):
    x1 = x_ref[...] + gt_ref[...] * _dot(mg_ref[...], w_ref[...])
    x1_ref[...] = x1
    h2 = _rms(x1) * g2_ref[...] * (1.0 + sc_ref[...]) + sh_ref[...]
    h2_ref[...] = h2.astype(h2_ref.dtype)
    cmb_ref[...] = _route(_dot3(h2, wr_ref[...]) + br_ref[...])


def _outproj(merged, w_out, x, mod, g2, w_router, b_router, tm, tiles_per_batch):
    m, d = x.shape
    rb = mod.shape[2]
    const2 = lambda i: (0, 0)
    return pl.pallas_call(
        _outproj_kernel,
        grid=(m // tm,),
        in_specs=[pl.BlockSpec((tm, d), lambda i: (i, 0)),
                  pl.BlockSpec((d, d), const2),
                  pl.BlockSpec((tm, d), lambda i: (i, 0)),
                  _mod_spec(2, rb, d, tiles_per_batch),
                  pl.BlockSpec((1, d), const2),
                  _mod_spec(3, rb, d, tiles_per_batch),
                  _mod_spec(4, rb, d, tiles_per_batch),
                  pl.BlockSpec((d, LANES), const2),
                  pl.BlockSpec((1, LANES), const2)],
        out_specs=[pl.BlockSpec((tm, d), lambda i: (i, 0)),
                   pl.BlockSpec((tm, d), lambda i: (i, 0)),
                   pl.BlockSpec((tm, LANES), lambda i: (i, 0))],
        out_shape=[jax.ShapeDtypeStruct((m, d), F32),
                   jax.ShapeDtypeStruct((m, d), BF16),
                   jax.ShapeDtypeStruct((m, LANES), F32)],
        compiler_params=_cparams(("arbitrary",)),
        name="outproj",
    )(merged, w_out, x, mod, g2, mod, mod, w_router, b_router)


def _moe_kernel(h_ref, cmb_ref, wg_ref, wu_ref, wd_ref, x1_ref, gt_ref, gf_ref, y_ref, acc_ref):
    e = pl.program_id(1)

    @pl.when(e == 0)
    def _():
        acc_ref[...] = jnp.zeros(acc_ref.shape, F32)

    h = h_ref[...]
    a = _dot(h, wg_ref[...])
    u = _dot(h, wu_ref[...])
    lane = lax.broadcasted_iota(jnp.int32, cmb_ref.shape, 1)
    cw = jnp.sum(jnp.where(lane == e, cmb_ref[...], 0.0), axis=-1, keepdims=True)
    hid = (_silu(a) * u * cw).astype(BF16)
    acc_ref[...] += _dot(hid, wd_ref[...])

    @pl.when(e == pl.num_programs(1) - 1)
    def _():
        x2 = x1_ref[...] + gt_ref[...] * acc_ref[...]
        y_ref[...] = _rms(x2) * gf_ref[...]


def _moe(h2, cmb, wg, wu, wd, x1, mod, g_final, tm, tiles_per_batch):
    m, d = x1.shape
    rb = mod.shape[2]
    ne, _, de = wg.shape
    return pl.pallas_call(
        _moe_kernel,
        grid=(m // tm, ne),
        in_specs=[pl.BlockSpec((tm, d), lambda i, e: (i, 0)),
                  pl.BlockSpec((tm, LANES), lambda i, e: (i, 0)),
                  pl.BlockSpec((None, d, de), lambda i, e: (e, 0, 0)),
                  pl.BlockSpec((None, d, de), lambda i, e: (e, 0, 0)),
                  pl.BlockSpec((None, de, d), lambda i, e: (e, 0, 0)),
                  pl.BlockSpec((tm, d), lambda i, e: (i, 0)),
                  pl.BlockSpec((None, None, rb, d), lambda i, e: (5, i // tiles_per_batch, 0, 0)),
                  pl.BlockSpec((1, d), lambda i, e: (0, 0))],
        out_specs=pl.BlockSpec((tm, d), lambda i, e: (i, 0)),
        out_shape=jax.ShapeDtypeStruct((m, d), F32),
        scratch_shapes=[pltpu.VMEM((tm, d), F32)],
        compiler_params=_cparams(("arbitrary", "arbitrary")),
        name="moe",
    )(h2, cmb, wg, wu, wd, x1, mod, g_final)


def _prep_weights(w_ada, b_ada, g_norm1, g_norm2, g_final, w_in, w_a2, b_a, b_f, g_gla_norm, w_up_a, w_up_b, w_out,
                  w_grp, b_grp, w_exp, b_exp, w_gate_e, w_up_e, w_down_e):
    w = w_in[0]
    d = D_MODEL
    o_lra = 2 * GLA_KW + 2 * GLA_VW
    o_fox = o_lra + GLA_RANK
    o_fb = o_fox + 3 * FOX_W
    o_g = o_fb + FOX_HEADS
    pad = jnp.zeros((d, LANES - GLA_RANK - FOX_HEADS), F32)
    wall, mask = _gla_tables()
    bf_row = jnp.zeros((1, LANES), F32).at[0, FB_LANE:FB_LANE + FOX_HEADS].set(b_f[0])
    w_router = jnp.concatenate([w_exp[0], w_grp[0], jnp.zeros((d, LANES - N_EXPERTS - N_GROUPS), F32)], axis=1)
    b_router = jnp.concatenate([b_exp[0], b_grp[0], jnp.zeros((LANES - N_EXPERTS - N_GROUPS,), F32)])[None, :]
    return dict(
        w_ada=w_ada[0], b_ada=b_ada[0][None, :],
        g1=g_norm1[0][None, :], g2=g_norm2[0][None, :], gf=g_final[None, :],
        w_gla=w[:, 0:o_lra].astype(BF16),
        w_q=w[:, o_fox:o_fox + FOX_W].astype(BF16),
        w_k=w[:, o_fox + FOX_W:o_fox + 2 * FOX_W].astype(BF16),
        w_v=w[:, o_fox + 2 * FOX_W:o_fox + 3 * FOX_W].astype(BF16),
        w_gates=w[:, o_g:o_g + 2 * d].astype(BF16),
        w_small=jnp.concatenate([w[:, o_lra:o_lra + GLA_RANK], w[:, o_fb:o_fb + FOX_HEADS], pad], axis=1).astype(BF16),
        wall=jnp.asarray(wall, BF16), mask=jnp.asarray(mask, F32),
        w_a2=jnp.concatenate([w_a2[0], jnp.zeros((LANES - GLA_RANK, GLA_KW), F32)], axis=0), b_a=b_a[0][None, :], bf_row=bf_row, g_gla=g_gla_norm[0][None, :],
        w_up_a=w_up_a[0].astype(BF16), w_up_b=w_up_b[0].astype(BF16), w_out=w_out[0].astype(BF16),
        w_router=w_router, b_router=b_router,
        wg=w_gate_e[0].astype(BF16), wu=w_up_e[0].astype(BF16), wd=w_down_e[0].astype(BF16),
    )


def _project(h, p, tm):
    tn = 1024
    (pg,) = _mm(h, p["w_gla"], tm, tn)
    (qb,) = _mm(h, p["w_q"], tm, tn, (BF16,))
    kb, kb16 = _mm(h, p["w_k"], tm, tn, (F32, BF16))
    vb, vb16 = _mm(h, p["w_v"], tm, tn, (F32, BF16))
    (gates,) = _mm(h, p["w_gates"], tm, tn)
    (small,) = _mm(h, p["w_small"], tm, LANES)
    return pg, qb, kb, kb16, vb, vb16, gates, small


def _tail(x, oa, ob, gates, mod, p, tm, tiles_per_batch):
    merged = _merge(oa, ob, p["w_up_a"], p["w_up_b"], gates, tm)
    x1, h2, cmb = _outproj(merged, p["w_out"], x, mod, p["g2"], p["w_router"], p["b_router"],
                           min(tm, 256), tiles_per_batch * (tm // min(tm, 256)))
    tmm = min(tm, 512)
    return _moe(h2, cmb, p["wg"], p["wu"], p["wd"], x1, mod, p["gf"], tmm, tiles_per_batch * (tm // tmm))


def kernel(x_prompt, x_sample, cache_k, cache_v, cache_logf, state_gla, page_table, c_prompt, c_sample, w_ada, b_ada,
           g_norm1, g_norm2, g_final, w_in, w_a2, b_a, b_f, g_gla_norm, w_up_a, w_up_b, w_out, w_grp, b_grp, w_exp,
           b_exp, w_gate_e, w_up_e, w_down_e):
    p = _prep_weights(w_ada, b_ada, g_norm1, g_norm2, g_final, w_in, w_a2, b_a, b_f, g_gla_norm, w_up_a, w_up_b,
                      w_out, w_grp, b_grp, w_exp, b_exp, w_gate_e, w_up_e, w_down_e)
    bp, seq, d = x_prompt.shape
    bs, t, _ = x_sample.shape

    mod = _adaln(jnp.concatenate([c_prompt, c_sample], axis=0), p["w_ada"], p["b_ada"])
    mod_p = mod[:bp].reshape(bp, 6, 1, d).transpose(1, 0, 2, 3)
    mod_s = jnp.repeat(mod[bp:].reshape(bs, 6, d), t, axis=0).transpose(1, 0, 2)[:, None]

    tm = 1024
    tpb = seq // tm
    xp = x_prompt.reshape(bp * seq, d)
    hp = _normmod(xp, p["g1"], mod_p, tm, tpb)
    pg, qb, kb, kb16, vb, vb16, gates, small = _project(hp, p, tm)
    s0 = jnp.zeros((bp, GLA_HEADS, GLA_DK, GLA_DV), F32)
    oa, s_p = _gla_prompt(pg, small, s0, p["wall"], p["mask"], p["w_a2"], p["b_a"], p["g_gla"], bp, seq)
    lf_p, fcol, ft = _fox_bias_prompt(small, p["bf_row"], bp, seq)
    ob = _fox_prompt(qb, kb16, vb16, fcol, ft, bp, seq)
    y_p = _tail(xp, oa, ob, gates, mod_p, p, tm, tpb)

    rows = bs * t
    xs = x_sample.reshape(rows, d)
    hs = _normmod(xs, p["g1"], mod_s, rows, 1)
    pg_s, qs, ks, _, vs, _, gates_s, small_s = _project(hs, p, rows)
    oa_s, s_s = _gla_sample(pg_s.reshape(bs, t, -1), small_s.reshape(bs, t, LANES), state_gla[0], p["wall"], p["mask"],
                            p["w_a2"], p["b_a"], p["g_gla"], bs, t)
    lf_s, fcol_s, ft_s = _fox_bias_sample(small_s, p["bf_row"], t)
    fcol8 = jnp.pad(fcol_s.reshape(bs, t, LANES), ((0, 0), (0, ROWS8 - t), (0, 0)))
    ft8 = jnp.pad(ft_s.reshape(FOX_HEADS, bs, t).transpose(1, 0, 2), ((0, 0), (0, 0), (0, LANES - t)))
    n_pool = cache_k.shape[1]
    ob_s = _fox_sample(page_table, qs.astype(F32).reshape(bs, t, FOX_W), ks.reshape(bs, t, FOX_W),
                       vs.reshape(bs, t, FOX_W), fcol8, ft8,
                       cache_k[0].reshape(n_pool, PAGE, FOX_W), cache_v[0].reshape(n_pool, PAGE, FOX_W),
                       cache_logf[0].transpose(0, 2, 1), t)
    y_s = _tail(xs, oa_s.reshape(rows, GLA_VW).astype(BF16), ob_s.reshape(rows, FOX_W).astype(BF16), gates_s, mod_s,
                p, rows, 1)

    return (y_p.reshape(bp, seq, d), y_s.reshape(bs, t, d),
            kb.reshape(1, bp, seq, FOX_HEADS, FOX_DH), vb.reshape(1, bp, seq, FOX_HEADS, FOX_DH),
            lf_p.reshape(1, bp, seq, FOX_HEADS), s_p[None],
            ks.reshape(1, bs, t, FOX_HEADS, FOX_DH), vs.reshape(1, bs, t, FOX_HEADS, FOX_DH),
            lf_s.reshape(1, bs, t, FOX_HEADS), s_s[None])
```

```python
import functools

import numpy as np
import jax
import jax.numpy as jnp
from jax import lax
from jax.experimental import pallas as pl
from jax.experimental.pallas import tpu as pltpu

F32 = jnp.float32
BF16 = jnp.bfloat16

D_MODEL = 2048
GLA_HEADS = 4
GLA_DK = 128
GLA_DV = 256
GLA_RANK = 16
GLA_TAU = 16.0
FOX_HEADS = 8
FOX_DH = 128
PAGE = 128
N_GROUPS = 4
EXP_PER_GROUP = 4
N_EXPERTS = 16
D_EXPERT = 512
RMS_EPS = 1e-6
GLA_KW = GLA_HEADS * GLA_DK
GLA_VW = GLA_HEADS * GLA_DV
FOX_W = FOX_HEADS * FOX_DH
GLA_CHUNK = 128
GLA_LEVELS = 7
LANES = 128
NEG = -1e30
VMEM_LIMIT = 56 * 1024 * 1024


def _cparams(sem):
    return pltpu.CompilerParams(dimension_semantics=sem, vmem_limit_bytes=VMEM_LIMIT)


def _dot(a, b):
    return jnp.dot(a, b, preferred_element_type=F32)


def _dot_nt(a, b):
    return lax.dot_general(a, b, (((1,), (1,)), ((), ())), preferred_element_type=F32)


def _dot_tn(a, b):
    return lax.dot_general(a, b, (((0,), (0,)), ((), ())), preferred_element_type=F32)


def _split2(x):
    hi = x.astype(BF16)
    lo = (x - hi.astype(F32)).astype(BF16)
    return hi, lo


def _split3(x):
    hi = x.astype(BF16)
    r = x - hi.astype(F32)
    mid = r.astype(BF16)
    lo = (r - mid.astype(F32)).astype(BF16)
    return hi, mid, lo


def _dot3(a, b):
    ah, al = _split2(a)
    bh, bl = _split2(b)
    return _dot(ah, bh) + _dot(ah, bl) + _dot(al, bh)


def _dot_sel(w01, x):
    hi, mid, lo = _split3(x)
    return _dot(w01, hi) + _dot(w01, mid) + _dot(w01, lo)


def _log_sigmoid(x):
    return jnp.minimum(x, 0.0) - jnp.log1p(jnp.exp(-jnp.abs(x)))


def _silu(x):
    return x * jax.nn.sigmoid(x)


def _adaln_kernel(c_ref, w_ref, b_ref, o_ref):
    o_ref[...] = _dot3(_silu(c_ref[...]), w_ref[...]) + b_ref[...]


def _adaln(c, w, b, tn=512):
    nb, d = c.shape
    n = w.shape[1]
    return pl.pallas_call(
        _adaln_kernel,
        grid=(n // tn,),
        in_specs=[pl.BlockSpec((nb, d), lambda j: (0, 0)),
                  pl.BlockSpec((d, tn), lambda j: (0, j)),
                  pl.BlockSpec((1, tn), lambda j: (0, j))],
        out_specs=pl.BlockSpec((nb, tn), lambda j: (0, j)),
        out_shape=jax.ShapeDtypeStruct((nb, n), F32),
        compiler_params=_cparams(("arbitrary",)),
        name="adaln",
    )(c, w, b)


def _rms(x):
    return x * lax.rsqrt(jnp.mean(x * x, axis=-1, keepdims=True) + RMS_EPS)


def _normmod_kernel(x_ref, g_ref, sh_ref, sc_ref, o_ref):
    y = _rms(x_ref[...]) * g_ref[...]
    o_ref[...] = (y * (1.0 + sc_ref[...]) + sh_ref[...]).astype(o_ref.dtype)


def _mod_spec(k, rb, d, tiles_per_batch):
    return pl.BlockSpec((None, None, rb, d), lambda i: (k, i // tiles_per_batch, 0, 0))


def _normmod(x, g, mod, tm, tiles_per_batch):
    m, d = x.shape
    rb = mod.shape[2]
    return pl.pallas_call(
        _normmod_kernel,
        grid=(m // tm,),
        in_specs=[pl.BlockSpec((tm, d), lambda i: (i, 0)),
                  pl.BlockSpec((1, d), lambda i: (0, 0)),
                  _mod_spec(0, rb, d, tiles_per_batch),
                  _mod_spec(1, rb, d, tiles_per_batch)],
        out_specs=pl.BlockSpec((tm, d), lambda i: (i, 0)),
        out_shape=jax.ShapeDtypeStruct((m, d), BF16),
        compiler_params=_cparams(("arbitrary",)),
        name="normmod",
    )(x, g, mod, mod)


def _mm_kernel(x_ref, w_ref, *o_refs):
    r = _dot(x_ref[...], w_ref[...])
    for o_ref in o_refs:
        o_ref[...] = r.astype(o_ref.dtype)


def _mm(x, w, tm, tn, out_dtypes=(F32,)):
    m, k = x.shape
    n = w.shape[1]
    outs = pl.pallas_call(
        _mm_kernel,
        grid=(n // tn, m // tm),
        in_specs=[pl.BlockSpec((tm, k), lambda j, i: (i, 0)),
                  pl.BlockSpec((k, tn), lambda j, i: (0, j))],
        out_specs=[pl.BlockSpec((tm, tn), lambda j, i: (i, j)) for _ in out_dtypes],
        out_shape=[jax.ShapeDtypeStruct((m, n), dt) for dt in out_dtypes],
        compiler_params=_cparams(("arbitrary", "arbitrary")),
        name="proj_mm",
    )(x, w)
    return outs


def _gla_tables():
    c, p = GLA_CHUNK, GLA_LEVELS
    t = np.arange(c)[:, None]
    m = np.arange(c)[None, :]
    wall = np.zeros((p + 2, c, c), np.float32)
    mask = np.zeros((p + 1, c, c), np.float32)
    for l in range(p):
        half = 1 << l
        pos = t % (2 * half)
        mid = t - pos + half
        right = pos >= half
        wall[l] = np.where(right, (m >= mid) & (m <= t), (m > t) & (m < mid))
        s = m
        mask[l] = ((t >> (l + 1)) == (s >> (l + 1))) & (((t >> l) & 1) == 1) & (((s >> l) & 1) == 0)
    wall[p] = m <= t
    wall[p + 1] = m > t
    mask[p] = t == m
    return wall.reshape((p + 2) * c, c), mask


def _gla_body(q_ref, k_ref, v_ref, ra_ref, sm_ref, wall_ref, mask_ref, wa2_ref, ba_ref, g_ref,
              og_ref, st_ref, *, nchunk, valid, store_rows):
    c, p = GLA_CHUNK, GLA_LEVELS
    row = lax.broadcasted_iota(jnp.int32, (c, GLA_DK), 0)
    wall = wall_ref[...]
    for ci in range(nchunk):
        rows = pl.ds(ci * c, c)
        x = _dot3(sm_ref[rows, :], wa2_ref[...]) + ba_ref[...]
        la = _log_sigmoid(x) * (1.0 / GLA_TAU)
        if valid < c:
            rowh = lax.broadcasted_iota(jnp.int32, la.shape, 0)
            la = jnp.where(rowh < valid, la, 0.0)
        e_all = jnp.exp(_dot_sel(wall, la))
        for h in range(GLA_HEADS):
            ks = slice(h * GLA_DK, (h + 1) * GLA_DK)
            vs = slice(h * GLA_DV, (h + 1) * GLA_DV)
            q = q_ref[rows, ks] * (GLA_DK ** -0.5)
            k = k_ref[rows, ks]
            vb = v_ref[rows, vs].astype(BF16)
            a = mask_ref[p] * _dot_nt(q.astype(BF16), k.astype(BF16))
            for l in range(p):
                el = e_all[l * c:(l + 1) * c, ks]
                xl = (jnp.where(((row >> l) & 1) == 1, q, k) * el).astype(BF16)
                a = a + mask_ref[l] * _dot_nt(xl, xl)
            st = st_ref[h]
            qc = (q * e_all[p * c:(p + 1) * c, ks]).astype(BF16)
            o = _dot(a.astype(BF16), vb) + _dot_nt(qc, st.astype(BF16))
            kr = (k * e_all[(p + 1) * c:(p + 2) * c, ks]).astype(BF16)
            dec = e_all[(p + 1) * c - 1:(p + 1) * c, ks]
            st_ref[h] = dec * st + _dot_tn(vb, kr)
            og = _rms(o) * g_ref[...] * _silu(ra_ref[rows, vs])
            if store_rows < c:
                og_ref[:, vs] = og[:store_rows].astype(og_ref.dtype)
            else:
                og_ref[rows, vs] = og.astype(og_ref.dtype)


def _gla_prompt_kernel(q_ref, k_ref, v_ref, ra_ref, sm_ref, wall_ref, mask_ref, wa2_ref, ba_ref, g_ref, s0_ref,
                       og_ref, s_ref, st_ref, *, nchunk):
    n = pl.program_id(1)

    @pl.when(n == 0)
    def _():
        for h in range(GLA_HEADS):
            st_ref[h] = s0_ref[h].T

    _gla_body(q_ref, k_ref, v_ref, ra_ref, sm_ref, wall_ref, mask_ref, wa2_ref, ba_ref, g_ref, og_ref, st_ref,
              nchunk=nchunk, valid=GLA_CHUNK, store_rows=GLA_CHUNK)

    @pl.when(n == pl.num_programs(1) - 1)
    def _():
        for h in range(GLA_HEADS):
            s_ref[h] = st_ref[h].T


def _gla_prompt(pg, small, s0, wall, mask, wa2, ba, g, batch, seq, tb=256):
    nblk = seq // tb
    rowmap = lambda cb: (lambda b, n: (b * nblk + n, cb))
    const2 = lambda b, n: (0, 0)
    return pl.pallas_call(
        functools.partial(_gla_prompt_kernel, nchunk=tb // GLA_CHUNK),
        grid=(batch, nblk),
        in_specs=[pl.BlockSpec((tb, GLA_KW), rowmap(0)),
                  pl.BlockSpec((tb, GLA_KW), rowmap(1)),
                  pl.BlockSpec((tb, GLA_VW), rowmap(1)),
                  pl.BlockSpec((tb, GLA_VW), rowmap(2)),
                  pl.BlockSpec((tb, LANES), rowmap(0)),
                  pl.BlockSpec(wall.shape, const2),
                  pl.BlockSpec(mask.shape, lambda b, n: (0, 0, 0)),
                  pl.BlockSpec(wa2.shape, const2),
                  pl.BlockSpec(ba.shape, const2),
                  pl.BlockSpec(g.shape, const2),
                  pl.BlockSpec((None, GLA_HEADS, GLA_DK, GLA_DV), lambda b, n: (b, 0, 0, 0))],
        out_specs=[pl.BlockSpec((tb, GLA_VW), rowmap(0)),
                   pl.BlockSpec((None, GLA_HEADS, GLA_DK, GLA_DV), lambda b, n: (b, 0, 0, 0))],
        out_shape=[jax.ShapeDtypeStruct((batch * seq, GLA_VW), BF16),
                   jax.ShapeDtypeStruct((batch, GLA_HEADS, GLA_DK, GLA_DV), F32)],
        scratch_shapes=[pltpu.VMEM((GLA_HEADS, GLA_DV, GLA_DK), F32)],
        compiler_params=_cparams(("arbitrary", "arbitrary")),
        name="gla_prompt",
    )(pg, pg, pg, pg, small, wall, mask, wa2, ba, g, s0)


def _gla_sample_kernel(pg_ref, sm_ref, wall_ref, mask_ref, wa2_ref, ba_ref, g_ref, s0_ref,
                       og_ref, s_ref, pad_ref, smpad_ref, st_ref, *, t):
    @pl.when(pl.program_id(0) == 0)
    def _():
        pad_ref[...] = jnp.zeros(pad_ref.shape, F32)
        smpad_ref[...] = jnp.zeros(smpad_ref.shape, F32)

    pad_ref[0:t, :] = pg_ref[...]
    smpad_ref[0:t, :] = sm_ref[...]
    for h in range(GLA_HEADS):
        st_ref[h] = s0_ref[h].T
    q_ref = pad_ref.at[:, 0:GLA_KW]
    k_ref = pad_ref.at[:, GLA_KW:2 * GLA_KW]
    v_ref = pad_ref.at[:, 2 * GLA_KW:2 * GLA_KW + GLA_VW]
    ra_ref = pad_ref.at[:, 2 * GLA_KW + GLA_VW:2 * GLA_KW + 2 * GLA_VW]
    _gla_body(q_ref, k_ref, v_ref, ra_ref, smpad_ref, wall_ref, mask_ref, wa2_ref, ba_ref, g_ref, og_ref, st_ref,
              nchunk=1, valid=t, store_rows=t)
    for h in range(GLA_HEADS):
        s_ref[h] = st_ref[h].T


def _gla_sample(pg, small, s0, wall, mask, wa2, ba, g, batch, t):
    width = pg.shape[-1]
    const2 = lambda b: (0, 0)
    return pl.pallas_call(
        functools.partial(_gla_sample_kernel, t=t),
        grid=(batch,),
        in_specs=[pl.BlockSpec((None, t, width), lambda b: (b, 0, 0)),
                  pl.BlockSpec((None, t, LANES), lambda b: (b, 0, 0)),
                  pl.BlockSpec(wall.shape, const2),
                  pl.BlockSpec(mask.shape, lambda b: (0, 0, 0)),
                  pl.BlockSpec(wa2.shape, const2),
                  pl.BlockSpec(ba.shape, const2),
                  pl.BlockSpec(g.shape, const2),
                  pl.BlockSpec((None, GLA_HEADS, GLA_DK, GLA_DV), lambda b: (b, 0, 0, 0))],
        out_specs=[pl.BlockSpec((None, t, GLA_VW), lambda b: (b, 0, 0)),
                   pl.BlockSpec((None, GLA_HEADS, GLA_DK, GLA_DV), lambda b: (b, 0, 0, 0))],
        out_shape=[jax.ShapeDtypeStruct((batch, t, GLA_VW), F32),
                   jax.ShapeDtypeStruct((batch, GLA_HEADS, GLA_DK, GLA_DV), F32)],
        scratch_shapes=[pltpu.VMEM((GLA_CHUNK, width), F32),
                        pltpu.VMEM((GLA_CHUNK, LANES), F32),
                        pltpu.VMEM((GLA_HEADS, GLA_DV, GLA_DK), F32)],
        compiler_params=_cparams(("arbitrary",)),
        name="gla_sample",
    )(pg, small, wall, mask, wa2, ba, g, s0)


FB_LANE = GLA_RANK


def _fox_bias_prompt_kernel(sm_ref, bf_ref, tri_ref, lf_ref, fc_ref, ft_ref, carry_ref):
    @pl.when(pl.program_id(1) == 0)
    def _():
        carry_ref[...] = jnp.zeros(carry_ref.shape, F32)

    lf = _log_sigmoid(sm_ref[...] + bf_ref[...])
    lf_ref[...] = lf[:, FB_LANE:FB_LANE + FOX_HEADS]
    cum = _dot_sel(tri_ref[...], lf) + carry_ref[...]
    carry_ref[...] = cum[cum.shape[0] - 1:, :]
    fc_ref[...] = cum
    ft_ref[...] = cum.T[FB_LANE:FB_LANE + FOX_HEADS, :]


def _fox_bias_prompt(small, bf_row, batch, seq, tb=256):
    nblk = seq // tb
    tri = jnp.asarray(np.tril(np.ones((tb, tb), np.float32)), BF16)
    return pl.pallas_call(
        _fox_bias_prompt_kernel,
        grid=(batch, nblk),
        in_specs=[pl.BlockSpec((tb, LANES), lambda b, n: (b * nblk + n, 0)),
                  pl.BlockSpec((1, LANES), lambda b, n: (0, 0)),
                  pl.BlockSpec((tb, tb), lambda b, n: (0, 0))],
        out_specs=[pl.BlockSpec((tb, FOX_HEADS), lambda b, n: (b * nblk + n, 0)),
                   pl.BlockSpec((tb, LANES), lambda b, n: (b * nblk + n, 0)),
                   pl.BlockSpec((None, FOX_HEADS, tb), lambda b, n: (b, 0, n))],
        out_shape=[jax.ShapeDtypeStruct((batch * seq, FOX_HEADS), F32),
                   jax.ShapeDtypeStruct((batch * seq, LANES), F32),
                   jax.ShapeDtypeStruct((batch, FOX_HEADS, seq), F32)],
        scratch_shapes=[pltpu.VMEM((1, LANES), F32)],
        compiler_params=_cparams(("arbitrary", "arbitrary")),
        name="fox_bias_prompt",
    )(small, bf_row, tri)


def _fox_bias_sample_kernel(sm_ref, bf_ref, sel_ref, lf_ref, fn_ref):
    lf = _log_sigmoid(sm_ref[...] + bf_ref[...])
    lf_ref[...] = lf[:, FB_LANE:FB_LANE + FOX_HEADS]
    cum = _dot_sel(sel_ref[...], lf)
    fn_ref[...] = cum[:, FB_LANE:FB_LANE + FOX_HEADS]


def _fox_bias_sample(small, bf_row, t):
    rows = small.shape[0]
    r = np.arange(rows)
    sel = ((r[:, None] // t) == (r[None, :] // t)) & (r[None, :] <= r[:, None])
    sel = jnp.asarray(sel.astype(np.float32), BF16)
    full = lambda shape: pl.BlockSpec(shape, lambda i: tuple(0 for _ in shape))
    return pl.pallas_call(
        _fox_bias_sample_kernel,
        grid=(1,),
        in_specs=[full((rows, LANES)), full((1, LANES)), full((rows, rows))],
        out_specs=[full((rows, FOX_HEADS)), full((rows, FOX_HEADS))],
        out_shape=[jax.ShapeDtypeStruct((rows, FOX_HEADS), F32),
                   jax.ShapeDtypeStruct((rows, FOX_HEADS), F32)],
        compiler_params=_cparams(("arbitrary",)),
        name="fox_bias_sample",
    )(small, bf_row, sel)


FOX_HEADS_PER_STEP = 4


def _fox_prompt_kernel(q_ref, k_ref, v_ref, fc_ref, ft_ref, o_ref, *, tq, tk):
    hb = FOX_HEADS_PER_STEP
    hg = pl.program_id(1)
    i = pl.program_id(2)
    lane = lax.broadcasted_iota(jnp.int32, (tq, LANES), 1)
    rowi = lax.broadcasted_iota(jnp.int32, (tq, tk), 0)
    coli = lax.broadcasted_iota(jnp.int32, (tq, tk), 1)
    fc = fc_ref[...]
    qs, f_ts = [], []
    for hh in range(hb):
        hs = slice(hh * FOX_DH, (hh + 1) * FOX_DH)
        qs.append((q_ref[:, hs].astype(F32) * (FOX_DH ** -0.5)).astype(BF16))
        f_ts.append(jnp.sum(jnp.where(lane == hg * hb + hh + FB_LANE, fc, 0.0), axis=-1, keepdims=True))

    def step(j, carry, masked):
        ks = pl.ds(pl.multiple_of(j * tk, tk), tk)
        out = []
        for hh in range(hb):
            hs = slice(hh * FOX_DH, (hh + 1) * FOX_DH)
            m, l, acc = carry[hh]
            s = _dot_nt(qs[hh], k_ref[ks, hs])
            s = s + f_ts[hh] - ft_ref[hh:hh + 1, ks]
            if masked:
                s = jnp.where(rowi >= coli, s, NEG)
            m_new = jnp.maximum(m, jnp.max(s, axis=-1, keepdims=True))
            p = jnp.exp(s - m_new)
            alpha = jnp.exp(m - m_new)
            l = alpha * l + jnp.sum(p, axis=-1, keepdims=True)
            acc = alpha * acc + _dot(p.astype(BF16), v_ref[ks, hs])
            out.append((m_new, l, acc))
        return tuple(out)

    init = tuple((jnp.full((tq, 1), NEG, F32), jnp.zeros((tq, 1), F32), jnp.zeros((tq, FOX_DH), F32))
                 for _ in range(hb))
    nfull = i * (tq // tk)
    carry = lax.fori_loop(0, nfull, lambda j, c: step(j, c, False), init)
    carry = step(nfull, carry, True)
    for hh in range(hb):
        m, l, acc = carry[hh]
        o_ref[:, hh * FOX_DH:(hh + 1) * FOX_DH] = (acc / l).astype(o_ref.dtype)


def _fox_prompt(qb, kb, vb, fcol, ft, batch, seq, tq=256):
    nq = seq // tq
    hb = FOX_HEADS_PER_STEP
    wb = hb * FOX_DH
    return pl.pallas_call(
        functools.partial(_fox_prompt_kernel, tq=tq, tk=tq),
        grid=(batch, FOX_HEADS // hb, nq),
        in_specs=[pl.BlockSpec((tq, wb), lambda b, h, i: (b * nq + i, h)),
                  pl.BlockSpec((seq, wb), lambda b, h, i: (b, h)),
                  pl.BlockSpec((seq, wb), lambda b, h, i: (b, h)),
                  pl.BlockSpec((tq, LANES), lambda b, h, i: (b * nq + i, 0)),
                  pl.BlockSpec((None, None, hb, seq), lambda b, h, i: (b, h, 0, 0))],
        out_specs=pl.BlockSpec((tq, wb), lambda b, h, i: (b * nq + i, h)),
        out_shape=jax.ShapeDtypeStruct((batch * seq, FOX_W), BF16),
        compiler_params=_cparams(("arbitrary", "arbitrary", "arbitrary")),
        name="fox_prompt",
    )(qb, kb, vb, fcol, ft.reshape(batch, FOX_HEADS // hb, hb, seq))


PAGES_PER_STEP = 8
ROWS8 = 8
QROWS = FOX_HEADS * ROWS8
PAGE_FLAT = PAGE * FOX_HEADS


def _dot_sel_rhs(x, w01):
    hi, mid, lo = _split3(x)
    m = x.shape[0]
    stacked = jnp.concatenate([hi.astype(F32), mid.astype(F32), lo.astype(F32)], axis=0).astype(BF16)
    r = _dot(stacked, w01)
    return r[0:m] + r[m:2 * m] + r[2 * m:3 * m]


def _fox_sample_kernel(pt_ref, q_ref, kn_ref, vn_ref, fn_ref, fnrow_ref, madd_ref, maddn_ref, usuf_ref, tot_ref,
                       *refs, t):
    g = PAGES_PER_STEP
    k_refs, v_refs, lf_refs = refs[0:g], refs[g:2 * g], refs[2 * g:3 * g]
    o_ref = refs[3 * g]
    q_sc, m_sc, l_sc, acc_sc, carry_sc = refs[3 * g + 1:]
    j = pl.program_id(1)
    fn_t = fn_ref[...][:, 0:1]

    @pl.when(j == 0)
    def _():
        q_sc[...] = jnp.zeros(q_sc.shape, F32)
        for h in range(FOX_HEADS):
            q_sc[h * ROWS8:h * ROWS8 + t, :] = q_ref[:, h * FOX_DH:(h + 1) * FOX_DH] * (FOX_DH ** -0.5)
        carry_sc[...] = jnp.zeros(carry_sc.shape, F32)
        pad = jnp.zeros((LANES - t * FOX_HEADS, FOX_DH), F32)
        kn = jnp.concatenate([kn_ref[...].reshape(t * FOX_HEADS, FOX_DH), pad], axis=0)
        vn = jnp.concatenate([vn_ref[...].reshape(t * FOX_HEADS, FOX_DH), pad], axis=0)
        s = _dot_nt(q_sc[...], kn) + fn_t - fnrow_ref[...] + maddn_ref[...]
        m = jnp.max(s, axis=-1, keepdims=True)
        p = jnp.exp(s - m)
        m_sc[...] = jnp.broadcast_to(m, m_sc.shape)
        l_sc[...] = jnp.broadcast_to(jnp.sum(p, axis=-1, keepdims=True), l_sc.shape)
        acc_sc[...] = _dot(p, vn)

    lf = jnp.concatenate([lf_refs[gi][...] for gi in range(g)], axis=0)
    r_in = _dot_sel_rhs(lf, usuf_ref[...])
    page_tot = _dot_sel_rhs((r_in + lf)[:, 0:LANES], tot_ref[...])
    carry = carry_sc[...]
    q = q_sc[...]
    madd = madd_ref[...]
    m_old = m_sc[...]
    m_new = m_old
    s_list = []
    for gi in range(g):
        bias = madd + (r_in[gi:gi + 1, :] + carry)
        carry = carry + page_tot[gi:gi + 1, :]
        s = _dot_nt(q, k_refs[gi][...].reshape(PAGE_FLAT, FOX_DH)) + fn_t + bias
        s_list.append(s)
        m_new = jnp.maximum(m_new, jnp.max(s, axis=-1, keepdims=True))
    carry_sc[...] = carry
    alpha = jnp.exp(m_old - m_new)
    l = alpha * l_sc[...]
    acc = alpha * acc_sc[...]
    m_col = m_new[:, 0:1]
    for gi in range(g):
        p = jnp.exp(s_list[gi] - m_col)
        l = l + jnp.sum(p, axis=-1, keepdims=True)
        acc = acc + _dot(p, v_refs[gi][...].reshape(PAGE_FLAT, FOX_DH))
    m_sc[...] = m_new
    l_sc[...] = l
    acc_sc[...] = acc

    @pl.when(j == pl.num_programs(1) - 1)
    def _():
        o = acc / l
        for h in range(FOX_HEADS):
            o_ref[:, h * FOX_DH:(h + 1) * FOX_DH] = o[h * ROWS8:h * ROWS8 + t, :].astype(o_ref.dtype)


def _fox_sample_tables(t):
    row = np.arange(QROWS)[:, None]
    col = np.arange(PAGE_FLAT)[None, :]
    madd = np.where((row // ROWS8) == (col % FOX_HEADS), 0.0, NEG).astype(np.float32)
    coln = np.arange(LANES)[None, :]
    ok = (coln < t * FOX_HEADS) & ((row // ROWS8) == (coln % FOX_HEADS)) & ((coln // FOX_HEADS) <= (row % ROWS8))
    maddn = np.where(ok, 0.0, NEG).astype(np.float32)
    src = np.arange(PAGE_FLAT)[:, None]
    same_head = (src % FOX_HEADS) == (col % FOX_HEADS)
    usuf = (same_head & ((src // FOX_HEADS) > (col // FOX_HEADS))).astype(np.float32)
    lane = np.arange(LANES)[:, None]
    tot = ((lane < FOX_HEADS) & (lane == (col % FOX_HEADS))).astype(np.float32)
    return (jnp.asarray(madd), jnp.asarray(maddn), jnp.asarray(usuf, BF16), jnp.asarray(tot, BF16))


def _fox_sample(page_table, q, kn, vn, fn, cache_k, cache_v, cache_lf, t):
    batch, npages = page_table.shape
    g = PAGES_PER_STEP
    nsteps = npages // g
    madd, maddn, usuf, tot = _fox_sample_tables(t)
    fn_rows = jnp.pad(fn.transpose(0, 2, 1), ((0, 0), (0, 0), (0, ROWS8 - t))).reshape(batch, QROWS, 1)
    fn_rows = jnp.broadcast_to(fn_rows, (batch, QROWS, LANES))
    fn_cols = jnp.pad(fn.reshape(batch, 1, t * FOX_HEADS), ((0, 0), (0, 0), (0, LANES - t * FOX_HEADS)))

    def page_map5(gi):
        return lambda b, j, pt: (0, pt[b, npages - 1 - (j * g + gi)], 0, 0, 0)

    def page_map3(gi):
        return lambda b, j, pt: (pt[b, npages - 1 - (j * g + gi)], 0, 0)

    seq3 = lambda b, j, pt: (b, 0, 0)
    seq4 = lambda b, j, pt: (b, 0, 0, 0)
    const2 = lambda b, j, pt: (0, 0)
    in_specs = [pl.BlockSpec((None, t, FOX_W), seq3),
                pl.BlockSpec((None, t, FOX_HEADS, FOX_DH), seq4),
                pl.BlockSpec((None, t, FOX_HEADS, FOX_DH), seq4),
                pl.BlockSpec((None, QROWS, LANES), seq3),
                pl.BlockSpec((None, 1, LANES), seq3),
                pl.BlockSpec(madd.shape, const2),
                pl.BlockSpec(maddn.shape, const2),
                pl.BlockSpec(usuf.shape, const2),
                pl.BlockSpec(tot.shape, const2)]
    in_specs += [pl.BlockSpec((None, None, PAGE, FOX_HEADS, FOX_DH), page_map5(gi)) for gi in range(g)]
    in_specs += [pl.BlockSpec((None, None, PAGE, FOX_HEADS, FOX_DH), page_map5(gi)) for gi in range(g)]
    in_specs += [pl.BlockSpec((None, 1, PAGE_FLAT), page_map3(gi)) for gi in range(g)]
    grid_spec = pltpu.PrefetchScalarGridSpec(
        num_scalar_prefetch=1,
        grid=(batch, nsteps),
        in_specs=in_specs,
        out_specs=pl.BlockSpec((None, t, FOX_W), seq3),
        scratch_shapes=[pltpu.VMEM((QROWS, FOX_DH), F32),
                        pltpu.VMEM((QROWS, LANES), F32),
                        pltpu.VMEM((QROWS, LANES), F32),
                        pltpu.VMEM((QROWS, FOX_DH), F32),
                        pltpu.VMEM((1, PAGE_FLAT), F32)],
    )
    return pl.pallas_call(
        functools.partial(_fox_sample_kernel, t=t),
        grid_spec=grid_spec,
        out_shape=jax.ShapeDtypeStruct((batch, t, FOX_W), F32),
        compiler_params=_cparams(("arbitrary", "arbitrary")),
        name="fox_sample",
    )(page_table, q, kn, vn, fn_rows, fn_cols, madd, maddn, usuf, tot,
      *([cache_k] * g), *([cache_v] * g), *([cache_lf] * g))


def _merge_kernel(oa_ref, ob_ref, wa_ref, wb_ref, ga_ref, gb_ref, o_ref):
    ua = _dot(oa_ref[...], wa_ref[...])
    ub = _dot(ob_ref[...], wb_ref[...])
    o_ref[...] = (jax.nn.sigmoid(ga_ref[...]) * ua + jax.nn.sigmoid(gb_ref[...]) * ub).astype(o_ref.dtype)


def _merge(oa, ob, wa, wb, gates, tm, tn=1024):
    m = oa.shape[0]
    d = wa.shape[1]
    nj = d // tn
    return pl.pallas_call(
        _merge_kernel,
        grid=(nj, m // tm),
        in_specs=[pl.BlockSpec((tm, GLA_VW), lambda j, i: (i, 0)),
                  pl.BlockSpec((tm, FOX_W), lambda j, i: (i, 0)),
                  pl.BlockSpec((GLA_VW, tn), lambda j, i: (0, j)),
                  pl.BlockSpec((FOX_W, tn), lambda j, i: (0, j)),
                  pl.BlockSpec((tm, tn), lambda j, i: (i, j)),
                  pl.BlockSpec((tm, tn), lambda j, i: (i, nj + j))],
        out_specs=pl.BlockSpec((tm, tn), lambda j, i: (i, j)),
        out_shape=jax.ShapeDtypeStruct((m, d), BF16),
        compiler_params=_cparams(("arbitrary", "arbitrary")),
        name="merge",
    )(oa, ob, wa, wb, gates, gates)


ROUTER_GROUP_LANE = N_EXPERTS


def _route(logits):
    lane_i = lax.broadcasted_iota(jnp.int32, logits.shape, 1)
    lane = lane_i.astype(F32)
    grp_of_lane = (lane_i >> 2).astype(F32)
    big = float(LANES)
    is_grp = (lane_i >= ROUTER_GROUP_LANE) & (lane_i < ROUTER_GROUP_LANE + N_GROUPS)
    gl = jnp.where(is_grp, logits, NEG)
    gmax = jnp.max(gl, axis=-1, keepdims=True)
    g_idx = jnp.min(jnp.where(is_grp & (gl == gmax), lane - ROUTER_GROUP_LANE, big), axis=-1, keepdims=True)
    g_w = 1.0 / jnp.sum(jnp.where(is_grp, jnp.exp(gl - gmax), 0.0), axis=-1, keepdims=True)
    in_grp = (lane_i < N_EXPERTS) & (grp_of_lane == g_idx)
    e1 = jnp.where(in_grp, logits, NEG)
    v1 = jnp.max(e1, axis=-1, keepdims=True)
    i1 = jnp.min(jnp.where(in_grp & (e1 == v1), lane, big), axis=-1, keepdims=True)
    rest = in_grp & (lane != i1)
    e2 = jnp.where(rest, logits, NEG)
    v2 = jnp.max(e2, axis=-1, keepdims=True)
    i2 = jnp.min(jnp.where(rest & (e2 == v2), lane, big), axis=-1, keepdims=True)
    r = jnp.exp(v2 - v1)
    w1 = g_w / (1.0 + r)
    w2 = g_w * r / (1.0 + r)
    return jnp.where(lane == i1, w1, 0.0) + jnp.where(lane == i2, w2, 0.0)


def _outproj_kernel(mg_ref, w_ref, x_ref, gt_ref, g2_ref, sh_ref, sc_ref, wr_ref, br_ref,
                    x1_ref, h2_ref, cmb_ref):
    x1 = x_ref[...] + gt_ref[...] * _dot(mg_ref[...], w_ref[...])
    x1_ref[...] = x1
    h2 = _rms(x1) * g2_ref[...] * (1.0 + sc_ref[...]) + sh_ref[...]
    h2_ref[...] = h2.astype(h2_ref.dtype)
    cmb_ref[...] = _route(_dot3(h2, wr_ref[...]) + br_ref[...])


def _outproj(merged, w_out, x, mod, g2, w_router, b_router, tm, tiles_per_batch):
    m, d = x.shape
    rb = mod.shape[2]
    const2 = lambda i: (0, 0)
    return pl.pallas_call(
        _outproj_kernel,
        grid=(m // tm,),
        in_specs=[pl.BlockSpec((tm, d), lambda i: (i, 0)),
                  pl.BlockSpec((d, d), const2),
                  pl.BlockSpec((tm, d), lambda i: (i, 0)),
                  _mod_spec(2, rb, d, tiles_per_batch),
                  pl.BlockSpec((1, d), const2),
                  _mod_spec(3, rb, d, tiles_per_batch),
                  _mod_spec(4, rb, d, tiles_per_batch),
                  pl.BlockSpec((d, LANES), const2),
                  pl.BlockSpec((1, LANES), const2)],
        out_specs=[pl.BlockSpec((tm, d), lambda i: (i, 0)),
                   pl.BlockSpec((tm, d), lambda i: (i, 0)),
                   pl.BlockSpec((tm, LANES), lambda i: (i, 0))],
        out_shape=[jax.ShapeDtypeStruct((m, d), F32),
                   jax.ShapeDtypeStruct((m, d), BF16),
                   jax.ShapeDtypeStruct((m, LANES), F32)],
        compiler_params=_cparams(("arbitrary",)),
        name="outproj",
    )(merged, w_out, x, mod, g2, mod, mod, w_router, b_router)


def _moe_kernel(h_ref, cmb_ref, wg_ref, wu_ref, wd_ref, x1_ref, gt_ref, gf_ref, y_ref, acc_ref):
    e = pl.program_id(1)

    @pl.when(e == 0)
    def _():
        acc_ref[...] = jnp.zeros(acc_ref.shape, F32)

    h = h_ref[...]
    a = _dot(h, wg_ref[...])
    u = _dot(h, wu_ref[...])
    lane = lax.broadcasted_iota(jnp.int32, cmb_ref.shape, 1)
    cw = jnp.sum(jnp.where(lane == e, cmb_ref[...], 0.0), axis=-1, keepdims=True)
    hid = (_silu(a) * u * cw).astype(BF16)
    acc_ref[...] += _dot(hid, wd_ref[...])

    @pl.when(e == pl.num_programs(1) - 1)
    def _():
        x2 = x1_ref[...] + gt_ref[...] * acc_ref[...]
        y_ref[...] = _rms(x2) * gf_ref[...]


def _moe(h2, cmb, wg, wu, wd, x1, mod, g_final, tm, tiles_per_batch):
    m, d = x1.shape
    rb = mod.shape[2]
    ne, _, de = wg.shape
    return pl.pallas_call(
        _moe_kernel,
        grid=(m // tm, ne),
        in_specs=[pl.BlockSpec((tm, d), lambda i, e: (i, 0)),
                  pl.BlockSpec((tm, LANES), lambda i, e: (i, 0)),
                  pl.BlockSpec((None, d, de), lambda i, e: (e, 0, 0)),
                  pl.BlockSpec((None, d, de), lambda i, e: (e, 0, 0)),
                  pl.BlockSpec((None, de, d), lambda i, e: (e, 0, 0)),
                  pl.BlockSpec((tm, d), lambda i, e: (i, 0)),
                  pl.BlockSpec((None, None, rb, d), lambda i, e: (5, i // tiles_per_batch, 0, 0)),
                  pl.BlockSpec((1, d), lambda i, e: (0, 0))],
        out_specs=pl.BlockSpec((tm, d), lambda i, e: (i, 0)),
        out_shape=jax.ShapeDtypeStruct((m, d), F32),
        scratch_shapes=[pltpu.VMEM((tm, d), F32)],
        compiler_params=_cparams(("arbitrary", "arbitrary")),
        name="moe",
    )(h2, cmb, wg, wu, wd, x1, mod, g_final)


def _prep_weights(w_ada, b_ada, g_norm1, g_norm2, g_final, w_in, w_a2, b_a, b_f, g_gla_norm, w_up_a, w_up_b, w_out,
                  w_grp, b_grp, w_exp, b_exp, w_gate_e, w_up_e, w_down_e):
    w = w_in[0]
    d = D_MODEL
    o_lra = 2 * GLA_KW + 2 * GLA_VW
    o_fox = o_lra + GLA_RANK
    o_fb = o_fox + 3 * FOX_W
    o_g = o_fb + FOX_HEADS
    pad = jnp.zeros((d, LANES - GLA_RANK - FOX_HEADS), F32)
    wall, mask = _gla_tables()
    bf_row = jnp.zeros((1, LANES), F32).at[0, FB_LANE:FB_LANE + FOX_HEADS].set(b_f[0])
    w_router = jnp.concatenate([w_exp[0], w_grp[0], jnp.zeros((d, LANES - N_EXPERTS - N_GROUPS), F32)], axis=1)
    b_router = jnp.concatenate([b_exp[0], b_grp[0], jnp.zeros((LANES - N_EXPERTS - N_GROUPS,), F32)])[None, :]
    return dict(
        w_ada=w_ada[0], b_ada=b_ada[0][None, :],
        g1=g_norm1[0][None, :], g2=g_norm2[0][None, :], gf=g_final[None, :],
        w_gla=w[:, 0:o_lra].astype(BF16),
        w_q=w[:, o_fox:o_fox + FOX_W].astype(BF16),
        w_k=w[:, o_fox + FOX_W:o_fox + 2 * FOX_W].astype(BF16),
        w_v=w[:, o_fox + 2 * FOX_W:o_fox + 3 * FOX_W].astype(BF16),
        w_gates=w[:, o_g:o_g + 2 * d].astype(BF16),
        w_small=jnp.concatenate([w[:, o_lra:o_lra + GLA_RANK], w[:, o_fb:o_fb + FOX_HEADS], pad], axis=1).astype(BF16),
        wall=jnp.asarray(wall, BF16), mask=jnp.asarray(mask, F32),
        w_a2=jnp.concatenate([w_a2[0], jnp.zeros((LANES - GLA_RANK, GLA_KW), F32)], axis=0), b_a=b_a[0][None, :], bf_row=bf_row, g_gla=g_gla_norm[0][None, :],
        w_up_a=w_up_a[0].astype(BF16), w_up_b=w_up_b[0].astype(BF16), w_out=w_out[0].astype(BF16),
        w_router=w_router, b_router=b_router,
        wg=w_gate_e[0].astype(BF16), wu=w_up_e[0].astype(BF16), wd=w_down_e[0].astype(BF16),
    )


def _project(h, p, tm):
    tn = 1024
    (pg,) = _mm(h, p["w_gla"], tm, tn)
    (qb,) = _mm(h, p["w_q"], tm, tn, (BF16,))
    kb, kb16 = _mm(h, p["w_k"], tm, tn, (F32, BF16))
    vb, vb16 = _mm(h, p["w_v"], tm, tn, (F32, BF16))
    (gates,) = _mm(h, p["w_gates"], tm, tn)
    (small,) = _mm(h, p["w_small"], tm, LANES)
    return pg, qb, kb, kb16, vb, vb16, gates, small


def _tail(x, oa, ob, gates, mod, p, tm, tiles_per_batch):
    merged = _merge(oa, ob, p["w_up_a"], p["w_up_b"], gates, tm)
    x1, h2, cmb = _outproj(merged, p["w_out"], x, mod, p["g2"], p["w_router"], p["b_router"],
                           min(tm, 256), tiles_per_batch * (tm // min(tm, 256)))
    tmm = min(tm, 512)
    return _moe(h2, cmb, p["wg"], p["wu"], p["wd"], x1, mod, p["gf"], tmm, tiles_per_batch * (tm // tmm))


def kernel(x_prompt, x_sample, cache_k, cache_v, cache_logf, state_gla, page_table, c_prompt, c_sample, w_ada, b_ada,
           g_norm1, g_norm2, g_final, w_in, w_a2, b_a, b_f, g_gla_norm, w_up_a, w_up_b, w_out, w_grp, b_grp, w_exp,
           b_exp, w_gate_e, w_up_e, w_down_e):
    p = _prep_weights(w_ada, b_ada, g_norm1, g_norm2, g_final, w_in, w_a2, b_a, b_f, g_gla_norm, w_up_a, w_up_b,
                      w_out, w_grp, b_grp, w_exp, b_exp, w_gate_e, w_up_e, w_down_e)
    bp, seq, d = x_prompt.shape
    bs, t, _ = x_sample.shape

    mod = _adaln(jnp.concatenate([c_prompt, c_sample], axis=0), p["w_ada"], p["b_ada"])
    mod_p = mod[:bp].reshape(bp, 6, 1, d).transpose(1, 0, 2, 3)
    mod_s = jnp.repeat(mod[bp:].reshape(bs, 6, d), t, axis=0).transpose(1, 0, 2)[:, None]

    tm = 1024
    tpb = seq // tm
    xp = x_prompt.reshape(bp * seq, d)
    hp = _normmod(xp, p["g1"], mod_p, tm, tpb)
    pg, qb, kb, kb16, vb, vb16, gates, small = _project(hp, p, tm)
    s0 = jnp.zeros((bp, GLA_HEADS, GLA_DK, GLA_DV), F32)
    oa, s_p = _gla_prompt(pg, small, s0, p["wall"], p["mask"], p["w_a2"], p["b_a"], p["g_gla"], bp, seq)
    lf_p, fcol, ft = _fox_bias_prompt(small, p["bf_row"], bp, seq)
    ob = _fox_prompt(qb, kb16, vb16, fcol, ft, bp, seq)
    y_p = _tail(xp, oa, ob, gates, mod_p, p, tm, tpb)

    rows = bs * t
    xs = x_sample.reshape(rows, d)
    hs = _normmod(xs, p["g1"], mod_s, rows, 1)
    pg_s, qs, ks, _, vs, _, gates_s, small_s = _project(hs, p, rows)
    oa_s, s_s = _gla_sample(pg_s.reshape(bs, t, -1), small_s.reshape(bs, t, LANES), state_gla[0], p["wall"], p["mask"],
                            p["w_a2"], p["b_a"], p["g_gla"], bs, t)
    lf_s, fn_s = _fox_bias_sample(small_s, p["bf_row"], t)
    n_pool = cache_k.shape[1]
    ob_s = _fox_sample(page_table, qs.astype(F32).reshape(bs, t, FOX_W), ks.reshape(bs, t, FOX_HEADS, FOX_DH),
                       vs.reshape(bs, t, FOX_HEADS, FOX_DH), fn_s.reshape(bs, t, FOX_HEADS),
                       cache_k, cache_v, cache_logf[0].reshape(n_pool, 1, PAGE_FLAT), t)
    y_s = _tail(xs, oa_s.reshape(rows, GLA_VW).astype(BF16), ob_s.reshape(rows, FOX_W).astype(BF16), gates_s, mod_s,
                p, rows, 1)

    return (y_p.reshape(bp, seq, d), y_s.reshape(bs, t, d),
            kb.reshape(1, bp, seq, FOX_HEADS, FOX_DH), vb.reshape(1, bp, seq, FOX_HEADS, FOX_DH),
            lf_p.reshape(1, bp, seq, FOX_HEADS), s_p[None],
            ks.reshape(1, bs, t, FOX_HEADS, FOX_DH), vs.reshape(1, bs, t, FOX_HEADS, FOX_DH),
            lf_s.reshape(1, bs, t, FOX_HEADS), s_s[None])
```

```python
import functools

import numpy as np
import jax
import jax.numpy as jnp
from jax import lax
from jax.experimental import pallas as pl
from jax.experimental.pallas import tpu as pltpu

F32 = jnp.float32
BF16 = jnp.bfloat16

D_MODEL = 2048
GLA_HEADS = 4
GLA_DK = 128
GLA_DV = 256
GLA_RANK = 16
GLA_TAU = 16.0
FOX_HEADS = 8
FOX_DH = 128
PAGE = 128
N_GROUPS = 4
EXP_PER_GROUP = 4
N_EXPERTS = 16
D_EXPERT = 512
RMS_EPS = 1e-6
GLA_KW = GLA_HEADS * GLA_DK
GLA_VW = GLA_HEADS * GLA_DV
FOX_W = FOX_HEADS * FOX_DH
GLA_CHUNK = 128
GLA_LEVELS = 7
LANES = 128
NEG = -1e30
VMEM_LIMIT = 56 * 1024 * 1024


def _cparams(sem):
    return pltpu.CompilerParams(dimension_semantics=sem, vmem_limit_bytes=VMEM_LIMIT)


def _dot(a, b):
    return jnp.dot(a, b, preferred_element_type=F32)


def _dot_nt(a, b):
    return lax.dot_general(a, b, (((1,), (1,)), ((), ())), preferred_element_type=F32)


def _dot_tn(a, b):
    return lax.dot_general(a, b, (((0,), (0,)), ((), ())), preferred_element_type=F32)


def _split2(x):
    hi = x.astype(BF16)
    lo = (x - hi.astype(F32)).astype(BF16)
    return hi, lo


def _split3(x):
    hi = x.astype(BF16)
    r = x - hi.astype(F32)
    mid = r.astype(BF16)
    lo = (r - mid.astype(F32)).astype(BF16)
    return hi, mid, lo


def _dot3(a, b):
    ah, al = _split2(a)
    bh, bl = _split2(b)
    return _dot(ah, bh) + _dot(ah, bl) + _dot(al, bh)


def _dot_sel(w01, x):
    hi, mid, lo = _split3(x)
    return _dot(w01, hi) + _dot(w01, mid) + _dot(w01, lo)


def _log_sigmoid(x):
    return jnp.minimum(x, 0.0) - jnp.log1p(jnp.exp(-jnp.abs(x)))


def _silu(x):
    return x * jax.nn.sigmoid(x)


def _adaln_kernel(c_ref, w_ref, b_ref, o_ref):
    o_ref[...] = _dot3(_silu(c_ref[...]), w_ref[...]) + b_ref[...]


def _adaln(c, w, b, tn=512):
    nb, d = c.shape
    n = w.shape[1]
    return pl.pallas_call(
        _adaln_kernel,
        grid=(n // tn,),
        in_specs=[pl.BlockSpec((nb, d), lambda j: (0, 0)),
                  pl.BlockSpec((d, tn), lambda j: (0, j)),
                  pl.BlockSpec((1, tn), lambda j: (0, j))],
        out_specs=pl.BlockSpec((nb, tn), lambda j: (0, j)),
        out_shape=jax.ShapeDtypeStruct((nb, n), F32),
        compiler_params=_cparams(("arbitrary",)),
        name="adaln",
    )(c, w, b)


def _rms(x):
    return x * lax.rsqrt(jnp.mean(x * x, axis=-1, keepdims=True) + RMS_EPS)


def _normmod_kernel(x_ref, g_ref, sh_ref, sc_ref, o_ref):
    y = _rms(x_ref[...]) * g_ref[...]
    o_ref[...] = (y * (1.0 + sc_ref[...]) + sh_ref[...]).astype(o_ref.dtype)


def _mod_spec(k, rb, d, tiles_per_batch):
    return pl.BlockSpec((None, None, rb, d), lambda i: (k, i // tiles_per_batch, 0, 0))


def _normmod(x, g, mod, tm, tiles_per_batch):
    m, d = x.shape
    rb = mod.shape[2]
    return pl.pallas_call(
        _normmod_kernel,
        grid=(m // tm,),
        in_specs=[pl.BlockSpec((tm, d), lambda i: (i, 0)),
                  pl.BlockSpec((1, d), lambda i: (0, 0)),
                  _mod_spec(0, rb, d, tiles_per_batch),
                  _mod_spec(1, rb, d, tiles_per_batch)],
        out_specs=pl.BlockSpec((tm, d), lambda i: (i, 0)),
        out_shape=jax.ShapeDtypeStruct((m, d), BF16),
        compiler_params=_cparams(("arbitrary",)),
        name="normmod",
    )(x, g, mod, mod)


def _mm_kernel(x_ref, w_ref, *o_refs):
    r = _dot(x_ref[...], w_ref[...])
    for o_ref in o_refs:
        o_ref[...] = r.astype(o_ref.dtype)


def _mm(x, w, tm, tn, out_dtypes=(F32,)):
    m, k = x.shape
    n = w.shape[1]
    outs = pl.pallas_call(
        _mm_kernel,
        grid=(n // tn, m // tm),
        in_specs=[pl.BlockSpec((tm, k), lambda j, i: (i, 0)),
                  pl.BlockSpec((k, tn), lambda j, i: (0, j))],
        out_specs=[pl.BlockSpec((tm, tn), lambda j, i: (i, j)) for _ in out_dtypes],
        out_shape=[jax.ShapeDtypeStruct((m, n), dt) for dt in out_dtypes],
        compiler_params=_cparams(("arbitrary", "arbitrary")),
        name="proj_mm",
    )(x, w)
    return outs


def _gla_tables():
    c, p = GLA_CHUNK, GLA_LEVELS
    t = np.arange(c)[:, None]
    m = np.arange(c)[None, :]
    wall = np.zeros((p + 2, c, c), np.float32)
    mask = np.zeros((p + 1, c, c), np.float32)
    for l in range(p):
        half = 1 << l
        pos = t % (2 * half)
        mid = t - pos + half
        right = pos >= half
        wall[l] = np.where(right, (m >= mid) & (m <= t), (m > t) & (m < mid))
        s = m
        mask[l] = ((t >> (l + 1)) == (s >> (l + 1))) & (((t >> l) & 1) == 1) & (((s >> l) & 1) == 0)
    wall[p] = m <= t
    wall[p + 1] = m > t
    mask[p] = t == m
    return wall.reshape((p + 2) * c, c), mask


def _gla_body(q_ref, k_ref, v_ref, ra_ref, sm_ref, wall_ref, mask_ref, wa2_ref, ba_ref, g_ref,
              og_ref, st_ref, *, nchunk, valid, store_rows):
    c, p = GLA_CHUNK, GLA_LEVELS
    row = lax.broadcasted_iota(jnp.int32, (c, GLA_DK), 0)
    wall = wall_ref[...]
    for ci in range(nchunk):
        rows = pl.ds(ci * c, c)
        x = _dot3(sm_ref[rows, :], wa2_ref[...]) + ba_ref[...]
        la = _log_sigmoid(x) * (1.0 / GLA_TAU)
        if valid < c:
            rowh = lax.broadcasted_iota(jnp.int32, la.shape, 0)
            la = jnp.where(rowh < valid, la, 0.0)
        e_all = jnp.exp(_dot_sel(wall, la))
        for h in range(GLA_HEADS):
            ks = slice(h * GLA_DK, (h + 1) * GLA_DK)
            vs = slice(h * GLA_DV, (h + 1) * GLA_DV)
            q = q_ref[rows, ks] * (GLA_DK ** -0.5)
            k = k_ref[rows, ks]
            vb = v_ref[rows, vs].astype(BF16)
            a = mask_ref[p] * _dot_nt(q.astype(BF16), k.astype(BF16))
            for l in range(p):
                el = e_all[l * c:(l + 1) * c, ks]
                xl = (jnp.where(((row >> l) & 1) == 1, q, k) * el).astype(BF16)
                a = a + mask_ref[l] * _dot_nt(xl, xl)
            st = st_ref[h]
            qc = (q * e_all[p * c:(p + 1) * c, ks]).astype(BF16)
            o = _dot(a.astype(BF16), vb) + _dot_nt(qc, st.astype(BF16))
            kr = (k * e_all[(p + 1) * c:(p + 2) * c, ks]).astype(BF16)
            dec = e_all[(p + 1) * c - 1:(p + 1) * c, ks]
            st_ref[h] = dec * st + _dot_tn(vb, kr)
            og = _rms(o) * g_ref[...] * _silu(ra_ref[rows, vs])
            if store_rows < c:
                og_ref[:, vs] = og[:store_rows].astype(og_ref.dtype)
            else:
                og_ref[rows, vs] = og.astype(og_ref.dtype)


def _gla_prompt_kernel(q_ref, k_ref, v_ref, ra_ref, sm_ref, wall_ref, mask_ref, wa2_ref, ba_ref, g_ref, s0_ref,
                       og_ref, s_ref, st_ref, *, nchunk):
    n = pl.program_id(1)

    @pl.when(n == 0)
    def _():
        for h in range(GLA_HEADS):
            st_ref[h] = s0_ref[h].T

    _gla_body(q_ref, k_ref, v_ref, ra_ref, sm_ref, wall_ref, mask_ref, wa2_ref, ba_ref, g_ref, og_ref, st_ref,
              nchunk=nchunk, valid=GLA_CHUNK, store_rows=GLA_CHUNK)

    @pl.when(n == pl.num_programs(1) - 1)
    def _():
        for h in range(GLA_HEADS):
            s_ref[h] = st_ref[h].T


def _gla_prompt(pg, small, s0, wall, mask, wa2, ba, g, batch, seq, tb=256):
    nblk = seq // tb
    rowmap = lambda cb: (lambda b, n: (b * nblk + n, cb))
    const2 = lambda b, n: (0, 0)
    return pl.pallas_call(
        functools.partial(_gla_prompt_kernel, nchunk=tb // GLA_CHUNK),
        grid=(batch, nblk),
        in_specs=[pl.BlockSpec((tb, GLA_KW), rowmap(0)),
                  pl.BlockSpec((tb, GLA_KW), rowmap(1)),
                  pl.BlockSpec((tb, GLA_VW), rowmap(1)),
                  pl.BlockSpec((tb, GLA_VW), rowmap(2)),
                  pl.BlockSpec((tb, LANES), rowmap(0)),
                  pl.BlockSpec(wall.shape, const2),
                  pl.BlockSpec(mask.shape, lambda b, n: (0, 0, 0)),
                  pl.BlockSpec(wa2.shape, const2),
                  pl.BlockSpec(ba.shape, const2),
                  pl.BlockSpec(g.shape, const2),
                  pl.BlockSpec((None, GLA_HEADS, GLA_DK, GLA_DV), lambda b, n: (b, 0, 0, 0))],
        out_specs=[pl.BlockSpec((tb, GLA_VW), rowmap(0)),
                   pl.BlockSpec((None, GLA_HEADS, GLA_DK, GLA_DV), lambda b, n: (b, 0, 0, 0))],
        out_shape=[jax.ShapeDtypeStruct((batch * seq, GLA_VW), BF16),
                   jax.ShapeDtypeStruct((batch, GLA_HEADS, GLA_DK, GLA_DV), F32)],
        scratch_shapes=[pltpu.VMEM((GLA_HEADS, GLA_DV, GLA_DK), F32)],
        compiler_params=_cparams(("arbitrary", "arbitrary")),
        name="gla_prompt",
    )(pg, pg, pg, pg, small, wall, mask, wa2, ba, g, s0)


def _gla_sample_kernel(pg_ref, sm_ref, wall_ref, mask_ref, wa2_ref, ba_ref, g_ref, s0_ref,
                       og_ref, s_ref, pad_ref, smpad_ref, st_ref, *, t):
    @pl.when(pl.program_id(0) == 0)
    def _():
        pad_ref[...] = jnp.zeros(pad_ref.shape, F32)
        smpad_ref[...] = jnp.zeros(smpad_ref.shape, F32)

    pad_ref[0:t, :] = pg_ref[...]
    smpad_ref[0:t, :] = sm_ref[...]
    for h in range(GLA_HEADS):
        st_ref[h] = s0_ref[h].T
    q_ref = pad_ref.at[:, 0:GLA_KW]
    k_ref = pad_ref.at[:, GLA_KW:2 * GLA_KW]
    v_ref = pad_ref.at[:, 2 * GLA_KW:2 * GLA_KW + GLA_VW]
    ra_ref = pad_ref.at[:, 2 * GLA_KW + GLA_VW:2 * GLA_KW + 2 * GLA_VW]
    _gla_body(q_ref, k_ref, v_ref, ra_ref, smpad_ref, wall_ref, mask_ref, wa2_ref, ba_ref, g_ref, og_ref, st_ref,
              nchunk=1, valid=t, store_rows=t)
    for h in range(GLA_HEADS):
        s_ref[h] = st_ref[h].T


def _gla_sample(pg, small, s0, wall, mask, wa2, ba, g, batch, t):
    width = pg.shape[-1]
    const2 = lambda b: (0, 0)
    return pl.pallas_call(
        functools.partial(_gla_sample_kernel, t=t),
        grid=(batch,),
        in_specs=[pl.BlockSpec((None, t, width), lambda b: (b, 0, 0)),
                  pl.BlockSpec((None, t, LANES), lambda b: (b, 0, 0)),
                  pl.BlockSpec(wall.shape, const2),
                  pl.BlockSpec(mask.shape, lambda b: (0, 0, 0)),
                  pl.BlockSpec(wa2.shape, const2),
                  pl.BlockSpec(ba.shape, const2),
                  pl.BlockSpec(g.shape, const2),
                  pl.BlockSpec((None, GLA_HEADS, GLA_DK, GLA_DV), lambda b: (b, 0, 0, 0))],
        out_specs=[pl.BlockSpec((None, t, GLA_VW), lambda b: (b, 0, 0)),
                   pl.BlockSpec((None, GLA_HEADS, GLA_DK, GLA_DV), lambda b: (b, 0, 0, 0))],
        out_shape=[jax.ShapeDtypeStruct((batch, t, GLA_VW), F32),
                   jax.ShapeDtypeStruct((batch, GLA_HEADS, GLA_DK, GLA_DV), F32)],
        scratch_shapes=[pltpu.VMEM((GLA_CHUNK, width), F32),
                        pltpu.VMEM((GLA_CHUNK, LANES), F32),
                        pltpu.VMEM((GLA_HEADS, GLA_DV, GLA_DK), F32)],
        compiler_params=_cparams(("arbitrary",)),
        name="gla_sample",
    )(pg, small, wall, mask, wa2, ba, g, s0)


FB_LANE = GLA_RANK


def _fox_bias_prompt_kernel(sm_ref, bf_ref, tri_ref, lf_ref, fc_ref, ft_ref, carry_ref):
    @pl.when(pl.program_id(1) == 0)
    def _():
        carry_ref[...] = jnp.zeros(carry_ref.shape, F32)

    lf = _log_sigmoid(sm_ref[...] + bf_ref[...])
    lf_ref[...] = lf[:, FB_LANE:FB_LANE + FOX_HEADS]
    cum = _dot_sel(tri_ref[...], lf) + carry_ref[...]
    carry_ref[...] = cum[cum.shape[0] - 1:, :]
    fc_ref[...] = cum
    ft_ref[...] = cum.T[FB_LANE:FB_LANE + FOX_HEADS, :]


def _fox_bias_prompt(small, bf_row, batch, seq, tb=256):
    nblk = seq // tb
    tri = jnp.asarray(np.tril(np.ones((tb, tb), np.float32)), BF16)
    return pl.pallas_call(
        _fox_bias_prompt_kernel,
        grid=(batch, nblk),
        in_specs=[pl.BlockSpec((tb, LANES), lambda b, n: (b * nblk + n, 0)),
                  pl.BlockSpec((1, LANES), lambda b, n: (0, 0)),
                  pl.BlockSpec((tb, tb), lambda b, n: (0, 0))],
        out_specs=[pl.BlockSpec((tb, FOX_HEADS), lambda b, n: (b * nblk + n, 0)),
                   pl.BlockSpec((tb, LANES), lambda b, n: (b * nblk + n, 0)),
                   pl.BlockSpec((None, FOX_HEADS, tb), lambda b, n: (b, 0, n))],
        out_shape=[jax.ShapeDtypeStruct((batch * seq, FOX_HEADS), F32),
                   jax.ShapeDtypeStruct((batch * seq, LANES), F32),
                   jax.ShapeDtypeStruct((batch, FOX_HEADS, seq), F32)],
        scratch_shapes=[pltpu.VMEM((1, LANES), F32)],
        compiler_params=_cparams(("arbitrary", "arbitrary")),
        name="fox_bias_prompt",
    )(small, bf_row, tri)


def _fox_bias_sample_kernel(sm_ref, bf_ref, sel_ref, lf_ref, fn_ref):
    lf = _log_sigmoid(sm_ref[...] + bf_ref[...])
    lf_ref[...] = lf[:, FB_LANE:FB_LANE + FOX_HEADS]
    cum = _dot_sel(sel_ref[...], lf)
    fn_ref[...] = cum[:, FB_LANE:FB_LANE + FOX_HEADS]


def _fox_bias_sample(small, bf_row, t):
    rows = small.shape[0]
    r = np.arange(rows)
    sel = ((r[:, None] // t) == (r[None, :] // t)) & (r[None, :] <= r[:, None])
    sel = jnp.asarray(sel.astype(np.float32), BF16)
    full = lambda shape: pl.BlockSpec(shape, lambda i: tuple(0 for _ in shape))
    return pl.pallas_call(
        _fox_bias_sample_kernel,
        grid=(1,),
        in_specs=[full((rows, LANES)), full((1, LANES)), full((rows, rows))],
        out_specs=[full((rows, FOX_HEADS)), full((rows, FOX_HEADS))],
        out_shape=[jax.ShapeDtypeStruct((rows, FOX_HEADS), F32),
                   jax.ShapeDtypeStruct((rows, FOX_HEADS), F32)],
        compiler_params=_cparams(("arbitrary",)),
        name="fox_bias_sample",
    )(small, bf_row, sel)


FOX_HEADS_PER_STEP = 4


def _fox_prompt_kernel(q_ref, k_ref, v_ref, fc_ref, ft_ref, o_ref, *, tq, tk):
    hb = FOX_HEADS_PER_STEP
    hg = pl.program_id(1)
    i = pl.program_id(2)
    lane = lax.broadcasted_iota(jnp.int32, (tq, LANES), 1)
    rowi = lax.broadcasted_iota(jnp.int32, (tq, tk), 0)
    coli = lax.broadcasted_iota(jnp.int32, (tq, tk), 1)
    fc = fc_ref[...]
    qs, f_ts = [], []
    for hh in range(hb):
        hs = slice(hh * FOX_DH, (hh + 1) * FOX_DH)
        qs.append((q_ref[:, hs].astype(F32) * (FOX_DH ** -0.5)).astype(BF16))
        f_ts.append(jnp.sum(jnp.where(lane == hg * hb + hh + FB_LANE, fc, 0.0), axis=-1, keepdims=True))

    def step(j, carry, masked):
        ks = pl.ds(pl.multiple_of(j * tk, tk), tk)
        out = []
        for hh in range(hb):
            hs = slice(hh * FOX_DH, (hh + 1) * FOX_DH)
            m, l, acc = carry[hh]
            s = _dot_nt(qs[hh], k_ref[ks, hs])
            s = s + f_ts[hh] - ft_ref[hh:hh + 1, ks]
            if masked:
                s = jnp.where(rowi >= coli, s, NEG)
            m_new = jnp.maximum(m, jnp.max(s, axis=-1, keepdims=True))
            p = jnp.exp(s - m_new)
            alpha = jnp.exp(m - m_new)
            l = alpha * l + jnp.sum(p, axis=-1, keepdims=True)
            acc = alpha * acc + _dot(p.astype(BF16), v_ref[ks, hs])
            out.append((m_new, l, acc))
        return tuple(out)

    init = tuple((jnp.full((tq, 1), NEG, F32), jnp.zeros((tq, 1), F32), jnp.zeros((tq, FOX_DH), F32))
                 for _ in range(hb))
    nfull = i * (tq // tk)
    carry = lax.fori_loop(0, nfull, lambda j, c: step(j, c, False), init)
    carry = step(nfull, carry, True)
    for hh in range(hb):
        m, l, acc = carry[hh]
        o_ref[:, hh * FOX_DH:(hh + 1) * FOX_DH] = (acc / l).astype(o_ref.dtype)


def _fox_prompt(qb, kb, vb, fcol, ft, batch, seq, tq=256):
    nq = seq // tq
    hb = FOX_HEADS_PER_STEP
    wb = hb * FOX_DH
    return pl.pallas_call(
        functools.partial(_fox_prompt_kernel, tq=tq, tk=tq),
        grid=(batch, FOX_HEADS // hb, nq),
        in_specs=[pl.BlockSpec((tq, wb), lambda b, h, i: (b * nq + i, h)),
                  pl.BlockSpec((seq, wb), lambda b, h, i: (b, h)),
                  pl.BlockSpec((seq, wb), lambda b, h, i: (b, h)),
                  pl.BlockSpec((tq, LANES), lambda b, h, i: (b * nq + i, 0)),
                  pl.BlockSpec((None, None, hb, seq), lambda b, h, i: (b, h, 0, 0))],
        out_specs=pl.BlockSpec((tq, wb), lambda b, h, i: (b * nq + i, h)),
        out_shape=jax.ShapeDtypeStruct((batch * seq, FOX_W), BF16),
        compiler_params=_cparams(("arbitrary", "arbitrary", "arbitrary")),
        name="fox_prompt",
    )(qb, kb, vb, fcol, ft.reshape(batch, FOX_HEADS // hb, hb, seq))


PAGES_PER_STEP = 8
ROWS8 = 8
QROWS = FOX_HEADS * ROWS8
PAGE_FLAT = PAGE * FOX_HEADS


def _dot_sel_rhs(x, w01):
    hi, mid, lo = _split3(x)
    m = x.shape[0]
    stacked = jnp.concatenate([hi.astype(F32), mid.astype(F32), lo.astype(F32)], axis=0).astype(BF16)
    r = _dot(stacked, w01)
    return r[0:m] + r[m:2 * m] + r[2 * m:3 * m]


def _fox_sample_kernel(pt_ref, q_ref, kn_ref, vn_ref, fn_ref, fnrow_ref, madd_ref, maddn_ref, usuf_ref, tot_ref,
                       *refs, t):
    g = PAGES_PER_STEP
    k_refs, v_refs, lf_refs = refs[0:g], refs[g:2 * g], refs[2 * g:3 * g]
    o_ref = refs[3 * g]
    q_sc, m_sc, l_sc, acc_sc, carry_sc = refs[3 * g + 1:]
    j = pl.program_id(1)
    fn_t = fn_ref[...][:, 0:1]

    @pl.when(j == 0)
    def _():
        q_sc[...] = jnp.zeros(q_sc.shape, F32)
        for h in range(FOX_HEADS):
            q_sc[h * ROWS8:h * ROWS8 + t, :] = q_ref[:, h * FOX_DH:(h + 1) * FOX_DH] * (FOX_DH ** -0.5)
        carry_sc[...] = jnp.zeros(carry_sc.shape, F32)
        pad = jnp.zeros((LANES - t * FOX_HEADS, FOX_DH), F32)
        kn = jnp.concatenate([kn_ref[...].reshape(t * FOX_HEADS, FOX_DH), pad], axis=0)
        vn = jnp.concatenate([vn_ref[...].reshape(t * FOX_HEADS, FOX_DH), pad], axis=0)
        s = _dot_nt(q_sc[...], kn) + fn_t - fnrow_ref[...] + maddn_ref[...]
        m = jnp.max(s, axis=-1, keepdims=True)
        p = jnp.exp(s - m)
        m_sc[...] = jnp.broadcast_to(m, m_sc.shape)
        l_sc[...] = jnp.broadcast_to(jnp.sum(p, axis=-1, keepdims=True), l_sc.shape)
        acc_sc[...] = _dot(p, vn)

    lf = jnp.concatenate([lf_refs[gi][...] for gi in range(g)], axis=0)
    r_in = _dot_sel_rhs(lf, usuf_ref[...])
    page_tot = _dot_sel_rhs((r_in + lf)[:, 0:LANES], tot_ref[...])
    carry = carry_sc[...]
    q = q_sc[...]
    madd = madd_ref[...]
    m_old = m_sc[...]
    m_new = m_old
    s_list = []
    for gi in range(g):
        bias = madd + (r_in[gi:gi + 1, :] + carry)
        carry = carry + page_tot[gi:gi + 1, :]
        s = _dot_nt(q, k_refs[gi][...].reshape(PAGE_FLAT, FOX_DH)) + fn_t + bias
        s_list.append(s)
        m_new = jnp.maximum(m_new, jnp.max(s, axis=-1, keepdims=True))
    carry_sc[...] = carry
    alpha = jnp.exp(m_old - m_new)
    l = alpha * l_sc[...]
    acc = alpha * acc_sc[...]
    m_col = m_new[:, 0:1]
    for gi in range(g):
        p = jnp.exp(s_list[gi] - m_col)
        l = l + jnp.sum(p, axis=-1, keepdims=True)
        acc = acc + _dot(p, v_refs[gi][...].reshape(PAGE_FLAT, FOX_DH))
    m_sc[...] = m_new
    l_sc[...] = l
    acc_sc[...] = acc

    @pl.when(j == pl.num_programs(1) - 1)
    def _():
        o = acc / l
        for h in range(FOX_HEADS):
            o_ref[:, h * FOX_DH:(h + 1) * FOX_DH] = o[h * ROWS8:h * ROWS8 + t, :].astype(o_ref.dtype)


def _fox_sample_tables(t):
    row = np.arange(QROWS)[:, None]
    col = np.arange(PAGE_FLAT)[None, :]
    madd = np.where((row // ROWS8) == (col % FOX_HEADS), 0.0, NEG).astype(np.float32)
    coln = np.arange(LANES)[None, :]
    ok = (coln < t * FOX_HEADS) & ((row // ROWS8) == (coln % FOX_HEADS)) & ((coln // FOX_HEADS) <= (row % ROWS8))
    maddn = np.where(ok, 0.0, NEG).astype(np.float32)
    src = np.arange(PAGE_FLAT)[:, None]
    same_head = (src % FOX_HEADS) == (col % FOX_HEADS)
    usuf = (same_head & ((src // FOX_HEADS) > (col // FOX_HEADS))).astype(np.float32)
    lane = np.arange(LANES)[:, None]
    tot = ((lane < FOX_HEADS) & (lane == (col % FOX_HEADS))).astype(np.float32)
    return (jnp.asarray(madd), jnp.asarray(maddn), jnp.asarray(usuf, BF16), jnp.asarray(tot, BF16))


def _fox_sample(page_table, q, kn, vn, fn, cache_k, cache_v, cache_lf, t):
    batch, npages = page_table.shape
    g = PAGES_PER_STEP
    nsteps = npages // g
    madd, maddn, usuf, tot = _fox_sample_tables(t)
    fn_rows = jnp.pad(fn.transpose(0, 2, 1), ((0, 0), (0, 0), (0, ROWS8 - t))).reshape(batch, QROWS, 1)
    fn_rows = jnp.broadcast_to(fn_rows, (batch, QROWS, LANES))
    fn_cols = jnp.pad(fn.reshape(batch, 1, t * FOX_HEADS), ((0, 0), (0, 0), (0, LANES - t * FOX_HEADS)))

    def page_map5(gi):
        return lambda b, j, pt: (0, pt[b, npages - 1 - (j * g + gi)], 0, 0, 0)

    def page_map3(gi):
        return lambda b, j, pt: (pt[b, npages - 1 - (j * g + gi)], 0, 0)

    seq3 = lambda b, j, pt: (b, 0, 0)
    seq4 = lambda b, j, pt: (b, 0, 0, 0)
    const2 = lambda b, j, pt: (0, 0)
    in_specs = [pl.BlockSpec((None, t, FOX_W), seq3),
                pl.BlockSpec((None, t, FOX_HEADS, FOX_DH), seq4),
                pl.BlockSpec((None, t, FOX_HEADS, FOX_DH), seq4),
                pl.BlockSpec((None, QROWS, LANES), seq3),
                pl.BlockSpec((None, 1, LANES), seq3),
                pl.BlockSpec(madd.shape, const2),
                pl.BlockSpec(maddn.shape, const2),
                pl.BlockSpec(usuf.shape, const2),
                pl.BlockSpec(tot.shape, const2)]
    in_specs += [pl.BlockSpec((None, None, PAGE, FOX_HEADS, FOX_DH), page_map5(gi)) for gi in range(g)]
    in_specs += [pl.BlockSpec((None, None, PAGE, FOX_HEADS, FOX_DH), page_map5(gi)) for gi in range(g)]
    in_specs += [pl.BlockSpec((None, 1, PAGE_FLAT), page_map3(gi)) for gi in range(g)]
    grid_spec = pltpu.PrefetchScalarGridSpec(
        num_scalar_prefetch=1,
        grid=(batch, nsteps),
        in_specs=in_specs,
        out_specs=pl.BlockSpec((None, t, FOX_W), seq3),
        scratch_shapes=[pltpu.VMEM((QROWS, FOX_DH), F32),
                        pltpu.VMEM((QROWS, LANES), F32),
                        pltpu.VMEM((QROWS, LANES), F32),
                        pltpu.VMEM((QROWS, FOX_DH), F32),
                        pltpu.VMEM((1, PAGE_FLAT), F32)],
    )
    return pl.pallas_call(
        functools.partial(_fox_sample_kernel, t=t),
        grid_spec=grid_spec,
        out_shape=jax.ShapeDtypeStruct((batch, t, FOX_W), F32),
        compiler_params=_cparams(("arbitrary", "arbitrary")),
        name="fox_sample",
    )(page_table, q, kn, vn, fn_rows, fn_cols, madd, maddn, usuf, tot,
      *([cache_k] * g), *([cache_v] * g), *([cache_lf] * g))


def _merge_kernel(oa_ref, ob_ref, wa_ref, wb_ref, ga_ref, gb_ref, o_ref):
    ua = _dot(oa_ref[...], wa_ref[...])
    ub = _dot(ob_ref[...], wb_ref[...])
    o_ref[...] = (jax.nn.sigmoid(ga_ref[...]) * ua + jax.nn.sigmoid(gb_ref[...]) * ub).astype(o_ref.dtype)


def _merge(oa, ob, wa, wb, gates, tm, tn=1024):
    m = oa.shape[0]
    d = wa.shape[1]
    nj = d // tn
    return pl.pallas_call(
        _merge_kernel,
        grid=(nj, m // tm),
        in_specs=[pl.BlockSpec((tm, GLA_VW), lambda j, i: (i, 0)),
                  pl.BlockSpec((tm, FOX_W), lambda j, i: (i, 0)),
                  pl.BlockSpec((GLA_VW, tn), lambda j, i: (0, j)),
                  pl.BlockSpec((FOX_W, tn), lambda j, i: (0, j)),
                  pl.BlockSpec((tm, tn), lambda j, i: (i, j)),
                  pl.BlockSpec((tm, tn), lambda j, i: (i, nj + j))],
        out_specs=pl.BlockSpec((tm, tn), lambda j, i: (i, j)),
        out_shape=jax.ShapeDtypeStruct((m, d), BF16),
        compiler_params=_cparams(("arbitrary", "arbitrary")),
        name="merge",
    )(oa, ob, wa, wb, gates, gates)


ROUTER_GROUP_LANE = N_EXPERTS


def _route(logits):
    lane_i = lax.broadcasted_iota(jnp.int32, logits.shape, 1)
    lane = lane_i.astype(F32)
    grp_of_lane = (lane_i >> 2).astype(F32)
    big = float(LANES)
    is_grp = (lane_i >= ROUTER_GROUP_LANE) & (lane_i < ROUTER_GROUP_LANE + N_GROUPS)
    gl = jnp.where(is_grp, logits, NEG)
    gmax = jnp.max(gl, axis=-1, keepdims=True)
    g_idx = jnp.min(jnp.where(is_grp & (gl == gmax), lane - ROUTER_GROUP_LANE, big), axis=-1, keepdims=True)
    g_w = 1.0 / jnp.sum(jnp.where(is_grp, jnp.exp(gl - gmax), 0.0), axis=-1, keepdims=True)
    in_grp = (lane_i < N_EXPERTS) & (grp_of_lane == g_idx)
    e1 = jnp.where(in_grp, logits, NEG)
    v1 = jnp.max(e1, axis=-1, keepdims=True)
    i1 = jnp.min(jnp.where(in_grp & (e1 == v1), lane, big), axis=-1, keepdims=True)
    rest = in_grp & (lane != i1)
    e2 = jnp.where(rest, logits, NEG)
    v2 = jnp.max(e2, axis=-1, keepdims=True)
    i2 = jnp.min(jnp.where(rest & (e2 == v2), lane, big), axis=-1, keepdims=True)
    r = jnp.exp(v2 - v1)
    w1 = g_w / (1.0 + r)
    w2 = g_w * r / (1.0 + r)
    grp_onehot = jnp.where(is_grp & (lane - ROUTER_GROUP_LANE == g_idx), 1.0, 0.0)
    return jnp.where(lane == i1, w1, 0.0) + jnp.where(lane == i2, w2, 0.0) + grp_onehot


HX_W = D_MODEL + LANES


def _outproj_kernel(mg_ref, w_ref, x_ref, gt_ref, g2_ref, sh_ref, sc_ref, wr_ref, br_ref, x1_ref, hx_ref):
    x1 = x_ref[...] + gt_ref[...] * _dot(mg_ref[...], w_ref[...])
    x1_ref[...] = x1
    h2 = _rms(x1) * g2_ref[...] * (1.0 + sc_ref[...]) + sh_ref[...]
    hx_ref[:, 0:D_MODEL] = h2
    hx_ref[:, D_MODEL:HX_W] = _route(_dot3(h2, wr_ref[...]) + br_ref[...])


def _outproj(merged, w_out, x, mod, g2, w_router, b_router, tm, tiles_per_batch):
    m, d = x.shape
    rb = mod.shape[2]
    const2 = lambda i: (0, 0)
    return pl.pallas_call(
        _outproj_kernel,
        grid=(m // tm,),
        in_specs=[pl.BlockSpec((tm, d), lambda i: (i, 0)),
                  pl.BlockSpec((d, d), const2),
                  pl.BlockSpec((tm, d), lambda i: (i, 0)),
                  _mod_spec(2, rb, d, tiles_per_batch),
                  pl.BlockSpec((1, d), const2),
                  _mod_spec(3, rb, d, tiles_per_batch),
                  _mod_spec(4, rb, d, tiles_per_batch),
                  pl.BlockSpec((d, LANES), const2),
                  pl.BlockSpec((1, LANES), const2)],
        out_specs=[pl.BlockSpec((tm, d), lambda i: (i, 0)),
                   pl.BlockSpec((tm, HX_W), lambda i: (i, 0))],
        out_shape=[jax.ShapeDtypeStruct((m, d), F32),
                   jax.ShapeDtypeStruct((m, HX_W), F32)],
        compiler_params=_cparams(("arbitrary",)),
        name="outproj",
    )(merged, w_out, x, mod, g2, mod, mod, w_router, b_router)


MOE_TILE = 512
PLAN_TILE = 512


def _moe_plan_kernel(r_ref, tri_ref, excl_ref, pos_ref, tg_ref, nu_ref, cnt_sc, off_sc, run_sc):
    ph = pl.program_id(0)
    n = pl.program_id(1)
    lane = lax.broadcasted_iota(jnp.int32, (1, LANES), 1)
    lane_t = lax.broadcasted_iota(jnp.int32, r_ref.shape, 1)
    is_grp = (lane_t >= ROUTER_GROUP_LANE) & (lane_t < ROUTER_GROUP_LANE + N_GROUPS)
    g4 = jnp.where(is_grp, r_ref[...], 0.0)

    @pl.when((ph == 0) & (n == 0))
    def _():
        cnt_sc[...] = jnp.zeros(cnt_sc.shape, F32)

    @pl.when(ph == 0)
    def _():
        cnt_sc[...] += jnp.sum(g4, axis=0, keepdims=True)

    @pl.when((ph == 1) & (n == 0))
    def _():
        padded = jnp.floor((cnt_sc[...] + (MOE_TILE - 1)) * (1.0 / MOE_TILE)) * MOE_TILE
        off = _dot_sel_rhs(jnp.broadcast_to(padded, (8, LANES)), excl_ref[...])[0:1]
        off_sc[...] = off
        run_sc[...] = jnp.zeros(run_sc.shape, F32)
        end = off + padded
        tile_start = lane.astype(F32) * MOE_TILE
        tg = jnp.zeros((1, LANES), F32)
        for g in range(N_GROUPS):
            end_g = jnp.sum(jnp.where(lane == ROUTER_GROUP_LANE + g, end, 0.0), axis=-1, keepdims=True)
            tg = tg + jnp.where(end_g <= tile_start, 1.0, 0.0)
        tg_ref[...] = jnp.minimum(tg, N_GROUPS - 1.0).astype(jnp.int32)
        total = jnp.sum(jnp.where(lane == ROUTER_GROUP_LANE + N_GROUPS - 1, end, 0.0), axis=-1, keepdims=True)
        nu_ref[...] = jnp.broadcast_to(total * (1.0 / MOE_TILE), (1, LANES)).astype(jnp.int32)

    @pl.when(ph == 1)
    def _():
        rank = _dot(tri_ref[...], g4.astype(BF16)) + run_sc[...]
        run_sc[...] += jnp.sum(g4, axis=0, keepdims=True)
        posv = g4 * (off_sc[...] + rank)
        hi, mid, lo = _split3(posv)
        ones = jnp.ones((8, LANES), BF16)
        row = _dot_nt(ones, hi) + _dot_nt(ones, mid) + _dot_nt(ones, lo)
        pos_ref[...] = row[0:1].astype(jnp.int32)


def _moe_plan(hx, ntiles):
    t = hx.shape[0]
    nblk = t // PLAN_TILE
    tri = jnp.asarray(np.tril(np.ones((PLAN_TILE, PLAN_TILE), np.float32), -1), BF16)
    excl = jnp.asarray(np.triu(np.ones((LANES, LANES), np.float32), 1), BF16)
    assert ntiles <= LANES
    pos, tg, nu = pl.pallas_call(
        _moe_plan_kernel,
        grid=(2, nblk),
        in_specs=[pl.BlockSpec((PLAN_TILE, LANES), lambda ph, n: (n, D_MODEL // LANES)),
                  pl.BlockSpec((PLAN_TILE, PLAN_TILE), lambda ph, n: (0, 0)),
                  pl.BlockSpec((LANES, LANES), lambda ph, n: (0, 0))],
        out_specs=[pl.BlockSpec((1, PLAN_TILE), lambda ph, n: (0, n * ph)),
                   pl.BlockSpec((1, LANES), lambda ph, n: (0, 0)),
                   pl.BlockSpec((1, LANES), lambda ph, n: (0, 0))],
        out_shape=[jax.ShapeDtypeStruct((1, t), jnp.int32),
                   jax.ShapeDtypeStruct((1, LANES), jnp.int32),
                   jax.ShapeDtypeStruct((1, LANES), jnp.int32)],
        scratch_shapes=[pltpu.VMEM((1, LANES), F32), pltpu.VMEM((1, LANES), F32), pltpu.VMEM((1, LANES), F32)],
        compiler_params=_cparams(("arbitrary", "arbitrary")),
        name="moe_plan",
    )(hx, tri, excl)
    return pos.reshape(t), tg.reshape(LANES), nu.reshape(LANES)[0:1]


DISPATCH_ROWS = 512


def _row_copy(src_ref, src_row, dst_ref, dst_row, sem):
    return pltpu.make_async_copy(src_ref.at[pl.ds(src_row, 1)], dst_ref.at[pl.ds(dst_row, 1)], sem)


def _moe_dispatch_kernel(pos_ref, hx_ref, init_ref, xs_ref, sem):
    del init_ref
    base = pl.program_id(0) * DISPATCH_ROWS

    def start(r, c):
        _row_copy(hx_ref, base + r, xs_ref, pos_ref[base + r], sem).start()
        return c

    def wait(r, c):
        _row_copy(hx_ref, base + r, xs_ref, pos_ref[base + r], sem).wait()
        return c

    lax.fori_loop(0, DISPATCH_ROWS, start, 0)
    lax.fori_loop(0, DISPATCH_ROWS, wait, 0)


def _moe_dispatch(hx, pos, rows_out):
    t, w = hx.shape
    grid_spec = pltpu.PrefetchScalarGridSpec(
        num_scalar_prefetch=1,
        grid=(t // DISPATCH_ROWS,),
        in_specs=[pl.BlockSpec(memory_space=pl.ANY), pl.BlockSpec(memory_space=pl.ANY)],
        out_specs=pl.BlockSpec(memory_space=pl.ANY),
        scratch_shapes=[pltpu.SemaphoreType.DMA(())],
    )
    return pl.pallas_call(
        _moe_dispatch_kernel,
        grid_spec=grid_spec,
        out_shape=jax.ShapeDtypeStruct((rows_out, w), F32),
        input_output_aliases={2: 0},
        compiler_params=_cparams(("arbitrary",)),
        name="moe_dispatch",
    )(pos, hx, jnp.zeros((rows_out, w), F32))


def _moe_grouped_kernel(tg_ref, nu_ref, xs_ref, wg_ref, wu_ref, wd_ref, ys_ref, xb_sc, acc_sc):
    i = pl.program_id(0)
    ei = pl.program_id(1)
    used = i < nu_ref[0]

    @pl.when(used & (ei == 0))
    def _():
        xb_sc[...] = xs_ref[:, 0:D_MODEL].astype(BF16)
        acc_sc[...] = jnp.zeros(acc_sc.shape, F32)

    @pl.when(used)
    def _():
        h = xb_sc[...]
        a = _dot(h, wg_ref[...])
        u = _dot(h, wu_ref[...])
        lane = lax.broadcasted_iota(jnp.int32, (xs_ref.shape[0], LANES), 1)
        e = tg_ref[i] * EXP_PER_GROUP + ei
        cw = jnp.sum(jnp.where(lane == e, xs_ref[:, D_MODEL:HX_W], 0.0), axis=-1, keepdims=True)
        hid = (_silu(a) * u * cw).astype(BF16)
        acc_sc[...] += _dot(hid, wd_ref[...])

    @pl.when(ei == pl.num_programs(1) - 1)
    def _():
        ys_ref[...] = jnp.where(used, acc_sc[...], 0.0)


def _moe_grouped(xs, tg, nu, wg, wu, wd):
    rows = xs.shape[0]
    ne, d, de = wg.shape
    ntiles = rows // MOE_TILE

    def wmap(i, ei, tg_ref, nu_ref):
        e = jnp.where(i < nu_ref[0], tg_ref[i] * EXP_PER_GROUP + ei, ne - 1)
        return (e, 0, 0)

    grid_spec = pltpu.PrefetchScalarGridSpec(
        num_scalar_prefetch=2,
        grid=(ntiles, EXP_PER_GROUP),
        in_specs=[pl.BlockSpec((MOE_TILE, HX_W), lambda i, ei, tg_ref, nu_ref: (i, 0)),
                  pl.BlockSpec((None, d, de), wmap),
                  pl.BlockSpec((None, d, de), wmap),
                  pl.BlockSpec((None, de, d), wmap)],
        out_specs=pl.BlockSpec((MOE_TILE, d), lambda i, ei, tg_ref, nu_ref: (i, 0)),
        scratch_shapes=[pltpu.VMEM((MOE_TILE, d), BF16), pltpu.VMEM((MOE_TILE, d), F32)],
    )
    return pl.pallas_call(
        _moe_grouped_kernel,
        grid_spec=grid_spec,
        out_shape=jax.ShapeDtypeStruct((rows, d), F32),
        compiler_params=_cparams(("arbitrary", "arbitrary")),
        name="moe_grouped",
    )(tg, nu, xs, wg, wu, wd)


COMBINE_ROWS = 256


def _moe_combine_kernel(pos_ref, ys_ref, x1_ref, gt_ref, gf_ref, y_ref, buf, sem):
    i = pl.program_id(0)
    n = pl.num_programs(0)

    def fetch(tile, slot):
        def body(r, c):
            _row_copy(ys_ref, pos_ref[tile * COMBINE_ROWS + r], buf.at[slot], r, sem.at[slot]).start()
            return c
        lax.fori_loop(0, COMBINE_ROWS, body, 0)

    @pl.when(i == 0)
    def _():
        fetch(0, 0)

    @pl.when(i + 1 < n)
    def _():
        fetch(i + 1, (i + 1) % 2)

    slot = i % 2

    def wait(r, c):
        _row_copy(ys_ref, pos_ref[i * COMBINE_ROWS + r], buf.at[slot], r, sem.at[slot]).wait()
        return c

    lax.fori_loop(0, COMBINE_ROWS, wait, 0)
    x2 = x1_ref[...] + gt_ref[...] * buf[slot]
    y_ref[...] = _rms(x2) * gf_ref[...]


def _moe_combine(ys, pos, x1, mod, g_final, tiles_per_batch):
    m, d = x1.shape
    rb = mod.shape[2]
    grid_spec = pltpu.PrefetchScalarGridSpec(
        num_scalar_prefetch=1,
        grid=(m // COMBINE_ROWS,),
        in_specs=[pl.BlockSpec(memory_space=pl.ANY),
                  pl.BlockSpec((COMBINE_ROWS, d), lambda i, pos_ref: (i, 0)),
                  pl.BlockSpec((None, None, rb, d), lambda i, pos_ref: (5, i // tiles_per_batch, 0, 0)),
                  pl.BlockSpec((1, d), lambda i, pos_ref: (0, 0))],
        out_specs=pl.BlockSpec((COMBINE_ROWS, d), lambda i, pos_ref: (i, 0)),
        scratch_shapes=[pltpu.VMEM((2, COMBINE_ROWS, d), F32), pltpu.SemaphoreType.DMA((2,))],
    )
    return pl.pallas_call(
        _moe_combine_kernel,
        grid_spec=grid_spec,
        out_shape=jax.ShapeDtypeStruct((m, d), F32),
        compiler_params=_cparams(("arbitrary",)),
        name="moe_combine",
    )(pos, ys, x1, mod, g_final)


def _moe_kernel(h_ref, cmb_ref, wg_ref, wu_ref, wd_ref, x1_ref, gt_ref, gf_ref, y_ref, acc_ref):
    e = pl.program_id(1)

    @pl.when(e == 0)
    def _():
        acc_ref[...] = jnp.zeros(acc_ref.shape, F32)

    h = h_ref[...]
    a = _dot(h, wg_ref[...])
    u = _dot(h, wu_ref[...])
    lane = lax.broadcasted_iota(jnp.int32, cmb_ref.shape, 1)
    cw = jnp.sum(jnp.where(lane == e, cmb_ref[...], 0.0), axis=-1, keepdims=True)
    hid = (_silu(a) * u * cw).astype(BF16)
    acc_ref[...] += _dot(hid, wd_ref[...])

    @pl.when(e == pl.num_programs(1) - 1)
    def _():
        x2 = x1_ref[...] + gt_ref[...] * acc_ref[...]
        y_ref[...] = _rms(x2) * gf_ref[...]


def _moe(h2, cmb, wg, wu, wd, x1, mod, g_final, tm, tiles_per_batch):
    m, d = x1.shape
    rb = mod.shape[2]
    ne, _, de = wg.shape
    return pl.pallas_call(
        _moe_kernel,
        grid=(m // tm, ne),
        in_specs=[pl.BlockSpec((tm, d), lambda i, e: (i, 0)),
                  pl.BlockSpec((tm, LANES), lambda i, e: (i, 0)),
                  pl.BlockSpec((None, d, de), lambda i, e: (e, 0, 0)),
                  pl.BlockSpec((None, d, de), lambda i, e: (e, 0, 0)),
                  pl.BlockSpec((None, de, d), lambda i, e: (e, 0, 0)),
                  pl.BlockSpec((tm, d), lambda i, e: (i, 0)),
                  pl.BlockSpec((None, None, rb, d), lambda i, e: (5, i // tiles_per_batch, 0, 0)),
                  pl.BlockSpec((1, d), lambda i, e: (0, 0))],
        out_specs=pl.BlockSpec((tm, d), lambda i, e: (i, 0)),
        out_shape=jax.ShapeDtypeStruct((m, d), F32),
        scratch_shapes=[pltpu.VMEM((tm, d), F32)],
        compiler_params=_cparams(("arbitrary", "arbitrary")),
        name="moe",
    )(h2, cmb, wg, wu, wd, x1, mod, g_final)


def _prep_weights(w_ada, b_ada, g_norm1, g_norm2, g_final, w_in, w_a2, b_a, b_f, g_gla_norm, w_up_a, w_up_b, w_out,
                  w_grp, b_grp, w_exp, b_exp, w_gate_e, w_up_e, w_down_e):
    w = w_in[0]
    d = D_MODEL
    o_lra = 2 * GLA_KW + 2 * GLA_VW
    o_fox = o_lra + GLA_RANK
    o_fb = o_fox + 3 * FOX_W
    o_g = o_fb + FOX_HEADS
    pad = jnp.zeros((d, LANES - GLA_RANK - FOX_HEADS), F32)
    wall, mask = _gla_tables()
    bf_row = jnp.zeros((1, LANES), F32).at[0, FB_LANE:FB_LANE + FOX_HEADS].set(b_f[0])
    w_router = jnp.concatenate([w_exp[0], w_grp[0], jnp.zeros((d, LANES - N_EXPERTS - N_GROUPS), F32)], axis=1)
    b_router = jnp.concatenate([b_exp[0], b_grp[0], jnp.zeros((LANES - N_EXPERTS - N_GROUPS,), F32)])[None, :]
    return dict(
        w_ada=w_ada[0], b_ada=b_ada[0][None, :],
        g1=g_norm1[0][None, :], g2=g_norm2[0][None, :], gf=g_final[None, :],
        w_gla=w[:, 0:o_lra].astype(BF16),
        w_q=w[:, o_fox:o_fox + FOX_W].astype(BF16),
        w_k=w[:, o_fox + FOX_W:o_fox + 2 * FOX_W].astype(BF16),
        w_v=w[:, o_fox + 2 * FOX_W:o_fox + 3 * FOX_W].astype(BF16),
        w_gates=w[:, o_g:o_g + 2 * d].astype(BF16),
        w_small=jnp.concatenate([w[:, o_lra:o_lra + GLA_RANK], w[:, o_fb:o_fb + FOX_HEADS], pad], axis=1).astype(BF16),
        wall=jnp.asarray(wall, BF16), mask=jnp.asarray(mask, F32),
        w_a2=jnp.concatenate([w_a2[0], jnp.zeros((LANES - GLA_RANK, GLA_KW), F32)], axis=0), b_a=b_a[0][None, :], bf_row=bf_row, g_gla=g_gla_norm[0][None, :],
        w_up_a=w_up_a[0].astype(BF16), w_up_b=w_up_b[0].astype(BF16), w_out=w_out[0].astype(BF16),
        w_router=w_router, b_router=b_router,
        wg=w_gate_e[0].astype(BF16), wu=w_up_e[0].astype(BF16), wd=w_down_e[0].astype(BF16),
    )


def _project(h, p, tm):
    tn = 1024
    (pg,) = _mm(h, p["w_gla"], tm, tn)
    (qb,) = _mm(h, p["w_q"], tm, tn, (BF16,))
    kb, kb16 = _mm(h, p["w_k"], tm, tn, (F32, BF16))
    vb, vb16 = _mm(h, p["w_v"], tm, tn, (F32, BF16))
    (gates,) = _mm(h, p["w_gates"], tm, tn)
    (small,) = _mm(h, p["w_small"], tm, LANES)
    return pg, qb, kb, kb16, vb, vb16, gates, small


def _tail(x, oa, ob, gates, mod, p, tm, tiles_per_batch, grouped):
    merged = _merge(oa, ob, p["w_up_a"], p["w_up_b"], gates, tm)
    tmo = min(tm, 256)
    x1, hx = _outproj(merged, p["w_out"], x, mod, p["g2"], p["w_router"], p["b_router"],
                      tmo, tiles_per_batch * (tm // tmo))
    if not grouped:
        tmm = min(tm, 512)
        return _moe(hx[:, 0:D_MODEL].astype(BF16), hx[:, D_MODEL:HX_W], p["wg"], p["wu"], p["wd"], x1, mod, p["gf"],
                    tmm, tiles_per_batch * (tm // tmm))
    rows_out = x.shape[0] + N_GROUPS * MOE_TILE
    pos, tg, nu = _moe_plan(hx, rows_out // MOE_TILE)
    xs = _moe_dispatch(hx, pos, rows_out)
    ys = _moe_grouped(xs, tg, nu, p["wg"], p["wu"], p["wd"])
    return _moe_combine(ys, pos, x1, mod, p["gf"], tiles_per_batch * (tm // COMBINE_ROWS))


def kernel(x_prompt, x_sample, cache_k, cache_v, cache_logf, state_gla, page_table, c_prompt, c_sample, w_ada, b_ada,
           g_norm1, g_norm2, g_final, w_in, w_a2, b_a, b_f, g_gla_norm, w_up_a, w_up_b, w_out, w_grp, b_grp, w_exp,
           b_exp, w_gate_e, w_up_e, w_down_e):
    p = _prep_weights(w_ada, b_ada, g_norm1, g_norm2, g_final, w_in, w_a2, b_a, b_f, g_gla_norm, w_up_a, w_up_b,
                      w_out, w_grp, b_grp, w_exp, b_exp, w_gate_e, w_up_e, w_down_e)
    bp, seq, d = x_prompt.shape
    bs, t, _ = x_sample.shape

    mod = _adaln(jnp.concatenate([c_prompt, c_sample], axis=0), p["w_ada"], p["b_ada"])
    mod_p = mod[:bp].reshape(bp, 6, 1, d).transpose(1, 0, 2, 3)
    mod_s = jnp.repeat(mod[bp:].reshape(bs, 6, d), t, axis=0).transpose(1, 0, 2)[:, None]

    tm = 1024
    tpb = seq // tm
    xp = x_prompt.reshape(bp * seq, d)
    hp = _normmod(xp, p["g1"], mod_p, tm, tpb)
    pg, qb, kb, kb16, vb, vb16, gates, small = _project(hp, p, tm)
    s0 = jnp.zeros((bp, GLA_HEADS, GLA_DK, GLA_DV), F32)
    oa, s_p = _gla_prompt(pg, small, s0, p["wall"], p["mask"], p["w_a2"], p["b_a"], p["g_gla"], bp, seq)
    lf_p, fcol, ft = _fox_bias_prompt(small, p["bf_row"], bp, seq)
    ob = _fox_prompt(qb, kb16, vb16, fcol, ft, bp, seq)
    y_p = _tail(xp, oa, ob, gates, mod_p, p, tm, tpb, grouped=True)

    rows = bs * t
    xs = x_sample.reshape(rows, d)
    hs = _normmod(xs, p["g1"], mod_s, rows, 1)
    pg_s, qs, ks, _, vs, _, gates_s, small_s = _project(hs, p, rows)
    oa_s, s_s = _gla_sample(pg_s.reshape(bs, t, -1), small_s.reshape(bs, t, LANES), state_gla[0], p["wall"], p["mask"],
                            p["w_a2"], p["b_a"], p["g_gla"], bs, t)
    lf_s, fn_s = _fox_bias_sample(small_s, p["bf_row"], t)
    n_pool = cache_k.shape[1]
    ob_s = _fox_sample(page_table, qs.astype(F32).reshape(bs, t, FOX_W), ks.reshape(bs, t, FOX_HEADS, FOX_DH),
                       vs.reshape(bs, t, FOX_HEADS, FOX_DH), fn_s.reshape(bs, t, FOX_HEADS),
                       cache_k, cache_v, cache_logf[0].reshape(n_pool, 1, PAGE_FLAT), t)
    y_s = _tail(xs, oa_s.reshape(rows, GLA_VW).astype(BF16), ob_s.reshape(rows, FOX_W).astype(BF16), gates_s, mod_s,
                p, rows, 1, grouped=False)

    return (y_p.reshape(bp, seq, d), y_s.reshape(bs, t, d),
            kb.reshape(1, bp, seq, FOX_HEADS, FOX_DH), vb.reshape(1, bp, seq, FOX_HEADS, FOX_DH),
            lf_p.reshape(1, bp, seq, FOX_HEADS), s_p[None],
            ks.reshape(1, bs, t, FOX_HEADS, FOX_DH), vs.reshape(1, bs, t, FOX_HEADS, FOX_DH),
            lf_s.reshape(1, bs, t, FOX_HEADS), s_s[None])
```

```python
import functools

import numpy as np
import jax
import jax.numpy as jnp
from jax import lax
from jax.experimental import pallas as pl
from jax.experimental.pallas import tpu as pltpu

F32 = jnp.float32
BF16 = jnp.bfloat16

D_MODEL = 2048
GLA_HEADS = 4
GLA_DK = 128
GLA_DV = 256
GLA_RANK = 16
GLA_TAU = 16.0
FOX_HEADS = 8
FOX_DH = 128
PAGE = 128
N_GROUPS = 4
EXP_PER_GROUP = 4
N_EXPERTS = 16
D_EXPERT = 512
RMS_EPS = 1e-6
GLA_KW = GLA_HEADS * GLA_DK
GLA_VW = GLA_HEADS * GLA_DV
FOX_W = FOX_HEADS * FOX_DH
GLA_CHUNK = 128
GLA_LEVELS = 7
LANES = 128
NEG = -1e30
VMEM_LIMIT = 56 * 1024 * 1024


def _cparams(sem):
    return pltpu.CompilerParams(dimension_semantics=sem, vmem_limit_bytes=VMEM_LIMIT)


def _dot(a, b):
    return jnp.dot(a, b, preferred_element_type=F32)


def _dot_nt(a, b):
    return lax.dot_general(a, b, (((1,), (1,)), ((), ())), preferred_element_type=F32)


def _dot_tn(a, b):
    return lax.dot_general(a, b, (((0,), (0,)), ((), ())), preferred_element_type=F32)


def _split2(x):
    hi = x.astype(BF16)
    lo = (x - hi.astype(F32)).astype(BF16)
    return hi, lo


def _split3(x):
    hi = x.astype(BF16)
    r = x - hi.astype(F32)
    mid = r.astype(BF16)
    lo = (r - mid.astype(F32)).astype(BF16)
    return hi, mid, lo


def _dot3(a, b):
    ah, al = _split2(a)
    bh, bl = _split2(b)
    return _dot(ah, bh) + _dot(ah, bl) + _dot(al, bh)


def _dot_sel(w01, x):
    hi, mid, lo = _split3(x)
    return _dot(w01, hi) + _dot(w01, mid) + _dot(w01, lo)


def _log_sigmoid(x):
    return jnp.minimum(x, 0.0) - jnp.log1p(jnp.exp(-jnp.abs(x)))


def _silu(x):
    return x * jax.nn.sigmoid(x)


def _adaln_kernel(c_ref, w_ref, b_ref, o_ref):
    o_ref[...] = _dot3(_silu(c_ref[...]), w_ref[...]) + b_ref[...]


def _adaln(c, w, b, tn=512):
    nb, d = c.shape
    n = w.shape[1]
    return pl.pallas_call(
        _adaln_kernel,
        grid=(n // tn,),
        in_specs=[pl.BlockSpec((nb, d), lambda j: (0, 0)),
                  pl.BlockSpec((d, tn), lambda j: (0, j)),
                  pl.BlockSpec((1, tn), lambda j: (0, j))],
        out_specs=pl.BlockSpec((nb, tn), lambda j: (0, j)),
        out_shape=jax.ShapeDtypeStruct((nb, n), F32),
        compiler_params=_cparams(("arbitrary",)),
        name="adaln",
    )(c, w, b)


def _rms(x):
    return x * lax.rsqrt(jnp.mean(x * x, axis=-1, keepdims=True) + RMS_EPS)


def _normmod_kernel(x_ref, g_ref, sh_ref, sc_ref, o_ref):
    y = _rms(x_ref[...]) * g_ref[...]
    o_ref[...] = (y * (1.0 + sc_ref[...]) + sh_ref[...]).astype(o_ref.dtype)


def _mod_spec(k, rb, d, tiles_per_batch):
    return pl.BlockSpec((None, None, rb, d), lambda i: (k, i // tiles_per_batch, 0, 0))


def _normmod(x, g, mod, tm, tiles_per_batch):
    m, d = x.shape
    rb = mod.shape[2]
    return pl.pallas_call(
        _normmod_kernel,
        grid=(m // tm,),
        in_specs=[pl.BlockSpec((tm, d), lambda i: (i, 0)),
                  pl.BlockSpec((1, d), lambda i: (0, 0)),
                  _mod_spec(0, rb, d, tiles_per_batch),
                  _mod_spec(1, rb, d, tiles_per_batch)],
        out_specs=pl.BlockSpec((tm, d), lambda i: (i, 0)),
        out_shape=jax.ShapeDtypeStruct((m, d), BF16),
        compiler_params=_cparams(("arbitrary",)),
        name="normmod",
    )(x, g, mod, mod)


def _mm_kernel(x_ref, w_ref, *o_refs):
    r = _dot(x_ref[...], w_ref[...])
    for o_ref in o_refs:
        o_ref[...] = r.astype(o_ref.dtype)


def _mm(x, w, tm, tn, out_dtypes=(F32,)):
    m, k = x.shape
    n = w.shape[1]
    outs = pl.pallas_call(
        _mm_kernel,
        grid=(n // tn, m // tm),
        in_specs=[pl.BlockSpec((tm, k), lambda j, i: (i, 0)),
                  pl.BlockSpec((k, tn), lambda j, i: (0, j))],
        out_specs=[pl.BlockSpec((tm, tn), lambda j, i: (i, j)) for _ in out_dtypes],
        out_shape=[jax.ShapeDtypeStruct((m, n), dt) for dt in out_dtypes],
        compiler_params=_cparams(("arbitrary", "arbitrary")),
        name="proj_mm",
    )(x, w)
    return outs


def _gla_tables():
    c, p = GLA_CHUNK, GLA_LEVELS
    t = np.arange(c)[:, None]
    m = np.arange(c)[None, :]
    wall = np.zeros((p + 2, c, c), np.float32)
    mask = np.zeros((p + 1, c, c), np.float32)
    for l in range(p):
        half = 1 << l
        pos = t % (2 * half)
        mid = t - pos + half
        right = pos >= half
        wall[l] = np.where(right, (m >= mid) & (m <= t), (m > t) & (m < mid))
        s = m
        mask[l] = ((t >> (l + 1)) == (s >> (l + 1))) & (((t >> l) & 1) == 1) & (((s >> l) & 1) == 0)
    wall[p] = m <= t
    wall[p + 1] = m > t
    mask[p] = t == m
    return wall.reshape((p + 2) * c, c), mask


def _gla_body(q_ref, k_ref, v_ref, ra_ref, sm_ref, wall_ref, mask_ref, wa2_ref, ba_ref, g_ref,
              og_ref, st_ref, *, nchunk, valid, store_rows):
    c, p = GLA_CHUNK, GLA_LEVELS
    row = lax.broadcasted_iota(jnp.int32, (c, GLA_DK), 0)
    wall = wall_ref[...]
    for ci in range(nchunk):
        rows = pl.ds(ci * c, c)
        x = _dot3(sm_ref[rows, :], wa2_ref[...]) + ba_ref[...]
        la = _log_sigmoid(x) * (1.0 / GLA_TAU)
        if valid < c:
            rowh = lax.broadcasted_iota(jnp.int32, la.shape, 0)
            la = jnp.where(rowh < valid, la, 0.0)
        e_all = jnp.exp(_dot_sel(wall, la))
        for h in range(GLA_HEADS):
            ks = slice(h * GLA_DK, (h + 1) * GLA_DK)
            vs = slice(h * GLA_DV, (h + 1) * GLA_DV)
            q = q_ref[rows, ks] * (GLA_DK ** -0.5)
            k = k_ref[rows, ks]
            vb = v_ref[rows, vs].astype(BF16)
            a = mask_ref[p] * _dot_nt(q.astype(BF16), k.astype(BF16))
            for l in range(p):
                el = e_all[l * c:(l + 1) * c, ks]
                xl = (jnp.where(((row >> l) & 1) == 1, q, k) * el).astype(BF16)
                a = a + mask_ref[l] * _dot_nt(xl, xl)
            st = st_ref[h]
            qc = (q * e_all[p * c:(p + 1) * c, ks]).astype(BF16)
            o = _dot(a.astype(BF16), vb) + _dot_nt(qc, st.astype(BF16))
            kr = (k * e_all[(p + 1) * c:(p + 2) * c, ks]).astype(BF16)
            dec = e_all[(p + 1) * c - 1:(p + 1) * c, ks]
            st_ref[h] = dec * st + _dot_tn(vb, kr)
            og = _rms(o) * g_ref[...] * _silu(ra_ref[rows, vs])
            if store_rows < c:
                og_ref[:, vs] = og[:store_rows].astype(og_ref.dtype)
            else:
                og_ref[rows, vs] = og.astype(og_ref.dtype)


def _gla_prompt_kernel(q_ref, k_ref, v_ref, ra_ref, sm_ref, wall_ref, mask_ref, wa2_ref, ba_ref, g_ref, s0_ref,
                       og_ref, s_ref, st_ref, *, nchunk):
    n = pl.program_id(1)

    @pl.when(n == 0)
    def _():
        for h in range(GLA_HEADS):
            st_ref[h] = s0_ref[h].T

    _gla_body(q_ref, k_ref, v_ref, ra_ref, sm_ref, wall_ref, mask_ref, wa2_ref, ba_ref, g_ref, og_ref, st_ref,
              nchunk=nchunk, valid=GLA_CHUNK, store_rows=GLA_CHUNK)

    @pl.when(n == pl.num_programs(1) - 1)
    def _():
        for h in range(GLA_HEADS):
            s_ref[h] = st_ref[h].T


def _gla_prompt(pg, small, s0, wall, mask, wa2, ba, g, batch, seq, tb=256):
    nblk = seq // tb
    rowmap = lambda cb: (lambda b, n: (b * nblk + n, cb))
    const2 = lambda b, n: (0, 0)
    return pl.pallas_call(
        functools.partial(_gla_prompt_kernel, nchunk=tb // GLA_CHUNK),
        grid=(batch, nblk),
        in_specs=[pl.BlockSpec((tb, GLA_KW), rowmap(0)),
                  pl.BlockSpec((tb, GLA_KW), rowmap(1)),
                  pl.BlockSpec((tb, GLA_VW), rowmap(1)),
                  pl.BlockSpec((tb, GLA_VW), rowmap(2)),
                  pl.BlockSpec((tb, LANES), rowmap(0)),
                  pl.BlockSpec(wall.shape, const2),
                  pl.BlockSpec(mask.shape, lambda b, n: (0, 0, 0)),
                  pl.BlockSpec(wa2.shape, const2),
                  pl.BlockSpec(ba.shape, const2),
                  pl.BlockSpec(g.shape, const2),
                  pl.BlockSpec((None, GLA_HEADS, GLA_DK, GLA_DV), lambda b, n: (b, 0, 0, 0))],
        out_specs=[pl.BlockSpec((tb, GLA_VW), rowmap(0)),
                   pl.BlockSpec((None, GLA_HEADS, GLA_DK, GLA_DV), lambda b, n: (b, 0, 0, 0))],
        out_shape=[jax.ShapeDtypeStruct((batch * seq, GLA_VW), BF16),
                   jax.ShapeDtypeStruct((batch, GLA_HEADS, GLA_DK, GLA_DV), F32)],
        scratch_shapes=[pltpu.VMEM((GLA_HEADS, GLA_DV, GLA_DK), F32)],
        compiler_params=_cparams(("arbitrary", "arbitrary")),
        name="gla_prompt",
    )(pg, pg, pg, pg, small, wall, mask, wa2, ba, g, s0)


def _gla_sample_kernel(pg_ref, sm_ref, wall_ref, mask_ref, wa2_ref, ba_ref, g_ref, s0_ref,
                       og_ref, s_ref, pad_ref, smpad_ref, st_ref, *, t):
    @pl.when(pl.program_id(0) == 0)
    def _():
        pad_ref[...] = jnp.zeros(pad_ref.shape, F32)
        smpad_ref[...] = jnp.zeros(smpad_ref.shape, F32)

    pad_ref[0:t, :] = pg_ref[...]
    smpad_ref[0:t, :] = sm_ref[...]
    for h in range(GLA_HEADS):
        st_ref[h] = s0_ref[h].T
    q_ref = pad_ref.at[:, 0:GLA_KW]
    k_ref = pad_ref.at[:, GLA_KW:2 * GLA_KW]
    v_ref = pad_ref.at[:, 2 * GLA_KW:2 * GLA_KW + GLA_VW]
    ra_ref = pad_ref.at[:, 2 * GLA_KW + GLA_VW:2 * GLA_KW + 2 * GLA_VW]
    _gla_body(q_ref, k_ref, v_ref, ra_ref, smpad_ref, wall_ref, mask_ref, wa2_ref, ba_ref, g_ref, og_ref, st_ref,
              nchunk=1, valid=t, store_rows=t)
    for h in range(GLA_HEADS):
        s_ref[h] = st_ref[h].T


def _gla_sample(pg, small, s0, wall, mask, wa2, ba, g, batch, t):
    width = pg.shape[-1]
    const2 = lambda b: (0, 0)
    return pl.pallas_call(
        functools.partial(_gla_sample_kernel, t=t),
        grid=(batch,),
        in_specs=[pl.BlockSpec((None, t, width), lambda b: (b, 0, 0)),
                  pl.BlockSpec((None, t, LANES), lambda b: (b, 0, 0)),
                  pl.BlockSpec(wall.shape, const2),
                  pl.BlockSpec(mask.shape, lambda b: (0, 0, 0)),
                  pl.BlockSpec(wa2.shape, const2),
                  pl.BlockSpec(ba.shape, const2),
                  pl.BlockSpec(g.shape, const2),
                  pl.BlockSpec((None, GLA_HEADS, GLA_DK, GLA_DV), lambda b: (b, 0, 0, 0))],
        out_specs=[pl.BlockSpec((None, t, GLA_VW), lambda b: (b, 0, 0)),
                   pl.BlockSpec((None, GLA_HEADS, GLA_DK, GLA_DV), lambda b: (b, 0, 0, 0))],
        out_shape=[jax.ShapeDtypeStruct((batch, t, GLA_VW), F32),
                   jax.ShapeDtypeStruct((batch, GLA_HEADS, GLA_DK, GLA_DV), F32)],
        scratch_shapes=[pltpu.VMEM((GLA_CHUNK, width), F32),
                        pltpu.VMEM((GLA_CHUNK, LANES), F32),
                        pltpu.VMEM((GLA_HEADS, GLA_DV, GLA_DK), F32)],
        compiler_params=_cparams(("arbitrary",)),
        name="gla_sample",
    )(pg, small, wall, mask, wa2, ba, g, s0)


FB_LANE = GLA_RANK


def _fox_bias_prompt_kernel(sm_ref, bf_ref, tri_ref, lf_ref, fc_ref, ft_ref, carry_ref):
    @pl.when(pl.program_id(1) == 0)
    def _():
        carry_ref[...] = jnp.zeros(carry_ref.shape, F32)

    lf = _log_sigmoid(sm_ref[...] + bf_ref[...])
    lf_ref[...] = lf[:, FB_LANE:FB_LANE + FOX_HEADS]
    cum = _dot_sel(tri_ref[...], lf) + carry_ref[...]
    carry_ref[...] = cum[cum.shape[0] - 1:, :]
    fc_ref[...] = cum
    ft_ref[...] = cum.T[FB_LANE:FB_LANE + FOX_HEADS, :]


def _fox_bias_prompt(small, bf_row, batch, seq, tb=256):
    nblk = seq // tb
    tri = jnp.asarray(np.tril(np.ones((tb, tb), np.float32)), BF16)
    return pl.pallas_call(
        _fox_bias_prompt_kernel,
        grid=(batch, nblk),
        in_specs=[pl.BlockSpec((tb, LANES), lambda b, n: (b * nblk + n, 0)),
                  pl.BlockSpec((1, LANES), lambda b, n: (0, 0)),
                  pl.BlockSpec((tb, tb), lambda b, n: (0, 0))],
        out_specs=[pl.BlockSpec((tb, FOX_HEADS), lambda b, n: (b * nblk + n, 0)),
                   pl.BlockSpec((tb, LANES), lambda b, n: (b * nblk + n, 0)),
                   pl.BlockSpec((None, FOX_HEADS, tb), lambda b, n: (b, 0, n))],
        out_shape=[jax.ShapeDtypeStruct((batch * seq, FOX_HEADS), F32),
                   jax.ShapeDtypeStruct((batch * seq, LANES), F32),
                   jax.ShapeDtypeStruct((batch, FOX_HEADS, seq), F32)],
        scratch_shapes=[pltpu.VMEM((1, LANES), F32)],
        compiler_params=_cparams(("arbitrary", "arbitrary")),
        name="fox_bias_prompt",
    )(small, bf_row, tri)


def _fox_bias_sample_kernel(sm_ref, bf_ref, sel_ref, lf_ref, fn_ref):
    lf = _log_sigmoid(sm_ref[...] + bf_ref[...])
    lf_ref[...] = lf[:, FB_LANE:FB_LANE + FOX_HEADS]
    cum = _dot_sel(sel_ref[...], lf)
    fn_ref[...] = cum[:, FB_LANE:FB_LANE + FOX_HEADS]


def _fox_bias_sample(small, bf_row, t):
    rows = small.shape[0]
    r = np.arange(rows)
    sel = ((r[:, None] // t) == (r[None, :] // t)) & (r[None, :] <= r[:, None])
    sel = jnp.asarray(sel.astype(np.float32), BF16)
    full = lambda shape: pl.BlockSpec(shape, lambda i: tuple(0 for _ in shape))
    return pl.pallas_call(
        _fox_bias_sample_kernel,
        grid=(1,),
        in_specs=[full((rows, LANES)), full((1, LANES)), full((rows, rows))],
        out_specs=[full((rows, FOX_HEADS)), full((rows, FOX_HEADS))],
        out_shape=[jax.ShapeDtypeStruct((rows, FOX_HEADS), F32),
                   jax.ShapeDtypeStruct((rows, FOX_HEADS), F32)],
        compiler_params=_cparams(("arbitrary",)),
        name="fox_bias_sample",
    )(small, bf_row, sel)


FOX_HEADS_PER_STEP = 4


def _fox_prompt_kernel(q_ref, k_ref, v_ref, fc_ref, ft_ref, o_ref, *, tq, tk):
    hb = FOX_HEADS_PER_STEP
    hg = pl.program_id(1)
    i = pl.program_id(2)
    lane = lax.broadcasted_iota(jnp.int32, (tq, LANES), 1)
    rowi = lax.broadcasted_iota(jnp.int32, (tq, tk), 0)
    coli = lax.broadcasted_iota(jnp.int32, (tq, tk), 1)
    fc = fc_ref[...]
    qs, f_ts = [], []
    for hh in range(hb):
        hs = slice(hh * FOX_DH, (hh + 1) * FOX_DH)
        qs.append((q_ref[:, hs].astype(F32) * (FOX_DH ** -0.5)).astype(BF16))
        f_ts.append(jnp.sum(jnp.where(lane == hg * hb + hh + FB_LANE, fc, 0.0), axis=-1, keepdims=True))

    def step(j, carry, masked):
        ks = pl.ds(pl.multiple_of(j * tk, tk), tk)
        out = []
        for hh in range(hb):
            hs = slice(hh * FOX_DH, (hh + 1) * FOX_DH)
            m, l, acc = carry[hh]
            s = _dot_nt(qs[hh], k_ref[ks, hs])
            s = s + f_ts[hh] - ft_ref[hh:hh + 1, ks]
            if masked:
                s = jnp.where(rowi >= coli, s, NEG)
            m_new = jnp.maximum(m, jnp.max(s, axis=-1, keepdims=True))
            p = jnp.exp(s - m_new)
            alpha = jnp.exp(m - m_new)
            l = alpha * l + jnp.sum(p, axis=-1, keepdims=True)
            acc = alpha * acc + _dot(p.astype(BF16), v_ref[ks, hs])
            out.append((m_new, l, acc))
        return tuple(out)

    init = tuple((jnp.full((tq, 1), NEG, F32), jnp.zeros((tq, 1), F32), jnp.zeros((tq, FOX_DH), F32))
                 for _ in range(hb))
    nfull = i * (tq // tk)
    carry = lax.fori_loop(0, nfull, lambda j, c: step(j, c, False), init)
    carry = step(nfull, carry, True)
    for hh in range(hb):
        m, l, acc = carry[hh]
        o_ref[:, hh * FOX_DH:(hh + 1) * FOX_DH] = (acc / l).astype(o_ref.dtype)


def _fox_prompt(qb, kb, vb, fcol, ft, batch, seq, tq=256):
    nq = seq // tq
    hb = FOX_HEADS_PER_STEP
    wb = hb * FOX_DH
    return pl.pallas_call(
        functools.partial(_fox_prompt_kernel, tq=tq, tk=tq),
        grid=(batch, FOX_HEADS // hb, nq),
        in_specs=[pl.BlockSpec((tq, wb), lambda b, h, i: (b * nq + i, h)),
                  pl.BlockSpec((seq, wb), lambda b, h, i: (b, h)),
                  pl.BlockSpec((seq, wb), lambda b, h, i: (b, h)),
                  pl.BlockSpec((tq, LANES), lambda b, h, i: (b * nq + i, 0)),
                  pl.BlockSpec((None, None, hb, seq), lambda b, h, i: (b, h, 0, 0))],
        out_specs=pl.BlockSpec((tq, wb), lambda b, h, i: (b * nq + i, h)),
        out_shape=jax.ShapeDtypeStruct((batch * seq, FOX_W), BF16),
        compiler_params=_cparams(("arbitrary", "arbitrary", "arbitrary")),
        name="fox_prompt",
    )(qb, kb, vb, fcol, ft.reshape(batch, FOX_HEADS // hb, hb, seq))


PAGES_PER_STEP = 8
ROWS8 = 8
QROWS = FOX_HEADS * ROWS8
PAGE_FLAT = PAGE * FOX_HEADS


def _dot_sel_rhs(x, w01):
    hi, mid, lo = _split3(x)
    m = x.shape[0]
    stacked = jnp.concatenate([hi.astype(F32), mid.astype(F32), lo.astype(F32)], axis=0).astype(BF16)
    r = _dot(stacked, w01)
    return r[0:m] + r[m:2 * m] + r[2 * m:3 * m]


def _fox_sample_kernel(pt_ref, q_ref, kn_ref, vn_ref, fn_ref, fnrow_ref, madd_ref, maddn_ref, usuf_ref, tot_ref,
                       *refs, t):
    g = PAGES_PER_STEP
    k_refs, v_refs, lf_refs = refs[0:g], refs[g:2 * g], refs[2 * g:3 * g]
    o_ref = refs[3 * g]
    q_sc, m_sc, l_sc, acc_sc, carry_sc = refs[3 * g + 1:]
    j = pl.program_id(1)
    fn_t = fn_ref[...][:, 0:1]

    @pl.when(j == 0)
    def _():
        q_sc[...] = jnp.zeros(q_sc.shape, F32)
        for h in range(FOX_HEADS):
            q_sc[h * ROWS8:h * ROWS8 + t, :] = q_ref[:, h * FOX_DH:(h + 1) * FOX_DH] * (FOX_DH ** -0.5)
        carry_sc[...] = jnp.zeros(carry_sc.shape, F32)
        pad = jnp.zeros((LANES - t * FOX_HEADS, FOX_DH), F32)
        kn = jnp.concatenate([kn_ref[...].reshape(t * FOX_HEADS, FOX_DH), pad], axis=0)
        vn = jnp.concatenate([vn_ref[...].reshape(t * FOX_HEADS, FOX_DH), pad], axis=0)
        s = _dot_nt(q_sc[...], kn) + fn_t - fnrow_ref[...] + maddn_ref[...]
        m = jnp.max(s, axis=-1, keepdims=True)
        p = jnp.exp(s - m)
        m_sc[...] = jnp.broadcast_to(m, m_sc.shape)
        l_sc[...] = jnp.broadcast_to(jnp.sum(p, axis=-1, keepdims=True), l_sc.shape)
        acc_sc[...] = _dot(p, vn)

    lf = jnp.concatenate([lf_refs[gi][...] for gi in range(g)], axis=0)
    r_in = _dot_sel_rhs(lf, usuf_ref[...])
    page_tot = _dot_sel_rhs((r_in + lf)[:, 0:LANES], tot_ref[...])
    carry = carry_sc[...]
    q = q_sc[...]
    madd = madd_ref[...] + fn_t
    m_old = m_sc[...]
    m_new = m_old
    s_list = []
    for gi in range(g):
        bias = madd + (r_in[gi:gi + 1, :] + carry)
        carry = carry + page_tot[gi:gi + 1, :]
        s = _dot_nt(q, k_refs[gi][...].reshape(PAGE_FLAT, FOX_DH)) + bias
        s_list.append(s)
        m_new = jnp.maximum(m_new, jnp.max(s, axis=-1, keepdims=True))
    carry_sc[...] = carry
    alpha = jnp.exp(m_old - m_new)
    l = alpha * l_sc[...]
    acc = alpha * acc_sc[...]
    m_col = m_new[:, 0:1]
    for gi in range(g):
        p = jnp.exp(s_list[gi] - m_col)
        l = l + jnp.sum(p, axis=-1, keepdims=True)
        acc = acc + _dot(p, v_refs[gi][...].reshape(PAGE_FLAT, FOX_DH))
    m_sc[...] = m_new
    l_sc[...] = l
    acc_sc[...] = acc

    @pl.when(j == pl.num_programs(1) - 1)
    def _():
        o = acc / l
        for h in range(FOX_HEADS):
            o_ref[:, h * FOX_DH:(h + 1) * FOX_DH] = o[h * ROWS8:h * ROWS8 + t, :].astype(o_ref.dtype)


def _fox_sample_tables(t):
    row = np.arange(QROWS)[:, None]
    col = np.arange(PAGE_FLAT)[None, :]
    madd = np.where((row // ROWS8) == (col % FOX_HEADS), 0.0, NEG).astype(np.float32)
    coln = np.arange(LANES)[None, :]
    ok = (coln < t * FOX_HEADS) & ((row // ROWS8) == (coln % FOX_HEADS)) & ((coln // FOX_HEADS) <= (row % ROWS8))
    maddn = np.where(ok, 0.0, NEG).astype(np.float32)
    src = np.arange(PAGE_FLAT)[:, None]
    same_head = (src % FOX_HEADS) == (col % FOX_HEADS)
    usuf = (same_head & ((src // FOX_HEADS) > (col // FOX_HEADS))).astype(np.float32)
    lane = np.arange(LANES)[:, None]
    tot = ((lane < FOX_HEADS) & (lane == (col % FOX_HEADS))).astype(np.float32)
    return (jnp.asarray(madd), jnp.asarray(maddn), jnp.asarray(usuf, BF16), jnp.asarray(tot, BF16))


def _fox_sample(page_table, q, kn, vn, fn, cache_k, cache_v, cache_lf, t):
    batch, npages = page_table.shape
    g = PAGES_PER_STEP
    nsteps = npages // g
    madd, maddn, usuf, tot = _fox_sample_tables(t)
    fn_rows = jnp.pad(fn.transpose(0, 2, 1), ((0, 0), (0, 0), (0, ROWS8 - t))).reshape(batch, QROWS, 1)
    fn_rows = jnp.broadcast_to(fn_rows, (batch, QROWS, LANES))
    fn_cols = jnp.pad(fn.reshape(batch, 1, t * FOX_HEADS), ((0, 0), (0, 0), (0, LANES - t * FOX_HEADS)))

    def page_map5(gi):
        return lambda b, j, pt: (0, pt[b, npages - 1 - (j * g + gi)], 0, 0, 0)

    def page_map3(gi):
        return lambda b, j, pt: (pt[b, npages - 1 - (j * g + gi)], 0, 0)

    seq3 = lambda b, j, pt: (b, 0, 0)
    seq4 = lambda b, j, pt: (b, 0, 0, 0)
    const2 = lambda b, j, pt: (0, 0)
    in_specs = [pl.BlockSpec((None, t, FOX_W), seq3),
                pl.BlockSpec((None, t, FOX_HEADS, FOX_DH), seq4),
                pl.BlockSpec((None, t, FOX_HEADS, FOX_DH), seq4),
                pl.BlockSpec((None, QROWS, LANES), seq3),
                pl.BlockSpec((None, 1, LANES), seq3),
                pl.BlockSpec(madd.shape, const2),
                pl.BlockSpec(maddn.shape, const2),
                pl.BlockSpec(usuf.shape, const2),
                pl.BlockSpec(tot.shape, const2)]
    in_specs += [pl.BlockSpec((None, None, PAGE, FOX_HEADS, FOX_DH), page_map5(gi)) for gi in range(g)]
    in_specs += [pl.BlockSpec((None, None, PAGE, FOX_HEADS, FOX_DH), page_map5(gi)) for gi in range(g)]
    in_specs += [pl.BlockSpec((None, 1, PAGE_FLAT), page_map3(gi)) for gi in range(g)]
    grid_spec = pltpu.PrefetchScalarGridSpec(
        num_scalar_prefetch=1,
        grid=(batch, nsteps),
        in_specs=in_specs,
        out_specs=pl.BlockSpec((None, t, FOX_W), seq3),
        scratch_shapes=[pltpu.VMEM((QROWS, FOX_DH), F32),
                        pltpu.VMEM((QROWS, LANES), F32),
                        pltpu.VMEM((QROWS, LANES), F32),
                        pltpu.VMEM((QROWS, FOX_DH), F32),
                        pltpu.VMEM((1, PAGE_FLAT), F32)],
    )
    return pl.pallas_call(
        functools.partial(_fox_sample_kernel, t=t),
        grid_spec=grid_spec,
        out_shape=jax.ShapeDtypeStruct((batch, t, FOX_W), F32),
        compiler_params=_cparams(("arbitrary", "arbitrary")),
        name="fox_sample",
    )(page_table, q, kn, vn, fn_rows, fn_cols, madd, maddn, usuf, tot,
      *([cache_k] * g), *([cache_v] * g), *([cache_lf] * g))


def _merge_kernel(oa_ref, ob_ref, wa_ref, wb_ref, ga_ref, gb_ref, o_ref):
    ua = _dot(oa_ref[...], wa_ref[...])
    ub = _dot(ob_ref[...], wb_ref[...])
    o_ref[...] = (jax.nn.sigmoid(ga_ref[...]) * ua + jax.nn.sigmoid(gb_ref[...]) * ub).astype(o_ref.dtype)


def _merge(oa, ob, wa, wb, gates, tm, tn=1024):
    m = oa.shape[0]
    d = wa.shape[1]
    nj = d // tn
    return pl.pallas_call(
        _merge_kernel,
        grid=(nj, m // tm),
        in_specs=[pl.BlockSpec((tm, GLA_VW), lambda j, i: (i, 0)),
                  pl.BlockSpec((tm, FOX_W), lambda j, i: (i, 0)),
                  pl.BlockSpec((GLA_VW, tn), lambda j, i: (0, j)),
                  pl.BlockSpec((FOX_W, tn), lambda j, i: (0, j)),
                  pl.BlockSpec((tm, tn), lambda j, i: (i, j)),
                  pl.BlockSpec((tm, tn), lambda j, i: (i, nj + j))],
        out_specs=pl.BlockSpec((tm, tn), lambda j, i: (i, j)),
        out_shape=jax.ShapeDtypeStruct((m, d), BF16),
        compiler_params=_cparams(("arbitrary", "arbitrary")),
        name="merge",
    )(oa, ob, wa, wb, gates, gates)


ROUTER_GROUP_LANE = N_EXPERTS


def _route(logits):
    lane_i = lax.broadcasted_iota(jnp.int32, logits.shape, 1)
    lane = lane_i.astype(F32)
    grp_of_lane = (lane_i >> 2).astype(F32)
    big = float(LANES)
    is_grp = (lane_i >= ROUTER_GROUP_LANE) & (lane_i < ROUTER_GROUP_LANE + N_GROUPS)
    gl = jnp.where(is_grp, logits, NEG)
    gmax = jnp.max(gl, axis=-1, keepdims=True)
    g_idx = jnp.min(jnp.where(is_grp & (gl == gmax), lane - ROUTER_GROUP_LANE, big), axis=-1, keepdims=True)
    g_w = 1.0 / jnp.sum(jnp.where(is_grp, jnp.exp(gl - gmax), 0.0), axis=-1, keepdims=True)
    in_grp = (lane_i < N_EXPERTS) & (grp_of_lane == g_idx)
    e1 = jnp.where(in_grp, logits, NEG)
    v1 = jnp.max(e1, axis=-1, keepdims=True)
    i1 = jnp.min(jnp.where(in_grp & (e1 == v1), lane, big), axis=-1, keepdims=True)
    rest = in_grp & (lane != i1)
    e2 = jnp.where(rest, logits, NEG)
    v2 = jnp.max(e2, axis=-1, keepdims=True)
    i2 = jnp.min(jnp.where(rest & (e2 == v2), lane, big), axis=-1, keepdims=True)
    r = jnp.exp(v2 - v1)
    w1 = g_w / (1.0 + r)
    w2 = g_w * r / (1.0 + r)
    grp_onehot = jnp.where(is_grp & (lane - ROUTER_GROUP_LANE == g_idx), 1.0, 0.0)
    return jnp.where(lane == i1, w1, 0.0) + jnp.where(lane == i2, w2, 0.0) + grp_onehot


HX_W = D_MODEL + LANES


def _outproj_kernel(mg_ref, w_ref, x_ref, gt_ref, g2_ref, sh_ref, sc_ref, wr_ref, br_ref, x1_ref, hx_ref):
    x1 = x_ref[...] + gt_ref[...] * _dot(mg_ref[...], w_ref[...])
    x1_ref[...] = x1
    h2 = _rms(x1) * g2_ref[...] * (1.0 + sc_ref[...]) + sh_ref[...]
    hx_ref[:, 0:D_MODEL] = h2
    hx_ref[:, D_MODEL:HX_W] = _route(_dot3(h2, wr_ref[...]) + br_ref[...])


def _outproj(merged, w_out, x, mod, g2, w_router, b_router, tm, tiles_per_batch):
    m, d = x.shape
    rb = mod.shape[2]
    const2 = lambda i: (0, 0)
    return pl.pallas_call(
        _outproj_kernel,
        grid=(m // tm,),
        in_specs=[pl.BlockSpec((tm, d), lambda i: (i, 0)),
                  pl.BlockSpec((d, d), const2),
                  pl.BlockSpec((tm, d), lambda i: (i, 0)),
                  _mod_spec(2, rb, d, tiles_per_batch),
                  pl.BlockSpec((1, d), const2),
                  _mod_spec(3, rb, d, tiles_per_batch),
                  _mod_spec(4, rb, d, tiles_per_batch),
                  pl.BlockSpec((d, LANES), const2),
                  pl.BlockSpec((1, LANES), const2)],
        out_specs=[pl.BlockSpec((tm, d), lambda i: (i, 0)),
                   pl.BlockSpec((tm, HX_W), lambda i: (i, 0))],
        out_shape=[jax.ShapeDtypeStruct((m, d), F32),
                   jax.ShapeDtypeStruct((m, HX_W), F32)],
        compiler_params=_cparams(("arbitrary",)),
        name="outproj",
    )(merged, w_out, x, mod, g2, mod, mod, w_router, b_router)


MOE_TILE = 512
PLAN_TILE = 512


def _moe_plan_kernel(r_ref, tri_ref, excl_ref, pos_ref, tg_ref, nu_ref, cnt_sc, off_sc, run_sc):
    ph = pl.program_id(0)
    n = pl.program_id(1)
    lane = lax.broadcasted_iota(jnp.int32, (1, LANES), 1)
    lane_t = lax.broadcasted_iota(jnp.int32, r_ref.shape, 1)
    is_grp = (lane_t >= ROUTER_GROUP_LANE) & (lane_t < ROUTER_GROUP_LANE + N_GROUPS)
    g4 = jnp.where(is_grp, r_ref[...], 0.0)

    @pl.when((ph == 0) & (n == 0))
    def _():
        cnt_sc[...] = jnp.zeros(cnt_sc.shape, F32)

    @pl.when(ph == 0)
    def _():
        cnt_sc[...] += jnp.sum(g4, axis=0, keepdims=True)

    @pl.when((ph == 1) & (n == 0))
    def _():
        padded = jnp.floor((cnt_sc[...] + (MOE_TILE - 1)) * (1.0 / MOE_TILE)) * MOE_TILE
        off = _dot_sel_rhs(jnp.broadcast_to(padded, (8, LANES)), excl_ref[...])[0:1]
        off_sc[...] = off
        run_sc[...] = jnp.zeros(run_sc.shape, F32)
        end = off + padded
        tile_start = lane.astype(F32) * MOE_TILE
        tg = jnp.zeros((1, LANES), F32)
        for g in range(N_GROUPS):
            end_g = jnp.sum(jnp.where(lane == ROUTER_GROUP_LANE + g, end, 0.0), axis=-1, keepdims=True)
            tg = tg + jnp.where(end_g <= tile_start, 1.0, 0.0)
        tg_ref[...] = jnp.minimum(tg, N_GROUPS - 1.0).astype(jnp.int32)
        total = jnp.sum(jnp.where(lane == ROUTER_GROUP_LANE + N_GROUPS - 1, end, 0.0), axis=-1, keepdims=True)
        nu_ref[...] = jnp.broadcast_to(total * (1.0 / MOE_TILE), (1, LANES)).astype(jnp.int32)

    @pl.when(ph == 1)
    def _():
        rank = _dot(tri_ref[...], g4.astype(BF16)) + run_sc[...]
        run_sc[...] += jnp.sum(g4, axis=0, keepdims=True)
        posv = g4 * (off_sc[...] + rank)
        hi, mid, lo = _split3(posv)
        ones = jnp.ones((8, LANES), BF16)
        row = _dot_nt(ones, hi) + _dot_nt(ones, mid) + _dot_nt(ones, lo)
        pos_ref[...] = row[0:1].astype(jnp.int32)


def _moe_plan(hx, ntiles):
    t = hx.shape[0]
    nblk = t // PLAN_TILE
    tri = jnp.asarray(np.tril(np.ones((PLAN_TILE, PLAN_TILE), np.float32), -1), BF16)
    excl = jnp.asarray(np.triu(np.ones((LANES, LANES), np.float32), 1), BF16)
    assert ntiles <= LANES
    pos, tg, nu = pl.pallas_call(
        _moe_plan_kernel,
        grid=(2, nblk),
        in_specs=[pl.BlockSpec((PLAN_TILE, LANES), lambda ph, n: (n, D_MODEL // LANES)),
                  pl.BlockSpec((PLAN_TILE, PLAN_TILE), lambda ph, n: (0, 0)),
                  pl.BlockSpec((LANES, LANES), lambda ph, n: (0, 0))],
        out_specs=[pl.BlockSpec((1, PLAN_TILE), lambda ph, n: (0, n * ph)),
                   pl.BlockSpec((1, LANES), lambda ph, n: (0, 0)),
                   pl.BlockSpec((1, LANES), lambda ph, n: (0, 0))],
        out_shape=[jax.ShapeDtypeStruct((1, t), jnp.int32),
                   jax.ShapeDtypeStruct((1, LANES), jnp.int32),
                   jax.ShapeDtypeStruct((1, LANES), jnp.int32)],
        scratch_shapes=[pltpu.VMEM((1, LANES), F32), pltpu.VMEM((1, LANES), F32), pltpu.VMEM((1, LANES), F32)],
        compiler_params=_cparams(("arbitrary", "arbitrary")),
        name="moe_plan",
    )(hx, tri, excl)
    return pos.reshape(t), tg.reshape(LANES), nu.reshape(LANES)[0:1]


DISPATCH_ROWS = 512


def _row_copy(src_ref, src_row, dst_ref, dst_row, sem):
    return pltpu.make_async_copy(src_ref.at[pl.ds(src_row, 1)], dst_ref.at[pl.ds(dst_row, 1)], sem)


def _moe_dispatch_kernel(pos_ref, nu_ref, hx_ref, xs_ref, buf, sem, src_sc, *, t):
    i = pl.program_id(0)
    n = pl.num_programs(0)
    rows = n * DISPATCH_ROWS
    n_used = nu_ref[0] * (MOE_TILE // DISPATCH_ROWS)

    @pl.when(i == 0)
    def _():
        def clear(r, c):
            src_sc[r] = 0
            return c

        def invert(tk, c):
            src_sc[pos_ref[tk]] = tk
            return c

        lax.fori_loop(0, rows, clear, 0)
        lax.fori_loop(0, t, invert, 0)

    def fetch(tile, slot):
        def body(r, c):
            _row_copy(hx_ref, src_sc[tile * DISPATCH_ROWS + r], buf.at[slot], r, sem.at[slot]).start()
            return c
        lax.fori_loop(0, DISPATCH_ROWS, body, 0, unroll=8)

    @pl.when(i == 0)
    def _():
        fetch(0, 0)

    @pl.when(i + 1 < n_used)
    def _():
        fetch(i + 1, (i + 1) % 2)

    slot = i % 2

    @pl.when(i < n_used)
    def _():
        def wait(r, c):
            _row_copy(hx_ref, 0, buf.at[slot], r, sem.at[slot]).wait()
            return c
        lax.fori_loop(0, DISPATCH_ROWS, wait, 0, unroll=8)
        xs_ref[...] = buf[slot]

    @pl.when(i >= n_used)
    def _():
        xs_ref[...] = jnp.zeros(xs_ref.shape, F32)


def _moe_dispatch(hx, pos, nu, rows_out):
    t, w = hx.shape
    grid_spec = pltpu.PrefetchScalarGridSpec(
        num_scalar_prefetch=2,
        grid=(rows_out // DISPATCH_ROWS,),
        in_specs=[pl.BlockSpec(memory_space=pl.ANY)],
        out_specs=pl.BlockSpec((DISPATCH_ROWS, w), lambda i, pos_ref, nu_ref: (i, 0)),
        scratch_shapes=[pltpu.VMEM((2, DISPATCH_ROWS, w), F32), pltpu.SemaphoreType.DMA((2,)),
                        pltpu.SMEM((rows_out,), jnp.int32)],
    )
    return pl.pallas_call(
        functools.partial(_moe_dispatch_kernel, t=t),
        grid_spec=grid_spec,
        out_shape=jax.ShapeDtypeStruct((rows_out, w), F32),
        compiler_params=_cparams(("arbitrary",)),
        name="moe_dispatch",
    )(pos, nu, hx)


def _moe_grouped_kernel(tg_ref, nu_ref, xs_ref, wg_ref, wu_ref, wd_ref, ys_ref, xb_sc, acc_sc):
    i = pl.program_id(0)
    ei = pl.program_id(1)
    used = i < nu_ref[0]

    @pl.when(used & (ei == 0))
    def _():
        xb_sc[...] = xs_ref[:, 0:D_MODEL].astype(BF16)
        acc_sc[...] = jnp.zeros(acc_sc.shape, F32)

    @pl.when(used)
    def _():
        h = xb_sc[...]
        a = _dot(h, wg_ref[...])
        u = _dot(h, wu_ref[...])
        lane = lax.broadcasted_iota(jnp.int32, (xs_ref.shape[0], LANES), 1)
        e = tg_ref[i] * EXP_PER_GROUP + ei
        cw = jnp.sum(jnp.where(lane == e, xs_ref[:, D_MODEL:HX_W], 0.0), axis=-1, keepdims=True)
        hid = (_silu(a) * u * cw).astype(BF16)
        acc_sc[...] += _dot(hid, wd_ref[...])

    @pl.when(ei == pl.num_programs(1) - 1)
    def _():
        ys_ref[...] = jnp.where(used, acc_sc[...], 0.0)


def _moe_grouped(xs, tg, nu, wg, wu, wd):
    rows = xs.shape[0]
    ne, d, de = wg.shape
    ntiles = rows // MOE_TILE

    def wmap(i, ei, tg_ref, nu_ref):
        e = jnp.where(i < nu_ref[0], tg_ref[i] * EXP_PER_GROUP + ei, ne - 1)
        return (e, 0, 0)

    grid_spec = pltpu.PrefetchScalarGridSpec(
        num_scalar_prefetch=2,
        grid=(ntiles, EXP_PER_GROUP),
        in_specs=[pl.BlockSpec((MOE_TILE, HX_W), lambda i, ei, tg_ref, nu_ref: (i, 0)),
                  pl.BlockSpec((None, d, de), wmap),
                  pl.BlockSpec((None, d, de), wmap),
                  pl.BlockSpec((None, de, d), wmap)],
        out_specs=pl.BlockSpec((MOE_TILE, d), lambda i, ei, tg_ref, nu_ref: (i, 0)),
        scratch_shapes=[pltpu.VMEM((MOE_TILE, d), BF16), pltpu.VMEM((MOE_TILE, d), F32)],
    )
    return pl.pallas_call(
        _moe_grouped_kernel,
        grid_spec=grid_spec,
        out_shape=jax.ShapeDtypeStruct((rows, d), F32),
        compiler_params=_cparams(("arbitrary", "arbitrary")),
        name="moe_grouped",
    )(tg, nu, xs, wg, wu, wd)


COMBINE_ROWS = 256


def _moe_combine_kernel(pos_ref, ys_ref, x1_ref, gt_ref, gf_ref, y_ref, buf, sem):
    i = pl.program_id(0)
    n = pl.num_programs(0)

    def fetch(tile, slot):
        def body(r, c):
            _row_copy(ys_ref, pos_ref[tile * COMBINE_ROWS + r], buf.at[slot], r, sem.at[slot]).start()
            return c
        lax.fori_loop(0, COMBINE_ROWS, body, 0, unroll=8)

    @pl.when(i == 0)
    def _():
        fetch(0, 0)

    @pl.when(i + 1 < n)
    def _():
        fetch(i + 1, (i + 1) % 2)

    slot = i % 2

    def wait(r, c):
        _row_copy(ys_ref, pos_ref[i * COMBINE_ROWS + r], buf.at[slot], r, sem.at[slot]).wait()
        return c

    lax.fori_loop(0, COMBINE_ROWS, wait, 0, unroll=8)
    x2 = x1_ref[...] + gt_ref[...] * buf[slot]
    y_ref[...] = _rms(x2) * gf_ref[...]


def _moe_combine(ys, pos, x1, mod, g_final, tiles_per_batch):
    m, d = x1.shape
    rb = mod.shape[2]
    grid_spec = pltpu.PrefetchScalarGridSpec(
        num_scalar_prefetch=1,
        grid=(m // COMBINE_ROWS,),
        in_specs=[pl.BlockSpec(memory_space=pl.ANY),
                  pl.BlockSpec((COMBINE_ROWS, d), lambda i, pos_ref: (i, 0)),
                  pl.BlockSpec((None, None, rb, d), lambda i, pos_ref: (5, i // tiles_per_batch, 0, 0)),
                  pl.BlockSpec((1, d), lambda i, pos_ref: (0, 0))],
        out_specs=pl.BlockSpec((COMBINE_ROWS, d), lambda i, pos_ref: (i, 0)),
        scratch_shapes=[pltpu.VMEM((2, COMBINE_ROWS, d), F32), pltpu.SemaphoreType.DMA((2,))],
    )
    return pl.pallas_call(
        _moe_combine_kernel,
        grid_spec=grid_spec,
        out_shape=jax.ShapeDtypeStruct((m, d), F32),
        compiler_params=_cparams(("arbitrary",)),
        name="moe_combine",
    )(pos, ys, x1, mod, g_final)


def _moe_kernel(h_ref, cmb_ref, wg_ref, wu_ref, wd_ref, x1_ref, gt_ref, gf_ref, y_ref, acc_ref):
    e = pl.program_id(1)

    @pl.when(e == 0)
    def _():
        acc_ref[...] = jnp.zeros(acc_ref.shape, F32)

    h = h_ref[...]
    a = _dot(h, wg_ref[...])
    u = _dot(h, wu_ref[...])
    lane = lax.broadcasted_iota(jnp.int32, cmb_ref.shape, 1)
    cw = jnp.sum(jnp.where(lane == e, cmb_ref[...], 0.0), axis=-1, keepdims=True)
    hid = (_silu(a) * u * cw).astype(BF16)
    acc_ref[...] += _dot(hid, wd_ref[...])

    @pl.when(e == pl.num_programs(1) - 1)
    def _():
        x2 = x1_ref[...] + gt_ref[...] * acc_ref[...]
        y_ref[...] = _rms(x2) * gf_ref[...]


def _moe(h2, cmb, wg, wu, wd, x1, mod, g_final, tm, tiles_per_batch):
    m, d = x1.shape
    rb = mod.shape[2]
    ne, _, de = wg.shape
    return pl.pallas_call(
        _moe_kernel,
        grid=(m // tm, ne),
        in_specs=[pl.BlockSpec((tm, d), lambda i, e: (i, 0)),
                  pl.BlockSpec((tm, LANES), lambda i, e: (i, 0)),
                  pl.BlockSpec((None, d, de), lambda i, e: (e, 0, 0)),
                  pl.BlockSpec((None, d, de), lambda i, e: (e, 0, 0)),
                  pl.BlockSpec((None, de, d), lambda i, e: (e, 0, 0)),
                  pl.BlockSpec((tm, d), lambda i, e: (i, 0)),
                  pl.BlockSpec((None, None, rb, d), lambda i, e: (5, i // tiles_per_batch, 0, 0)),
                  pl.BlockSpec((1, d), lambda i, e: (0, 0))],
        out_specs=pl.BlockSpec((tm, d), lambda i, e: (i, 0)),
        out_shape=jax.ShapeDtypeStruct((m, d), F32),
        scratch_shapes=[pltpu.VMEM((tm, d), F32)],
        compiler_params=_cparams(("arbitrary", "arbitrary")),
        name="moe",
    )(h2, cmb, wg, wu, wd, x1, mod, g_final)


def _prep_weights(w_ada, b_ada, g_norm1, g_norm2, g_final, w_in, w_a2, b_a, b_f, g_gla_norm, w_up_a, w_up_b, w_out,
                  w_grp, b_grp, w_exp, b_exp, w_gate_e, w_up_e, w_down_e):
    w = w_in.reshape(w_in.shape[1:])
    d = D_MODEL
    o_lra = 2 * GLA_KW + 2 * GLA_VW
    o_fox = o_lra + GLA_RANK
    o_fb = o_fox + 3 * FOX_W
    o_g = o_fb + FOX_HEADS
    pad = jnp.zeros((d, LANES - GLA_RANK - FOX_HEADS), F32)
    wall, mask = _gla_tables()
    bf_row = jnp.zeros((1, LANES), F32).at[0, FB_LANE:FB_LANE + FOX_HEADS].set(b_f[0])
    w_router = jnp.concatenate([w_exp[0], w_grp[0], jnp.zeros((d, LANES - N_EXPERTS - N_GROUPS), F32)], axis=1)
    b_router = jnp.concatenate([b_exp[0], b_grp[0], jnp.zeros((LANES - N_EXPERTS - N_GROUPS,), F32)])[None, :]
    return dict(
        w_ada=w_ada.reshape(w_ada.shape[1:]), b_ada=b_ada[0][None, :],
        g1=g_norm1[0][None, :], g2=g_norm2[0][None, :], gf=g_final[None, :],
        w_gla=w[:, 0:o_lra].astype(BF16),
        w_q=w[:, o_fox:o_fox + FOX_W].astype(BF16),
        w_k=w[:, o_fox + FOX_W:o_fox + 2 * FOX_W].astype(BF16),
        w_v=w[:, o_fox + 2 * FOX_W:o_fox + 3 * FOX_W].astype(BF16),
        w_gates=w[:, o_g:o_g + 2 * d].astype(BF16),
        w_small=jnp.concatenate([w[:, o_lra:o_lra + GLA_RANK], w[:, o_fb:o_fb + FOX_HEADS], pad], axis=1).astype(BF16),
        wall=jnp.asarray(wall, BF16), mask=jnp.asarray(mask, F32),
        w_a2=jnp.concatenate([w_a2[0], jnp.zeros((LANES - GLA_RANK, GLA_KW), F32)], axis=0), b_a=b_a[0][None, :], bf_row=bf_row, g_gla=g_gla_norm[0][None, :],
        w_up_a=w_up_a[0].astype(BF16), w_up_b=w_up_b[0].astype(BF16), w_out=w_out[0].astype(BF16),
        w_router=w_router, b_router=b_router,
        wg=w_gate_e[0].astype(BF16), wu=w_up_e[0].astype(BF16), wd=w_down_e[0].astype(BF16),
    )


def _project(h, p, tm):
    tn = 1024
    (pg,) = _mm(h, p["w_gla"], tm, tn)
    (qb,) = _mm(h, p["w_q"], tm, tn, (BF16,))
    kb, kb16 = _mm(h, p["w_k"], tm, tn, (F32, BF16))
    vb, vb16 = _mm(h, p["w_v"], tm, tn, (F32, BF16))
    (gates,) = _mm(h, p["w_gates"], tm, tn)
    (small,) = _mm(h, p["w_small"], tm, LANES)
    return pg, qb, kb, kb16, vb, vb16, gates, small


def _tail(x, oa, ob, gates, mod, p, tm, tiles_per_batch, grouped):
    merged = _merge(oa, ob, p["w_up_a"], p["w_up_b"], gates, tm)
    tmo = min(tm, 256)
    x1, hx = _outproj(merged, p["w_out"], x, mod, p["g2"], p["w_router"], p["b_router"],
                      tmo, tiles_per_batch * (tm // tmo))
    if not grouped:
        tmm = min(tm, 512)
        return _moe(hx[:, 0:D_MODEL].astype(BF16), hx[:, D_MODEL:HX_W], p["wg"], p["wu"], p["wd"], x1, mod, p["gf"],
                    tmm, tiles_per_batch * (tm // tmm))
    rows_out = x.shape[0] + N_GROUPS * MOE_TILE
    pos, tg, nu = _moe_plan(hx, rows_out // MOE_TILE)
    xs = _moe_dispatch(hx, pos, nu, rows_out)
    ys = _moe_grouped(xs, tg, nu, p["wg"], p["wu"], p["wd"])
    return _moe_combine(ys, pos, x1, mod, p["gf"], tiles_per_batch * (tm // COMBINE_ROWS))


def kernel(x_prompt, x_sample, cache_k, cache_v, cache_logf, state_gla, page_table, c_prompt, c_sample, w_ada, b_ada,
           g_norm1, g_norm2, g_final, w_in, w_a2, b_a, b_f, g_gla_norm, w_up_a, w_up_b, w_out, w_grp, b_grp, w_exp,
           b_exp, w_gate_e, w_up_e, w_down_e):
    p = _prep_weights(w_ada, b_ada, g_norm1, g_norm2, g_final, w_in, w_a2, b_a, b_f, g_gla_norm, w_up_a, w_up_b,
                      w_out, w_grp, b_grp, w_exp, b_exp, w_gate_e, w_up_e, w_down_e)
    bp, seq, d = x_prompt.shape
    bs, t, _ = x_sample.shape

    mod = _adaln(jnp.concatenate([c_prompt, c_sample], axis=0), p["w_ada"], p["b_ada"])
    mod_p = mod[:bp].reshape(bp, 6, 1, d).transpose(1, 0, 2, 3)
    mod_s = jnp.repeat(mod[bp:].reshape(bs, 6, d), t, axis=0).transpose(1, 0, 2)[:, None]

    tm = 1024
    tpb = seq // tm
    xp = x_prompt.reshape(bp * seq, d)
    hp = _normmod(xp, p["g1"], mod_p, tm, tpb)
    pg, qb, kb, kb16, vb, vb16, gates, small = _project(hp, p, tm)
    s0 = jnp.zeros((bp, GLA_HEADS, GLA_DK, GLA_DV), F32)
    oa, s_p = _gla_prompt(pg, small, s0, p["wall"], p["mask"], p["w_a2"], p["b_a"], p["g_gla"], bp, seq)
    lf_p, fcol, ft = _fox_bias_prompt(small, p["bf_row"], bp, seq)
    ob = _fox_prompt(qb, kb16, vb16, fcol, ft, bp, seq)
    y_p = _tail(xp, oa, ob, gates, mod_p, p, tm, tpb, grouped=True)

    rows = bs * t
    xs = x_sample.reshape(rows, d)
    hs = _normmod(xs, p["g1"], mod_s, rows, 1)
    pg_s, qs, ks, _, vs, _, gates_s, small_s = _project(hs, p, rows)
    oa_s, s_s = _gla_sample(pg_s.reshape(bs, t, -1), small_s.reshape(bs, t, LANES),
                            state_gla.reshape(state_gla.shape[1:]), p["wall"], p["mask"],
                            p["w_a2"], p["b_a"], p["g_gla"], bs, t)
    lf_s, fn_s = _fox_bias_sample(small_s, p["bf_row"], t)
    n_pool = cache_k.shape[1]
    ob_s = _fox_sample(page_table, qs.astype(F32).reshape(bs, t, FOX_W), ks.reshape(bs, t, FOX_HEADS, FOX_DH),
                       vs.reshape(bs, t, FOX_HEADS, FOX_DH), fn_s.reshape(bs, t, FOX_HEADS),
                       cache_k, cache_v, cache_logf.reshape(n_pool, 1, PAGE_FLAT), t)
    y_s = _tail(xs, oa_s.reshape(rows, GLA_VW).astype(BF16), ob_s.reshape(rows, FOX_W).astype(BF16), gates_s, mod_s,
                p, rows, 1, grouped=False)

    return (y_p.reshape(bp, seq, d), y_s.reshape(bs, t, d),
            kb.reshape(1, bp, seq, FOX_HEADS, FOX_DH), vb.reshape(1, bp, seq, FOX_HEADS, FOX_DH),
            lf_p.reshape(1, bp, seq, FOX_HEADS), s_p[None],
            ks.reshape(1, bs, t, FOX_HEADS, FOX_DH), vs.reshape(1, bs, t, FOX_HEADS, FOX_DH),
            lf_s.reshape(1, bs, t, FOX_HEADS), s_s[None])
```

```python
import functools

import numpy as np
import jax
import jax.numpy as jnp
from jax import lax
from jax.experimental import pallas as pl
from jax.experimental.pallas import tpu as pltpu

F32 = jnp.float32
BF16 = jnp.bfloat16

D_MODEL = 2048
GLA_HEADS = 4
GLA_DK = 128
GLA_DV = 256
GLA_RANK = 16
GLA_TAU = 16.0
FOX_HEADS = 8
FOX_DH = 128
PAGE = 128
N_GROUPS = 4
EXP_PER_GROUP = 4
N_EXPERTS = 16
D_EXPERT = 512
RMS_EPS = 1e-6
GLA_KW = GLA_HEADS * GLA_DK
GLA_VW = GLA_HEADS * GLA_DV
FOX_W = FOX_HEADS * FOX_DH
GLA_CHUNK = 128
GLA_LEVELS = 7
LANES = 128
NEG = -1e30
VMEM_LIMIT = 56 * 1024 * 1024


def _cparams(sem):
    return pltpu.CompilerParams(dimension_semantics=sem, vmem_limit_bytes=VMEM_LIMIT)


def _dot(a, b):
    return jnp.dot(a, b, preferred_element_type=F32)


def _dot_nt(a, b):
    return lax.dot_general(a, b, (((1,), (1,)), ((), ())), preferred_element_type=F32)


def _dot_tn(a, b):
    return lax.dot_general(a, b, (((0,), (0,)), ((), ())), preferred_element_type=F32)


def _split2(x):
    hi = x.astype(BF16)
    lo = (x - hi.astype(F32)).astype(BF16)
    return hi, lo


def _split3(x):
    hi = x.astype(BF16)
    r = x - hi.astype(F32)
    mid = r.astype(BF16)
    lo = (r - mid.astype(F32)).astype(BF16)
    return hi, mid, lo


def _dot3(a, b):
    ah, al = _split2(a)
    bh, bl = _split2(b)
    return _dot(ah, bh) + _dot(ah, bl) + _dot(al, bh)


def _dot_sel(w01, x):
    hi, mid, lo = _split3(x)
    return _dot(w01, hi) + _dot(w01, mid) + _dot(w01, lo)


def _log_sigmoid(x):
    return jnp.minimum(x, 0.0) - jnp.log1p(jnp.exp(-jnp.abs(x)))


def _silu(x):
    return x * jax.nn.sigmoid(x)


def _adaln_kernel(c_ref, w_ref, b_ref, o_ref):
    o_ref[...] = _dot3(_silu(c_ref[...]), w_ref[...]) + b_ref[...]


def _adaln(c, w, b, tn=512):
    nb, d = c.shape
    n = w.shape[1]
    return pl.pallas_call(
        _adaln_kernel,
        grid=(n // tn,),
        in_specs=[pl.BlockSpec((nb, d), lambda j: (0, 0)),
                  pl.BlockSpec((d, tn), lambda j: (0, j)),
                  pl.BlockSpec((1, tn), lambda j: (0, j))],
        out_specs=pl.BlockSpec((nb, tn), lambda j: (0, j)),
        out_shape=jax.ShapeDtypeStruct((nb, n), F32),
        compiler_params=_cparams(("arbitrary",)),
        name="adaln",
    )(c, w, b)


def _rms(x):
    return x * lax.rsqrt(jnp.mean(x * x, axis=-1, keepdims=True) + RMS_EPS)


def _normmod_kernel(x_ref, g_ref, sh_ref, sc_ref, o_ref):
    y = _rms(x_ref[...]) * g_ref[...]
    o_ref[...] = (y * (1.0 + sc_ref[...]) + sh_ref[...]).astype(o_ref.dtype)


def _mod_spec(k, rb, d, tiles_per_batch):
    return pl.BlockSpec((None, None, rb, d), lambda i: (k, i // tiles_per_batch, 0, 0))


def _normmod(x, g, mod, tm, tiles_per_batch):
    m, d = x.shape
    rb = mod.shape[2]
    return pl.pallas_call(
        _normmod_kernel,
        grid=(m // tm,),
        in_specs=[pl.BlockSpec((tm, d), lambda i: (i, 0)),
                  pl.BlockSpec((1, d), lambda i: (0, 0)),
                  _mod_spec(0, rb, d, tiles_per_batch),
                  _mod_spec(1, rb, d, tiles_per_batch)],
        out_specs=pl.BlockSpec((tm, d), lambda i: (i, 0)),
        out_shape=jax.ShapeDtypeStruct((m, d), BF16),
        compiler_params=_cparams(("arbitrary",)),
        name="normmod",
    )(x, g, mod, mod)


def _mm_kernel(x_ref, w_ref, *o_refs):
    r = _dot(x_ref[...], w_ref[...])
    for o_ref in o_refs:
        o_ref[...] = r.astype(o_ref.dtype)


def _mm(x, w, col0, n, tm, tn, out_dtypes=(F32,)):
    m, k = x.shape
    j0 = col0 // tn
    outs = pl.pallas_call(
        _mm_kernel,
        grid=(n // tn, m // tm),
        in_specs=[pl.BlockSpec((tm, k), lambda j, i: (i, 0)),
                  pl.BlockSpec((k, tn), lambda j, i: (0, j0 + j))],
        out_specs=[pl.BlockSpec((tm, tn), lambda j, i: (i, j)) for _ in out_dtypes],
        out_shape=[jax.ShapeDtypeStruct((m, n), dt) for dt in out_dtypes],
        compiler_params=_cparams(("arbitrary", "arbitrary")),
        name="proj_mm",
    )(x, w)
    return outs


def _mm_heads_kernel(x_ref, w_ref, o_ref, o16_ref):
    r = _dot(x_ref[...], w_ref[...])
    for h in range(FOX_HEADS):
        o_ref[:, h, :] = r[:, h * FOX_DH:(h + 1) * FOX_DH]
    o16_ref[...] = r.astype(BF16)


def _mm_heads(x, w, col0, tm):
    m, k = x.shape
    j0 = col0 // FOX_W
    return pl.pallas_call(
        _mm_heads_kernel,
        grid=(m // tm,),
        in_specs=[pl.BlockSpec((tm, k), lambda i: (i, 0)),
                  pl.BlockSpec((k, FOX_W), lambda i: (0, j0))],
        out_specs=[pl.BlockSpec((tm, FOX_HEADS, FOX_DH), lambda i: (i, 0, 0)),
                   pl.BlockSpec((tm, FOX_W), lambda i: (i, 0))],
        out_shape=[jax.ShapeDtypeStruct((m, FOX_HEADS, FOX_DH), F32),
                   jax.ShapeDtypeStruct((m, FOX_W), BF16)],
        compiler_params=_cparams(("arbitrary",)),
        name="proj_heads",
    )(x, w)


def _gla_tables():
    c, p = GLA_CHUNK, GLA_LEVELS
    t = np.arange(c)[:, None]
    m = np.arange(c)[None, :]
    wall = np.zeros((p + 2, c, c), np.float32)
    mask = np.zeros((p + 1, c, c), np.float32)
    for l in range(p):
        half = 1 << l
        pos = t % (2 * half)
        mid = t - pos + half
        right = pos >= half
        wall[l] = np.where(right, (m >= mid) & (m <= t), (m > t) & (m < mid))
        s = m
        mask[l] = ((t >> (l + 1)) == (s >> (l + 1))) & (((t >> l) & 1) == 1) & (((s >> l) & 1) == 0)
    wall[p] = m <= t
    wall[p + 1] = m > t
    mask[p] = t == m
    return wall.reshape((p + 2) * c, c), mask


def _gla_body(q_ref, k_ref, v_ref, ra_ref, sm_ref, wall_ref, mask_ref, wa2_ref, ba_ref, g_ref,
              og_ref, st_ref, *, nchunk, valid, store_rows):
    c, p = GLA_CHUNK, GLA_LEVELS
    row = lax.broadcasted_iota(jnp.int32, (c, GLA_DK), 0)
    wall = wall_ref[...]
    for ci in range(nchunk):
        rows = pl.ds(ci * c, c)
        x = _dot3(sm_ref[rows, :], wa2_ref[...]) + ba_ref[...]
        la = _log_sigmoid(x) * (1.0 / GLA_TAU)
        if valid < c:
            rowh = lax.broadcasted_iota(jnp.int32, la.shape, 0)
            la = jnp.where(rowh < valid, la, 0.0)
        e_all = jnp.exp(_dot_sel(wall, la))
        for h in range(GLA_HEADS):
            ks = slice(h * GLA_DK, (h + 1) * GLA_DK)
            vs = slice(h * GLA_DV, (h + 1) * GLA_DV)
            q = q_ref[rows, ks] * (GLA_DK ** -0.5)
            k = k_ref[rows, ks]
            vb = v_ref[rows, vs].astype(BF16)
            a = mask_ref[p] * _dot_nt(q.astype(BF16), k.astype(BF16))
            for l in range(p):
                el = e_all[l * c:(l + 1) * c, ks]
                xl = (jnp.where(((row >> l) & 1) == 1, q, k) * el).astype(BF16)
                a = a + mask_ref[l] * _dot_nt(xl, xl)
            st = st_ref[h]
            qc = (q * e_all[p * c:(p + 1) * c, ks]).astype(BF16)
            o = _dot(a.astype(BF16), vb) + _dot_nt(qc, st.astype(BF16))
            kr = (k * e_all[(p + 1) * c:(p + 2) * c, ks]).astype(BF16)
            dec = e_all[(p + 1) * c - 1:(p + 1) * c, ks]
            st_ref[h] = dec * st + _dot_tn(vb, kr)
            og = _rms(o) * g_ref[...] * _silu(ra_ref[rows, vs])
            if store_rows < c:
                og_ref[:, vs] = og[:store_rows].astype(og_ref.dtype)
            else:
                og_ref[rows, vs] = og.astype(og_ref.dtype)


def _gla_prompt_kernel(q_ref, k_ref, v_ref, ra_ref, sm_ref, wall_ref, mask_ref, wa2_ref, ba_ref, g_ref, s0_ref,
                       og_ref, s_ref, st_ref, *, nchunk):
    n = pl.program_id(1)

    @pl.when(n == 0)
    def _():
        for h in range(GLA_HEADS):
            st_ref[h] = s0_ref[h].T

    _gla_body(q_ref, k_ref, v_ref, ra_ref, sm_ref, wall_ref, mask_ref, wa2_ref, ba_ref, g_ref, og_ref, st_ref,
              nchunk=nchunk, valid=GLA_CHUNK, store_rows=GLA_CHUNK)

    @pl.when(n == pl.num_programs(1) - 1)
    def _():
        for h in range(GLA_HEADS):
            s_ref[h] = st_ref[h].T


def _gla_prompt(pg, small, s0, wall, mask, wa2, ba, g, batch, seq, tb=256):
    nblk = seq // tb
    rowmap = lambda cb: (lambda b, n: (b * nblk + n, cb))
    const2 = lambda b, n: (0, 0)
    return pl.pallas_call(
        functools.partial(_gla_prompt_kernel, nchunk=tb // GLA_CHUNK),
        grid=(batch, nblk),
        in_specs=[pl.BlockSpec((tb, GLA_KW), rowmap(0)),
                  pl.BlockSpec((tb, GLA_KW), rowmap(1)),
                  pl.BlockSpec((tb, GLA_VW), rowmap(1)),
                  pl.BlockSpec((tb, GLA_VW), rowmap(2)),
                  pl.BlockSpec((tb, LANES), rowmap(0)),
                  pl.BlockSpec(wall.shape, const2),
                  pl.BlockSpec(mask.shape, lambda b, n: (0, 0, 0)),
                  pl.BlockSpec(wa2.shape, const2),
                  pl.BlockSpec(ba.shape, const2),
                  pl.BlockSpec(g.shape, const2),
                  pl.BlockSpec((None, GLA_HEADS, GLA_DK, GLA_DV), lambda b, n: (b, 0, 0, 0))],
        out_specs=[pl.BlockSpec((tb, GLA_VW), rowmap(0)),
                   pl.BlockSpec((None, GLA_HEADS, GLA_DK, GLA_DV), lambda b, n: (b, 0, 0, 0))],
        out_shape=[jax.ShapeDtypeStruct((batch * seq, GLA_VW), BF16),
                   jax.ShapeDtypeStruct((batch, GLA_HEADS, GLA_DK, GLA_DV), F32)],
        scratch_shapes=[pltpu.VMEM((GLA_HEADS, GLA_DV, GLA_DK), F32)],
        compiler_params=_cparams(("arbitrary", "arbitrary")),
        name="gla_prompt",
    )(pg, pg, pg, pg, small, wall, mask, wa2, ba, g, s0)


def _gla_sample_kernel(pg_ref, sm_ref, wall_ref, mask_ref, wa2_ref, ba_ref, g_ref, s0_ref,
                       og_ref, s_ref, pad_ref, smpad_ref, st_ref, *, t):
    @pl.when(pl.program_id(0) == 0)
    def _():
        pad_ref[...] = jnp.zeros(pad_ref.shape, F32)
        smpad_ref[...] = jnp.zeros(smpad_ref.shape, F32)

    pad_ref[0:t, :] = pg_ref[...]
    smpad_ref[0:t, :] = sm_ref[...]
    for h in range(GLA_HEADS):
        st_ref[h] = s0_ref[h].T
    q_ref = pad_ref.at[:, 0:GLA_KW]
    k_ref = pad_ref.at[:, GLA_KW:2 * GLA_KW]
    v_ref = pad_ref.at[:, 2 * GLA_KW:2 * GLA_KW + GLA_VW]
    ra_ref = pad_ref.at[:, 2 * GLA_KW + GLA_VW:2 * GLA_KW + 2 * GLA_VW]
    _gla_body(q_ref, k_ref, v_ref, ra_ref, smpad_ref, wall_ref, mask_ref, wa2_ref, ba_ref, g_ref, og_ref, st_ref,
              nchunk=1, valid=t, store_rows=t)
    for h in range(GLA_HEADS):
        s_ref[h] = st_ref[h].T


def _gla_sample(pg, small, s0, wall, mask, wa2, ba, g, batch, t):
    width = pg.shape[-1]
    const2 = lambda b: (0, 0)
    return pl.pallas_call(
        functools.partial(_gla_sample_kernel, t=t),
        grid=(batch,),
        in_specs=[pl.BlockSpec((None, t, width), lambda b: (b, 0, 0)),
                  pl.BlockSpec((None, t, LANES), lambda b: (b, 0, 0)),
                  pl.BlockSpec(wall.shape, const2),
                  pl.BlockSpec(mask.shape, lambda b: (0, 0, 0)),
                  pl.BlockSpec(wa2.shape, const2),
                  pl.BlockSpec(ba.shape, const2),
                  pl.BlockSpec(g.shape, const2),
                  pl.BlockSpec((None, GLA_HEADS, GLA_DK, GLA_DV), lambda b: (b, 0, 0, 0))],
        out_specs=[pl.BlockSpec((None, t, GLA_VW), lambda b: (b, 0, 0)),
                   pl.BlockSpec((None, GLA_HEADS, GLA_DK, GLA_DV), lambda b: (b, 0, 0, 0))],
        out_shape=[jax.ShapeDtypeStruct((batch, t, GLA_VW), F32),
                   jax.ShapeDtypeStruct((batch, GLA_HEADS, GLA_DK, GLA_DV), F32)],
        scratch_shapes=[pltpu.VMEM((GLA_CHUNK, width), F32),
                        pltpu.VMEM((GLA_CHUNK, LANES), F32),
                        pltpu.VMEM((GLA_HEADS, GLA_DV, GLA_DK), F32)],
        compiler_params=_cparams(("arbitrary",)),
        name="gla_sample",
    )(pg, small, wall, mask, wa2, ba, g, s0)


FB_LANE = GLA_RANK


def _fox_bias_prompt_kernel(sm_ref, bf_ref, tri_ref, lf_ref, fc_ref, ft_ref, carry_ref):
    @pl.when(pl.program_id(1) == 0)
    def _():
        carry_ref[...] = jnp.zeros(carry_ref.shape, F32)

    lf = _log_sigmoid(sm_ref[...] + bf_ref[...])
    lf_ref[...] = lf[:, FB_LANE:FB_LANE + FOX_HEADS]
    cum = _dot_sel(tri_ref[...], lf) + carry_ref[...]
    carry_ref[...] = cum[cum.shape[0] - 1:, :]
    fc_ref[...] = cum
    ft_ref[...] = cum.T[FB_LANE:FB_LANE + FOX_HEADS, :]


def _fox_bias_prompt(small, bf_row, batch, seq, tb=256):
    nblk = seq // tb
    tri = jnp.asarray(np.tril(np.ones((tb, tb), np.float32)), BF16)
    return pl.pallas_call(
        _fox_bias_prompt_kernel,
        grid=(batch, nblk),
        in_specs=[pl.BlockSpec((tb, LANES), lambda b, n: (b * nblk + n, 0)),
                  pl.BlockSpec((1, LANES), lambda b, n: (0, 0)),
                  pl.BlockSpec((tb, tb), lambda b, n: (0, 0))],
        out_specs=[pl.BlockSpec((tb, FOX_HEADS), lambda b, n: (b * nblk + n, 0)),
                   pl.BlockSpec((tb, LANES), lambda b, n: (b * nblk + n, 0)),
                   pl.BlockSpec((None, FOX_HEADS, tb), lambda b, n: (b, 0, n))],
        out_shape=[jax.ShapeDtypeStruct((batch * seq, FOX_HEADS), F32),
                   jax.ShapeDtypeStruct((batch * seq, LANES), F32),
                   jax.ShapeDtypeStruct((batch, FOX_HEADS, seq), F32)],
        scratch_shapes=[pltpu.VMEM((1, LANES), F32)],
        compiler_params=_cparams(("arbitrary", "arbitrary")),
        name="fox_bias_prompt",
    )(small, bf_row, tri)


def _fox_bias_sample_kernel(sm_ref, bf_ref, sel_ref, lf_ref, fn_ref):
    lf = _log_sigmoid(sm_ref[...] + bf_ref[...])
    lf_ref[...] = lf[:, FB_LANE:FB_LANE + FOX_HEADS]
    cum = _dot_sel(sel_ref[...], lf)
    fn_ref[...] = cum[:, FB_LANE:FB_LANE + FOX_HEADS]


def _fox_bias_sample(small, bf_row, t):
    rows = small.shape[0]
    r = np.arange(rows)
    sel = ((r[:, None] // t) == (r[None, :] // t)) & (r[None, :] <= r[:, None])
    sel = jnp.asarray(sel.astype(np.float32), BF16)
    full = lambda shape: pl.BlockSpec(shape, lambda i: tuple(0 for _ in shape))
    return pl.pallas_call(
        _fox_bias_sample_kernel,
        grid=(1,),
        in_specs=[full((rows, LANES)), full((1, LANES)), full((rows, rows))],
        out_specs=[full((rows, FOX_HEADS)), full((rows, FOX_HEADS))],
        out_shape=[jax.ShapeDtypeStruct((rows, FOX_HEADS), F32),
                   jax.ShapeDtypeStruct((rows, FOX_HEADS), F32)],
        compiler_params=_cparams(("arbitrary",)),
        name="fox_bias_sample",
    )(small, bf_row, sel)


FOX_HEADS_PER_STEP = 4


def _fox_prompt_kernel(q_ref, k_ref, v_ref, fc_ref, ft_ref, o_ref, *, tq, tk):
    hb = FOX_HEADS_PER_STEP
    hg = pl.program_id(1)
    i = pl.program_id(2)
    lane = lax.broadcasted_iota(jnp.int32, (tq, LANES), 1)
    rowi = lax.broadcasted_iota(jnp.int32, (tq, tk), 0)
    coli = lax.broadcasted_iota(jnp.int32, (tq, tk), 1)
    fc = fc_ref[...]
    qs, f_ts = [], []
    for hh in range(hb):
        hs = slice(hh * FOX_DH, (hh + 1) * FOX_DH)
        qs.append((q_ref[:, hs].astype(F32) * (FOX_DH ** -0.5)).astype(BF16))
        f_ts.append(jnp.sum(jnp.where(lane == hg * hb + hh + FB_LANE, fc, 0.0), axis=-1, keepdims=True))

    def step(j, carry, masked):
        ks = pl.ds(pl.multiple_of(j * tk, tk), tk)
        out = []
        for hh in range(hb):
            hs = slice(hh * FOX_DH, (hh + 1) * FOX_DH)
            m, l, acc = carry[hh]
            s = _dot_nt(qs[hh], k_ref[ks, hs])
            s = s + f_ts[hh] - ft_ref[hh:hh + 1, ks]
            if masked:
                s = jnp.where(rowi >= coli, s, NEG)
            m_new = jnp.maximum(m, jnp.max(s, axis=-1, keepdims=True))
            p = jnp.exp(s - m_new)
            alpha = jnp.exp(m - m_new)
            l = alpha * l + jnp.sum(p, axis=-1, keepdims=True)
            acc = alpha * acc + _dot(p.astype(BF16), v_ref[ks, hs])
            out.append((m_new, l, acc))
        return tuple(out)

    init = tuple((jnp.full((tq, 1), NEG, F32), jnp.zeros((tq, 1), F32), jnp.zeros((tq, FOX_DH), F32))
                 for _ in range(hb))
    nfull = i * (tq // tk)
    carry = lax.fori_loop(0, nfull, lambda j, c: step(j, c, False), init)
    carry = step(nfull, carry, True)
    for hh in range(hb):
        m, l, acc = carry[hh]
        o_ref[:, hh * FOX_DH:(hh + 1) * FOX_DH] = (acc / l).astype(o_ref.dtype)


def _fox_prompt(qb, kb, vb, fcol, ft, batch, seq, tq=256):
    nq = seq // tq
    hb = FOX_HEADS_PER_STEP
    wb = hb * FOX_DH
    return pl.pallas_call(
        functools.partial(_fox_prompt_kernel, tq=tq, tk=tq),
        grid=(batch, FOX_HEADS // hb, nq),
        in_specs=[pl.BlockSpec((tq, wb), lambda b, h, i: (b * nq + i, h)),
                  pl.BlockSpec((seq, wb), lambda b, h, i: (b, h)),
                  pl.BlockSpec((seq, wb), lambda b, h, i: (b, h)),
                  pl.BlockSpec((tq, LANES), lambda b, h, i: (b * nq + i, 0)),
                  pl.BlockSpec((None, None, hb, seq), lambda b, h, i: (b, h, 0, 0))],
        out_specs=pl.BlockSpec((tq, wb), lambda b, h, i: (b * nq + i, h)),
        out_shape=jax.ShapeDtypeStruct((batch * seq, FOX_W), BF16),
        compiler_params=_cparams(("arbitrary", "arbitrary", "arbitrary")),
        name="fox_prompt",
    )(qb, kb, vb, fcol, ft.reshape(batch, FOX_HEADS // hb, hb, seq))


PAGES_PER_STEP = 8
ROWS8 = 8
QROWS = FOX_HEADS * ROWS8
PAGE_FLAT = PAGE * FOX_HEADS


def _dot_sel_rhs(x, w01):
    hi, mid, lo = _split3(x)
    m = x.shape[0]
    stacked = jnp.concatenate([hi.astype(F32), mid.astype(F32), lo.astype(F32)], axis=0).astype(BF16)
    r = _dot(stacked, w01)
    return r[0:m] + r[m:2 * m] + r[2 * m:3 * m]


def _fox_sample_kernel(pt_ref, q_ref, kn_ref, vn_ref, fn_ref, fnrow_ref, madd_ref, maddn_ref, usuf_ref, tot_ref,
                       *refs, t):
    g = PAGES_PER_STEP
    k_refs, v_refs, lf_refs = refs[0:g], refs[g:2 * g], refs[2 * g:3 * g]
    o_ref = refs[3 * g]
    q_sc, m_sc, l_sc, acc_sc, carry_sc = refs[3 * g + 1:]
    j = pl.program_id(1)
    fn_t = fn_ref[...][:, 0:1]

    @pl.when(j == 0)
    def _():
        q_sc[...] = jnp.zeros(q_sc.shape, F32)
        for h in range(FOX_HEADS):
            q_sc[h * ROWS8:h * ROWS8 + t, :] = q_ref[:, h * FOX_DH:(h + 1) * FOX_DH] * (FOX_DH ** -0.5)
        carry_sc[...] = jnp.zeros(carry_sc.shape, F32)
        pad = jnp.zeros((LANES - t * FOX_HEADS, FOX_DH), F32)
        kn = jnp.concatenate([kn_ref[...].reshape(t * FOX_HEADS, FOX_DH), pad], axis=0)
        vn = jnp.concatenate([vn_ref[...].reshape(t * FOX_HEADS, FOX_DH), pad], axis=0)
        s = _dot_nt(q_sc[...], kn) + fn_t - fnrow_ref[...] + maddn_ref[...]
        m = jnp.max(s, axis=-1, keepdims=True)
        p = jnp.exp(s - m)
        m_sc[...] = jnp.broadcast_to(m, m_sc.shape)
        l_sc[...] = jnp.broadcast_to(jnp.sum(p, axis=-1, keepdims=True), l_sc.shape)
        acc_sc[...] = _dot(p, vn)

    lf = jnp.concatenate([lf_refs[gi][...] for gi in range(g)], axis=0)
    r_in = _dot_sel_rhs(lf, usuf_ref[...])
    page_tot = _dot_sel_rhs((r_in + lf)[:, 0:LANES], tot_ref[...])
    carry = carry_sc[...]
    q = q_sc[...]
    madd = madd_ref[...] + fn_t
    m_old = m_sc[...]
    m_new = m_old
    s_list = []
    for gi in range(g):
        bias = madd + (r_in[gi:gi + 1, :] + carry)
        carry = carry + page_tot[gi:gi + 1, :]
        s = _dot_nt(q, k_refs[gi][...].reshape(PAGE_FLAT, FOX_DH)) + bias
        s_list.append(s)
        m_new = jnp.maximum(m_new, jnp.max(s, axis=-1, keepdims=True))
    carry_sc[...] = carry
    alpha = jnp.exp(m_old - m_new)
    l = alpha * l_sc[...]
    acc = alpha * acc_sc[...]
    m_col = m_new[:, 0:1]
    for gi in range(g):
        p = jnp.exp(s_list[gi] - m_col)
        l = l + jnp.sum(p, axis=-1, keepdims=True)
        acc = acc + _dot(p, v_refs[gi][...].reshape(PAGE_FLAT, FOX_DH))
    m_sc[...] = m_new
    l_sc[...] = l
    acc_sc[...] = acc

    @pl.when(j == pl.num_programs(1) - 1)
    def _():
        o = acc / l
        for h in range(FOX_HEADS):
            o_ref[:, h * FOX_DH:(h + 1) * FOX_DH] = o[h * ROWS8:h * ROWS8 + t, :].astype(o_ref.dtype)


def _fox_sample_tables(t):
    row = np.arange(QROWS)[:, None]
    col = np.arange(PAGE_FLAT)[None, :]
    madd = np.where((row // ROWS8) == (col % FOX_HEADS), 0.0, NEG).astype(np.float32)
    coln = np.arange(LANES)[None, :]
    ok = (coln < t * FOX_HEADS) & ((row // ROWS8) == (coln % FOX_HEADS)) & ((coln // FOX_HEADS) <= (row % ROWS8))
    maddn = np.where(ok, 0.0, NEG).astype(np.float32)
    src = np.arange(PAGE_FLAT)[:, None]
    same_head = (src % FOX_HEADS) == (col % FOX_HEADS)
    usuf = (same_head & ((src // FOX_HEADS) > (col // FOX_HEADS))).astype(np.float32)
    lane = np.arange(LANES)[:, None]
    tot = ((lane < FOX_HEADS) & (lane == (col % FOX_HEADS))).astype(np.float32)
    return (jnp.asarray(madd), jnp.asarray(maddn), jnp.asarray(usuf, BF16), jnp.asarray(tot, BF16))


def _fox_sample(page_table, q, kn, vn, fn, cache_k, cache_v, cache_lf, t):
    batch, npages = page_table.shape
    g = PAGES_PER_STEP
    nsteps = npages // g
    madd, maddn, usuf, tot = _fox_sample_tables(t)
    fn_rows = jnp.pad(fn.transpose(0, 2, 1), ((0, 0), (0, 0), (0, ROWS8 - t))).reshape(batch, QROWS, 1)
    fn_rows = jnp.broadcast_to(fn_rows, (batch, QROWS, LANES))
    fn_cols = jnp.pad(fn.reshape(batch, 1, t * FOX_HEADS), ((0, 0), (0, 0), (0, LANES - t * FOX_HEADS)))

    def page_map5(gi):
        return lambda b, j, pt: (0, pt[b, npages - 1 - (j * g + gi)], 0, 0, 0)

    def page_map3(gi):
        return lambda b, j, pt: (pt[b, npages - 1 - (j * g + gi)], 0, 0)

    seq3 = lambda b, j, pt: (b, 0, 0)
    seq4 = lambda b, j, pt: (b, 0, 0, 0)
    const2 = lambda b, j, pt: (0, 0)
    in_specs = [pl.BlockSpec((None, t, FOX_W), seq3),
                pl.BlockSpec((None, t, FOX_HEADS, FOX_DH), seq4),
                pl.BlockSpec((None, t, FOX_HEADS, FOX_DH), seq4),
                pl.BlockSpec((None, QROWS, LANES), seq3),
                pl.BlockSpec((None, 1, LANES), seq3),
                pl.BlockSpec(madd.shape, const2),
                pl.BlockSpec(maddn.shape, const2),
                pl.BlockSpec(usuf.shape, const2),
                pl.BlockSpec(tot.shape, const2)]
    in_specs += [pl.BlockSpec((None, None, PAGE, FOX_HEADS, FOX_DH), page_map5(gi)) for gi in range(g)]
    in_specs += [pl.BlockSpec((None, None, PAGE, FOX_HEADS, FOX_DH), page_map5(gi)) for gi in range(g)]
    in_specs += [pl.BlockSpec((None, 1, PAGE_FLAT), page_map3(gi)) for gi in range(g)]
    grid_spec = pltpu.PrefetchScalarGridSpec(
        num_scalar_prefetch=1,
        grid=(batch, nsteps),
        in_specs=in_specs,
        out_specs=pl.BlockSpec((None, t, FOX_W), seq3),
        scratch_shapes=[pltpu.VMEM((QROWS, FOX_DH), F32),
                        pltpu.VMEM((QROWS, LANES), F32),
                        pltpu.VMEM((QROWS, LANES), F32),
                        pltpu.VMEM((QROWS, FOX_DH), F32),
                        pltpu.VMEM((1, PAGE_FLAT), F32)],
    )
    return pl.pallas_call(
        functools.partial(_fox_sample_kernel, t=t),
        grid_spec=grid_spec,
        out_shape=jax.ShapeDtypeStruct((batch, t, FOX_W), F32),
        compiler_params=_cparams(("arbitrary", "arbitrary")),
        name="fox_sample",
    )(page_table, q, kn, vn, fn_rows, fn_cols, madd, maddn, usuf, tot,
      *([cache_k] * g), *([cache_v] * g), *([cache_lf] * g))


def _merge_kernel(oa_ref, ob_ref, wa_ref, wb_ref, ga_ref, gb_ref, o_ref):
    ua = _dot(oa_ref[...], wa_ref[...])
    ub = _dot(ob_ref[...], wb_ref[...])
    o_ref[...] = (jax.nn.sigmoid(ga_ref[...]) * ua + jax.nn.sigmoid(gb_ref[...]) * ub).astype(o_ref.dtype)


def _merge(oa, ob, wa, wb, gates, tm, tn=1024):
    m = oa.shape[0]
    d = wa.shape[1]
    nj = d // tn
    return pl.pallas_call(
        _merge_kernel,
        grid=(nj, m // tm),
        in_specs=[pl.BlockSpec((tm, GLA_VW), lambda j, i: (i, 0)),
                  pl.BlockSpec((tm, FOX_W), lambda j, i: (i, 0)),
                  pl.BlockSpec((GLA_VW, tn), lambda j, i: (0, j)),
                  pl.BlockSpec((FOX_W, tn), lambda j, i: (0, j)),
                  pl.BlockSpec((tm, tn), lambda j, i: (i, j)),
                  pl.BlockSpec((tm, tn), lambda j, i: (i, nj + j))],
        out_specs=pl.BlockSpec((tm, tn), lambda j, i: (i, j)),
        out_shape=jax.ShapeDtypeStruct((m, d), BF16),
        compiler_params=_cparams(("arbitrary", "arbitrary")),
        name="merge",
    )(oa, ob, wa, wb, gates, gates)


ROUTER_GROUP_LANE = N_EXPERTS


def _route(logits):
    lane_i = lax.broadcasted_iota(jnp.int32, logits.shape, 1)
    lane = lane_i.astype(F32)
    grp_of_lane = (lane_i >> 2).astype(F32)
    big = float(LANES)
    is_grp = (lane_i >= ROUTER_GROUP_LANE) & (lane_i < ROUTER_GROUP_LANE + N_GROUPS)
    gl = jnp.where(is_grp, logits, NEG)
    gmax = jnp.max(gl, axis=-1, keepdims=True)
    g_idx = jnp.min(jnp.where(is_grp & (gl == gmax), lane - ROUTER_GROUP_LANE, big), axis=-1, keepdims=True)
    g_w = 1.0 / jnp.sum(jnp.where(is_grp, jnp.exp(gl - gmax), 0.0), axis=-1, keepdims=True)
    in_grp = (lane_i < N_EXPERTS) & (grp_of_lane == g_idx)
    e1 = jnp.where(in_grp, logits, NEG)
    v1 = jnp.max(e1, axis=-1, keepdims=True)
    i1 = jnp.min(jnp.where(in_grp & (e1 == v1), lane, big), axis=-1, keepdims=True)
    rest = in_grp & (lane != i1)
    e2 = jnp.where(rest, logits, NEG)
    v2 = jnp.max(e2, axis=-1, keepdims=True)
    i2 = jnp.min(jnp.where(rest & (e2 == v2), lane, big), axis=-1, keepdims=True)
    r = jnp.exp(v2 - v1)
    w1 = g_w / (1.0 + r)
    w2 = g_w * r / (1.0 + r)
    grp_onehot = jnp.where(is_grp & (lane - ROUTER_GROUP_LANE == g_idx), 1.0, 0.0)
    return jnp.where(lane == i1, w1, 0.0) + jnp.where(lane == i2, w2, 0.0) + grp_onehot


HX_W = D_MODEL + LANES


def _outproj_kernel(mg_ref, w_ref, x_ref, gt_ref, g2_ref, sh_ref, sc_ref, wr_ref, br_ref, x1_ref, hx_ref):
    x1 = x_ref[...] + gt_ref[...] * _dot(mg_ref[...], w_ref[...])
    x1_ref[...] = x1
    h2 = _rms(x1) * g2_ref[...] * (1.0 + sc_ref[...]) + sh_ref[...]
    hx_ref[:, 0:D_MODEL] = h2
    hx_ref[:, D_MODEL:HX_W] = _route(_dot3(h2, wr_ref[...]) + br_ref[...])


def _outproj(merged, w_out, x, mod, g2, w_router, b_router, tm, tiles_per_batch):
    m, d = x.shape
    rb = mod.shape[2]
    const2 = lambda i: (0, 0)
    return pl.pallas_call(
        _outproj_kernel,
        grid=(m // tm,),
        in_specs=[pl.BlockSpec((tm, d), lambda i: (i, 0)),
                  pl.BlockSpec((d, d), const2),
                  pl.BlockSpec((tm, d), lambda i: (i, 0)),
                  _mod_spec(2, rb, d, tiles_per_batch),
                  pl.BlockSpec((1, d), const2),
                  _mod_spec(3, rb, d, tiles_per_batch),
                  _mod_spec(4, rb, d, tiles_per_batch),
                  pl.BlockSpec((d, LANES), const2),
                  pl.BlockSpec((1, LANES), const2)],
        out_specs=[pl.BlockSpec((tm, d), lambda i: (i, 0)),
                   pl.BlockSpec((tm, HX_W), lambda i: (i, 0))],
        out_shape=[jax.ShapeDtypeStruct((m, d), F32),
                   jax.ShapeDtypeStruct((m, HX_W), F32)],
        compiler_params=_cparams(("arbitrary",)),
        name="outproj",
    )(merged, w_out, x, mod, g2, mod, mod, w_router, b_router)


MOE_TILE = 512
PLAN_TILE = 512


def _moe_plan_kernel(r_ref, tri_ref, excl_ref, pos_ref, tg_ref, nu_ref, cnt_sc, off_sc, run_sc):
    ph = pl.program_id(0)
    n = pl.program_id(1)
    lane = lax.broadcasted_iota(jnp.int32, (1, LANES), 1)
    lane_t = lax.broadcasted_iota(jnp.int32, r_ref.shape, 1)
    is_grp = (lane_t >= ROUTER_GROUP_LANE) & (lane_t < ROUTER_GROUP_LANE + N_GROUPS)
    g4 = jnp.where(is_grp, r_ref[...], 0.0)

    @pl.when((ph == 0) & (n == 0))
    def _():
        cnt_sc[...] = jnp.zeros(cnt_sc.shape, F32)

    @pl.when(ph == 0)
    def _():
        cnt_sc[...] += jnp.sum(g4, axis=0, keepdims=True)

    @pl.when((ph == 1) & (n == 0))
    def _():
        padded = jnp.floor((cnt_sc[...] + (MOE_TILE - 1)) * (1.0 / MOE_TILE)) * MOE_TILE
        off = _dot_sel_rhs(jnp.broadcast_to(padded, (8, LANES)), excl_ref[...])[0:1]
        off_sc[...] = off
        run_sc[...] = jnp.zeros(run_sc.shape, F32)
        end = off + padded
        tile_start = lane.astype(F32) * MOE_TILE
        tg = jnp.zeros((1, LANES), F32)
        for g in range(N_GROUPS):
            end_g = jnp.sum(jnp.where(lane == ROUTER_GROUP_LANE + g, end, 0.0), axis=-1, keepdims=True)
            tg = tg + jnp.where(end_g <= tile_start, 1.0, 0.0)
        tg_ref[...] = jnp.minimum(tg, N_GROUPS - 1.0).astype(jnp.int32)
        total = jnp.sum(jnp.where(lane == ROUTER_GROUP_LANE + N_GROUPS - 1, end, 0.0), axis=-1, keepdims=True)
        nu_ref[...] = jnp.broadcast_to(total * (1.0 / MOE_TILE), (1, LANES)).astype(jnp.int32)

    @pl.when(ph == 1)
    def _():
        rank = _dot(tri_ref[...], g4.astype(BF16)) + run_sc[...]
        run_sc[...] += jnp.sum(g4, axis=0, keepdims=True)
        posv = g4 * (off_sc[...] + rank)
        hi, mid, lo = _split3(posv)
        ones = jnp.ones((8, LANES), BF16)
        row = _dot_nt(ones, hi) + _dot_nt(ones, mid) + _dot_nt(ones, lo)
        pos_ref[...] = row[0:1].astype(jnp.int32)


def _moe_plan(hx, ntiles):
    t = hx.shape[0]
    nblk = t // PLAN_TILE
    tri = jnp.asarray(np.tril(np.ones((PLAN_TILE, PLAN_TILE), np.float32), -1), BF16)
    excl = jnp.asarray(np.triu(np.ones((LANES, LANES), np.float32), 1), BF16)
    assert ntiles <= LANES
    pos, tg, nu = pl.pallas_call(
        _moe_plan_kernel,
        grid=(2, nblk),
        in_specs=[pl.BlockSpec((PLAN_TILE, LANES), lambda ph, n: (n, D_MODEL // LANES)),
                  pl.BlockSpec((PLAN_TILE, PLAN_TILE), lambda ph, n: (0, 0)),
                  pl.BlockSpec((LANES, LANES), lambda ph, n: (0, 0))],
        out_specs=[pl.BlockSpec((1, PLAN_TILE), lambda ph, n: (0, n * ph)),
                   pl.BlockSpec((1, LANES), lambda ph, n: (0, 0)),
                   pl.BlockSpec((1, LANES), lambda ph, n: (0, 0))],
        out_shape=[jax.ShapeDtypeStruct((1, t), jnp.int32),
                   jax.ShapeDtypeStruct((1, LANES), jnp.int32),
                   jax.ShapeDtypeStruct((1, LANES), jnp.int32)],
        scratch_shapes=[pltpu.VMEM((1, LANES), F32), pltpu.VMEM((1, LANES), F32), pltpu.VMEM((1, LANES), F32)],
        compiler_params=_cparams(("arbitrary", "arbitrary")),
        name="moe_plan",
    )(hx, tri, excl)
    return pos.reshape(t), tg.reshape(LANES), nu.reshape(LANES)[0:1]


def _row_copy(src_ref, src_row, dst_ref, dst_row, sem):
    return pltpu.make_async_copy(src_ref.at[pl.ds(src_row, 1)], dst_ref.at[pl.ds(dst_row, 1)], sem)


def _moe_grouped_kernel(pos_ref, tg_ref, nu_ref, hx_ref, wg_ref, wu_ref, wd_ref, ys_ref,
                        buf, sem, src_sc, xb_sc, acc_sc, *, t):
    i = pl.program_id(0)
    ei = pl.program_id(1)
    n_used = nu_ref[0]
    used = i < n_used
    slot = i % 2

    def fetch(tile, sl):
        def body(r, c):
            _row_copy(hx_ref, src_sc[tile * MOE_TILE + r], buf.at[sl], r, sem.at[sl]).start()
            return c
        lax.fori_loop(0, MOE_TILE, body, 0, unroll=8)

    @pl.when((i == 0) & (ei == 0))
    def _():
        def clear(r, c):
            src_sc[r] = 0
            return c

        def invert(tk, c):
            src_sc[pos_ref[tk]] = tk
            return c

        lax.fori_loop(0, src_sc.shape[0], clear, 0, unroll=8)
        lax.fori_loop(0, t, invert, 0, unroll=8)
        fetch(0, 0)

    @pl.when((ei == 0) & (i + 1 < n_used))
    def _():
        fetch(i + 1, (i + 1) % 2)

    @pl.when(used & (ei == 0))
    def _():
        def wait(r, c):
            _row_copy(hx_ref, 0, buf.at[slot], r, sem.at[slot]).wait()
            return c
        lax.fori_loop(0, MOE_TILE, wait, 0, unroll=8)
        xb_sc[...] = buf[slot, :, 0:D_MODEL].astype(BF16)
        acc_sc[...] = jnp.zeros(acc_sc.shape, F32)

    @pl.when(used)
    def _():
        h = xb_sc[...]
        a = _dot(h, wg_ref[...])
        u = _dot(h, wu_ref[...])
        lane = lax.broadcasted_iota(jnp.int32, (MOE_TILE, LANES), 1)
        e = tg_ref[i] * EXP_PER_GROUP + ei
        cw = jnp.sum(jnp.where(lane == e, buf[slot, :, D_MODEL:HX_W], 0.0), axis=-1, keepdims=True)
        hid = (_silu(a) * u * cw).astype(BF16)
        acc_sc[...] += _dot(hid, wd_ref[...])

    @pl.when(ei == pl.num_programs(1) - 1)
    def _():
        ys_ref[...] = jnp.where(used, acc_sc[...], 0.0)


def _moe_grouped(hx, pos, tg, nu, wg, wu, wd, rows_out):
    t = hx.shape[0]
    ne, d, de = wg.shape
    ntiles = rows_out // MOE_TILE

    def wmap(i, ei, pos_ref, tg_ref, nu_ref):
        e = jnp.where(i < nu_ref[0], tg_ref[i] * EXP_PER_GROUP + ei, ne - 1)
        return (e, 0, 0)

    grid_spec = pltpu.PrefetchScalarGridSpec(
        num_scalar_prefetch=3,
        grid=(ntiles, EXP_PER_GROUP),
        in_specs=[pl.BlockSpec(memory_space=pl.ANY),
                  pl.BlockSpec((None, d, de), wmap),
                  pl.BlockSpec((None, d, de), wmap),
                  pl.BlockSpec((None, de, d), wmap)],
        out_specs=pl.BlockSpec((MOE_TILE, d), lambda i, ei, pos_ref, tg_ref, nu_ref: (i, 0)),
        scratch_shapes=[pltpu.VMEM((2, MOE_TILE, HX_W), F32), pltpu.SemaphoreType.DMA((2,)),
                        pltpu.SMEM((rows_out,), jnp.int32),
                        pltpu.VMEM((MOE_TILE, d), BF16), pltpu.VMEM((MOE_TILE, d), F32)],
    )
    return pl.pallas_call(
        functools.partial(_moe_grouped_kernel, t=t),
        grid_spec=grid_spec,
        out_shape=jax.ShapeDtypeStruct((rows_out, d), F32),
        compiler_params=_cparams(("arbitrary", "arbitrary")),
        name="moe_grouped",
    )(pos, tg, nu, hx, wg, wu, wd)


COMBINE_ROWS = 256


def _moe_combine_kernel(pos_ref, ys_ref, x1_ref, gt_ref, gf_ref, y_ref, buf, sem):
    i = pl.program_id(0)
    n = pl.num_programs(0)

    def fetch(tile, slot):
        def body(r, c):
            _row_copy(ys_ref, pos_ref[tile * COMBINE_ROWS + r], buf.at[slot], r, sem.at[slot]).start()
            return c
        lax.fori_loop(0, COMBINE_ROWS, body, 0, unroll=8)

    @pl.when(i == 0)
    def _():
        fetch(0, 0)

    @pl.when(i + 1 < n)
    def _():
        fetch(i + 1, (i + 1) % 2)

    slot = i % 2

    def wait(r, c):
        _row_copy(ys_ref, pos_ref[i * COMBINE_ROWS + r], buf.at[slot], r, sem.at[slot]).wait()
        return c

    lax.fori_loop(0, COMBINE_ROWS, wait, 0, unroll=8)
    x2 = x1_ref[...] + gt_ref[...] * buf[slot]
    y_ref[...] = _rms(x2) * gf_ref[...]


def _moe_combine(ys, pos, x1, mod, g_final, tiles_per_batch):
    m, d = x1.shape
    rb = mod.shape[2]
    grid_spec = pltpu.PrefetchScalarGridSpec(
        num_scalar_prefetch=1,
        grid=(m // COMBINE_ROWS,),
        in_specs=[pl.BlockSpec(memory_space=pl.ANY),
                  pl.BlockSpec((COMBINE_ROWS, d), lambda i, pos_ref: (i, 0)),
                  pl.BlockSpec((None, None, rb, d), lambda i, pos_ref: (5, i // tiles_per_batch, 0, 0)),
                  pl.BlockSpec((1, d), lambda i, pos_ref: (0, 0))],
        out_specs=pl.BlockSpec((COMBINE_ROWS, d), lambda i, pos_ref: (i, 0)),
        scratch_shapes=[pltpu.VMEM((2, COMBINE_ROWS, d), F32), pltpu.SemaphoreType.DMA((2,))],
    )
    return pl.pallas_call(
        _moe_combine_kernel,
        grid_spec=grid_spec,
        out_shape=jax.ShapeDtypeStruct((m, d), F32),
        compiler_params=_cparams(("arbitrary",)),
        name="moe_combine",
    )(pos, ys, x1, mod, g_final)


def _moe_kernel(h_ref, cmb_ref, wg_ref, wu_ref, wd_ref, x1_ref, gt_ref, gf_ref, y_ref, acc_ref):
    e = pl.program_id(1)

    @pl.when(e == 0)
    def _():
        acc_ref[...] = jnp.zeros(acc_ref.shape, F32)

    h = h_ref[...]
    a = _dot(h, wg_ref[...])
    u = _dot(h, wu_ref[...])
    lane = lax.broadcasted_iota(jnp.int32, cmb_ref.shape, 1)
    cw = jnp.sum(jnp.where(lane == e, cmb_ref[...], 0.0), axis=-1, keepdims=True)
    hid = (_silu(a) * u * cw).astype(BF16)
    acc_ref[...] += _dot(hid, wd_ref[...])

    @pl.when(e == pl.num_programs(1) - 1)
    def _():
        x2 = x1_ref[...] + gt_ref[...] * acc_ref[...]
        y_ref[...] = _rms(x2) * gf_ref[...]


def _moe(h2, cmb, wg, wu, wd, x1, mod, g_final, tm, tiles_per_batch):
    m, d = x1.shape
    rb = mod.shape[2]
    ne, _, de = wg.shape
    return pl.pallas_call(
        _moe_kernel,
        grid=(m // tm, ne),
        in_specs=[pl.BlockSpec((tm, d), lambda i, e: (i, 0)),
                  pl.BlockSpec((tm, LANES), lambda i, e: (i, 0)),
                  pl.BlockSpec((None, d, de), lambda i, e: (e, 0, 0)),
                  pl.BlockSpec((None, d, de), lambda i, e: (e, 0, 0)),
                  pl.BlockSpec((None, de, d), lambda i, e: (e, 0, 0)),
                  pl.BlockSpec((tm, d), lambda i, e: (i, 0)),
                  pl.BlockSpec((None, None, rb, d), lambda i, e: (5, i // tiles_per_batch, 0, 0)),
                  pl.BlockSpec((1, d), lambda i, e: (0, 0))],
        out_specs=pl.BlockSpec((tm, d), lambda i, e: (i, 0)),
        out_shape=jax.ShapeDtypeStruct((m, d), F32),
        scratch_shapes=[pltpu.VMEM((tm, d), F32)],
        compiler_params=_cparams(("arbitrary", "arbitrary")),
        name="moe",
    )(h2, cmb, wg, wu, wd, x1, mod, g_final)


def _prep_weights(w_ada, b_ada, g_norm1, g_norm2, g_final, w_in, w_a2, b_a, b_f, g_gla_norm, w_up_a, w_up_b, w_out,
                  w_grp, b_grp, w_exp, b_exp, w_gate_e, w_up_e, w_down_e):
    w = w_in.reshape(w_in.shape[1:])
    d = D_MODEL
    o_lra = 2 * GLA_KW + 2 * GLA_VW
    o_fox = o_lra + GLA_RANK
    o_fb = o_fox + 3 * FOX_W
    o_g = o_fb + FOX_HEADS
    pad = jnp.zeros((d, LANES - GLA_RANK - FOX_HEADS), F32)
    wall, mask = _gla_tables()
    bf_row = jnp.zeros((1, LANES), F32).at[0, FB_LANE:FB_LANE + FOX_HEADS].set(b_f[0])
    w_router = jnp.concatenate([w_exp[0], w_grp[0], jnp.zeros((d, LANES - N_EXPERTS - N_GROUPS), F32)], axis=1)
    b_router = jnp.concatenate([b_exp[0], b_grp[0], jnp.zeros((LANES - N_EXPERTS - N_GROUPS,), F32)])[None, :]
    return dict(
        w_ada=w_ada.reshape(w_ada.shape[1:]), b_ada=b_ada[0][None, :],
        g1=g_norm1[0][None, :], g2=g_norm2[0][None, :], gf=g_final[None, :],
        w_proj=jnp.concatenate([w[:, 0:o_lra], w[:, o_fox:o_fb], w[:, o_g:o_g + 2 * d],
                                w[:, o_lra:o_lra + GLA_RANK], w[:, o_fb:o_fb + FOX_HEADS], pad], axis=1).astype(BF16),
        wall=jnp.asarray(wall, BF16), mask=jnp.asarray(mask, F32),
        w_a2=jnp.concatenate([w_a2[0], jnp.zeros((LANES - GLA_RANK, GLA_KW), F32)], axis=0), b_a=b_a[0][None, :], bf_row=bf_row, g_gla=g_gla_norm[0][None, :],
        w_up_a=w_up_a[0].astype(BF16), w_up_b=w_up_b[0].astype(BF16), w_out=w_out[0].astype(BF16),
        w_router=w_router, b_router=b_router,
        wg=w_gate_e[0].astype(BF16), wu=w_up_e[0].astype(BF16), wd=w_down_e[0].astype(BF16),
    )


def _project(h, p, tm):
    tn = 1024
    w = p["w_proj"]
    n_gla = 2 * GLA_KW + 2 * GLA_VW
    c_q = n_gla
    c_g = c_q + 3 * FOX_W
    c_s = c_g + 2 * D_MODEL
    (pg,) = _mm(h, w, 0, n_gla, tm, tn)
    (qb,) = _mm(h, w, c_q, FOX_W, tm, tn, (BF16,))
    kb, kb16 = _mm_heads(h, w, c_q + FOX_W, min(tm, 512))
    vb, vb16 = _mm_heads(h, w, c_q + 2 * FOX_W, min(tm, 512))
    (gates,) = _mm(h, w, c_g, 2 * D_MODEL, tm, tn)
    (small,) = _mm(h, w, c_s, LANES, tm, LANES)
    return pg, qb, kb, kb16, vb, vb16, gates, small


def _tail(x, oa, ob, gates, mod, p, tm, tiles_per_batch, grouped):
    merged = _merge(oa, ob, p["w_up_a"], p["w_up_b"], gates, tm)
    tmo = min(tm, 256)
    x1, hx = _outproj(merged, p["w_out"], x, mod, p["g2"], p["w_router"], p["b_router"],
                      tmo, tiles_per_batch * (tm // tmo))
    if not grouped:
        tmm = min(tm, 512)
        return _moe(hx[:, 0:D_MODEL].astype(BF16), hx[:, D_MODEL:HX_W], p["wg"], p["wu"], p["wd"], x1, mod, p["gf"],
                    tmm, tiles_per_batch * (tm // tmm))
    rows_out = x.shape[0] + N_GROUPS * MOE_TILE
    pos, tg, nu = _moe_plan(hx, rows_out // MOE_TILE)
    ys = _moe_grouped(hx, pos, tg, nu, p["wg"], p["wu"], p["wd"], rows_out)
    return _moe_combine(ys, pos, x1, mod, p["gf"], tiles_per_batch * (tm // COMBINE_ROWS))


def kernel(x_prompt, x_sample, cache_k, cache_v, cache_logf, state_gla, page_table, c_prompt, c_sample, w_ada, b_ada,
           g_norm1, g_norm2, g_final, w_in, w_a2, b_a, b_f, g_gla_norm, w_up_a, w_up_b, w_out, w_grp, b_grp, w_exp,
           b_exp, w_gate_e, w_up_e, w_down_e):
    p = _prep_weights(w_ada, b_ada, g_norm1, g_norm2, g_final, w_in, w_a2, b_a, b_f, g_gla_norm, w_up_a, w_up_b,
                      w_out, w_grp, b_grp, w_exp, b_exp, w_gate_e, w_up_e, w_down_e)
    bp, seq, d = x_prompt.shape
    bs, t, _ = x_sample.shape

    mod = _adaln(jnp.concatenate([c_prompt, c_sample], axis=0), p["w_ada"], p["b_ada"])
    mod_p = mod[:bp].reshape(bp, 6, 1, d).transpose(1, 0, 2, 3)
    mod_s = jnp.repeat(mod[bp:].reshape(bs, 6, d), t, axis=0).transpose(1, 0, 2)[:, None]

    tm = 1024
    tpb = seq // tm
    xp = x_prompt.reshape(bp * seq, d)
    hp = _normmod(xp, p["g1"], mod_p, tm, tpb)
    pg, qb, kb, kb16, vb, vb16, gates, small = _project(hp, p, tm)
    s0 = jnp.zeros((bp, GLA_HEADS, GLA_DK, GLA_DV), F32)
    oa, s_p = _gla_prompt(pg, small, s0, p["wall"], p["mask"], p["w_a2"], p["b_a"], p["g_gla"], bp, seq)
    lf_p, fcol, ft = _fox_bias_prompt(small, p["bf_row"], bp, seq)
    ob = _fox_prompt(qb, kb16, vb16, fcol, ft, bp, seq)
    y_p = _tail(xp, oa, ob, gates, mod_p, p, tm, tpb, grouped=True)

    rows = bs * t
    xs = x_sample.reshape(rows, d)
    hs = _normmod(xs, p["g1"], mod_s, rows, 1)
    pg_s, qs, ks, _, vs, _, gates_s, small_s = _project(hs, p, rows)
    oa_s, s_s = _gla_sample(pg_s.reshape(bs, t, -1), small_s.reshape(bs, t, LANES),
                            state_gla.reshape(state_gla.shape[1:]), p["wall"], p["mask"],
                            p["w_a2"], p["b_a"], p["g_gla"], bs, t)
    lf_s, fn_s = _fox_bias_sample(small_s, p["bf_row"], t)
    n_pool = cache_k.shape[1]
    ob_s = _fox_sample(page_table, qs.astype(F32).reshape(bs, t, FOX_W), ks.reshape(bs, t, FOX_HEADS, FOX_DH),
                       vs.reshape(bs, t, FOX_HEADS, FOX_DH), fn_s.reshape(bs, t, FOX_HEADS),
                       cache_k, cache_v, cache_logf.reshape(n_pool, 1, PAGE_FLAT), t)
    y_s = _tail(xs, oa_s.reshape(rows, GLA_VW).astype(BF16), ob_s.reshape(rows, FOX_W).astype(BF16), gates_s, mod_s,
                p, rows, 1, grouped=False)

    return (y_p.reshape(bp, seq, d), y_s.reshape(bs, t, d),
            kb.reshape(1, bp, seq, FOX_HEADS, FOX_DH), vb.reshape(1, bp, seq, FOX_HEADS, FOX_DH),
            lf_p.reshape(1, bp, seq, FOX_HEADS), s_p[None],
            ks.reshape(1, bs, t, FOX_HEADS, FOX_DH), vs.reshape(1, bs, t, FOX_HEADS, FOX_DH),
            lf_s.reshape(1, bs, t, FOX_HEADS), s_s[None])
```

```python
import functools

import numpy as np
import jax
import jax.numpy as jnp
from jax import lax
from jax.experimental import pallas as pl
from jax.experimental.pallas import tpu as pltpu

F32 = jnp.float32
BF16 = jnp.bfloat16

D_MODEL = 2048
GLA_HEADS = 4
GLA_DK = 128
GLA_DV = 256
GLA_RANK = 16
GLA_TAU = 16.0
FOX_HEADS = 8
FOX_DH = 128
PAGE = 128
N_GROUPS = 4
EXP_PER_GROUP = 4
N_EXPERTS = 16
D_EXPERT = 512
RMS_EPS = 1e-6
GLA_KW = GLA_HEADS * GLA_DK
GLA_VW = GLA_HEADS * GLA_DV
FOX_W = FOX_HEADS * FOX_DH
GLA_CHUNK = 128
GLA_LEVELS = 7
SAMPLE_LEVELS = 2
LANES = 128
NEG = -1e30
VMEM_LIMIT = 56 * 1024 * 1024


def _cparams(sem):
    return pltpu.CompilerParams(dimension_semantics=sem, vmem_limit_bytes=VMEM_LIMIT)


def _dot(a, b):
    return jnp.dot(a, b, preferred_element_type=F32)


def _dot_nt(a, b):
    return lax.dot_general(a, b, (((1,), (1,)), ((), ())), preferred_element_type=F32)


def _dot_tn(a, b):
    return lax.dot_general(a, b, (((0,), (0,)), ((), ())), preferred_element_type=F32)


def _split2(x):
    hi = x.astype(BF16)
    lo = (x - hi.astype(F32)).astype(BF16)
    return hi, lo


def _split3(x):
    hi = x.astype(BF16)
    r = x - hi.astype(F32)
    mid = r.astype(BF16)
    lo = (r - mid.astype(F32)).astype(BF16)
    return hi, mid, lo


def _dot3(a, b):
    ah, al = _split2(a)
    bh, bl = _split2(b)
    return _dot(ah, bh) + _dot(ah, bl) + _dot(al, bh)


def _dot_sel(w01, x):
    hi, mid, lo = _split3(x)
    return _dot(w01, hi) + _dot(w01, mid) + _dot(w01, lo)


def _log_sigmoid(x):
    return jnp.minimum(x, 0.0) - jnp.log1p(jnp.exp(-jnp.abs(x)))


def _silu(x):
    return x * jax.nn.sigmoid(x)


def _adaln_kernel(c_ref, w_ref, b_ref, o_ref):
    o_ref[...] = _dot3(_silu(c_ref[...]), w_ref[...]) + b_ref[...]


def _adaln(c, w, b, tn=512):
    nb, d = c.shape
    n = w.shape[1]
    return pl.pallas_call(
        _adaln_kernel,
        grid=(n // tn,),
        in_specs=[pl.BlockSpec((nb, d), lambda j: (0, 0)),
                  pl.BlockSpec((d, tn), lambda j: (0, j)),
                  pl.BlockSpec((1, tn), lambda j: (0, j))],
        out_specs=pl.BlockSpec((nb, tn), lambda j: (0, j)),
        out_shape=jax.ShapeDtypeStruct((nb, n), F32),
        compiler_params=_cparams(("arbitrary",)),
        name="adaln",
    )(c, w, b)


def _rms(x):
    return x * lax.rsqrt(jnp.mean(x * x, axis=-1, keepdims=True) + RMS_EPS)


def _normmod_kernel(x_ref, g_ref, sh_ref, sc_ref, o_ref):
    y = _rms(x_ref[...]) * g_ref[...]
    o_ref[...] = (y * (1.0 + sc_ref[...]) + sh_ref[...]).astype(o_ref.dtype)


def _mod_spec(k, rb, d, tiles_per_batch):
    return pl.BlockSpec((None, None, rb, d), lambda i: (k, i // tiles_per_batch, 0, 0))


def _normmod(x, g, mod, tm, tiles_per_batch):
    m, d = x.shape
    rb = mod.shape[2]
    return pl.pallas_call(
        _normmod_kernel,
        grid=(m // tm,),
        in_specs=[pl.BlockSpec((tm, d), lambda i: (i, 0)),
                  pl.BlockSpec((1, d), lambda i: (0, 0)),
                  _mod_spec(0, rb, d, tiles_per_batch),
                  _mod_spec(1, rb, d, tiles_per_batch)],
        out_specs=pl.BlockSpec((tm, d), lambda i: (i, 0)),
        out_shape=jax.ShapeDtypeStruct((m, d), BF16),
        compiler_params=_cparams(("arbitrary",)),
        name="normmod",
    )(x, g, mod, mod)


def _mm_kernel(x_ref, w_ref, *o_refs):
    r = _dot(x_ref[...], w_ref[...])
    for o_ref in o_refs:
        o_ref[...] = r.astype(o_ref.dtype)


def _mm(x, w, tm, tn, out_dtypes=(F32,)):
    m, k = x.shape
    n = w.shape[1]
    outs = pl.pallas_call(
        _mm_kernel,
        grid=(n // tn, m // tm),
        in_specs=[pl.BlockSpec((tm, k), lambda j, i: (i, 0)),
                  pl.BlockSpec((k, tn), lambda j, i: (0, j))],
        out_specs=[pl.BlockSpec((tm, tn), lambda j, i: (i, j)) for _ in out_dtypes],
        out_shape=[jax.ShapeDtypeStruct((m, n), dt) for dt in out_dtypes],
        compiler_params=_cparams(("arbitrary", "arbitrary")),
        name="proj_mm",
    )(x, w)
    return outs


def _window_specs(k, col0, tn):
    off = col0 % LANES
    cbase = col0 - off
    assert cbase % tn == 0
    specs = [pl.BlockSpec((k, tn), lambda j, i: (0, cbase // tn + j))]
    if off:
        specs.append(pl.BlockSpec((k, LANES), lambda j, i: (0, (cbase + (j + 1) * tn) // LANES)))
    return off, specs


def _load_window(w_refs, off, tn):
    if not off:
        return w_refs[0][...].astype(BF16)
    wide = jnp.concatenate([w_refs[0][...], w_refs[1][...]], axis=1)
    return wide[:, off:off + tn].astype(BF16)


def _mm_win_kernel(x_ref, *refs, off, tn, nw):
    w_refs, o_refs, wb = refs[:nw], refs[nw:-1], refs[-1]

    @pl.when(pl.program_id(1) == 0)
    def _():
        wb[...] = _load_window(w_refs, off, tn)

    r = _dot(x_ref[...], wb[...])
    for o_ref in o_refs:
        o_ref[...] = r.astype(o_ref.dtype)


def _mm_win(x, w, col0, n, tm, tn, out_dtypes=(F32,)):
    m, k = x.shape
    off, wspecs = _window_specs(k, col0, tn)
    return pl.pallas_call(
        functools.partial(_mm_win_kernel, off=off, tn=tn, nw=len(wspecs)),
        grid=(n // tn, m // tm),
        in_specs=[pl.BlockSpec((tm, k), lambda j, i: (i, 0))] + wspecs,
        out_specs=[pl.BlockSpec((tm, tn), lambda j, i: (i, j)) for _ in out_dtypes],
        out_shape=[jax.ShapeDtypeStruct((m, n), dt) for dt in out_dtypes],
        scratch_shapes=[pltpu.VMEM((k, tn), BF16)],
        compiler_params=_cparams(("arbitrary", "arbitrary")),
        name="proj_mm",
    )(x, *([w] * len(wspecs)))


def _mm_heads_kernel(x_ref, *refs, off, nw):
    w_refs, (o_ref, o16_ref, wb) = refs[:nw], refs[nw:]

    @pl.when(pl.program_id(1) == 0)
    def _():
        wb[...] = _load_window(w_refs, off, FOX_W)

    r = _dot(x_ref[...], wb[...])
    for h in range(FOX_HEADS):
        o_ref[:, h, :] = r[:, h * FOX_DH:(h + 1) * FOX_DH]
    o16_ref[...] = r.astype(BF16)


def _mm_heads(x, w, col0, tm):
    m, k = x.shape
    off, wspecs = _window_specs(k, col0, FOX_W)
    return pl.pallas_call(
        functools.partial(_mm_heads_kernel, off=off, nw=len(wspecs)),
        grid=(1, m // tm),
        in_specs=[pl.BlockSpec((tm, k), lambda j, i: (i, 0))] + wspecs,
        out_specs=[pl.BlockSpec((tm, FOX_HEADS, FOX_DH), lambda j, i: (i, 0, 0)),
                   pl.BlockSpec((tm, FOX_W), lambda j, i: (i, 0))],
        out_shape=[jax.ShapeDtypeStruct((m, FOX_HEADS, FOX_DH), F32),
                   jax.ShapeDtypeStruct((m, FOX_W), BF16)],
        scratch_shapes=[pltpu.VMEM((k, FOX_W), BF16)],
        compiler_params=_cparams(("arbitrary", "arbitrary")),
        name="proj_heads",
    )(x, *([w] * len(wspecs)))


def _gla_tables(p=GLA_LEVELS):
    c = GLA_CHUNK
    t = np.arange(c)[:, None]
    m = np.arange(c)[None, :]
    wall = np.zeros((p + 2, c, c), np.float32)
    mask = np.zeros((p + 1, c, c), np.float32)
    for l in range(p):
        half = 1 << l
        pos = t % (2 * half)
        mid = t - pos + half
        right = pos >= half
        wall[l] = np.where(right, (m >= mid) & (m <= t), (m > t) & (m < mid))
        s = m
        mask[l] = ((t >> (l + 1)) == (s >> (l + 1))) & (((t >> l) & 1) == 1) & (((s >> l) & 1) == 0)
    wall[p] = m <= t
    wall[p + 1] = m > t
    mask[p] = t == m
    return wall.reshape((p + 2) * c, c), mask


def _gla_body(q_ref, k_ref, v_ref, ra_ref, sm_ref, wall_ref, mask_ref, wa2_ref, ba_ref, g_ref,
              og_ref, st_ref, *, nchunk, valid, store_rows):
    c, p = GLA_CHUNK, mask_ref.shape[0] - 1
    row = lax.broadcasted_iota(jnp.int32, (c, GLA_DK), 0)
    wall = wall_ref[...]
    for ci in range(nchunk):
        rows = pl.ds(ci * c, c)
        x = _dot3(sm_ref[rows, :], wa2_ref[...]) + ba_ref[...]
        la = _log_sigmoid(x) * (1.0 / GLA_TAU)
        if valid < c:
            rowh = lax.broadcasted_iota(jnp.int32, la.shape, 0)
            la = jnp.where(rowh < valid, la, 0.0)
        e_all = jnp.exp(_dot_sel(wall, la))
        for h in range(GLA_HEADS):
            ks = slice(h * GLA_DK, (h + 1) * GLA_DK)
            vs = slice(h * GLA_DV, (h + 1) * GLA_DV)
            q = q_ref[rows, ks] * (GLA_DK ** -0.5)
            k = k_ref[rows, ks]
            vb = v_ref[rows, vs].astype(BF16)
            a = mask_ref[p] * _dot_nt(q.astype(BF16), k.astype(BF16))
            for l in range(p):
                el = e_all[l * c:(l + 1) * c, ks]
                xl = (jnp.where(((row >> l) & 1) == 1, q, k) * el).astype(BF16)
                a = a + mask_ref[l] * _dot_nt(xl, xl)
            st = st_ref[h]
            qc = (q * e_all[p * c:(p + 1) * c, ks]).astype(BF16)
            o = _dot(a.astype(BF16), vb) + _dot_nt(qc, st.astype(BF16))
            kr = (k * e_all[(p + 1) * c:(p + 2) * c, ks]).astype(BF16)
            dec = e_all[(p + 1) * c - 1:(p + 1) * c, ks]
            st_ref[h] = dec * st + _dot_tn(vb, kr)
            og = _rms(o) * g_ref[...] * _silu(ra_ref[rows, vs])
            if store_rows < c:
                og_ref[:, vs] = og[:store_rows].astype(og_ref.dtype)
            else:
                og_ref[rows, vs] = og.astype(og_ref.dtype)


def _gla_prompt_kernel(q_ref, k_ref, v_ref, ra_ref, sm_ref, wall_ref, mask_ref, wa2_ref, ba_ref, g_ref, s0_ref,
                       og_ref, s_ref, st_ref, *, nchunk):
    n = pl.program_id(1)

    @pl.when(n == 0)
    def _():
        for h in range(GLA_HEADS):
            st_ref[h] = s0_ref[h].T

    _gla_body(q_ref, k_ref, v_ref, ra_ref, sm_ref, wall_ref, mask_ref, wa2_ref, ba_ref, g_ref, og_ref, st_ref,
              nchunk=nchunk, valid=GLA_CHUNK, store_rows=GLA_CHUNK)

    @pl.when(n == pl.num_programs(1) - 1)
    def _():
        for h in range(GLA_HEADS):
            s_ref[h] = st_ref[h].T


def _gla_prompt(pg, small, s0, wall, mask, wa2, ba, g, batch, seq, tb=256):
    nblk = seq // tb
    rowmap = lambda cb: (lambda b, n: (b * nblk + n, cb))
    const2 = lambda b, n: (0, 0)
    return pl.pallas_call(
        functools.partial(_gla_prompt_kernel, nchunk=tb // GLA_CHUNK),
        grid=(batch, nblk),
        in_specs=[pl.BlockSpec((tb, GLA_KW), rowmap(0)),
                  pl.BlockSpec((tb, GLA_KW), rowmap(1)),
                  pl.BlockSpec((tb, GLA_VW), rowmap(1)),
                  pl.BlockSpec((tb, GLA_VW), rowmap(2)),
                  pl.BlockSpec((tb, LANES), rowmap(0)),
                  pl.BlockSpec(wall.shape, const2),
                  pl.BlockSpec(mask.shape, lambda b, n: (0, 0, 0)),
                  pl.BlockSpec(wa2.shape, const2),
                  pl.BlockSpec(ba.shape, const2),
                  pl.BlockSpec(g.shape, const2),
                  pl.BlockSpec((None, GLA_HEADS, GLA_DK, GLA_DV), lambda b, n: (b, 0, 0, 0))],
        out_specs=[pl.BlockSpec((tb, GLA_VW), rowmap(0)),
                   pl.BlockSpec((None, GLA_HEADS, GLA_DK, GLA_DV), lambda b, n: (b, 0, 0, 0))],
        out_shape=[jax.ShapeDtypeStruct((batch * seq, GLA_VW), BF16),
                   jax.ShapeDtypeStruct((batch, GLA_HEADS, GLA_DK, GLA_DV), F32)],
        scratch_shapes=[pltpu.VMEM((GLA_HEADS, GLA_DV, GLA_DK), F32)],
        compiler_params=_cparams(("arbitrary", "arbitrary")),
        name="gla_prompt",
    )(pg, pg, pg, pg, small, wall, mask, wa2, ba, g, s0)


def _gla_sample_kernel(pg_ref, sm_ref, wall_ref, mask_ref, wa2_ref, ba_ref, g_ref, s0_ref,
                       og_ref, s_ref, pad_ref, smpad_ref, st_ref, *, t):
    @pl.when(pl.program_id(0) == 0)
    def _():
        pad_ref[...] = jnp.zeros(pad_ref.shape, F32)
        smpad_ref[...] = jnp.zeros(smpad_ref.shape, F32)

    pad_ref[0:t, :] = pg_ref[...]
    smpad_ref[0:t, :] = sm_ref[...]
    for h in range(GLA_HEADS):
        st_ref[h] = s0_ref[h].T
    q_ref = pad_ref.at[:, 0:GLA_KW]
    k_ref = pad_ref.at[:, GLA_KW:2 * GLA_KW]
    v_ref = pad_ref.at[:, 2 * GLA_KW:2 * GLA_KW + GLA_VW]
    ra_ref = pad_ref.at[:, 2 * GLA_KW + GLA_VW:2 * GLA_KW + 2 * GLA_VW]
    _gla_body(q_ref, k_ref, v_ref, ra_ref, smpad_ref, wall_ref, mask_ref, wa2_ref, ba_ref, g_ref, og_ref, st_ref,
              nchunk=1, valid=t, store_rows=t)
    for h in range(GLA_HEADS):
        s_ref[h] = st_ref[h].T


def _gla_sample(pg, small, s0, wall, mask, wa2, ba, g, batch, t):
    width = pg.shape[-1]
    const2 = lambda b: (0, 0)
    return pl.pallas_call(
        functools.partial(_gla_sample_kernel, t=t),
        grid=(batch,),
        in_specs=[pl.BlockSpec((None, t, width), lambda b: (b, 0, 0)),
                  pl.BlockSpec((None, t, LANES), lambda b: (b, 0, 0)),
                  pl.BlockSpec(wall.shape, const2),
                  pl.BlockSpec(mask.shape, lambda b: (0, 0, 0)),
                  pl.BlockSpec(wa2.shape, const2),
                  pl.BlockSpec(ba.shape, const2),
                  pl.BlockSpec(g.shape, const2),
                  pl.BlockSpec((None, GLA_HEADS, GLA_DK, GLA_DV), lambda b: (b, 0, 0, 0))],
        out_specs=[pl.BlockSpec((None, t, GLA_VW), lambda b: (b, 0, 0)),
                   pl.BlockSpec((None, GLA_HEADS, GLA_DK, GLA_DV), lambda b: (b, 0, 0, 0))],
        out_shape=[jax.ShapeDtypeStruct((batch, t, GLA_VW), F32),
                   jax.ShapeDtypeStruct((batch, GLA_HEADS, GLA_DK, GLA_DV), F32)],
        scratch_shapes=[pltpu.VMEM((GLA_CHUNK, width), F32),
                        pltpu.VMEM((GLA_CHUNK, LANES), F32),
                        pltpu.VMEM((GLA_HEADS, GLA_DV, GLA_DK), F32)],
        compiler_params=_cparams(("arbitrary",)),
        name="gla_sample",
    )(pg, small, wall, mask, wa2, ba, g, s0)


FB_LANE = GLA_RANK


def _fox_bias_prompt_kernel(sm_ref, bf_ref, tri_ref, lf_ref, fc_ref, ft_ref, carry_ref):
    @pl.when(pl.program_id(1) == 0)
    def _():
        carry_ref[...] = jnp.zeros(carry_ref.shape, F32)

    lf = _log_sigmoid(sm_ref[...] + bf_ref[...])
    lf_ref[...] = lf[:, FB_LANE:FB_LANE + FOX_HEADS]
    cum = _dot_sel(tri_ref[...], lf) + carry_ref[...]
    carry_ref[...] = cum[cum.shape[0] - 1:, :]
    fc_ref[...] = cum
    ft_ref[...] = cum.T[FB_LANE:FB_LANE + FOX_HEADS, :]


def _fox_bias_prompt(small, bf_row, batch, seq, tb=256):
    nblk = seq // tb
    tri = jnp.asarray(np.tril(np.ones((tb, tb), np.float32)), BF16)
    return pl.pallas_call(
        _fox_bias_prompt_kernel,
        grid=(batch, nblk),
        in_specs=[pl.BlockSpec((tb, LANES), lambda b, n: (b * nblk + n, 0)),
                  pl.BlockSpec((1, LANES), lambda b, n: (0, 0)),
                  pl.BlockSpec((tb, tb), lambda b, n: (0, 0))],
        out_specs=[pl.BlockSpec((tb, FOX_HEADS), lambda b, n: (b * nblk + n, 0)),
                   pl.BlockSpec((tb, LANES), lambda b, n: (b * nblk + n, 0)),
                   pl.BlockSpec((None, FOX_HEADS, tb), lambda b, n: (b, 0, n))],
        out_shape=[jax.ShapeDtypeStruct((batch * seq, FOX_HEADS), F32),
                   jax.ShapeDtypeStruct((batch * seq, LANES), F32),
                   jax.ShapeDtypeStruct((batch, FOX_HEADS, seq), F32)],
        scratch_shapes=[pltpu.VMEM((1, LANES), F32)],
        compiler_params=_cparams(("arbitrary", "arbitrary")),
        name="fox_bias_prompt",
    )(small, bf_row, tri)


def _fox_bias_sample_kernel(sm_ref, bf_ref, sel_ref, lf_ref, fn_ref):
    lf = _log_sigmoid(sm_ref[...] + bf_ref[...])
    lf_ref[...] = lf[:, FB_LANE:FB_LANE + FOX_HEADS]
    cum = _dot_sel(sel_ref[...], lf)
    fn_ref[...] = cum[:, FB_LANE:FB_LANE + FOX_HEADS]


def _fox_bias_sample(small, bf_row, t):
    rows = small.shape[0]
    r = np.arange(rows)
    sel = ((r[:, None] // t) == (r[None, :] // t)) & (r[None, :] <= r[:, None])
    sel = jnp.asarray(sel.astype(np.float32), BF16)
    full = lambda shape: pl.BlockSpec(shape, lambda i: tuple(0 for _ in shape))
    return pl.pallas_call(
        _fox_bias_sample_kernel,
        grid=(1,),
        in_specs=[full((rows, LANES)), full((1, LANES)), full((rows, rows))],
        out_specs=[full((rows, FOX_HEADS)), full((rows, FOX_HEADS))],
        out_shape=[jax.ShapeDtypeStruct((rows, FOX_HEADS), F32),
                   jax.ShapeDtypeStruct((rows, FOX_HEADS), F32)],
        compiler_params=_cparams(("arbitrary",)),
        name="fox_bias_sample",
    )(small, bf_row, sel)


FOX_HEADS_PER_STEP = 4


def _fox_prompt_kernel(q_ref, k_ref, v_ref, fc_ref, ft_ref, o_ref, *, tq, tk):
    hb = FOX_HEADS_PER_STEP
    hg = pl.program_id(1)
    i = pl.program_id(2)
    lane = lax.broadcasted_iota(jnp.int32, (tq, LANES), 1)
    rowi = lax.broadcasted_iota(jnp.int32, (tq, tk), 0)
    coli = lax.broadcasted_iota(jnp.int32, (tq, tk), 1)
    fc = fc_ref[...]
    qs, f_ts = [], []
    for hh in range(hb):
        hs = slice(hh * FOX_DH, (hh + 1) * FOX_DH)
        qs.append((q_ref[:, hs].astype(F32) * (FOX_DH ** -0.5)).astype(BF16))
        f_ts.append(jnp.sum(jnp.where(lane == hg * hb + hh + FB_LANE, fc, 0.0), axis=-1, keepdims=True))

    def step(j, carry, masked):
        ks = pl.ds(pl.multiple_of(j * tk, tk), tk)
        out = []
        for hh in range(hb):
            hs = slice(hh * FOX_DH, (hh + 1) * FOX_DH)
            m, l, acc = carry[hh]
            s = _dot_nt(qs[hh], k_ref[ks, hs])
            s = s + f_ts[hh] - ft_ref[hh:hh + 1, ks]
            if masked:
                s = jnp.where(rowi >= coli, s, NEG)
            m_new = jnp.maximum(m, jnp.max(s, axis=-1, keepdims=True))
            p = jnp.exp(s - m_new)
            alpha = jnp.exp(m - m_new)
            l = alpha * l + jnp.sum(p, axis=-1, keepdims=True)
            acc = alpha * acc + _dot(p.astype(BF16), v_ref[ks, hs])
            out.append((m_new, l, acc))
        return tuple(out)

    init = tuple((jnp.full((tq, 1), NEG, F32), jnp.zeros((tq, 1), F32), jnp.zeros((tq, FOX_DH), F32))
                 for _ in range(hb))
    nfull = i * (tq // tk)
    carry = lax.fori_loop(0, nfull, lambda j, c: step(j, c, False), init)
    carry = step(nfull, carry, True)
    for hh in range(hb):
        m, l, acc = carry[hh]
        o_ref[:, hh * FOX_DH:(hh + 1) * FOX_DH] = (acc / l).astype(o_ref.dtype)


def _fox_prompt(qb, kb, vb, fcol, ft, batch, seq, tq=256):
    nq = seq // tq
    hb = FOX_HEADS_PER_STEP
    wb = hb * FOX_DH
    return pl.pallas_call(
        functools.partial(_fox_prompt_kernel, tq=tq, tk=tq),
        grid=(batch, FOX_HEADS // hb, nq),
        in_specs=[pl.BlockSpec((tq, wb), lambda b, h, i: (b * nq + i, h)),
                  pl.BlockSpec((seq, wb), lambda b, h, i: (b, h)),
                  pl.BlockSpec((seq, wb), lambda b, h, i: (b, h)),
                  pl.BlockSpec((tq, LANES), lambda b, h, i: (b * nq + i, 0)),
                  pl.BlockSpec((None, None, hb, seq), lambda b, h, i: (b, h, 0, 0))],
        out_specs=pl.BlockSpec((tq, wb), lambda b, h, i: (b * nq + i, h)),
        out_shape=jax.ShapeDtypeStruct((batch * seq, FOX_W), BF16),
        compiler_params=_cparams(("arbitrary", "arbitrary", "arbitrary")),
        name="fox_prompt",
    )(qb, kb, vb, fcol, ft.reshape(batch, FOX_HEADS // hb, hb, seq))


PAGES_PER_STEP = 16
ROWS8 = 8
QROWS = FOX_HEADS * ROWS8
PAGE_FLAT = PAGE * FOX_HEADS


def _dot_sel_rhs(x, w01):
    hi, mid, lo = _split3(x)
    m = x.shape[0]
    stacked = jnp.concatenate([hi.astype(F32), mid.astype(F32), lo.astype(F32)], axis=0).astype(BF16)
    r = _dot(stacked, w01)
    return r[0:m] + r[m:2 * m] + r[2 * m:3 * m]


def _fox_sample_kernel(pt_ref, q_ref, kn_ref, vn_ref, fn_ref, fnrow_ref, madd_ref, maddn_ref, usuf_ref, tot_ref,
                       *refs, t):
    g = PAGES_PER_STEP
    k_refs, v_refs, lf_refs = refs[0:g], refs[g:2 * g], refs[2 * g:3 * g]
    o_ref = refs[3 * g]
    q_sc, m_sc, l_sc, acc_sc, carry_sc = refs[3 * g + 1:]
    j = pl.program_id(1)
    fn_t = fn_ref[...][:, 0:1]

    @pl.when(j == 0)
    def _():
        q_sc[...] = jnp.zeros(q_sc.shape, F32)
        for h in range(FOX_HEADS):
            q_sc[h * ROWS8:h * ROWS8 + t, :] = q_ref[:, h * FOX_DH:(h + 1) * FOX_DH] * (FOX_DH ** -0.5)
        carry_sc[...] = jnp.zeros(carry_sc.shape, F32)
        pad = jnp.zeros((LANES - t * FOX_HEADS, FOX_DH), F32)
        kn = jnp.concatenate([kn_ref[...].reshape(t * FOX_HEADS, FOX_DH), pad], axis=0)
        vn = jnp.concatenate([vn_ref[...].reshape(t * FOX_HEADS, FOX_DH), pad], axis=0)
        s = _dot_nt(q_sc[...], kn) + fn_t - fnrow_ref[...] + maddn_ref[...]
        m = jnp.max(s, axis=-1, keepdims=True)
        p = jnp.exp(s - m)
        m_sc[...] = jnp.broadcast_to(m, m_sc.shape)
        l_sc[...] = jnp.broadcast_to(jnp.sum(p, axis=-1, keepdims=True), l_sc.shape)
        acc_sc[...] = _dot(p, vn)

    lf = jnp.concatenate([lf_refs[gi][...] for gi in range(g)], axis=0)
    r_in = _dot_sel_rhs(lf, usuf_ref[...])
    page_tot = _dot_sel_rhs((r_in + lf)[:, 0:LANES], tot_ref[...])
    carry = carry_sc[...]
    q = q_sc[...]
    madd = madd_ref[...] + fn_t
    m_old = m_sc[...]
    m_new = m_old
    s_list = []
    for gi in range(g):
        bias = madd + (r_in[gi:gi + 1, :] + carry)
        carry = carry + page_tot[gi:gi + 1, :]
        s = _dot_nt(q, k_refs[gi][...].reshape(PAGE_FLAT, FOX_DH)) + bias
        s_list.append(s)
        m_new = jnp.maximum(m_new, jnp.max(s, axis=-1, keepdims=True))
    carry_sc[...] = carry
    alpha = jnp.exp(m_old - m_new)
    l = alpha * l_sc[...]
    acc = alpha * acc_sc[...]
    m_col = m_new[:, 0:1]
    for gi in range(g):
        p = jnp.exp(s_list[gi] - m_col)
        l = l + jnp.sum(p, axis=-1, keepdims=True)
        acc = acc + _dot(p, v_refs[gi][...].reshape(PAGE_FLAT, FOX_DH))
    m_sc[...] = m_new
    l_sc[...] = l
    acc_sc[...] = acc

    @pl.when(j == pl.num_programs(1) - 1)
    def _():
        o = acc / l
        for h in range(FOX_HEADS):
            o_ref[:, h * FOX_DH:(h + 1) * FOX_DH] = o[h * ROWS8:h * ROWS8 + t, :].astype(o_ref.dtype)


def _fox_sample_tables(t):
    row = np.arange(QROWS)[:, None]
    col = np.arange(PAGE_FLAT)[None, :]
    madd = np.where((row // ROWS8) == (col % FOX_HEADS), 0.0, NEG).astype(np.float32)
    coln = np.arange(LANES)[None, :]
    ok = (coln < t * FOX_HEADS) & ((row // ROWS8) == (coln % FOX_HEADS)) & ((coln // FOX_HEADS) <= (row % ROWS8))
    maddn = np.where(ok, 0.0, NEG).astype(np.float32)
    src = np.arange(PAGE_FLAT)[:, None]
    same_head = (src % FOX_HEADS) == (col % FOX_HEADS)
    usuf = (same_head & ((src // FOX_HEADS) > (col // FOX_HEADS))).astype(np.float32)
    lane = np.arange(LANES)[:, None]
    tot = ((lane < FOX_HEADS) & (lane == (col % FOX_HEADS))).astype(np.float32)
    return (jnp.asarray(madd), jnp.asarray(maddn), jnp.asarray(usuf, BF16), jnp.asarray(tot, BF16))


def _fox_sample(page_table, q, kn, vn, fn, cache_k, cache_v, cache_lf, t):
    batch, npages = page_table.shape
    g = PAGES_PER_STEP
    nsteps = npages // g
    madd, maddn, usuf, tot = _fox_sample_tables(t)
    fn_rows = jnp.pad(fn.transpose(0, 2, 1), ((0, 0), (0, 0), (0, ROWS8 - t))).reshape(batch, QROWS, 1)
    fn_rows = jnp.broadcast_to(fn_rows, (batch, QROWS, LANES))
    fn_cols = jnp.pad(fn.reshape(batch, 1, t * FOX_HEADS), ((0, 0), (0, 0), (0, LANES - t * FOX_HEADS)))

    def page_map5(gi):
        return lambda b, j, pt: (0, pt[b, npages - 1 - (j * g + gi)], 0, 0, 0)

    def page_map3(gi):
        return lambda b, j, pt: (pt[b, npages - 1 - (j * g + gi)], 0, 0)

    seq3 = lambda b, j, pt: (b, 0, 0)
    seq4 = lambda b, j, pt: (b, 0, 0, 0)
    const2 = lambda b, j, pt: (0, 0)
    in_specs = [pl.BlockSpec((None, t, FOX_W), seq3),
                pl.BlockSpec((None, t, FOX_HEADS, FOX_DH), seq4),
                pl.BlockSpec((None, t, FOX_HEADS, FOX_DH), seq4),
                pl.BlockSpec((None, QROWS, LANES), seq3),
                pl.BlockSpec((None, 1, LANES), seq3),
                pl.BlockSpec(madd.shape, const2),
                pl.BlockSpec(maddn.shape, const2),
                pl.BlockSpec(usuf.shape, const2),
                pl.BlockSpec(tot.shape, const2)]
    in_specs += [pl.BlockSpec((None, None, PAGE, FOX_HEADS, FOX_DH), page_map5(gi)) for gi in range(g)]
    in_specs += [pl.BlockSpec((None, None, PAGE, FOX_HEADS, FOX_DH), page_map5(gi)) for gi in range(g)]
    in_specs += [pl.BlockSpec((None, 1, PAGE_FLAT), page_map3(gi)) for gi in range(g)]
    grid_spec = pltpu.PrefetchScalarGridSpec(
        num_scalar_prefetch=1,
        grid=(batch, nsteps),
        in_specs=in_specs,
        out_specs=pl.BlockSpec((None, t, FOX_W), seq3),
        scratch_shapes=[pltpu.VMEM((QROWS, FOX_DH), F32),
                        pltpu.VMEM((QROWS, LANES), F32),
                        pltpu.VMEM((QROWS, LANES), F32),
                        pltpu.VMEM((QROWS, FOX_DH), F32),
                        pltpu.VMEM((1, PAGE_FLAT), F32)],
    )
    return pl.pallas_call(
        functools.partial(_fox_sample_kernel, t=t),
        grid_spec=grid_spec,
        out_shape=jax.ShapeDtypeStruct((batch, t, FOX_W), F32),
        compiler_params=_cparams(("arbitrary", "arbitrary")),
        name="fox_sample",
    )(page_table, q, kn, vn, fn_rows, fn_cols, madd, maddn, usuf, tot,
      *([cache_k] * g), *([cache_v] * g), *([cache_lf] * g))


def _merge_kernel(oa_ref, ob_ref, wa_ref, wb_ref, ga_ref, gb_ref, o_ref):
    ua = _dot(oa_ref[...], wa_ref[...])
    ub = _dot(ob_ref[...], wb_ref[...])
    o_ref[...] = (jax.nn.sigmoid(ga_ref[...]) * ua + jax.nn.sigmoid(gb_ref[...]) * ub).astype(o_ref.dtype)


def _merge(oa, ob, wa, wb, gates, tm, tn=1024):
    m = oa.shape[0]
    d = wa.shape[1]
    nj = d // tn
    return pl.pallas_call(
        _merge_kernel,
        grid=(nj, m // tm),
        in_specs=[pl.BlockSpec((tm, GLA_VW), lambda j, i: (i, 0)),
                  pl.BlockSpec((tm, FOX_W), lambda j, i: (i, 0)),
                  pl.BlockSpec((GLA_VW, tn), lambda j, i: (0, j)),
                  pl.BlockSpec((FOX_W, tn), lambda j, i: (0, j)),
                  pl.BlockSpec((tm, tn), lambda j, i: (i, j)),
                  pl.BlockSpec((tm, tn), lambda j, i: (i, nj + j))],
        out_specs=pl.BlockSpec((tm, tn), lambda j, i: (i, j)),
        out_shape=jax.ShapeDtypeStruct((m, d), BF16),
        compiler_params=_cparams(("arbitrary", "arbitrary")),
        name="merge",
    )(oa, ob, wa, wb, gates, gates)


ROUTER_GROUP_LANE = N_EXPERTS


def _route(logits):
    lane_i = lax.broadcasted_iota(jnp.int32, logits.shape, 1)
    lane = lane_i.astype(F32)
    grp_of_lane = (lane_i >> 2).astype(F32)
    big = float(LANES)
    is_grp = (lane_i >= ROUTER_GROUP_LANE) & (lane_i < ROUTER_GROUP_LANE + N_GROUPS)
    gl = jnp.where(is_grp, logits, NEG)
    gmax = jnp.max(gl, axis=-1, keepdims=True)
    g_idx = jnp.min(jnp.where(is_grp & (gl == gmax), lane - ROUTER_GROUP_LANE, big), axis=-1, keepdims=True)
    g_w = 1.0 / jnp.sum(jnp.where(is_grp, jnp.exp(gl - gmax), 0.0), axis=-1, keepdims=True)
    in_grp = (lane_i < N_EXPERTS) & (grp_of_lane == g_idx)
    e1 = jnp.where(in_grp, logits, NEG)
    v1 = jnp.max(e1, axis=-1, keepdims=True)
    i1 = jnp.min(jnp.where(in_grp & (e1 == v1), lane, big), axis=-1, keepdims=True)
    rest = in_grp & (lane != i1)
    e2 = jnp.where(rest, logits, NEG)
    v2 = jnp.max(e2, axis=-1, keepdims=True)
    i2 = jnp.min(jnp.where(rest & (e2 == v2), lane, big), axis=-1, keepdims=True)
    r = jnp.exp(v2 - v1)
    w1 = g_w / (1.0 + r)
    w2 = g_w * r / (1.0 + r)
    grp_onehot = jnp.where(is_grp & (lane - ROUTER_GROUP_LANE == g_idx), 1.0, 0.0)
    return jnp.where(lane == i1, w1, 0.0) + jnp.where(lane == i2, w2, 0.0) + grp_onehot


HX_W = D_MODEL + LANES


def _outproj_kernel(mg_ref, w_ref, x_ref, gt_ref, g2_ref, sh_ref, sc_ref, wr_ref, br_ref, x1_ref, hx_ref):
    x1 = x_ref[...] + gt_ref[...] * _dot(mg_ref[...], w_ref[...])
    x1_ref[...] = x1
    h2 = _rms(x1) * g2_ref[...] * (1.0 + sc_ref[...]) + sh_ref[...]
    hx_ref[:, 0:D_MODEL] = h2
    hx_ref[:, D_MODEL:HX_W] = _route(_dot3(h2, wr_ref[...]) + br_ref[...])


def _outproj(merged, w_out, x, mod, g2, w_router, b_router, tm, tiles_per_batch):
    m, d = x.shape
    rb = mod.shape[2]
    const2 = lambda i: (0, 0)
    return pl.pallas_call(
        _outproj_kernel,
        grid=(m // tm,),
        in_specs=[pl.BlockSpec((tm, d), lambda i: (i, 0)),
                  pl.BlockSpec((d, d), const2),
                  pl.BlockSpec((tm, d), lambda i: (i, 0)),
                  _mod_spec(2, rb, d, tiles_per_batch),
                  pl.BlockSpec((1, d), const2),
                  _mod_spec(3, rb, d, tiles_per_batch),
                  _mod_spec(4, rb, d, tiles_per_batch),
                  pl.BlockSpec((d, LANES), const2),
                  pl.BlockSpec((1, LANES), const2)],
        out_specs=[pl.BlockSpec((tm, d), lambda i: (i, 0)),
                   pl.BlockSpec((tm, HX_W), lambda i: (i, 0))],
        out_shape=[jax.ShapeDtypeStruct((m, d), F32),
                   jax.ShapeDtypeStruct((m, HX_W), F32)],
        compiler_params=_cparams(("arbitrary",)),
        name="outproj",
    )(merged, w_out, x, mod, g2, mod, mod, w_router, b_router)


MOE_TILE = 512
PLAN_TILE = 512


def _moe_plan_kernel(r_ref, tri_ref, excl_ref, pos_ref, tg_ref, nu_ref, cnt_sc, off_sc, run_sc):
    ph = pl.program_id(0)
    n = pl.program_id(1)
    lane = lax.broadcasted_iota(jnp.int32, (1, LANES), 1)
    lane_t = lax.broadcasted_iota(jnp.int32, r_ref.shape, 1)
    is_grp = (lane_t >= ROUTER_GROUP_LANE) & (lane_t < ROUTER_GROUP_LANE + N_GROUPS)
    g4 = jnp.where(is_grp, r_ref[...], 0.0)

    @pl.when((ph == 0) & (n == 0))
    def _():
        cnt_sc[...] = jnp.zeros(cnt_sc.shape, F32)

    @pl.when(ph == 0)
    def _():
        cnt_sc[...] += jnp.sum(g4, axis=0, keepdims=True)

    @pl.when((ph == 1) & (n == 0))
    def _():
        padded = jnp.floor((cnt_sc[...] + (MOE_TILE - 1)) * (1.0 / MOE_TILE)) * MOE_TILE
        off = _dot_sel_rhs(jnp.broadcast_to(padded, (8, LANES)), excl_ref[...])[0:1]
        off_sc[...] = off
        run_sc[...] = jnp.zeros(run_sc.shape, F32)
        end = off + padded
        tile_start = lane.astype(F32) * MOE_TILE
        tg = jnp.zeros((1, LANES), F32)
        for g in range(N_GROUPS):
            end_g = jnp.sum(jnp.where(lane == ROUTER_GROUP_LANE + g, end, 0.0), axis=-1, keepdims=True)
            tg = tg + jnp.where(end_g <= tile_start, 1.0, 0.0)
        tg_ref[...] = jnp.minimum(tg, N_GROUPS - 1.0).astype(jnp.int32)
        total = jnp.sum(jnp.where(lane == ROUTER_GROUP_LANE + N_GROUPS - 1, end, 0.0), axis=-1, keepdims=True)
        nu_ref[...] = jnp.broadcast_to(total * (1.0 / MOE_TILE), (1, LANES)).astype(jnp.int32)

    @pl.when(ph == 1)
    def _():
        rank = _dot(tri_ref[...], g4.astype(BF16)) + run_sc[...]
        run_sc[...] += jnp.sum(g4, axis=0, keepdims=True)
        posv = g4 * (off_sc[...] + rank)
        hi, mid, lo = _split3(posv)
        ones = jnp.ones((8, LANES), BF16)
        row = _dot_nt(ones, hi) + _dot_nt(ones, mid) + _dot_nt(ones, lo)
        pos_ref[...] = row[0:1].astype(jnp.int32)


def _moe_plan(hx, ntiles):
    t = hx.shape[0]
    nblk = t // PLAN_TILE
    tri = jnp.asarray(np.tril(np.ones((PLAN_TILE, PLAN_TILE), np.float32), -1), BF16)
    excl = jnp.asarray(np.triu(np.ones((LANES, LANES), np.float32), 1), BF16)
    assert ntiles <= LANES
    pos, tg, nu = pl.pallas_call(
        _moe_plan_kernel,
        grid=(2, nblk),
        in_specs=[pl.BlockSpec((PLAN_TILE, LANES), lambda ph, n: (n, D_MODEL // LANES)),
                  pl.BlockSpec((PLAN_TILE, PLAN_TILE), lambda ph, n: (0, 0)),
                  pl.BlockSpec((LANES, LANES), lambda ph, n: (0, 0))],
        out_specs=[pl.BlockSpec((1, PLAN_TILE), lambda ph, n: (0, n * ph)),
                   pl.BlockSpec((1, LANES), lambda ph, n: (0, 0)),
                   pl.BlockSpec((1, LANES), lambda ph, n: (0, 0))],
        out_shape=[jax.ShapeDtypeStruct((1, t), jnp.int32),
                   jax.ShapeDtypeStruct((1, LANES), jnp.int32),
                   jax.ShapeDtypeStruct((1, LANES), jnp.int32)],
        scratch_shapes=[pltpu.VMEM((1, LANES), F32), pltpu.VMEM((1, LANES), F32), pltpu.VMEM((1, LANES), F32)],
        compiler_params=_cparams(("arbitrary", "arbitrary")),
        name="moe_plan",
    )(hx, tri, excl)
    return pos.reshape(t), tg.reshape(LANES), nu.reshape(LANES)[0:1]


def _row_copy(src_ref, src_row, dst_ref, dst_row, sem):
    return pltpu.make_async_copy(src_ref.at[pl.ds(src_row, 1)], dst_ref.at[pl.ds(dst_row, 1)], sem)


def _moe_grouped_kernel(pos_ref, tg_ref, nu_ref, hx_ref, wg_ref, wu_ref, wd_ref, ys_ref,
                        buf, sem, src_sc, xb_sc, acc_sc, *, t):
    i = pl.program_id(0)
    ei = pl.program_id(1)
    n_used = nu_ref[0]
    used = i < n_used
    slot = i % 2

    def fetch(tile, sl):
        def body(r, c):
            _row_copy(hx_ref, src_sc[tile * MOE_TILE + r], buf.at[sl], r, sem.at[sl]).start()
            return c
        lax.fori_loop(0, MOE_TILE, body, 0, unroll=8)

    @pl.when((i == 0) & (ei == 0))
    def _():
        def clear(r, c):
            src_sc[r] = 0
            return c

        def invert(tk, c):
            src_sc[pos_ref[tk]] = tk
            return c

        lax.fori_loop(0, src_sc.shape[0], clear, 0, unroll=8)
        lax.fori_loop(0, t, invert, 0, unroll=8)
        fetch(0, 0)

    @pl.when((ei == 0) & (i + 1 < n_used))
    def _():
        fetch(i + 1, (i + 1) % 2)

    @pl.when(used & (ei == 0))
    def _():
        def wait(r, c):
            _row_copy(hx_ref, 0, buf.at[slot], r, sem.at[slot]).wait()
            return c
        lax.fori_loop(0, MOE_TILE, wait, 0, unroll=8)
        xb_sc[...] = buf[slot, :, 0:D_MODEL].astype(BF16)
        acc_sc[...] = jnp.zeros(acc_sc.shape, F32)

    @pl.when(used)
    def _():
        h = xb_sc[...]
        a = _dot(h, wg_ref[...])
        u = _dot(h, wu_ref[...])
        lane = lax.broadcasted_iota(jnp.int32, (MOE_TILE, LANES), 1)
        e = tg_ref[i] * EXP_PER_GROUP + ei
        cw = jnp.sum(jnp.where(lane == e, buf[slot, :, D_MODEL:HX_W], 0.0), axis=-1, keepdims=True)
        hid = (_silu(a) * u * cw).astype(BF16)
        acc_sc[...] += _dot(hid, wd_ref[...])

    @pl.when(ei == pl.num_programs(1) - 1)
    def _():
        ys_ref[...] = jnp.where(used, acc_sc[...], 0.0)


def _moe_grouped(hx, pos, tg, nu, wg, wu, wd, rows_out):
    t = hx.shape[0]
    ne, d, de = wg.shape
    ntiles = rows_out // MOE_TILE

    def wmap(i, ei, pos_ref, tg_ref, nu_ref):
        e = jnp.where(i < nu_ref[0], tg_ref[i] * EXP_PER_GROUP + ei, ne - 1)
        return (e, 0, 0)

    grid_spec = pltpu.PrefetchScalarGridSpec(
        num_scalar_prefetch=3,
        grid=(ntiles, EXP_PER_GROUP),
        in_specs=[pl.BlockSpec(memory_space=pl.ANY),
                  pl.BlockSpec((None, d, de), wmap),
                  pl.BlockSpec((None, d, de), wmap),
                  pl.BlockSpec((None, de, d), wmap)],
        out_specs=pl.BlockSpec((MOE_TILE, d), lambda i, ei, pos_ref, tg_ref, nu_ref: (i, 0)),
        scratch_shapes=[pltpu.VMEM((2, MOE_TILE, HX_W), F32), pltpu.SemaphoreType.DMA((2,)),
                        pltpu.SMEM((rows_out,), jnp.int32),
                        pltpu.VMEM((MOE_TILE, d), BF16), pltpu.VMEM((MOE_TILE, d), F32)],
    )
    return pl.pallas_call(
        functools.partial(_moe_grouped_kernel, t=t),
        grid_spec=grid_spec,
        out_shape=jax.ShapeDtypeStruct((rows_out, d), F32),
        compiler_params=_cparams(("arbitrary", "arbitrary")),
        name="moe_grouped",
    )(pos, tg, nu, hx, wg, wu, wd)


COMBINE_ROWS = 256


def _moe_combine_kernel(pos_ref, ys_ref, x1_ref, gt_ref, gf_ref, y_ref, buf, sem):
    i = pl.program_id(0)
    n = pl.num_programs(0)

    def fetch(tile, slot):
        def body(r, c):
            _row_copy(ys_ref, pos_ref[tile * COMBINE_ROWS + r], buf.at[slot], r, sem.at[slot]).start()
            return c
        lax.fori_loop(0, COMBINE_ROWS, body, 0, unroll=8)

    @pl.when(i == 0)
    def _():
        fetch(0, 0)

    @pl.when(i + 1 < n)
    def _():
        fetch(i + 1, (i + 1) % 2)

    slot = i % 2

    def wait(r, c):
        _row_copy(ys_ref, pos_ref[i * COMBINE_ROWS + r], buf.at[slot], r, sem.at[slot]).wait()
        return c

    lax.fori_loop(0, COMBINE_ROWS, wait, 0, unroll=8)
    x2 = x1_ref[...] + gt_ref[...] * buf[slot]
    y_ref[...] = _rms(x2) * gf_ref[...]


def _moe_combine(ys, pos, x1, mod, g_final, tiles_per_batch):
    m, d = x1.shape
    rb = mod.shape[2]
    grid_spec = pltpu.PrefetchScalarGridSpec(
        num_scalar_prefetch=1,
        grid=(m // COMBINE_ROWS,),
        in_specs=[pl.BlockSpec(memory_space=pl.ANY),
                  pl.BlockSpec((COMBINE_ROWS, d), lambda i, pos_ref: (i, 0)),
                  pl.BlockSpec((None, None, rb, d), lambda i, pos_ref: (5, i // tiles_per_batch, 0, 0)),
                  pl.BlockSpec((1, d), lambda i, pos_ref: (0, 0))],
        out_specs=pl.BlockSpec((COMBINE_ROWS, d), lambda i, pos_ref: (i, 0)),
        scratch_shapes=[pltpu.VMEM((2, COMBINE_ROWS, d), F32), pltpu.SemaphoreType.DMA((2,))],
    )
    return pl.pallas_call(
        _moe_combine_kernel,
        grid_spec=grid_spec,
        out_shape=jax.ShapeDtypeStruct((m, d), F32),
        compiler_params=_cparams(("arbitrary",)),
        name="moe_combine",
    )(pos, ys, x1, mod, g_final)


def _moe_kernel(h_ref, cmb_ref, wg_ref, wu_ref, wd_ref, x1_ref, gt_ref, gf_ref, y_ref, acc_ref):
    e = pl.program_id(1)

    @pl.when(e == 0)
    def _():
        acc_ref[...] = jnp.zeros(acc_ref.shape, F32)

    h = h_ref[...]
    a = _dot(h, wg_ref[...])
    u = _dot(h, wu_ref[...])
    lane = lax.broadcasted_iota(jnp.int32, cmb_ref.shape, 1)
    cw = jnp.sum(jnp.where(lane == e, cmb_ref[...], 0.0), axis=-1, keepdims=True)
    hid = (_silu(a) * u * cw).astype(BF16)
    acc_ref[...] += _dot(hid, wd_ref[...])

    @pl.when(e == pl.num_programs(1) - 1)
    def _():
        x2 = x1_ref[...] + gt_ref[...] * acc_ref[...]
        y_ref[...] = _rms(x2) * gf_ref[...]


def _moe(h2, cmb, wg, wu, wd, x1, mod, g_final, tm, tiles_per_batch):
    m, d = x1.shape
    rb = mod.shape[2]
    ne, _, de = wg.shape
    return pl.pallas_call(
        _moe_kernel,
        grid=(m // tm, ne),
        in_specs=[pl.BlockSpec((tm, d), lambda i, e: (i, 0)),
                  pl.BlockSpec((tm, LANES), lambda i, e: (i, 0)),
                  pl.BlockSpec((None, d, de), lambda i, e: (e, 0, 0)),
                  pl.BlockSpec((None, d, de), lambda i, e: (e, 0, 0)),
                  pl.BlockSpec((None, de, d), lambda i, e: (e, 0, 0)),
                  pl.BlockSpec((tm, d), lambda i, e: (i, 0)),
                  pl.BlockSpec((None, None, rb, d), lambda i, e: (5, i // tiles_per_batch, 0, 0)),
                  pl.BlockSpec((1, d), lambda i, e: (0, 0))],
        out_specs=pl.BlockSpec((tm, d), lambda i, e: (i, 0)),
        out_shape=jax.ShapeDtypeStruct((m, d), F32),
        scratch_shapes=[pltpu.VMEM((tm, d), F32)],
        compiler_params=_cparams(("arbitrary", "arbitrary")),
        name="moe",
    )(h2, cmb, wg, wu, wd, x1, mod, g_final)


def _prep_weights(w_ada, b_ada, g_norm1, g_norm2, g_final, w_in, w_a2, b_a, b_f, g_gla_norm, w_up_a, w_up_b, w_out,
                  w_grp, b_grp, w_exp, b_exp, w_gate_e, w_up_e, w_down_e):
    w = w_in.reshape(w_in.shape[1:])
    d = D_MODEL
    o_lra = 2 * GLA_KW + 2 * GLA_VW
    o_fox = o_lra + GLA_RANK
    o_fb = o_fox + 3 * FOX_W
    o_g = o_fb + FOX_HEADS
    pad = jnp.zeros((d, LANES - GLA_RANK - FOX_HEADS), F32)
    wall, mask = _gla_tables()
    wall_s, mask_s = _gla_tables(SAMPLE_LEVELS)
    bf_row =jnp.zeros((1, LANES), F32).at[0, FB_LANE:FB_LANE + FOX_HEADS].set(b_f[0])
    w_router = jnp.concatenate([w_exp[0], w_grp[0], jnp.zeros((d, LANES - N_EXPERTS - N_GROUPS), F32)], axis=1)
    b_router = jnp.concatenate([b_exp[0], b_grp[0], jnp.zeros((LANES - N_EXPERTS - N_GROUPS,), F32)])[None, :]
    return dict(
        w_ada=w_ada.reshape(w_ada.shape[1:]), b_ada=b_ada[0][None, :],
        g1=g_norm1[0][None, :], g2=g_norm2[0][None, :], gf=g_final[None, :],
        w_in=w,
        w_small=jnp.concatenate([w[:, o_lra:o_lra + GLA_RANK], w[:, o_fb:o_fb + FOX_HEADS], pad], axis=1).astype(BF16),
        wall=jnp.asarray(wall, BF16), mask=jnp.asarray(mask, F32),
        wall_s=jnp.asarray(wall_s, BF16), mask_s=jnp.asarray(mask_s, F32),
        w_a2=jnp.concatenate([w_a2[0], jnp.zeros((LANES - GLA_RANK, GLA_KW), F32)], axis=0), b_a=b_a[0][None, :], bf_row=bf_row, g_gla=g_gla_norm[0][None, :],
        w_up_a=w_up_a[0].astype(BF16), w_up_b=w_up_b[0].astype(BF16), w_out=w_out[0].astype(BF16),
        w_router=w_router, b_router=b_router,
        wg=w_gate_e[0].astype(BF16), wu=w_up_e[0].astype(BF16), wd=w_down_e[0].astype(BF16),
    )


def _project(h, p, tm):
    tn = 512
    w = p["w_in"]
    n_gla = 2 * GLA_KW + 2 * GLA_VW
    c_q = n_gla + GLA_RANK
    c_g = c_q + 3 * FOX_W + FOX_HEADS
    (pg,) = _mm_win(h, w, 0, n_gla, tm, tn)
    (qb,) = _mm_win(h, w, c_q, FOX_W, tm, tn, (BF16,))
    kb, kb16 = _mm_heads(h, w, c_q + FOX_W, min(tm, 512))
    vb, vb16 = _mm_heads(h, w, c_q + 2 * FOX_W, min(tm, 512))
    (gates,) = _mm_win(h, w, c_g, 2 * D_MODEL, tm, tn)
    (small,) = _mm(h, p["w_small"], tm, LANES)
    return pg, qb, kb, kb16, vb, vb16, gates, small


def _tail(x, oa, ob, gates, mod, p, tm, tiles_per_batch, grouped):
    merged = _merge(oa, ob, p["w_up_a"], p["w_up_b"], gates, tm)
    tmo = min(tm, 256)
    x1, hx = _outproj(merged, p["w_out"], x, mod, p["g2"], p["w_router"], p["b_router"],
                      tmo, tiles_per_batch * (tm // tmo))
    if not grouped:
        tmm = min(tm, 512)
        return _moe(hx[:, 0:D_MODEL].astype(BF16), hx[:, D_MODEL:HX_W], p["wg"], p["wu"], p["wd"], x1, mod, p["gf"],
                    tmm, tiles_per_batch * (tm // tmm))
    rows_out = x.shape[0] + N_GROUPS * MOE_TILE
    pos, tg, nu = _moe_plan(hx, rows_out // MOE_TILE)
    ys = _moe_grouped(hx, pos, tg, nu, p["wg"], p["wu"], p["wd"], rows_out)
    return _moe_combine(ys, pos, x1, mod, p["gf"], tiles_per_batch * (tm // COMBINE_ROWS))


def kernel(x_prompt, x_sample, cache_k, cache_v, cache_logf, state_gla, page_table, c_prompt, c_sample, w_ada, b_ada,
           g_norm1, g_norm2, g_final, w_in, w_a2, b_a, b_f, g_gla_norm, w_up_a, w_up_b, w_out, w_grp, b_grp, w_exp,
           b_exp, w_gate_e, w_up_e, w_down_e):
    p = _prep_weights(w_ada, b_ada, g_norm1, g_norm2, g_final, w_in, w_a2, b_a, b_f, g_gla_norm, w_up_a, w_up_b,
                      w_out, w_grp, b_grp, w_exp, b_exp, w_gate_e, w_up_e, w_down_e)
    bp, seq, d = x_prompt.shape
    bs, t, _ = x_sample.shape
    assert t <= 2 ** SAMPLE_LEVELS and t <= ROWS8

    mod = _adaln(jnp.concatenate([c_prompt, c_sample], axis=0), p["w_ada"], p["b_ada"])
    mod_p = mod[:bp].reshape(bp, 6, 1, d).transpose(1, 0, 2, 3)
    mod_s = jnp.repeat(mod[bp:].reshape(bs, 6, d), t, axis=0).transpose(1, 0, 2)[:, None]

    tm = 1024
    tpb = seq // tm
    xp = x_prompt.reshape(bp * seq, d)
    hp = _normmod(xp, p["g1"], mod_p, tm, tpb)
    pg, qb, kb, kb16, vb, vb16, gates, small = _project(hp, p, tm)
    s0 = jnp.zeros((bp, GLA_HEADS, GLA_DK, GLA_DV), F32)
    oa, s_p = _gla_prompt(pg, small, s0, p["wall"], p["mask"], p["w_a2"], p["b_a"], p["g_gla"], bp, seq)
    lf_p, fcol, ft = _fox_bias_prompt(small, p["bf_row"], bp, seq)
    ob = _fox_prompt(qb, kb16, vb16, fcol, ft, bp, seq)
    y_p = _tail(xp, oa, ob, gates, mod_p, p, tm, tpb, grouped=True)

    rows = bs * t
    xs = x_sample.reshape(rows, d)
    hs = _normmod(xs, p["g1"], mod_s, rows, 1)
    pg_s, qs, ks, _, vs, _, gates_s, small_s = _project(hs, p, rows)
    oa_s, s_s = _gla_sample(pg_s.reshape(bs, t, -1), small_s.reshape(bs, t, LANES),
                            state_gla.reshape(state_gla.shape[1:]), p["wall_s"], p["mask_s"],
                            p["w_a2"], p["b_a"], p["g_gla"], bs, t)
    lf_s, fn_s = _fox_bias_sample(small_s, p["bf_row"], t)
    n_pool = cache_k.shape[1]
    ob_s = _fox_sample(page_table, qs.astype(F32).reshape(bs, t, FOX_W), ks.reshape(bs, t, FOX_HEADS, FOX_DH),
                       vs.reshape(bs, t, FOX_HEADS, FOX_DH), fn_s.reshape(bs, t, FOX_HEADS),
                       cache_k, cache_v, cache_logf.reshape(n_pool, 1, PAGE_FLAT), t)
    y_s = _tail(xs, oa_s.reshape(rows, GLA_VW).astype(BF16), ob_s.reshape(rows, FOX_W).astype(BF16), gates_s, mod_s,
                p, rows, 1, grouped=False)

    return (y_p.reshape(bp, seq, d), y_s.reshape(bs, t, d),
            kb.reshape(1, bp, seq, FOX_HEADS, FOX_DH), vb.reshape(1, bp, seq, FOX_HEADS, FOX_DH),
            lf_p.reshape(1, bp, seq, FOX_HEADS), s_p[None],
            ks.reshape(1, bs, t, FOX_HEADS, FOX_DH), vs.reshape(1, bs, t, FOX_HEADS, FOX_DH),
            lf_s.reshape(1, bs, t, FOX_HEADS), s_s[None])
```

```python
import functools

import numpy as np
import jax
import jax.numpy as jnp
from jax import lax
from jax.experimental import pallas as pl
from jax.experimental.pallas import tpu as pltpu

F32 = jnp.float32
BF16 = jnp.bfloat16

D_MODEL = 2048
GLA_HEADS = 4
GLA_DK = 128
GLA_DV = 256
GLA_RANK = 16
GLA_TAU = 16.0
FOX_HEADS = 8
FOX_DH = 128
PAGE = 128
N_GROUPS = 4
EXP_PER_GROUP = 4
N_EXPERTS = 16
D_EXPERT = 512
RMS_EPS = 1e-6
GLA_KW = GLA_HEADS * GLA_DK
GLA_VW = GLA_HEADS * GLA_DV
FOX_W = FOX_HEADS * FOX_DH
GLA_CHUNK = 128
GLA_LEVELS = 7
SAMPLE_LEVELS = 2
LANES = 128
NEG = -1e30
VMEM_LIMIT = 56 * 1024 * 1024


def _cparams(sem):
    return pltpu.CompilerParams(dimension_semantics=sem, vmem_limit_bytes=VMEM_LIMIT)


def _dot(a, b):
    return jnp.dot(a, b, preferred_element_type=F32)


def _dot_nt(a, b):
    return lax.dot_general(a, b, (((1,), (1,)), ((), ())), preferred_element_type=F32)


def _dot_tn(a, b):
    return lax.dot_general(a, b, (((0,), (0,)), ((), ())), preferred_element_type=F32)


def _split2(x):
    hi = x.astype(BF16)
    lo = (x - hi.astype(F32)).astype(BF16)
    return hi, lo


def _split3(x):
    hi = x.astype(BF16)
    r = x - hi.astype(F32)
    mid = r.astype(BF16)
    lo = (r - mid.astype(F32)).astype(BF16)
    return hi, mid, lo


def _dot3(a, b):
    ah, al = _split2(a)
    bh, bl = _split2(b)
    return _dot(ah, bh) + _dot(ah, bl) + _dot(al, bh)


def _dot_sel(w01, x):
    hi, mid, lo = _split3(x)
    return _dot(w01, hi) + _dot(w01, mid) + _dot(w01, lo)


def _log_sigmoid(x):
    return jnp.minimum(x, 0.0) - jnp.log1p(jnp.exp(-jnp.abs(x)))


def _silu(x):
    return x * jax.nn.sigmoid(x)


def _adaln_kernel(c_ref, w_ref, b_ref, o_ref):
    o_ref[...] = _dot3(_silu(c_ref[...]), w_ref[...]) + b_ref[...]


def _adaln(c, w, b, tn=512):
    nb, d = c.shape
    n = w.shape[1]
    return pl.pallas_call(
        _adaln_kernel,
        grid=(n // tn,),
        in_specs=[pl.BlockSpec((nb, d), lambda j: (0, 0)),
                  pl.BlockSpec((d, tn), lambda j: (0, j)),
                  pl.BlockSpec((1, tn), lambda j: (0, j))],
        out_specs=pl.BlockSpec((nb, tn), lambda j: (0, j)),
        out_shape=jax.ShapeDtypeStruct((nb, n), F32),
        compiler_params=_cparams(("arbitrary",)),
        name="adaln",
    )(c, w, b)


def _rms(x):
    return x * lax.rsqrt(jnp.mean(x * x, axis=-1, keepdims=True) + RMS_EPS)


def _normmod_kernel(x_ref, g_ref, sh_ref, sc_ref, o_ref):
    y = _rms(x_ref[...]) * g_ref[...]
    o_ref[...] = (y * (1.0 + sc_ref[...]) + sh_ref[...]).astype(o_ref.dtype)


def _mod_spec(k, rb, d, tiles_per_batch):
    return pl.BlockSpec((None, None, rb, d), lambda i: (k, i // tiles_per_batch, 0, 0))


def _normmod(x, g, mod, tm, tiles_per_batch):
    m, d = x.shape
    rb = mod.shape[2]
    return pl.pallas_call(
        _normmod_kernel,
        grid=(m // tm,),
        in_specs=[pl.BlockSpec((tm, d), lambda i: (i, 0)),
                  pl.BlockSpec((1, d), lambda i: (0, 0)),
                  _mod_spec(0, rb, d, tiles_per_batch),
                  _mod_spec(1, rb, d, tiles_per_batch)],
        out_specs=pl.BlockSpec((tm, d), lambda i: (i, 0)),
        out_shape=jax.ShapeDtypeStruct((m, d), BF16),
        compiler_params=_cparams(("arbitrary",)),
        name="normmod",
    )(x, g, mod, mod)


def _wt_spec(k, row0, tn):
    assert row0 % 8 == 0 and tn % 8 == 0
    return pl.BlockSpec((pl.Element(tn), pl.Element(k)), lambda j, i: (pl.multiple_of(row0 + j * tn, 8), 0))


def _mm_wt_kernel(x_ref, wt_ref, *refs):
    o_refs, wb = refs[:-1], refs[-1]

    @pl.when(pl.program_id(1) == 0)
    def _():
        wb[...] = wt_ref[...].astype(BF16)

    r = _dot_nt(x_ref[...], wb[...])
    for o_ref in o_refs:
        o_ref[...] = r.astype(o_ref.dtype)


def _mm_wt(x, wt, row0, n, tm, tn, out_dtypes=(F32,)):
    m, k = x.shape
    return pl.pallas_call(
        _mm_wt_kernel,
        grid=(n // tn, m // tm),
        in_specs=[pl.BlockSpec((tm, k), lambda j, i: (i, 0)), _wt_spec(k, row0, tn)],
        out_specs=[pl.BlockSpec((tm, tn), lambda j, i: (i, j)) for _ in out_dtypes],
        out_shape=[jax.ShapeDtypeStruct((m, n), dt) for dt in out_dtypes],
        scratch_shapes=[pltpu.VMEM((tn, k), BF16)],
        compiler_params=_cparams(("arbitrary", "arbitrary")),
        name="proj_mm",
    )(x, wt)


def _mm_heads_kernel(x_ref, wt_ref, o_ref, o16_ref, wb):
    @pl.when(pl.program_id(1) == 0)
    def _():
        wb[...] = wt_ref[...].astype(BF16)

    r = _dot_nt(x_ref[...], wb[...])
    for h in range(FOX_HEADS):
        o_ref[:, h, :] = r[:, h * FOX_DH:(h + 1) * FOX_DH]
    o16_ref[...] = r.astype(BF16)


def _mm_heads(x, wt, row0, tm):
    m, k = x.shape
    return pl.pallas_call(
        _mm_heads_kernel,
        grid=(1, m // tm),
        in_specs=[pl.BlockSpec((tm, k), lambda j, i: (i, 0)), _wt_spec(k, row0, FOX_W)],
        out_specs=[pl.BlockSpec((tm, FOX_HEADS, FOX_DH), lambda j, i: (i, 0, 0)),
                   pl.BlockSpec((tm, FOX_W), lambda j, i: (i, 0))],
        out_shape=[jax.ShapeDtypeStruct((m, FOX_HEADS, FOX_DH), F32),
                   jax.ShapeDtypeStruct((m, FOX_W), BF16)],
        scratch_shapes=[pltpu.VMEM((FOX_W, k), BF16)],
        compiler_params=_cparams(("arbitrary", "arbitrary")),
        name="proj_heads",
    )(x, wt)


def _gla_tables(p=GLA_LEVELS):
    c = GLA_CHUNK
    t = np.arange(c)[:, None]
    m = np.arange(c)[None, :]
    wall = np.zeros((p + 2, c, c), np.float32)
    mask = np.zeros((p + 1, c, c), np.float32)
    for l in range(p):
        half = 1 << l
        pos = t % (2 * half)
        mid = t - pos + half
        right = pos >= half
        wall[l] = np.where(right, (m >= mid) & (m <= t), (m > t) & (m < mid))
        s = m
        mask[l] = ((t >> (l + 1)) == (s >> (l + 1))) & (((t >> l) & 1) == 1) & (((s >> l) & 1) == 0)
    wall[p] = m <= t
    wall[p + 1] = m > t
    mask[p] = t == m
    return wall.reshape((p + 2) * c, c), mask


def _gla_body(q_ref, k_ref, v_ref, ra_ref, sm_ref, wall_ref, mask_ref, wa2_ref, ba_ref, g_ref,
              og_ref, st_ref, *, nchunk, valid, store_rows):
    c, p = GLA_CHUNK, mask_ref.shape[0] - 1
    row = lax.broadcasted_iota(jnp.int32, (c, GLA_DK), 0)
    wall = wall_ref[...]
    for ci in range(nchunk):
        rows = pl.ds(ci * c, c)
        x = _dot3(sm_ref[rows, :], wa2_ref[...]) + ba_ref[...]
        la = _log_sigmoid(x) * (1.0 / GLA_TAU)
        if valid < c:
            rowh = lax.broadcasted_iota(jnp.int32, la.shape, 0)
            la = jnp.where(rowh < valid, la, 0.0)
        e_all = jnp.exp(_dot_sel(wall, la))
        for h in range(GLA_HEADS):
            ks = slice(h * GLA_DK, (h + 1) * GLA_DK)
            vs = slice(h * GLA_DV, (h + 1) * GLA_DV)
            q = q_ref[rows, ks] * (GLA_DK ** -0.5)
            k = k_ref[rows, ks]
            vb = v_ref[rows, vs].astype(BF16)
            a = mask_ref[p] * _dot_nt(q.astype(BF16), k.astype(BF16))
            for l in range(p):
                el = e_all[l * c:(l + 1) * c, ks]
                xl = (jnp.where(((row >> l) & 1) == 1, q, k) * el).astype(BF16)
                a = a + mask_ref[l] * _dot_nt(xl, xl)
            st = st_ref[h]
            qc = (q * e_all[p * c:(p + 1) * c, ks]).astype(BF16)
            o = _dot(a.astype(BF16), vb) + _dot_nt(qc, st.astype(BF16))
            kr = (k * e_all[(p + 1) * c:(p + 2) * c, ks]).astype(BF16)
            dec = e_all[(p + 1) * c - 1:(p + 1) * c, ks]
            st_ref[h] = dec * st + _dot_tn(vb, kr)
            og = _rms(o) * g_ref[...] * _silu(ra_ref[rows, vs])
            if store_rows < c:
                og_ref[:, vs] = og[:store_rows].astype(og_ref.dtype)
            else:
                og_ref[rows, vs] = og.astype(og_ref.dtype)


def _gla_prompt_kernel(q_ref, k_ref, v_ref, ra_ref, sm_ref, wall_ref, mask_ref, wa2_ref, ba_ref, g_ref, s0_ref,
                       og_ref, s_ref, st_ref, *, nchunk):
    n = pl.program_id(1)

    @pl.when(n == 0)
    def _():
        for h in range(GLA_HEADS):
            st_ref[h] = s0_ref[h].T

    _gla_body(q_ref, k_ref, v_ref, ra_ref, sm_ref, wall_ref, mask_ref, wa2_ref, ba_ref, g_ref, og_ref, st_ref,
              nchunk=nchunk, valid=GLA_CHUNK, store_rows=GLA_CHUNK)

    @pl.when(n == pl.num_programs(1) - 1)
    def _():
        for h in range(GLA_HEADS):
            s_ref[h] = st_ref[h].T


def _gla_prompt(pg, small, s0, wall, mask, wa2, ba, g, batch, seq, tb=256):
    nblk = seq // tb
    rowmap = lambda cb: (lambda b, n: (b * nblk + n, cb))
    const2 = lambda b, n: (0, 0)
    return pl.pallas_call(
        functools.partial(_gla_prompt_kernel, nchunk=tb // GLA_CHUNK),
        grid=(batch, nblk),
        in_specs=[pl.BlockSpec((tb, GLA_KW), rowmap(0)),
                  pl.BlockSpec((tb, GLA_KW), rowmap(1)),
                  pl.BlockSpec((tb, GLA_VW), rowmap(1)),
                  pl.BlockSpec((tb, GLA_VW), rowmap(2)),
                  pl.BlockSpec((tb, LANES), rowmap(0)),
                  pl.BlockSpec(wall.shape, const2),
                  pl.BlockSpec(mask.shape, lambda b, n: (0, 0, 0)),
                  pl.BlockSpec(wa2.shape, const2),
                  pl.BlockSpec(ba.shape, const2),
                  pl.BlockSpec(g.shape, const2),
                  pl.BlockSpec((None, GLA_HEADS, GLA_DK, GLA_DV), lambda b, n: (b, 0, 0, 0))],
        out_specs=[pl.BlockSpec((tb, GLA_VW), rowmap(0)),
                   pl.BlockSpec((None, GLA_HEADS, GLA_DK, GLA_DV), lambda b, n: (b, 0, 0, 0))],
        out_shape=[jax.ShapeDtypeStruct((batch * seq, GLA_VW), BF16),
                   jax.ShapeDtypeStruct((batch, GLA_HEADS, GLA_DK, GLA_DV), F32)],
        scratch_shapes=[pltpu.VMEM((GLA_HEADS, GLA_DV, GLA_DK), F32)],
        compiler_params=_cparams(("arbitrary", "arbitrary")),
        name="gla_prompt",
    )(pg, pg, pg, pg, small, wall, mask, wa2, ba, g, s0)


def _gla_sample_kernel(pg_ref, sm_ref, wall_ref, mask_ref, wa2_ref, ba_ref, g_ref, s0_ref,
                       og_ref, s_ref, pad_ref, smpad_ref, st_ref, *, t):
    @pl.when(pl.program_id(0) == 0)
    def _():
        pad_ref[...] = jnp.zeros(pad_ref.shape, F32)
        smpad_ref[...] = jnp.zeros(smpad_ref.shape, F32)

    pad_ref[0:t, :] = pg_ref[...]
    smpad_ref[0:t, :] = sm_ref[...]
    for h in range(GLA_HEADS):
        st_ref[h] = s0_ref[h].T
    q_ref = pad_ref.at[:, 0:GLA_KW]
    k_ref = pad_ref.at[:, GLA_KW:2 * GLA_KW]
    v_ref = pad_ref.at[:, 2 * GLA_KW:2 * GLA_KW + GLA_VW]
    ra_ref = pad_ref.at[:, 2 * GLA_KW + GLA_VW:2 * GLA_KW + 2 * GLA_VW]
    _gla_body(q_ref, k_ref, v_ref, ra_ref, smpad_ref, wall_ref, mask_ref, wa2_ref, ba_ref, g_ref, og_ref, st_ref,
              nchunk=1, valid=t, store_rows=t)
    for h in range(GLA_HEADS):
        s_ref[h] = st_ref[h].T


def _gla_sample(pg, small, s0, wall, mask, wa2, ba, g, batch, t):
    width = pg.shape[-1]
    const2 = lambda b: (0, 0)
    return pl.pallas_call(
        functools.partial(_gla_sample_kernel, t=t),
        grid=(batch,),
        in_specs=[pl.BlockSpec((None, t, width), lambda b: (b, 0, 0)),
                  pl.BlockSpec((None, t, LANES), lambda b: (b, 0, 0)),
                  pl.BlockSpec(wall.shape, const2),
                  pl.BlockSpec(mask.shape, lambda b: (0, 0, 0)),
                  pl.BlockSpec(wa2.shape, const2),
                  pl.BlockSpec(ba.shape, const2),
                  pl.BlockSpec(g.shape, const2),
                  pl.BlockSpec((None, GLA_HEADS, GLA_DK, GLA_DV), lambda b: (b, 0, 0, 0))],
        out_specs=[pl.BlockSpec((None, t, GLA_VW), lambda b: (b, 0, 0)),
                   pl.BlockSpec((None, GLA_HEADS, GLA_DK, GLA_DV), lambda b: (b, 0, 0, 0))],
        out_shape=[jax.ShapeDtypeStruct((batch, t, GLA_VW), F32),
                   jax.ShapeDtypeStruct((batch, GLA_HEADS, GLA_DK, GLA_DV), F32)],
        scratch_shapes=[pltpu.VMEM((GLA_CHUNK, width), F32),
                        pltpu.VMEM((GLA_CHUNK, LANES), F32),
                        pltpu.VMEM((GLA_HEADS, GLA_DV, GLA_DK), F32)],
        compiler_params=_cparams(("arbitrary",)),
        name="gla_sample",
    )(pg, small, wall, mask, wa2, ba, g, s0)


FB_LANE = GLA_RANK


def _fox_bias_prompt_kernel(sm_ref, bf_ref, tri_ref, lf_ref, fc_ref, ft_ref, carry_ref):
    @pl.when(pl.program_id(1) == 0)
    def _():
        carry_ref[...] = jnp.zeros(carry_ref.shape, F32)

    lf = _log_sigmoid(sm_ref[...] + bf_ref[...])
    lf_ref[...] = lf[:, FB_LANE:FB_LANE + FOX_HEADS]
    cum = _dot_sel(tri_ref[...], lf) + carry_ref[...]
    carry_ref[...] = cum[cum.shape[0] - 1:, :]
    fc_ref[...] = cum
    ft_ref[...] = cum.T[FB_LANE:FB_LANE + FOX_HEADS, :]


def _fox_bias_prompt(small, bf_row, batch, seq, tb=256):
    nblk = seq // tb
    tri = jnp.asarray(np.tril(np.ones((tb, tb), np.float32)), BF16)
    return pl.pallas_call(
        _fox_bias_prompt_kernel,
        grid=(batch, nblk),
        in_specs=[pl.BlockSpec((tb, LANES), lambda b, n: (b * nblk + n, 0)),
                  pl.BlockSpec((1, LANES), lambda b, n: (0, 0)),
                  pl.BlockSpec((tb, tb), lambda b, n: (0, 0))],
        out_specs=[pl.BlockSpec((tb, FOX_HEADS), lambda b, n: (b * nblk + n, 0)),
                   pl.BlockSpec((tb, LANES), lambda b, n: (b * nblk + n, 0)),
                   pl.BlockSpec((None, FOX_HEADS, tb), lambda b, n: (b, 0, n))],
        out_shape=[jax.ShapeDtypeStruct((batch * seq, FOX_HEADS), F32),
                   jax.ShapeDtypeStruct((batch * seq, LANES), F32),
                   jax.ShapeDtypeStruct((batch, FOX_HEADS, seq), F32)],
        scratch_shapes=[pltpu.VMEM((1, LANES), F32)],
        compiler_params=_cparams(("arbitrary", "arbitrary")),
        name="fox_bias_prompt",
    )(small, bf_row, tri)


def _fox_bias_sample_kernel(sm_ref, bf_ref, sel_ref, lf_ref, fn_ref):
    lf = _log_sigmoid(sm_ref[...] + bf_ref[...])
    lf_ref[...] = lf[:, FB_LANE:FB_LANE + FOX_HEADS]
    cum = _dot_sel(sel_ref[...], lf)
    fn_ref[...] = cum[:, FB_LANE:FB_LANE + FOX_HEADS]


def _fox_bias_sample(small, bf_row, t):
    rows = small.shape[0]
    r = np.arange(rows)
    sel = ((r[:, None] // t) == (r[None, :] // t)) & (r[None, :] <= r[:, None])
    sel = jnp.asarray(sel.astype(np.float32), BF16)
    full = lambda shape: pl.BlockSpec(shape, lambda i: tuple(0 for _ in shape))
    return pl.pallas_call(
        _fox_bias_sample_kernel,
        grid=(1,),
        in_specs=[full((rows, LANES)), full((1, LANES)), full((rows, rows))],
        out_specs=[full((rows, FOX_HEADS)), full((rows, FOX_HEADS))],
        out_shape=[jax.ShapeDtypeStruct((rows, FOX_HEADS), F32),
                   jax.ShapeDtypeStruct((rows, FOX_HEADS), F32)],
        compiler_params=_cparams(("arbitrary",)),
        name="fox_bias_sample",
    )(small, bf_row, sel)


FOX_HEADS_PER_STEP = 8


def _fox_prompt_kernel(q_ref, k_ref, v_ref, fc_ref, ft_ref, o_ref, *, tq, tk):
    hb = FOX_HEADS_PER_STEP
    hg = pl.program_id(1)
    i = pl.program_id(2)
    lane = lax.broadcasted_iota(jnp.int32, (tq, LANES), 1)
    rowi = lax.broadcasted_iota(jnp.int32, (tq, tk), 0)
    coli = lax.broadcasted_iota(jnp.int32, (tq, tk), 1)
    fc = fc_ref[...]
    qs, f_ts = [], []
    for hh in range(hb):
        hs = slice(hh * FOX_DH, (hh + 1) * FOX_DH)
        qs.append((q_ref[:, hs].astype(F32) * (FOX_DH ** -0.5)).astype(BF16))
        f_ts.append(jnp.sum(jnp.where(lane == hg * hb + hh + FB_LANE, fc, 0.0), axis=-1, keepdims=True))

    def step(j, carry, masked):
        ks = pl.ds(pl.multiple_of(j * tk, tk), tk)
        out = []
        for hh in range(hb):
            hs = slice(hh * FOX_DH, (hh + 1) * FOX_DH)
            m, l, acc = carry[hh]
            s = _dot_nt(qs[hh], k_ref[ks, hs])
            s = s + f_ts[hh] - ft_ref[hh:hh + 1, ks]
            if masked:
                s = jnp.where(rowi >= coli, s, NEG)
            m_new = jnp.maximum(m, jnp.max(s, axis=-1, keepdims=True))
            p = jnp.exp(s - m_new)
            alpha = jnp.exp(m - m_new)
            l = alpha * l + jnp.sum(p, axis=-1, keepdims=True)
            acc = alpha * acc + _dot(p.astype(BF16), v_ref[ks, hs])
            out.append((m_new, l, acc))
        return tuple(out)

    init = tuple((jnp.full((tq, 1), NEG, F32), jnp.zeros((tq, 1), F32), jnp.zeros((tq, FOX_DH), F32))
                 for _ in range(hb))
    nfull = i * (tq // tk)
    carry = lax.fori_loop(0, nfull, lambda j, c: step(j, c, False), init)
    carry = step(nfull, carry, True)
    for hh in range(hb):
        m, l, acc = carry[hh]
        o_ref[:, hh * FOX_DH:(hh + 1) * FOX_DH] = (acc / l).astype(o_ref.dtype)


def _fox_prompt(qb, kb, vb, fcol, ft, batch, seq, tq=256):
    nq = seq // tq
    hb = FOX_HEADS_PER_STEP
    wb = hb * FOX_DH
    return pl.pallas_call(
        functools.partial(_fox_prompt_kernel, tq=tq, tk=tq),
        grid=(batch, FOX_HEADS // hb, nq),
        in_specs=[pl.BlockSpec((tq, wb), lambda b, h, i: (b * nq + i, h)),
                  pl.BlockSpec((seq, wb), lambda b, h, i: (b, h)),
                  pl.BlockSpec((seq, wb), lambda b, h, i: (b, h)),
                  pl.BlockSpec((tq, LANES), lambda b, h, i: (b * nq + i, 0)),
                  pl.BlockSpec((None, None, hb, seq), lambda b, h, i: (b, h, 0, 0))],
        out_specs=pl.BlockSpec((tq, wb), lambda b, h, i: (b * nq + i, h)),
        out_shape=jax.ShapeDtypeStruct((batch * seq, FOX_W), BF16),
        compiler_params=_cparams(("arbitrary", "arbitrary", "arbitrary")),
        name="fox_prompt",
    )(qb, kb, vb, fcol, ft.reshape(batch, FOX_HEADS // hb, hb, seq))


PAGES_PER_STEP = 16
ROWS8 = 8
QROWS = FOX_HEADS * ROWS8
PAGE_FLAT = PAGE * FOX_HEADS


def _dot_sel_rhs(x, w01):
    hi, mid, lo = _split3(x)
    m = x.shape[0]
    stacked = jnp.concatenate([hi.astype(F32), mid.astype(F32), lo.astype(F32)], axis=0).astype(BF16)
    r = _dot(stacked, w01)
    return r[0:m] + r[m:2 * m] + r[2 * m:3 * m]


def _fox_sample_kernel(pt_ref, q_ref, kn_ref, vn_ref, fn_ref, fnrow_ref, madd_ref, maddn_ref, usuf_ref, tot_ref,
                       *refs, t):
    g = PAGES_PER_STEP
    k_refs, v_refs, lf_refs = refs[0:g], refs[g:2 * g], refs[2 * g:3 * g]
    o_ref = refs[3 * g]
    q_sc, m_sc, l_sc, acc_sc, carry_sc = refs[3 * g + 1:]
    j = pl.program_id(1)
    fn_t = fn_ref[...][:, 0:1]

    @pl.when(j == 0)
    def _():
        q_sc[...] = jnp.zeros(q_sc.shape, F32)
        for h in range(FOX_HEADS):
            q_sc[h * ROWS8:h * ROWS8 + t, :] = q_ref[:, h * FOX_DH:(h + 1) * FOX_DH] * (FOX_DH ** -0.5)
        carry_sc[...] = jnp.zeros(carry_sc.shape, F32)
        pad = jnp.zeros((LANES - t * FOX_HEADS, FOX_DH), F32)
        kn = jnp.concatenate([kn_ref[...].reshape(t * FOX_HEADS, FOX_DH), pad], axis=0)
        vn = jnp.concatenate([vn_ref[...].reshape(t * FOX_HEADS, FOX_DH), pad], axis=0)
        s = _dot_nt(q_sc[...], kn) + fn_t - fnrow_ref[...] + maddn_ref[...]
        m = jnp.max(s, axis=-1, keepdims=True)
        p = jnp.exp(s - m)
        m_sc[...] = jnp.broadcast_to(m, m_sc.shape)
        l_sc[...] = jnp.broadcast_to(jnp.sum(p, axis=-1, keepdims=True), l_sc.shape)
        acc_sc[...] = _dot(p, vn)

    lf = jnp.concatenate([lf_refs[gi][...] for gi in range(g)], axis=0)
    r_in = _dot_sel_rhs(lf, usuf_ref[...])
    page_tot = _dot_sel_rhs((r_in + lf)[:, 0:LANES], tot_ref[...])
    carry = carry_sc[...]
    q = q_sc[...]
    madd = madd_ref[...] + fn_t
    m_old = m_sc[...]
    m_new = m_old
    s_list = []
    for gi in range(g):
        bias = madd + (r_in[gi:gi + 1, :] + carry)
        carry = carry + page_tot[gi:gi + 1, :]
        s = _dot_nt(q, k_refs[gi][...].reshape(PAGE_FLAT, FOX_DH)) + bias
        s_list.append(s)
        m_new = jnp.maximum(m_new, jnp.max(s, axis=-1, keepdims=True))
    carry_sc[...] = carry
    alpha = jnp.exp(m_old - m_new)
    l = alpha * l_sc[...]
    acc = alpha * acc_sc[...]
    m_col = m_new[:, 0:1]
    for gi in range(g):
        p = jnp.exp(s_list[gi] - m_col)
        l = l + jnp.sum(p, axis=-1, keepdims=True)
        acc = acc + _dot(p, v_refs[gi][...].reshape(PAGE_FLAT, FOX_DH))
    m_sc[...] = m_new
    l_sc[...] = l
    acc_sc[...] = acc

    @pl.when(j == pl.num_programs(1) - 1)
    def _():
        o = acc / l
        for h in range(FOX_HEADS):
            o_ref[:, h * FOX_DH:(h + 1) * FOX_DH] = o[h * ROWS8:h * ROWS8 + t, :].astype(o_ref.dtype)


def _fox_sample_tables(t):
    row = np.arange(QROWS)[:, None]
    col = np.arange(PAGE_FLAT)[None, :]
    madd = np.where((row // ROWS8) == (col % FOX_HEADS), 0.0, NEG).astype(np.float32)
    coln = np.arange(LANES)[None, :]
    ok = (coln < t * FOX_HEADS) & ((row // ROWS8) == (coln % FOX_HEADS)) & ((coln // FOX_HEADS) <= (row % ROWS8))
    maddn = np.where(ok, 0.0, NEG).astype(np.float32)
    src = np.arange(PAGE_FLAT)[:, None]
    same_head = (src % FOX_HEADS) == (col % FOX_HEADS)
    usuf = (same_head & ((src // FOX_HEADS) > (col // FOX_HEADS))).astype(np.float32)
    lane = np.arange(LANES)[:, None]
    tot = ((lane < FOX_HEADS) & (lane == (col % FOX_HEADS))).astype(np.float32)
    return (jnp.asarray(madd), jnp.asarray(maddn), jnp.asarray(usuf, BF16), jnp.asarray(tot, BF16))


def _fox_sample(page_table, q, kn, vn, fn, cache_k, cache_v, cache_lf, t):
    batch, npages = page_table.shape
    g = PAGES_PER_STEP
    nsteps = npages // g
    madd, maddn, usuf, tot = _fox_sample_tables(t)
    fn_rows = jnp.pad(fn.transpose(0, 2, 1), ((0, 0), (0, 0), (0, ROWS8 - t))).reshape(batch, QROWS, 1)
    fn_rows = jnp.broadcast_to(fn_rows, (batch, QROWS, LANES))
    fn_cols = jnp.pad(fn.reshape(batch, 1, t * FOX_HEADS), ((0, 0), (0, 0), (0, LANES - t * FOX_HEADS)))

    def page_map5(gi):
        return lambda b, j, pt: (0, pt[b, npages - 1 - (j * g + gi)], 0, 0, 0)

    def page_map3(gi):
        return lambda b, j, pt: (pt[b, npages - 1 - (j * g + gi)], 0, 0)

    seq3 = lambda b, j, pt: (b, 0, 0)
    seq4 = lambda b, j, pt: (b, 0, 0, 0)
    const2 = lambda b, j, pt: (0, 0)
    in_specs = [pl.BlockSpec((None, t, FOX_W), seq3),
                pl.BlockSpec((None, t, FOX_HEADS, FOX_DH), seq4),
                pl.BlockSpec((None, t, FOX_HEADS, FOX_DH), seq4),
                pl.BlockSpec((None, QROWS, LANES), seq3),
                pl.BlockSpec((None, 1, LANES), seq3),
                pl.BlockSpec(madd.shape, const2),
                pl.BlockSpec(maddn.shape, const2),
                pl.BlockSpec(usuf.shape, const2),
                pl.BlockSpec(tot.shape, const2)]
    in_specs += [pl.BlockSpec((None, None, PAGE, FOX_HEADS, FOX_DH), page_map5(gi)) for gi in range(g)]
    in_specs += [pl.BlockSpec((None, None, PAGE, FOX_HEADS, FOX_DH), page_map5(gi)) for gi in range(g)]
    in_specs += [pl.BlockSpec((None, 1, PAGE_FLAT), page_map3(gi)) for gi in range(g)]
    grid_spec = pltpu.PrefetchScalarGridSpec(
        num_scalar_prefetch=1,
        grid=(batch, nsteps),
        in_specs=in_specs,
        out_specs=pl.BlockSpec((None, t, FOX_W), seq3),
        scratch_shapes=[pltpu.VMEM((QROWS, FOX_DH), F32),
                        pltpu.VMEM((QROWS, LANES), F32),
                        pltpu.VMEM((QROWS, LANES), F32),
                        pltpu.VMEM((QROWS, FOX_DH), F32),
                        pltpu.VMEM((1, PAGE_FLAT), F32)],
    )
    return pl.pallas_call(
        functools.partial(_fox_sample_kernel, t=t),
        grid_spec=grid_spec,
        out_shape=jax.ShapeDtypeStruct((batch, t, FOX_W), F32),
        compiler_params=_cparams(("arbitrary", "arbitrary")),
        name="fox_sample",
    )(page_table, q, kn, vn, fn_rows, fn_cols, madd, maddn, usuf, tot,
      *([cache_k] * g), *([cache_v] * g), *([cache_lf] * g))


def _merge_kernel(oa_ref, ob_ref, wa_ref, wb_ref, ga_ref, gb_ref, o_ref):
    ua = _dot(oa_ref[...], wa_ref[...])
    ub = _dot(ob_ref[...], wb_ref[...])
    ga = jax.nn.sigmoid(ga_ref[...].astype(F32))
    gb = jax.nn.sigmoid(gb_ref[...].astype(F32))
    o_ref[...] = (ga * ua + gb * ub).astype(o_ref.dtype)


def _merge(oa, ob, wa, wb, gates, tm, tn=1024):
    m = oa.shape[0]
    d = wa.shape[1]
    nj = d // tn
    return pl.pallas_call(
        _merge_kernel,
        grid=(nj, m // tm),
        in_specs=[pl.BlockSpec((tm, GLA_VW), lambda j, i: (i, 0)),
                  pl.BlockSpec((tm, FOX_W), lambda j, i: (i, 0)),
                  pl.BlockSpec((GLA_VW, tn), lambda j, i: (0, j)),
                  pl.BlockSpec((FOX_W, tn), lambda j, i: (0, j)),
                  pl.BlockSpec((tm, tn), lambda j, i: (i, j)),
                  pl.BlockSpec((tm, tn), lambda j, i: (i, nj + j))],
        out_specs=pl.BlockSpec((tm, tn), lambda j, i: (i, j)),
        out_shape=jax.ShapeDtypeStruct((m, d), BF16),
        compiler_params=_cparams(("arbitrary", "arbitrary")),
        name="merge",
    )(oa, ob, wa, wb, gates, gates)


ROUTER_GROUP_LANE = N_EXPERTS


def _route(logits):
    lane_i = lax.broadcasted_iota(jnp.int32, logits.shape, 1)
    lane = lane_i.astype(F32)
    grp_of_lane = (lane_i >> 2).astype(F32)
    big = float(LANES)
    is_grp = (lane_i >= ROUTER_GROUP_LANE) & (lane_i < ROUTER_GROUP_LANE + N_GROUPS)
    gl = jnp.where(is_grp, logits, NEG)
    gmax = jnp.max(gl, axis=-1, keepdims=True)
    g_idx = jnp.min(jnp.where(is_grp & (gl == gmax), lane - ROUTER_GROUP_LANE, big), axis=-1, keepdims=True)
    g_w = 1.0 / jnp.sum(jnp.where(is_grp, jnp.exp(gl - gmax), 0.0), axis=-1, keepdims=True)
    in_grp = (lane_i < N_EXPERTS) & (grp_of_lane == g_idx)
    e1 = jnp.where(in_grp, logits, NEG)
    v1 = jnp.max(e1, axis=-1, keepdims=True)
    i1 = jnp.min(jnp.where(in_grp & (e1 == v1), lane, big), axis=-1, keepdims=True)
    rest = in_grp & (lane != i1)
    e2 = jnp.where(rest, logits, NEG)
    v2 = jnp.max(e2, axis=-1, keepdims=True)
    i2 = jnp.min(jnp.where(rest & (e2 == v2), lane, big), axis=-1, keepdims=True)
    r = jnp.exp(v2 - v1)
    w1 = g_w / (1.0 + r)
    w2 = g_w * r / (1.0 + r)
    grp_onehot = jnp.where(is_grp & (lane - ROUTER_GROUP_LANE == g_idx), 1.0, 0.0)
    return jnp.where(lane == i1, w1, 0.0) + jnp.where(lane == i2, w2, 0.0) + grp_onehot


HX_W = D_MODEL + LANES


def _outproj_kernel(mg_ref, w_ref, x_ref, gt_ref, g2_ref, sh_ref, sc_ref, wr_ref, br_ref, x1_ref, hx_ref):
    x1 = x_ref[...] + gt_ref[...] * _dot(mg_ref[...], w_ref[...])
    x1_ref[...] = x1
    h2 = _rms(x1) * g2_ref[...] * (1.0 + sc_ref[...]) + sh_ref[...]
    hx_ref[:, 0:D_MODEL] = h2
    hx_ref[:, D_MODEL:HX_W] = _route(_dot3(h2, wr_ref[...]) + br_ref[...])


def _outproj(merged, w_out, x, mod, g2, w_router, b_router, tm, tiles_per_batch):
    m, d = x.shape
    rb = mod.shape[2]
    const2 = lambda i: (0, 0)
    return pl.pallas_call(
        _outproj_kernel,
        grid=(m // tm,),
        in_specs=[pl.BlockSpec((tm, d), lambda i: (i, 0)),
                  pl.BlockSpec((d, d), const2),
                  pl.BlockSpec((tm, d), lambda i: (i, 0)),
                  _mod_spec(2, rb, d, tiles_per_batch),
                  pl.BlockSpec((1, d), const2),
                  _mod_spec(3, rb, d, tiles_per_batch),
                  _mod_spec(4, rb, d, tiles_per_batch),
                  pl.BlockSpec((d, LANES), const2),
                  pl.BlockSpec((1, LANES), const2)],
        out_specs=[pl.BlockSpec((tm, d), lambda i: (i, 0)),
                   pl.BlockSpec((tm, HX_W), lambda i: (i, 0))],
        out_shape=[jax.ShapeDtypeStruct((m, d), F32),
                   jax.ShapeDtypeStruct((m, HX_W), F32)],
        compiler_params=_cparams(("arbitrary",)),
        name="outproj",
    )(merged, w_out, x, mod, g2, mod, mod, w_router, b_router)


MOE_TILE = 512
PLAN_TILE = 512


def _moe_plan_kernel(r_ref, tri_ref, excl_ref, pos_ref, tg_ref, nu_ref, cnt_sc, off_sc, run_sc):
    ph = pl.program_id(0)
    n = pl.program_id(1)
    lane = lax.broadcasted_iota(jnp.int32, (1, LANES), 1)
    lane_t = lax.broadcasted_iota(jnp.int32, r_ref.shape, 1)
    is_grp = (lane_t >= ROUTER_GROUP_LANE) & (lane_t < ROUTER_GROUP_LANE + N_GROUPS)
    g4 = jnp.where(is_grp, r_ref[...], 0.0)

    @pl.when((ph == 0) & (n == 0))
    def _():
        cnt_sc[...] = jnp.zeros(cnt_sc.shape, F32)

    @pl.when(ph == 0)
    def _():
        cnt_sc[...] += jnp.sum(g4, axis=0, keepdims=True)

    @pl.when((ph == 1) & (n == 0))
    def _():
        padded = jnp.floor((cnt_sc[...] + (MOE_TILE - 1)) * (1.0 / MOE_TILE)) * MOE_TILE
        off = _dot_sel_rhs(jnp.broadcast_to(padded, (8, LANES)), excl_ref[...])[0:1]
        off_sc[...] = off
        run_sc[...] = jnp.zeros(run_sc.shape, F32)
        end = off + padded
        tile_start = lane.astype(F32) * MOE_TILE
        tg = jnp.zeros((1, LANES), F32)
        for g in range(N_GROUPS):
            end_g = jnp.sum(jnp.where(lane == ROUTER_GROUP_LANE + g, end, 0.0), axis=-1, keepdims=True)
            tg = tg + jnp.where(end_g <= tile_start, 1.0, 0.0)
        tg_ref[...] = jnp.minimum(tg, N_GROUPS - 1.0).astype(jnp.int32)
        total = jnp.sum(jnp.where(lane == ROUTER_GROUP_LANE + N_GROUPS - 1, end, 0.0), axis=-1, keepdims=True)
        nu_ref[...] = jnp.broadcast_to(total * (1.0 / MOE_TILE), (1, LANES)).astype(jnp.int32)

    @pl.when(ph == 1)
    def _():
        rank = _dot(tri_ref[...], g4.astype(BF16)) + run_sc[...]
        run_sc[...] += jnp.sum(g4, axis=0, keepdims=True)
        posv = g4 * (off_sc[...] + rank)
        hi, mid, lo = _split3(posv)
        ones = jnp.ones((8, LANES), BF16)
        row = _dot_nt(ones, hi) + _dot_nt(ones, mid) + _dot_nt(ones, lo)
        pos_ref[...] = row[0:1].astype(jnp.int32)


def _moe_plan(hx, ntiles):
    t = hx.shape[0]
    nblk = t // PLAN_TILE
    tri = jnp.asarray(np.tril(np.ones((PLAN_TILE, PLAN_TILE), np.float32), -1), BF16)
    excl = jnp.asarray(np.triu(np.ones((LANES, LANES), np.float32), 1), BF16)
    assert ntiles <= LANES
    pos, tg, nu = pl.pallas_call(
        _moe_plan_kernel,
        grid=(2, nblk),
        in_specs=[pl.BlockSpec((PLAN_TILE, LANES), lambda ph, n: (n, D_MODEL // LANES)),
                  pl.BlockSpec((PLAN_TILE, PLAN_TILE), lambda ph, n: (0, 0)),
                  pl.BlockSpec((LANES, LANES), lambda ph, n: (0, 0))],
        out_specs=[pl.BlockSpec((1, PLAN_TILE), lambda ph, n: (0, n * ph)),
                   pl.BlockSpec((1, LANES), lambda ph, n: (0, 0)),
                   pl.BlockSpec((1, LANES), lambda ph, n: (0, 0))],
        out_shape=[jax.ShapeDtypeStruct((1, t), jnp.int32),
                   jax.ShapeDtypeStruct((1, LANES), jnp.int32),
                   jax.ShapeDtypeStruct((1, LANES), jnp.int32)],
        scratch_shapes=[pltpu.VMEM((1, LANES), F32), pltpu.VMEM((1, LANES), F32), pltpu.VMEM((1, LANES), F32)],
        compiler_params=_cparams(("arbitrary", "arbitrary")),
        name="moe_plan",
    )(hx, tri, excl)
    return pos.reshape(t), tg.reshape(LANES), nu.reshape(LANES)[0:1]


def _row_copy(src_ref, src_row, dst_ref, dst_row, sem):
    return pltpu.make_async_copy(src_ref.at[pl.ds(src_row, 1)], dst_ref.at[pl.ds(dst_row, 1)], sem)


def _moe_grouped_kernel(pos_ref, tg_ref, nu_ref, hx_ref, wg_ref, wu_ref, wd_ref, ys_ref,
                        buf, sem, src_sc, xb_sc, acc_sc, *, t):
    i = pl.program_id(0)
    ei = pl.program_id(1)
    n_used = nu_ref[0]
    used = i < n_used
    slot = i % 2

    def fetch(tile, sl):
        def body(r, c):
            _row_copy(hx_ref, src_sc[tile * MOE_TILE + r], buf.at[sl], r, sem.at[sl]).start()
            return c
        lax.fori_loop(0, MOE_TILE, body, 0, unroll=8)

    @pl.when((i == 0) & (ei == 0))
    def _():
        def clear(r, c):
            src_sc[r] = 0
            return c

        def invert(tk, c):
            src_sc[pos_ref[tk]] = tk
            return c

        lax.fori_loop(0, src_sc.shape[0], clear, 0, unroll=8)
        lax.fori_loop(0, t, invert, 0, unroll=8)
        fetch(0, 0)

    @pl.when((ei == 0) & (i + 1 < n_used))
    def _():
        fetch(i + 1, (i + 1) % 2)

    @pl.when(used & (ei == 0))
    def _():
        def wait(r, c):
            _row_copy(hx_ref, 0, buf.at[slot], r, sem.at[slot]).wait()
            return c
        lax.fori_loop(0, MOE_TILE, wait, 0, unroll=8)
        xb_sc[...] = buf[slot, :, 0:D_MODEL].astype(BF16)
        acc_sc[...] = jnp.zeros(acc_sc.shape, F32)

    @pl.when(used)
    def _():
        h = xb_sc[...]
        a = _dot(h, wg_ref[...])
        u = _dot(h, wu_ref[...])
        lane = lax.broadcasted_iota(jnp.int32, (MOE_TILE, LANES), 1)
        e = tg_ref[i] * EXP_PER_GROUP + ei
        cw = jnp.sum(jnp.where(lane == e, buf[slot, :, D_MODEL:HX_W], 0.0), axis=-1, keepdims=True)
        hid = (_silu(a) * u * cw).astype(BF16)
        acc_sc[...] += _dot(hid, wd_ref[...])

    @pl.when(ei == pl.num_programs(1) - 1)
    def _():
        ys_ref[...] = jnp.where(used, acc_sc[...], 0.0)


def _moe_grouped(hx, pos, tg, nu, wg, wu, wd, rows_out):
    t = hx.shape[0]
    ne, d, de = wg.shape
    ntiles = rows_out // MOE_TILE

    def wmap(i, ei, pos_ref, tg_ref, nu_ref):
        e = jnp.where(i < nu_ref[0], tg_ref[i] * EXP_PER_GROUP + ei, ne - 1)
        return (e, 0, 0)

    grid_spec = pltpu.PrefetchScalarGridSpec(
        num_scalar_prefetch=3,
        grid=(ntiles, EXP_PER_GROUP),
        in_specs=[pl.BlockSpec(memory_space=pl.ANY),
                  pl.BlockSpec((None, d, de), wmap),
                  pl.BlockSpec((None, d, de), wmap),
                  pl.BlockSpec((None, de, d), wmap)],
        out_specs=pl.BlockSpec((MOE_TILE, d), lambda i, ei, pos_ref, tg_ref, nu_ref: (i, 0)),
        scratch_shapes=[pltpu.VMEM((2, MOE_TILE, HX_W), F32), pltpu.SemaphoreType.DMA((2,)),
                        pltpu.SMEM((rows_out,), jnp.int32),
                        pltpu.VMEM((MOE_TILE, d), BF16), pltpu.VMEM((MOE_TILE, d), F32)],
    )
    return pl.pallas_call(
        functools.partial(_moe_grouped_kernel, t=t),
        grid_spec=grid_spec,
        out_shape=jax.ShapeDtypeStruct((rows_out, d), F32),
        compiler_params=_cparams(("arbitrary", "arbitrary")),
        name="moe_grouped",
    )(pos, tg, nu, hx, wg, wu, wd)


COMBINE_ROWS = 256


def _moe_combine_kernel(pos_ref, ys_ref, x1_ref, gt_ref, gf_ref, y_ref, buf, sem):
    i = pl.program_id(0)
    n = pl.num_programs(0)

    def fetch(tile, slot):
        def body(r, c):
            _row_copy(ys_ref, pos_ref[tile * COMBINE_ROWS + r], buf.at[slot], r, sem.at[slot]).start()
            return c
        lax.fori_loop(0, COMBINE_ROWS, body, 0, unroll=8)

    @pl.when(i == 0)
    def _():
        fetch(0, 0)

    @pl.when(i + 1 < n)
    def _():
        fetch(i + 1, (i + 1) % 2)

    slot = i % 2

    def wait(r, c):
        _row_copy(ys_ref, pos_ref[i * COMBINE_ROWS + r], buf.at[slot], r, sem.at[slot]).wait()
        return c

    lax.fori_loop(0, COMBINE_ROWS, wait, 0, unroll=8)
    x2 = x1_ref[...] + gt_ref[...] * buf[slot]
    y_ref[...] = _rms(x2) * gf_ref[...]


def _moe_combine(ys, pos, x1, mod, g_final, tiles_per_batch):
    m, d = x1.shape
    rb = mod.shape[2]
    grid_spec = pltpu.PrefetchScalarGridSpec(
        num_scalar_prefetch=1,
        grid=(m // COMBINE_ROWS,),
        in_specs=[pl.BlockSpec(memory_space=pl.ANY),
                  pl.BlockSpec((COMBINE_ROWS, d), lambda i, pos_ref: (i, 0)),
                  pl.BlockSpec((None, None, rb, d), lambda i, pos_ref: (5, i // tiles_per_batch, 0, 0)),
                  pl.BlockSpec((1, d), lambda i, pos_ref: (0, 0))],
        out_specs=pl.BlockSpec((COMBINE_ROWS, d), lambda i, pos_ref: (i, 0)),
        scratch_shapes=[pltpu.VMEM((2, COMBINE_ROWS, d), F32), pltpu.SemaphoreType.DMA((2,))],
    )
    return pl.pallas_call(
        _moe_combine_kernel,
        grid_spec=grid_spec,
        out_shape=jax.ShapeDtypeStruct((m, d), F32),
        compiler_params=_cparams(("arbitrary",)),
        name="moe_combine",
    )(pos, ys, x1, mod, g_final)


def _moe_kernel(h_ref, cmb_ref, wg_ref, wu_ref, wd_ref, x1_ref, gt_ref, gf_ref, y_ref, acc_ref):
    e = pl.program_id(1)

    @pl.when(e == 0)
    def _():
        acc_ref[...] = jnp.zeros(acc_ref.shape, F32)

    h = h_ref[...]
    a = _dot(h, wg_ref[...])
    u = _dot(h, wu_ref[...])
    lane = lax.broadcasted_iota(jnp.int32, cmb_ref.shape, 1)
    cw = jnp.sum(jnp.where(lane == e, cmb_ref[...], 0.0), axis=-1, keepdims=True)
    hid = (_silu(a) * u * cw).astype(BF16)
    acc_ref[...] += _dot(hid, wd_ref[...])

    @pl.when(e == pl.num_programs(1) - 1)
    def _():
        x2 = x1_ref[...] + gt_ref[...] * acc_ref[...]
        y_ref[...] = _rms(x2) * gf_ref[...]


def _moe(h2, cmb, wg, wu, wd, x1, mod, g_final, tm, tiles_per_batch):
    m, d = x1.shape
    rb = mod.shape[2]
    ne, _, de = wg.shape
    return pl.pallas_call(
        _moe_kernel,
        grid=(m // tm, ne),
        in_specs=[pl.BlockSpec((tm, d), lambda i, e: (i, 0)),
                  pl.BlockSpec((tm, LANES), lambda i, e: (i, 0)),
                  pl.BlockSpec((None, d, de), lambda i, e: (e, 0, 0)),
                  pl.BlockSpec((None, d, de), lambda i, e: (e, 0, 0)),
                  pl.BlockSpec((None, de, d), lambda i, e: (e, 0, 0)),
                  pl.BlockSpec((tm, d), lambda i, e: (i, 0)),
                  pl.BlockSpec((None, None, rb, d), lambda i, e: (5, i // tiles_per_batch, 0, 0)),
                  pl.BlockSpec((1, d), lambda i, e: (0, 0))],
        out_specs=pl.BlockSpec((tm, d), lambda i, e: (i, 0)),
        out_shape=jax.ShapeDtypeStruct((m, d), F32),
        scratch_shapes=[pltpu.VMEM((tm, d), F32)],
        compiler_params=_cparams(("arbitrary", "arbitrary")),
        name="moe",
    )(h2, cmb, wg, wu, wd, x1, mod, g_final)


def _prep_weights(w_ada, b_ada, g_norm1, g_norm2, g_final, w_in, w_a2, b_a, b_f, g_gla_norm, w_up_a, w_up_b, w_out,
                  w_grp, b_grp, w_exp, b_exp, w_gate_e, w_up_e, w_down_e):
    wt = w_in.reshape(w_in.shape[1:]).T
    d = D_MODEL
    o_lra = 2 * GLA_KW + 2 * GLA_VW
    o_fox = o_lra + GLA_RANK
    o_fb = o_fox + 3 * FOX_W
    o_g = o_fb + FOX_HEADS
    pad = jnp.zeros((LANES - GLA_RANK - FOX_HEADS, d), F32)
    wall, mask = _gla_tables()
    wall_s, mask_s = _gla_tables(SAMPLE_LEVELS)
    bf_row =jnp.zeros((1, LANES), F32).at[0, FB_LANE:FB_LANE + FOX_HEADS].set(b_f[0])
    w_router = jnp.concatenate([w_exp[0], w_grp[0], jnp.zeros((d, LANES - N_EXPERTS - N_GROUPS), F32)], axis=1)
    b_router = jnp.concatenate([b_exp[0], b_grp[0], jnp.zeros((LANES - N_EXPERTS - N_GROUPS,), F32)])[None, :]
    return dict(
        w_ada=w_ada.reshape(w_ada.shape[1:]), b_ada=b_ada[0][None, :],
        g1=g_norm1[0][None, :], g2=g_norm2[0][None, :], gf=g_final[None, :],
        w_in_t=wt,
        w_small_t=jnp.concatenate([wt[o_lra:o_lra + GLA_RANK], wt[o_fb:o_fb + FOX_HEADS], pad], axis=0),
        wall=jnp.asarray(wall, BF16), mask=jnp.asarray(mask, F32),
        wall_s=jnp.asarray(wall_s, BF16), mask_s=jnp.asarray(mask_s, F32),
        w_a2=jnp.concatenate([w_a2[0], jnp.zeros((LANES - GLA_RANK, GLA_KW), F32)], axis=0), b_a=b_a[0][None, :], bf_row=bf_row, g_gla=g_gla_norm[0][None, :],
        w_up_a=w_up_a[0].astype(BF16), w_up_b=w_up_b[0].astype(BF16), w_out=w_out[0].astype(BF16),
        w_router=w_router, b_router=b_router,
        wg=w_gate_e[0].astype(BF16), wu=w_up_e[0].astype(BF16), wd=w_down_e[0].astype(BF16),
    )


def _project(h, p, tm):
    tn = 1024
    wt = p["w_in_t"]
    n_gla = 2 * GLA_KW + 2 * GLA_VW
    c_q = n_gla + GLA_RANK
    c_g = c_q + 3 * FOX_W + FOX_HEADS
    (pg,) = _mm_wt(h, wt, 0, n_gla, tm, tn)
    (qb,) = _mm_wt(h, wt, c_q, FOX_W, tm, tn, (BF16,))
    kb, kb16 = _mm_heads(h, wt, c_q + FOX_W, min(tm, 512))
    vb, vb16 = _mm_heads(h, wt, c_q + 2 * FOX_W, min(tm, 512))
    (gates,) = _mm_wt(h, wt, c_g, 2 * D_MODEL, tm, tn, (BF16,))
    (small,) = _mm_wt(h, p["w_small_t"], 0, LANES, tm, LANES)
    return pg, qb, kb, kb16, vb, vb16, gates, small


def _tail(x, oa, ob, gates, mod, p, tm, tiles_per_batch, grouped):
    merged = _merge(oa, ob, p["w_up_a"], p["w_up_b"], gates, tm)
    tmo = min(tm, 256)
    x1, hx = _outproj(merged, p["w_out"], x, mod, p["g2"], p["w_router"], p["b_router"],
                      tmo, tiles_per_batch * (tm // tmo))
    if not grouped:
        tmm = min(tm, 512)
        return _moe(hx[:, 0:D_MODEL].astype(BF16), hx[:, D_MODEL:HX_W], p["wg"], p["wu"], p["wd"], x1, mod, p["gf"],
                    tmm, tiles_per_batch * (tm // tmm))
    rows_out = x.shape[0] + N_GROUPS * MOE_TILE
    pos, tg, nu = _moe_plan(hx, rows_out // MOE_TILE)
    ys = _moe_grouped(hx, pos, tg, nu, p["wg"], p["wu"], p["wd"], rows_out)
    return _moe_combine(ys, pos, x1, mod, p["gf"], tiles_per_batch * (tm // COMBINE_ROWS))


def kernel(x_prompt, x_sample, cache_k, cache_v, cache_logf, state_gla, page_table, c_prompt, c_sample, w_ada, b_ada,
           g_norm1, g_norm2, g_final, w_in, w_a2, b_a, b_f, g_gla_norm, w_up_a, w_up_b, w_out, w_grp, b_grp, w_exp,
           b_exp, w_gate_e, w_up_e, w_down_e):
    p = _prep_weights(w_ada, b_ada, g_norm1, g_norm2, g_final, w_in, w_a2, b_a, b_f, g_gla_norm, w_up_a, w_up_b,
                      w_out, w_grp, b_grp, w_exp, b_exp, w_gate_e, w_up_e, w_down_e)
    bp, seq, d = x_prompt.shape
    bs, t, _ = x_sample.shape
    assert t <= 2 ** SAMPLE_LEVELS and t <= ROWS8

    mod = _adaln(jnp.concatenate([c_prompt, c_sample], axis=0), p["w_ada"], p["b_ada"])
    mod_p = mod[:bp].reshape(bp, 6, 1, d).transpose(1, 0, 2, 3)
    mod_s = jnp.repeat(mod[bp:].reshape(bs, 6, d), t, axis=0).transpose(1, 0, 2)[:, None]

    tm = 1024
    tpb = seq // tm
    xp = x_prompt.reshape(bp * seq, d)
    hp = _normmod(xp, p["g1"], mod_p, tm, tpb)
    pg, qb, kb, kb16, vb, vb16, gates, small = _project(hp, p, tm)
    s0 = jnp.zeros((bp, GLA_HEADS, GLA_DK, GLA_DV), F32)
    oa, s_p = _gla_prompt(pg, small, s0, p["wall"], p["mask"], p["w_a2"], p["b_a"], p["g_gla"], bp, seq)
    lf_p, fcol, ft = _fox_bias_prompt(small, p["bf_row"], bp, seq)
    ob = _fox_prompt(qb, kb16, vb16, fcol, ft, bp, seq)
    y_p = _tail(xp, oa, ob, gates, mod_p, p, tm, tpb, grouped=True)

    rows = bs * t
    xs = x_sample.reshape(rows, d)
    hs = _normmod(xs, p["g1"], mod_s, rows, 1)
    pg_s, qs, ks, _, vs, _, gates_s, small_s = _project(hs, p, rows)
    oa_s, s_s = _gla_sample(pg_s.reshape(bs, t, -1), small_s.reshape(bs, t, LANES),
                            state_gla.reshape(state_gla.shape[1:]), p["wall_s"], p["mask_s"],
                            p["w_a2"], p["b_a"], p["g_gla"], bs, t)
    lf_s, fn_s = _fox_bias_sample(small_s, p["bf_row"], t)
    n_pool = cache_k.shape[1]
    ob_s = _fox_sample(page_table, qs.astype(F32).reshape(bs, t, FOX_W), ks.reshape(bs, t, FOX_HEADS, FOX_DH),
                       vs.reshape(bs, t, FOX_HEADS, FOX_DH), fn_s.reshape(bs, t, FOX_HEADS),
                       cache_k, cache_v, cache_logf.reshape(n_pool, 1, PAGE_FLAT), t)
    y_s = _tail(xs, oa_s.reshape(rows, GLA_VW).astype(BF16), ob_s.reshape(rows, FOX_W).astype(BF16), gates_s, mod_s,
                p, rows, 1, grouped=False)

    return (y_p.reshape(bp, seq, d), y_s.reshape(bs, t, d),
            kb.reshape(1, bp, seq, FOX_HEADS, FOX_DH), vb.reshape(1, bp, seq, FOX_HEADS, FOX_DH),
            lf_p.reshape(1, bp, seq, FOX_HEADS), s_p[None],
            ks.reshape(1, bs, t, FOX_HEADS, FOX_DH), vs.reshape(1, bs, t, FOX_HEADS, FOX_DH),
            lf_s.reshape(1, bs, t, FOX_HEADS), s_s[None])
```

```python
import functools

import numpy as np
import jax
import jax.numpy as jnp
from jax import lax
from jax.experimental import pallas as pl
from jax.experimental.pallas import tpu as pltpu

F32 = jnp.float32
BF16 = jnp.bfloat16

D_MODEL = 2048
GLA_HEADS = 4
GLA_DK = 128
GLA_DV = 256
GLA_RANK = 16
GLA_TAU = 16.0
FOX_HEADS = 8
FOX_DH = 128
PAGE = 128
N_GROUPS = 4
EXP_PER_GROUP = 4
N_EXPERTS = 16
D_EXPERT = 512
RMS_EPS = 1e-6
GLA_KW = GLA_HEADS * GLA_DK
GLA_VW = GLA_HEADS * GLA_DV
FOX_W = FOX_HEADS * FOX_DH
GLA_CHUNK = 128
GLA_LEVELS = 7
SAMPLE_LEVELS = 2
LANES = 128
NEG = -1e30
VMEM_LIMIT = 56 * 1024 * 1024


def _cparams(sem):
    return pltpu.CompilerParams(dimension_semantics=sem, vmem_limit_bytes=VMEM_LIMIT)


def _dot(a, b):
    return jnp.dot(a, b, preferred_element_type=F32)


def _dot_nt(a, b):
    return lax.dot_general(a, b, (((1,), (1,)), ((), ())), preferred_element_type=F32)


def _dot_tn(a, b):
    return lax.dot_general(a, b, (((0,), (0,)), ((), ())), preferred_element_type=F32)


def _split2(x):
    hi = x.astype(BF16)
    lo = (x - hi.astype(F32)).astype(BF16)
    return hi, lo


def _split3(x):
    hi = x.astype(BF16)
    r = x - hi.astype(F32)
    mid = r.astype(BF16)
    lo = (r - mid.astype(F32)).astype(BF16)
    return hi, mid, lo


def _dot3(a, b):
    ah, al = _split2(a)
    bh, bl = _split2(b)
    return _dot(ah, bh) + _dot(ah, bl) + _dot(al, bh)


def _dot_sel(w01, x):
    hi, mid, lo = _split3(x)
    return _dot(w01, hi) + _dot(w01, mid) + _dot(w01, lo)


def _log_sigmoid(x):
    return jnp.minimum(x, 0.0) - jnp.log1p(jnp.exp(-jnp.abs(x)))


def _silu(x):
    return x * jax.nn.sigmoid(x)


def _adaln_kernel(c_ref, w_ref, b_ref, o_ref):
    o_ref[...] = _dot3(_silu(c_ref[...]), w_ref[...]) + b_ref[...]


def _adaln(c, w, b, tn=512):
    nb, d = c.shape
    n = w.shape[1]
    return pl.pallas_call(
        _adaln_kernel,
        grid=(n // tn,),
        in_specs=[pl.BlockSpec((nb, d), lambda j: (0, 0)),
                  pl.BlockSpec((d, tn), lambda j: (0, j)),
                  pl.BlockSpec((1, tn), lambda j: (0, j))],
        out_specs=pl.BlockSpec((nb, tn), lambda j: (0, j)),
        out_shape=jax.ShapeDtypeStruct((nb, n), F32),
        compiler_params=_cparams(("arbitrary",)),
        name="adaln",
    )(c, w, b)


def _rms(x):
    return x * lax.rsqrt(jnp.mean(x * x, axis=-1, keepdims=True) + RMS_EPS)


def _normmod_kernel(x_ref, g_ref, sh_ref, sc_ref, o_ref):
    y = _rms(x_ref[...]) * g_ref[...]
    o_ref[...] = (y * (1.0 + sc_ref[...]) + sh_ref[...]).astype(o_ref.dtype)


def _mod_spec(k, rb, d, tiles_per_batch):
    return pl.BlockSpec((None, None, rb, d), lambda i: (k, i // tiles_per_batch, 0, 0))


def _normmod(x, g, mod, tm, tiles_per_batch):
    m, d = x.shape
    rb = mod.shape[2]
    return pl.pallas_call(
        _normmod_kernel,
        grid=(m // tm,),
        in_specs=[pl.BlockSpec((tm, d), lambda i: (i, 0)),
                  pl.BlockSpec((1, d), lambda i: (0, 0)),
                  _mod_spec(0, rb, d, tiles_per_batch),
                  _mod_spec(1, rb, d, tiles_per_batch)],
        out_specs=pl.BlockSpec((tm, d), lambda i: (i, 0)),
        out_shape=jax.ShapeDtypeStruct((m, d), BF16),
        compiler_params=_cparams(("arbitrary",)),
        name="normmod",
    )(x, g, mod, mod)


def _wt_spec(k, row0, tn):
    assert row0 % 8 == 0 and tn % 8 == 0
    return pl.BlockSpec((pl.Element(tn), pl.Element(k)), lambda j, i: (pl.multiple_of(row0 + j * tn, 8), 0))


def _mm_wt_kernel(x_ref, wt_ref, *refs):
    o_refs, wb = refs[:-1], refs[-1]

    @pl.when(pl.program_id(1) == 0)
    def _():
        wb[...] = wt_ref[...].astype(BF16)

    r = _dot_nt(x_ref[...], wb[...])
    for o_ref in o_refs:
        o_ref[...] = r.astype(o_ref.dtype)


def _mm_wt(x, wt, row0, n, tm, tn, out_dtypes=(F32,)):
    m, k = x.shape
    return pl.pallas_call(
        _mm_wt_kernel,
        grid=(n // tn, m // tm),
        in_specs=[pl.BlockSpec((tm, k), lambda j, i: (i, 0)), _wt_spec(k, row0, tn)],
        out_specs=[pl.BlockSpec((tm, tn), lambda j, i: (i, j)) for _ in out_dtypes],
        out_shape=[jax.ShapeDtypeStruct((m, n), dt) for dt in out_dtypes],
        scratch_shapes=[pltpu.VMEM((tn, k), BF16)],
        compiler_params=_cparams(("arbitrary", "arbitrary")),
        name="proj_mm",
    )(x, wt)


def _mm_heads_kernel(x_ref, wt_ref, o_ref, o16_ref, wb):
    @pl.when(pl.program_id(1) == 0)
    def _():
        wb[...] = wt_ref[...].astype(BF16)

    r = _dot_nt(x_ref[...], wb[...])
    for h in range(FOX_HEADS):
        o_ref[:, h, :] = r[:, h * FOX_DH:(h + 1) * FOX_DH]
    o16_ref[...] = r.astype(BF16)


def _mm_heads(x, wt, row0, tm):
    m, k = x.shape
    return pl.pallas_call(
        _mm_heads_kernel,
        grid=(1, m // tm),
        in_specs=[pl.BlockSpec((tm, k), lambda j, i: (i, 0)), _wt_spec(k, row0, FOX_W)],
        out_specs=[pl.BlockSpec((tm, FOX_HEADS, FOX_DH), lambda j, i: (i, 0, 0)),
                   pl.BlockSpec((tm, FOX_W), lambda j, i: (i, 0))],
        out_shape=[jax.ShapeDtypeStruct((m, FOX_HEADS, FOX_DH), F32),
                   jax.ShapeDtypeStruct((m, FOX_W), BF16)],
        scratch_shapes=[pltpu.VMEM((FOX_W, k), BF16)],
        compiler_params=_cparams(("arbitrary", "arbitrary")),
        name="proj_heads",
    )(x, wt)


def _gla_tables(p=GLA_LEVELS):
    c = GLA_CHUNK
    t = np.arange(c)[:, None]
    m = np.arange(c)[None, :]
    wall = np.zeros((p + 2, c, c), np.float32)
    mask = np.zeros((p + 1, c, c), np.float32)
    for l in range(p):
        half = 1 << l
        pos = t % (2 * half)
        mid = t - pos + half
        right = pos >= half
        wall[l] = np.where(right, (m >= mid) & (m <= t), (m > t) & (m < mid))
        s = m
        mask[l] = ((t >> (l + 1)) == (s >> (l + 1))) & (((t >> l) & 1) == 1) & (((s >> l) & 1) == 0)
    wall[p] = m <= t
    wall[p + 1] = m > t
    mask[p] = t == m
    return wall.reshape((p + 2) * c, c), mask


def _gla_body(q_ref, k_ref, v_ref, ra_ref, sm_ref, wall_ref, mask_ref, wa2_ref, ba_ref, g_ref,
              og_ref, st_ref, *, nchunk, valid, store_rows):
    c, p = GLA_CHUNK, mask_ref.shape[0] - 1
    row = lax.broadcasted_iota(jnp.int32, (c, GLA_DK), 0)
    wall = wall_ref[...]
    for ci in range(nchunk):
        rows = pl.ds(ci * c, c)
        x = _dot3(sm_ref[rows, :], wa2_ref[...]) + ba_ref[...]
        la = _log_sigmoid(x) * (1.0 / GLA_TAU)
        if valid < c:
            rowh = lax.broadcasted_iota(jnp.int32, la.shape, 0)
            la = jnp.where(rowh < valid, la, 0.0)
        e_all = jnp.exp(_dot_sel(wall, la))
        for h in range(GLA_HEADS):
            ks = slice(h * GLA_DK, (h + 1) * GLA_DK)
            vs = slice(h * GLA_DV, (h + 1) * GLA_DV)
            q = q_ref[rows, ks] * (GLA_DK ** -0.5)
            k = k_ref[rows, ks]
            vb = v_ref[rows, vs].astype(BF16)
            a = mask_ref[p] * _dot_nt(q.astype(BF16), k.astype(BF16))
            for l in range(p):
                el = e_all[l * c:(l + 1) * c, ks]
                xl = (jnp.where(((row >> l) & 1) == 1, q, k) * el).astype(BF16)
                a = a + mask_ref[l] * _dot_nt(xl, xl)
            st = st_ref[h]
            qc = (q * e_all[p * c:(p + 1) * c, ks]).astype(BF16)
            o = _dot(a.astype(BF16), vb) + _dot_nt(qc, st.astype(BF16))
            kr = (k * e_all[(p + 1) * c:(p + 2) * c, ks]).astype(BF16)
            dec = e_all[(p + 1) * c - 1:(p + 1) * c, ks]
            st_ref[h] = dec * st + _dot_tn(vb, kr)
            og = _rms(o) * g_ref[...] * _silu(ra_ref[rows, vs])
            if store_rows < c:
                og_ref[:, vs] = og[:store_rows].astype(og_ref.dtype)
            else:
                og_ref[rows, vs] = og.astype(og_ref.dtype)


def _gla_prompt_kernel(q_ref, k_ref, v_ref, ra_ref, sm_ref, wall_ref, mask_ref, wa2_ref, ba_ref, g_ref, s0_ref,
                       og_ref, s_ref, st_ref, *, nchunk):
    n = pl.program_id(1)

    @pl.when(n == 0)
    def _():
        for h in range(GLA_HEADS):
            st_ref[h] = s0_ref[h].T

    _gla_body(q_ref, k_ref, v_ref, ra_ref, sm_ref, wall_ref, mask_ref, wa2_ref, ba_ref, g_ref, og_ref, st_ref,
              nchunk=nchunk, valid=GLA_CHUNK, store_rows=GLA_CHUNK)

    @pl.when(n == pl.num_programs(1) - 1)
    def _():
        for h in range(GLA_HEADS):
            s_ref[h] = st_ref[h].T


def _gla_prompt(pg, small, s0, wall, mask, wa2, ba, g, batch, seq, tb=256):
    nblk = seq // tb
    rowmap = lambda cb: (lambda b, n: (b * nblk + n, cb))
    const2 = lambda b, n: (0, 0)
    return pl.pallas_call(
        functools.partial(_gla_prompt_kernel, nchunk=tb // GLA_CHUNK),
        grid=(batch, nblk),
        in_specs=[pl.BlockSpec((tb, GLA_KW), rowmap(0)),
                  pl.BlockSpec((tb, GLA_KW), rowmap(1)),
                  pl.BlockSpec((tb, GLA_VW), rowmap(1)),
                  pl.BlockSpec((tb, GLA_VW), rowmap(2)),
                  pl.BlockSpec((tb, LANES), rowmap(0)),
                  pl.BlockSpec(wall.shape, const2),
                  pl.BlockSpec(mask.shape, lambda b, n: (0, 0, 0)),
                  pl.BlockSpec(wa2.shape, const2),
                  pl.BlockSpec(ba.shape, const2),
                  pl.BlockSpec(g.shape, const2),
                  pl.BlockSpec((None, GLA_HEADS, GLA_DK, GLA_DV), lambda b, n: (b, 0, 0, 0))],
        out_specs=[pl.BlockSpec((tb, GLA_VW), rowmap(0)),
                   pl.BlockSpec((None, GLA_HEADS, GLA_DK, GLA_DV), lambda b, n: (b, 0, 0, 0))],
        out_shape=[jax.ShapeDtypeStruct((batch * seq, GLA_VW), BF16),
                   jax.ShapeDtypeStruct((batch, GLA_HEADS, GLA_DK, GLA_DV), F32)],
        scratch_shapes=[pltpu.VMEM((GLA_HEADS, GLA_DV, GLA_DK), F32)],
        compiler_params=_cparams(("arbitrary", "arbitrary")),
        name="gla_prompt",
    )(pg, pg, pg, pg, small, wall, mask, wa2, ba, g, s0)


def _gla_sample_kernel(pg_ref, sm_ref, wall_ref, mask_ref, wa2_ref, ba_ref, g_ref, s0_ref,
                       og_ref, s_ref, pad_ref, smpad_ref, st_ref, *, t):
    @pl.when(pl.program_id(0) == 0)
    def _():
        pad_ref[...] = jnp.zeros(pad_ref.shape, F32)
        smpad_ref[...] = jnp.zeros(smpad_ref.shape, F32)

    pad_ref[0:t, :] = pg_ref[...]
    smpad_ref[0:t, :] = sm_ref[...]
    for h in range(GLA_HEADS):
        st_ref[h] = s0_ref[h].T
    q_ref = pad_ref.at[:, 0:GLA_KW]
    k_ref = pad_ref.at[:, GLA_KW:2 * GLA_KW]
    v_ref = pad_ref.at[:, 2 * GLA_KW:2 * GLA_KW + GLA_VW]
    ra_ref = pad_ref.at[:, 2 * GLA_KW + GLA_VW:2 * GLA_KW + 2 * GLA_VW]
    _gla_body(q_ref, k_ref, v_ref, ra_ref, smpad_ref, wall_ref, mask_ref, wa2_ref, ba_ref, g_ref, og_ref, st_ref,
              nchunk=1, valid=t, store_rows=t)
    for h in range(GLA_HEADS):
        s_ref[h] = st_ref[h].T


def _gla_sample(pg, small, s0, wall, mask, wa2, ba, g, batch, t):
    width = pg.shape[-1]
    const2 = lambda b: (0, 0)
    return pl.pallas_call(
        functools.partial(_gla_sample_kernel, t=t),
        grid=(batch,),
        in_specs=[pl.BlockSpec((None, t, width), lambda b: (b, 0, 0)),
                  pl.BlockSpec((None, t, LANES), lambda b: (b, 0, 0)),
                  pl.BlockSpec(wall.shape, const2),
                  pl.BlockSpec(mask.shape, lambda b: (0, 0, 0)),
                  pl.BlockSpec(wa2.shape, const2),
                  pl.BlockSpec(ba.shape, const2),
                  pl.BlockSpec(g.shape, const2),
                  pl.BlockSpec((None, GLA_HEADS, GLA_DK, GLA_DV), lambda b: (b, 0, 0, 0))],
        out_specs=[pl.BlockSpec((None, t, GLA_VW), lambda b: (b, 0, 0)),
                   pl.BlockSpec((None, GLA_HEADS, GLA_DK, GLA_DV), lambda b: (b, 0, 0, 0))],
        out_shape=[jax.ShapeDtypeStruct((batch, t, GLA_VW), F32),
                   jax.ShapeDtypeStruct((batch, GLA_HEADS, GLA_DK, GLA_DV), F32)],
        scratch_shapes=[pltpu.VMEM((GLA_CHUNK, width), F32),
                        pltpu.VMEM((GLA_CHUNK, LANES), F32),
                        pltpu.VMEM((GLA_HEADS, GLA_DV, GLA_DK), F32)],
        compiler_params=_cparams(("arbitrary",)),
        name="gla_sample",
    )(pg, small, wall, mask, wa2, ba, g, s0)


FB_LANE = GLA_RANK


def _fox_bias_prompt_kernel(sm_ref, bf_ref, tri_ref, lf_ref, fc_ref, ft_ref, carry_ref):
    @pl.when(pl.program_id(1) == 0)
    def _():
        carry_ref[...] = jnp.zeros(carry_ref.shape, F32)

    lf = _log_sigmoid(sm_ref[...] + bf_ref[...])
    lf_ref[...] = lf[:, FB_LANE:FB_LANE + FOX_HEADS]
    cum = _dot_sel(tri_ref[...], lf) + carry_ref[...]
    carry_ref[...] = cum[cum.shape[0] - 1:, :]
    fc_ref[...] = cum
    ft_ref[...] = cum.T[FB_LANE:FB_LANE + FOX_HEADS, :]


def _fox_bias_prompt(small, bf_row, batch, seq, tb=256):
    nblk = seq // tb
    tri = jnp.asarray(np.tril(np.ones((tb, tb), np.float32)), BF16)
    return pl.pallas_call(
        _fox_bias_prompt_kernel,
        grid=(batch, nblk),
        in_specs=[pl.BlockSpec((tb, LANES), lambda b, n: (b * nblk + n, 0)),
                  pl.BlockSpec((1, LANES), lambda b, n: (0, 0)),
                  pl.BlockSpec((tb, tb), lambda b, n: (0, 0))],
        out_specs=[pl.BlockSpec((tb, FOX_HEADS), lambda b, n: (b * nblk + n, 0)),
                   pl.BlockSpec((tb, LANES), lambda b, n: (b * nblk + n, 0)),
                   pl.BlockSpec((None, FOX_HEADS, tb), lambda b, n: (b, 0, n))],
        out_shape=[jax.ShapeDtypeStruct((batch * seq, FOX_HEADS), F32),
                   jax.ShapeDtypeStruct((batch * seq, LANES), F32),
                   jax.ShapeDtypeStruct((batch, FOX_HEADS, seq), F32)],
        scratch_shapes=[pltpu.VMEM((1, LANES), F32)],
        compiler_params=_cparams(("arbitrary", "arbitrary")),
        name="fox_bias_prompt",
    )(small, bf_row, tri)


def _fox_bias_sample_kernel(sm_ref, bf_ref, sel_ref, lf_ref, fn_ref):
    lf = _log_sigmoid(sm_ref[...] + bf_ref[...])
    lf_ref[...] = lf[:, FB_LANE:FB_LANE + FOX_HEADS]
    cum = _dot_sel(sel_ref[...], lf)
    fn_ref[...] = cum[:, FB_LANE:FB_LANE + FOX_HEADS]


def _fox_bias_sample(small, bf_row, t):
    rows = small.shape[0]
    r = np.arange(rows)
    sel = ((r[:, None] // t) == (r[None, :] // t)) & (r[None, :] <= r[:, None])
    sel = jnp.asarray(sel.astype(np.float32), BF16)
    full = lambda shape: pl.BlockSpec(shape, lambda i: tuple(0 for _ in shape))
    return pl.pallas_call(
        _fox_bias_sample_kernel,
        grid=(1,),
        in_specs=[full((rows, LANES)), full((1, LANES)), full((rows, rows))],
        out_specs=[full((rows, FOX_HEADS)), full((rows, FOX_HEADS))],
        out_shape=[jax.ShapeDtypeStruct((rows, FOX_HEADS), F32),
                   jax.ShapeDtypeStruct((rows, FOX_HEADS), F32)],
        compiler_params=_cparams(("arbitrary",)),
        name="fox_bias_sample",
    )(small, bf_row, sel)


FOX_HEADS_PER_STEP = 8


def _fox_prompt_kernel(q_ref, k_ref, v_ref, fc_ref, ft_ref, o_ref, *, tq, tk):
    hb = FOX_HEADS_PER_STEP
    hg = pl.program_id(1)
    i = pl.program_id(2)
    lane = lax.broadcasted_iota(jnp.int32, (tq, LANES), 1)
    rowi = lax.broadcasted_iota(jnp.int32, (tq, tk), 0)
    coli = lax.broadcasted_iota(jnp.int32, (tq, tk), 1)
    fc = fc_ref[...]
    qs, f_ts = [], []
    for hh in range(hb):
        hs = slice(hh * FOX_DH, (hh + 1) * FOX_DH)
        qs.append((q_ref[:, hs].astype(F32) * (FOX_DH ** -0.5)).astype(BF16))
        f_ts.append(jnp.sum(jnp.where(lane == hg * hb + hh + FB_LANE, fc, 0.0), axis=-1, keepdims=True))

    def step(j, carry, masked):
        ks = pl.ds(pl.multiple_of(j * tk, tk), tk)
        out = []
        for hh in range(hb):
            hs = slice(hh * FOX_DH, (hh + 1) * FOX_DH)
            m, l, acc = carry[hh]
            s = _dot_nt(qs[hh], k_ref[ks, hs])
            s = s + f_ts[hh] - ft_ref[hh:hh + 1, ks]
            if masked:
                s = jnp.where(rowi >= coli, s, NEG)
            m_new = jnp.maximum(m, jnp.max(s, axis=-1, keepdims=True))
            p = jnp.exp(s - m_new)
            alpha = jnp.exp(m - m_new)
            l = alpha * l + jnp.sum(p, axis=-1, keepdims=True)
            acc = alpha * acc + _dot(p.astype(BF16), v_ref[ks, hs])
            out.append((m_new, l, acc))
        return tuple(out)

    init = tuple((jnp.full((tq, 1), NEG, F32), jnp.zeros((tq, 1), F32), jnp.zeros((tq, FOX_DH), F32))
                 for _ in range(hb))
    nfull = i * (tq // tk)
    carry = lax.fori_loop(0, nfull, lambda j, c: step(j, c, False), init)
    carry = step(nfull, carry, True)
    for hh in range(hb):
        m, l, acc = carry[hh]
        o_ref[:, hh * FOX_DH:(hh + 1) * FOX_DH] = (acc / l).astype(o_ref.dtype)


def _fox_prompt(qb, kb, vb, fcol, ft, batch, seq, tq=256):
    nq = seq // tq
    hb = FOX_HEADS_PER_STEP
    wb = hb * FOX_DH
    return pl.pallas_call(
        functools.partial(_fox_prompt_kernel, tq=tq, tk=tq),
        grid=(batch, FOX_HEADS // hb, nq),
        in_specs=[pl.BlockSpec((tq, wb), lambda b, h, i: (b * nq + i, h)),
                  pl.BlockSpec((seq, wb), lambda b, h, i: (b, h)),
                  pl.BlockSpec((seq, wb), lambda b, h, i: (b, h)),
                  pl.BlockSpec((tq, LANES), lambda b, h, i: (b * nq + i, 0)),
                  pl.BlockSpec((None, None, hb, seq), lambda b, h, i: (b, h, 0, 0))],
        out_specs=pl.BlockSpec((tq, wb), lambda b, h, i: (b * nq + i, h)),
        out_shape=jax.ShapeDtypeStruct((batch * seq, FOX_W), BF16),
        compiler_params=_cparams(("arbitrary", "arbitrary", "arbitrary")),
        name="fox_prompt",
    )(qb, kb, vb, fcol, ft.reshape(batch, FOX_HEADS // hb, hb, seq))


PAGES_PER_STEP = 16
ROWS8 = 8
QROWS = FOX_HEADS * ROWS8
PAGE_FLAT = PAGE * FOX_HEADS


def _dot_sel_rhs(x, w01):
    hi, mid, lo = _split3(x)
    m = x.shape[0]
    stacked = jnp.concatenate([hi.astype(F32), mid.astype(F32), lo.astype(F32)], axis=0).astype(BF16)
    r = _dot(stacked, w01)
    return r[0:m] + r[m:2 * m] + r[2 * m:3 * m]


def _fox_sample_kernel(pt_ref, q_ref, kn_ref, vn_ref, fn_ref, fnrow_ref, madd_ref, maddn_ref, usuf_ref, pfx_ref,
                       hsum_ref, pexp_ref, hmask_ref, *refs, t):
    g = PAGES_PER_STEP
    k_refs, v_refs, lf_refs = refs[0:g], refs[g:2 * g], refs[2 * g:3 * g]
    o_ref = refs[3 * g]
    q_sc, m_sc, l_sc, acc_sc, carry_sc = refs[3 * g + 1:]
    j = pl.program_id(1)
    fn_t = fn_ref[...][:, 0:1]

    @pl.when(j == 0)
    def _():
        q_sc[...] = jnp.zeros(q_sc.shape, F32)
        for h in range(FOX_HEADS):
            q_sc[h * ROWS8:h * ROWS8 + t, :] = q_ref[:, h * FOX_DH:(h + 1) * FOX_DH] * (FOX_DH ** -0.5)
        carry_sc[...] = jnp.zeros(carry_sc.shape, F32)
        pad = jnp.zeros((LANES - t * FOX_HEADS, FOX_DH), F32)
        kn = jnp.concatenate([kn_ref[...].reshape(t * FOX_HEADS, FOX_DH), pad], axis=0)
        vn = jnp.concatenate([vn_ref[...].reshape(t * FOX_HEADS, FOX_DH), pad], axis=0)
        s = _dot_nt(q_sc[...], kn) + fn_t - fnrow_ref[...] + maddn_ref[...]
        m = jnp.max(s, axis=-1, keepdims=True)
        p = jnp.exp(s - m)
        m_sc[...] = jnp.broadcast_to(m, m_sc.shape)
        l_sc[...] = jnp.broadcast_to(jnp.sum(p, axis=-1, keepdims=True), l_sc.shape)
        acc_sc[...] = _dot(p, vn)

    lf = jnp.concatenate([lf_refs[gi][...] for gi in range(g)], axis=0)
    tot = jnp.broadcast_to(jnp.sum(lf, axis=-1, keepdims=True), lf.shape)
    carry = carry_sc[...]
    r_hk = (_dot_sel_rhs(lf, usuf_ref[...]) + _dot_sel(pfx_ref[...], tot)
            + jnp.concatenate([carry] * g, axis=0))
    carry_sc[...] = carry + _dot_sel(hsum_ref[...], tot)
    spread = _dot_sel_rhs(r_hk, pexp_ref[...]) * hmask_ref[...]
    r_flat = jnp.sum(spread.reshape(g, FOX_HEADS, PAGE_FLAT), axis=1)
    q = q_sc[...]
    madd = madd_ref[...] + fn_t
    m_old = m_sc[...]
    m_new = m_old
    s_list = []
    for gi in range(g):
        bias = madd + r_flat[gi:gi + 1, :]
        s = _dot_nt(q, k_refs[gi][...].reshape(PAGE_FLAT, FOX_DH)) + bias
        s_list.append(s)
        m_new = jnp.maximum(m_new, jnp.max(s, axis=-1, keepdims=True))
    alpha = jnp.exp(m_old - m_new)
    l = alpha * l_sc[...]
    acc = alpha * acc_sc[...]
    m_col = m_new[:, 0:1]
    for gi in range(g):
        p = jnp.exp(s_list[gi] - m_col)
        l = l + jnp.sum(p, axis=-1, keepdims=True)
        acc = acc + _dot(p, v_refs[gi][...].reshape(PAGE_FLAT, FOX_DH))
    m_sc[...] = m_new
    l_sc[...] = l
    acc_sc[...] = acc

    @pl.when(j == pl.num_programs(1) - 1)
    def _():
        o = acc / l
        for h in range(FOX_HEADS):
            o_ref[:, h * FOX_DH:(h + 1) * FOX_DH] = o[h * ROWS8:h * ROWS8 + t, :].astype(o_ref.dtype)


def _fox_sample_tables(t):
    row = np.arange(QROWS)[:, None]
    col = np.arange(PAGE_FLAT)[None, :]
    madd = np.where((row // ROWS8) == (col % FOX_HEADS), 0.0, NEG).astype(np.float32)
    coln = np.arange(LANES)[None, :]
    ok = (coln < t * FOX_HEADS) & ((row // ROWS8) == (coln % FOX_HEADS)) & ((coln // FOX_HEADS) <= (row % ROWS8))
    maddn = np.where(ok, 0.0, NEG).astype(np.float32)
    key = np.arange(PAGE)
    usuf = (key[:, None] > key[None, :]).astype(np.float32)
    gh = np.arange(PAGES_PER_STEP * FOX_HEADS)
    same_head = (gh[:, None] % FOX_HEADS) == (gh[None, :] % FOX_HEADS)
    pfx = (same_head & ((gh[None, :] // FOX_HEADS) < (gh[:, None] // FOX_HEADS))).astype(np.float32)
    hsum = (np.arange(FOX_HEADS)[:, None] == (gh[None, :] % FOX_HEADS)).astype(np.float32)
    pexp = (key[:, None] == (col // FOX_HEADS)).astype(np.float32)
    hmask = ((gh[:, None] % FOX_HEADS) == (col % FOX_HEADS)).astype(np.float32)
    return (jnp.asarray(madd), jnp.asarray(maddn), jnp.asarray(usuf, BF16), jnp.asarray(pfx, BF16),
            jnp.asarray(hsum, BF16), jnp.asarray(pexp, BF16), jnp.asarray(hmask))


def _fox_sample(page_table, q, kn, vn, fn, cache_k, cache_v, cache_lf, t):
    batch, npages = page_table.shape
    g = PAGES_PER_STEP
    nsteps = npages // g
    tables = _fox_sample_tables(t)
    fn_rows = jnp.pad(fn.transpose(0, 2, 1), ((0, 0), (0, 0), (0, ROWS8 - t))).reshape(batch, QROWS, 1)
    fn_rows = jnp.broadcast_to(fn_rows, (batch, QROWS, LANES))
    fn_cols = jnp.pad(fn.reshape(batch, 1, t * FOX_HEADS), ((0, 0), (0, 0), (0, LANES - t * FOX_HEADS)))

    def page_map5(gi):
        return lambda b, j, pt: (0, pt[b, npages - 1 - (j * g + gi)], 0, 0, 0)

    def page_map3(gi):
        return lambda b, j, pt: (pt[b, npages - 1 - (j * g + gi)], 0, 0)

    seq3 = lambda b, j, pt: (b, 0, 0)
    seq4 = lambda b, j, pt: (b, 0, 0, 0)
    const2 = lambda b, j, pt: (0, 0)
    in_specs = [pl.BlockSpec((None, t, FOX_W), seq3),
                pl.BlockSpec((None, t, FOX_HEADS, FOX_DH), seq4),
                pl.BlockSpec((None, t, FOX_HEADS, FOX_DH), seq4),
                pl.BlockSpec((None, QROWS, LANES), seq3),
                pl.BlockSpec((None, 1, LANES), seq3)]
    in_specs += [pl.BlockSpec(tb.shape, const2) for tb in tables]
    in_specs += [pl.BlockSpec((None, None, PAGE, FOX_HEADS, FOX_DH), page_map5(gi)) for gi in range(g)]
    in_specs += [pl.BlockSpec((None, None, PAGE, FOX_HEADS, FOX_DH), page_map5(gi)) for gi in range(g)]
    in_specs += [pl.BlockSpec((None, FOX_HEADS, PAGE), page_map3(gi)) for gi in range(g)]
    grid_spec = pltpu.PrefetchScalarGridSpec(
        num_scalar_prefetch=1,
        grid=(batch, nsteps),
        in_specs=in_specs,
        out_specs=pl.BlockSpec((None, t, FOX_W), seq3),
        scratch_shapes=[pltpu.VMEM((QROWS, FOX_DH), F32),
                        pltpu.VMEM((QROWS, LANES), F32),
                        pltpu.VMEM((QROWS, LANES), F32),
                        pltpu.VMEM((QROWS, FOX_DH), F32),
                        pltpu.VMEM((FOX_HEADS, PAGE), F32)],
    )
    return pl.pallas_call(
        functools.partial(_fox_sample_kernel, t=t),
        grid_spec=grid_spec,
        out_shape=jax.ShapeDtypeStruct((batch, t, FOX_W), F32),
        compiler_params=_cparams(("arbitrary", "arbitrary")),
        name="fox_sample",
    )(page_table, q, kn, vn, fn_rows, fn_cols, *tables,
      *([cache_k] * g), *([cache_v] * g), *([cache_lf] * g))


def _merge_kernel(oa_ref, ob_ref, wa_ref, wb_ref, ga_ref, gb_ref, o_ref):
    ua = _dot(oa_ref[...], wa_ref[...])
    ub = _dot(ob_ref[...], wb_ref[...])
    ga = jax.nn.sigmoid(ga_ref[...].astype(F32))
    gb = jax.nn.sigmoid(gb_ref[...].astype(F32))
    o_ref[...] = (ga * ua + gb * ub).astype(o_ref.dtype)


def _merge(oa, ob, wa, wb, gates, tm, tn=1024):
    m = oa.shape[0]
    d = wa.shape[1]
    nj = d // tn
    return pl.pallas_call(
        _merge_kernel,
        grid=(nj, m // tm),
        in_specs=[pl.BlockSpec((tm, GLA_VW), lambda j, i: (i, 0)),
                  pl.BlockSpec((tm, FOX_W), lambda j, i: (i, 0)),
                  pl.BlockSpec((GLA_VW, tn), lambda j, i: (0, j)),
                  pl.BlockSpec((FOX_W, tn), lambda j, i: (0, j)),
                  pl.BlockSpec((tm, tn), lambda j, i: (i, j)),
                  pl.BlockSpec((tm, tn), lambda j, i: (i, nj + j))],
        out_specs=pl.BlockSpec((tm, tn), lambda j, i: (i, j)),
        out_shape=jax.ShapeDtypeStruct((m, d), BF16),
        compiler_params=_cparams(("arbitrary", "arbitrary")),
        name="merge",
    )(oa, ob, wa, wb, gates, gates)


ROUTER_GROUP_LANE = N_EXPERTS


def _route(logits):
    lane_i = lax.broadcasted_iota(jnp.int32, logits.shape, 1)
    lane = lane_i.astype(F32)
    grp_of_lane = (lane_i >> 2).astype(F32)
    big = float(LANES)
    is_grp = (lane_i >= ROUTER_GROUP_LANE) & (lane_i < ROUTER_GROUP_LANE + N_GROUPS)
    gl = jnp.where(is_grp, logits, NEG)
    gmax = jnp.max(gl, axis=-1, keepdims=True)
    g_idx = jnp.min(jnp.where(is_grp & (gl == gmax), lane - ROUTER_GROUP_LANE, big), axis=-1, keepdims=True)
    g_w = 1.0 / jnp.sum(jnp.where(is_grp, jnp.exp(gl - gmax), 0.0), axis=-1, keepdims=True)
    in_grp = (lane_i < N_EXPERTS) & (grp_of_lane == g_idx)
    e1 = jnp.where(in_grp, logits, NEG)
    v1 = jnp.max(e1, axis=-1, keepdims=True)
    i1 = jnp.min(jnp.where(in_grp & (e1 == v1), lane, big), axis=-1, keepdims=True)
    rest = in_grp & (lane != i1)
    e2 = jnp.where(rest, logits, NEG)
    v2 = jnp.max(e2, axis=-1, keepdims=True)
    i2 = jnp.min(jnp.where(rest & (e2 == v2), lane, big), axis=-1, keepdims=True)
    r = jnp.exp(v2 - v1)
    w1 = g_w / (1.0 + r)
    w2 = g_w * r / (1.0 + r)
    grp_onehot = jnp.where(is_grp & (lane - ROUTER_GROUP_LANE == g_idx), 1.0, 0.0)
    return jnp.where(lane == i1, w1, 0.0) + jnp.where(lane == i2, w2, 0.0) + grp_onehot


HX_W = D_MODEL + LANES


def _outproj_kernel(mg_ref, w_ref, x_ref, gt_ref, g2_ref, sh_ref, sc_ref, wr_ref, br_ref, x1_ref, hx_ref):
    x1 = x_ref[...] + gt_ref[...] * _dot(mg_ref[...], w_ref[...])
    x1_ref[...] = x1
    h2 = _rms(x1) * g2_ref[...] * (1.0 + sc_ref[...]) + sh_ref[...]
    hx_ref[:, 0:D_MODEL] = h2
    hx_ref[:, D_MODEL:HX_W] = _route(_dot3(h2, wr_ref[...]) + br_ref[...])


def _outproj(merged, w_out, x, mod, g2, w_router, b_router, tm, tiles_per_batch):
    m, d = x.shape
    rb = mod.shape[2]
    const2 = lambda i: (0, 0)
    return pl.pallas_call(
        _outproj_kernel,
        grid=(m // tm,),
        in_specs=[pl.BlockSpec((tm, d), lambda i: (i, 0)),
                  pl.BlockSpec((d, d), const2),
                  pl.BlockSpec((tm, d), lambda i: (i, 0)),
                  _mod_spec(2, rb, d, tiles_per_batch),
                  pl.BlockSpec((1, d), const2),
                  _mod_spec(3, rb, d, tiles_per_batch),
                  _mod_spec(4, rb, d, tiles_per_batch),
                  pl.BlockSpec((d, LANES), const2),
                  pl.BlockSpec((1, LANES), const2)],
        out_specs=[pl.BlockSpec((tm, d), lambda i: (i, 0)),
                   pl.BlockSpec((tm, HX_W), lambda i: (i, 0))],
        out_shape=[jax.ShapeDtypeStruct((m, d), F32),
                   jax.ShapeDtypeStruct((m, HX_W), F32)],
        compiler_params=_cparams(("arbitrary",)),
        name="outproj",
    )(merged, w_out, x, mod, g2, mod, mod, w_router, b_router)


MOE_TILE = 512
PLAN_TILE = 512


def _moe_plan_kernel(r_ref, tri_ref, excl_ref, pos_ref, tg_ref, nu_ref, cnt_sc, off_sc, run_sc):
    ph = pl.program_id(0)
    n = pl.program_id(1)
    lane = lax.broadcasted_iota(jnp.int32, (1, LANES), 1)
    lane_t = lax.broadcasted_iota(jnp.int32, r_ref.shape, 1)
    is_grp = (lane_t >= ROUTER_GROUP_LANE) & (lane_t < ROUTER_GROUP_LANE + N_GROUPS)
    g4 = jnp.where(is_grp, r_ref[...], 0.0)

    @pl.when((ph == 0) & (n == 0))
    def _():
        cnt_sc[...] = jnp.zeros(cnt_sc.shape, F32)

    @pl.when(ph == 0)
    def _():
        cnt_sc[...] += jnp.sum(g4, axis=0, keepdims=True)

    @pl.when((ph == 1) & (n == 0))
    def _():
        padded = jnp.floor((cnt_sc[...] + (MOE_TILE - 1)) * (1.0 / MOE_TILE)) * MOE_TILE
        off = _dot_sel_rhs(jnp.broadcast_to(padded, (8, LANES)), excl_ref[...])[0:1]
        off_sc[...] = off
        run_sc[...] = jnp.zeros(run_sc.shape, F32)
        end = off + padded
        tile_start = lane.astype(F32) * MOE_TILE
        tg = jnp.zeros((1, LANES), F32)
        for g in range(N_GROUPS):
            end_g = jnp.sum(jnp.where(lane == ROUTER_GROUP_LANE + g, end, 0.0), axis=-1, keepdims=True)
            tg = tg + jnp.where(end_g <= tile_start, 1.0, 0.0)
        tg_ref[...] = jnp.minimum(tg, N_GROUPS - 1.0).astype(jnp.int32)
        total = jnp.sum(jnp.where(lane == ROUTER_GROUP_LANE + N_GROUPS - 1, end, 0.0), axis=-1, keepdims=True)
        nu_ref[...] = jnp.broadcast_to(total * (1.0 / MOE_TILE), (1, LANES)).astype(jnp.int32)

    @pl.when(ph == 1)
    def _():
        rank = _dot(tri_ref[...], g4.astype(BF16)) + run_sc[...]
        run_sc[...] += jnp.sum(g4, axis=0, keepdims=True)
        posv = g4 * (off_sc[...] + rank)
        hi, mid, lo = _split3(posv)
        ones = jnp.ones((8, LANES), BF16)
        row = _dot_nt(ones, hi) + _dot_nt(ones, mid) + _dot_nt(ones, lo)
        pos_ref[...] = row[0:1].astype(jnp.int32)


def _moe_plan(hx, ntiles):
    t = hx.shape[0]
    nblk = t // PLAN_TILE
    tri = jnp.asarray(np.tril(np.ones((PLAN_TILE, PLAN_TILE), np.float32), -1), BF16)
    excl = jnp.asarray(np.triu(np.ones((LANES, LANES), np.float32), 1), BF16)
    assert ntiles <= LANES
    pos, tg, nu = pl.pallas_call(
        _moe_plan_kernel,
        grid=(2, nblk),
        in_specs=[pl.BlockSpec((PLAN_TILE, LANES), lambda ph, n: (n, D_MODEL // LANES)),
                  pl.BlockSpec((PLAN_TILE, PLAN_TILE), lambda ph, n: (0, 0)),
                  pl.BlockSpec((LANES, LANES), lambda ph, n: (0, 0))],
        out_specs=[pl.BlockSpec((1, PLAN_TILE), lambda ph, n: (0, n * ph)),
                   pl.BlockSpec((1, LANES), lambda ph, n: (0, 0)),
                   pl.BlockSpec((1, LANES), lambda ph, n: (0, 0))],
        out_shape=[jax.ShapeDtypeStruct((1, t), jnp.int32),
                   jax.ShapeDtypeStruct((1, LANES), jnp.int32),
                   jax.ShapeDtypeStruct((1, LANES), jnp.int32)],
        scratch_shapes=[pltpu.VMEM((1, LANES), F32), pltpu.VMEM((1, LANES), F32), pltpu.VMEM((1, LANES), F32)],
        compiler_params=_cparams(("arbitrary", "arbitrary")),
        name="moe_plan",
    )(hx, tri, excl)
    return pos.reshape(t), tg.reshape(LANES), nu.reshape(LANES)[0:1]


def _row_copy(src_ref, src_row, dst_ref, dst_row, sem):
    return pltpu.make_async_copy(src_ref.at[pl.ds(src_row, 1)], dst_ref.at[pl.ds(dst_row, 1)], sem)


def _moe_grouped_kernel(pos_ref, tg_ref, nu_ref, hx_ref, wg_ref, wu_ref, wd_ref, ys_ref,
                        buf, sem, src_sc, xb_sc, *, t):
    i = pl.program_id(0)
    ei = pl.program_id(1)
    n_used = nu_ref[0]
    used = i < n_used
    slot = i % 2

    def fetch(tile, sl):
        def body(r, c):
            _row_copy(hx_ref, src_sc[tile * MOE_TILE + r], buf.at[sl], r, sem.at[sl]).start()
            return c
        lax.fori_loop(0, MOE_TILE, body, 0, unroll=8)

    @pl.when((i == 0) & (ei == 0))
    def _():
        def clear(r, c):
            src_sc[r] = 0
            return c

        def invert(tk, c):
            src_sc[pos_ref[tk]] = tk
            return c

        lax.fori_loop(0, src_sc.shape[0], clear, 0, unroll=8)
        lax.fori_loop(0, t, invert, 0, unroll=8)
        fetch(0, 0)

    @pl.when((ei == 0) & (i + 1 < n_used))
    def _():
        fetch(i + 1, (i + 1) % 2)

    @pl.when(used & (ei == 0))
    def _():
        def wait(r, c):
            _row_copy(hx_ref, 0, buf.at[slot], r, sem.at[slot]).wait()
            return c
        lax.fori_loop(0, MOE_TILE, wait, 0, unroll=8)
        xb_sc[...] = buf[slot, :, 0:D_MODEL].astype(BF16)

    @pl.when(ei == 0)
    def _():
        ys_ref[...] = jnp.zeros(ys_ref.shape, F32)

    @pl.when(used)
    def _():
        h = xb_sc[...]
        a = _dot(h, wg_ref[...].astype(BF16))
        u = _dot(h, wu_ref[...].astype(BF16))
        lane = lax.broadcasted_iota(jnp.int32, (MOE_TILE, LANES), 1)
        e = tg_ref[i] * EXP_PER_GROUP + ei
        cw = jnp.sum(jnp.where(lane == e, buf[slot, :, D_MODEL:HX_W], 0.0), axis=-1, keepdims=True)
        hid = (_silu(a) * u * cw).astype(BF16)
        ys_ref[...] += _dot(hid, wd_ref[...].astype(BF16))


def _moe_grouped(hx, pos, tg, nu, wg, wu, wd, rows_out):
    t = hx.shape[0]
    ne, d, de = wg.shape
    ntiles = rows_out // MOE_TILE

    def wmap(i, ei, pos_ref, tg_ref, nu_ref):
        e = jnp.where(i < nu_ref[0], tg_ref[i] * EXP_PER_GROUP + ei, ne - 1)
        return (e, 0, 0)

    grid_spec = pltpu.PrefetchScalarGridSpec(
        num_scalar_prefetch=3,
        grid=(ntiles, EXP_PER_GROUP),
        in_specs=[pl.BlockSpec(memory_space=pl.ANY),
                  pl.BlockSpec((None, d, de), wmap),
                  pl.BlockSpec((None, d, de), wmap),
                  pl.BlockSpec((None, de, d), wmap)],
        out_specs=pl.BlockSpec((MOE_TILE, d), lambda i, ei, pos_ref, tg_ref, nu_ref: (i, 0)),
        scratch_shapes=[pltpu.VMEM((2, MOE_TILE, HX_W), F32), pltpu.SemaphoreType.DMA((2,)),
                        pltpu.SMEM((rows_out,), jnp.int32),
                        pltpu.VMEM((MOE_TILE, d), BF16)],
    )
    return pl.pallas_call(
        functools.partial(_moe_grouped_kernel, t=t),
        grid_spec=grid_spec,
        out_shape=jax.ShapeDtypeStruct((rows_out, d), F32),
        compiler_params=_cparams(("arbitrary", "arbitrary")),
        name="moe_grouped",
    )(pos, tg, nu, hx, wg, wu, wd)


COMBINE_ROWS = 256


def _moe_combine_kernel(pos_ref, ys_ref, x1_ref, gt_ref, gf_ref, y_ref, buf, sem):
    i = pl.program_id(0)
    n = pl.num_programs(0)

    def fetch(tile, slot):
        def body(r, c):
            _row_copy(ys_ref, pos_ref[tile * COMBINE_ROWS + r], buf.at[slot], r, sem.at[slot]).start()
            return c
        lax.fori_loop(0, COMBINE_ROWS, body, 0, unroll=8)

    @pl.when(i == 0)
    def _():
        fetch(0, 0)

    @pl.when(i + 1 < n)
    def _():
        fetch(i + 1, (i + 1) % 2)

    slot = i % 2

    def wait(r, c):
        _row_copy(ys_ref, pos_ref[i * COMBINE_ROWS + r], buf.at[slot], r, sem.at[slot]).wait()
        return c

    lax.fori_loop(0, COMBINE_ROWS, wait, 0, unroll=8)
    x2 = x1_ref[...] + gt_ref[...] * buf[slot]
    y_ref[...] = _rms(x2) * gf_ref[...]


def _moe_combine(ys, pos, x1, mod, g_final, tiles_per_batch):
    m, d = x1.shape
    rb = mod.shape[2]
    grid_spec = pltpu.PrefetchScalarGridSpec(
        num_scalar_prefetch=1,
        grid=(m // COMBINE_ROWS,),
        in_specs=[pl.BlockSpec(memory_space=pl.ANY),
                  pl.BlockSpec((COMBINE_ROWS, d), lambda i, pos_ref: (i, 0)),
                  pl.BlockSpec((None, None, rb, d), lambda i, pos_ref: (5, i // tiles_per_batch, 0, 0)),
                  pl.BlockSpec((1, d), lambda i, pos_ref: (0, 0))],
        out_specs=pl.BlockSpec((COMBINE_ROWS, d), lambda i, pos_ref: (i, 0)),
        scratch_shapes=[pltpu.VMEM((2, COMBINE_ROWS, d), F32), pltpu.SemaphoreType.DMA((2,))],
    )
    return pl.pallas_call(
        _moe_combine_kernel,
        grid_spec=grid_spec,
        out_shape=jax.ShapeDtypeStruct((m, d), F32),
        compiler_params=_cparams(("arbitrary",)),
        name="moe_combine",
    )(pos, ys, x1, mod, g_final)


def _moe_kernel(h_ref, cmb_ref, wg_ref, wu_ref, wd_ref, x1_ref, gt_ref, gf_ref, y_ref, acc_ref):
    e = pl.program_id(1)

    @pl.when(e == 0)
    def _():
        acc_ref[...] = jnp.zeros(acc_ref.shape, F32)

    h = h_ref[...]
    a = _dot(h, wg_ref[...].astype(BF16))
    u = _dot(h, wu_ref[...].astype(BF16))
    lane = lax.broadcasted_iota(jnp.int32, cmb_ref.shape, 1)
    cw = jnp.sum(jnp.where(lane == e, cmb_ref[...], 0.0), axis=-1, keepdims=True)
    hid = (_silu(a) * u * cw).astype(BF16)
    acc_ref[...] += _dot(hid, wd_ref[...].astype(BF16))

    @pl.when(e == pl.num_programs(1) - 1)
    def _():
        x2 = x1_ref[...] + gt_ref[...] * acc_ref[...]
        y_ref[...] = _rms(x2) * gf_ref[...]


def _moe(h2, cmb, wg, wu, wd, x1, mod, g_final, tm, tiles_per_batch):
    m, d = x1.shape
    rb = mod.shape[2]
    ne, _, de = wg.shape
    return pl.pallas_call(
        _moe_kernel,
        grid=(m // tm, ne),
        in_specs=[pl.BlockSpec((tm, d), lambda i, e: (i, 0)),
                  pl.BlockSpec((tm, LANES), lambda i, e: (i, 0)),
                  pl.BlockSpec((None, d, de), lambda i, e: (e, 0, 0)),
                  pl.BlockSpec((None, d, de), lambda i, e: (e, 0, 0)),
                  pl.BlockSpec((None, de, d), lambda i, e: (e, 0, 0)),
                  pl.BlockSpec((tm, d), lambda i, e: (i, 0)),
                  pl.BlockSpec((None, None, rb, d), lambda i, e: (5, i // tiles_per_batch, 0, 0)),
                  pl.BlockSpec((1, d), lambda i, e: (0, 0))],
        out_specs=pl.BlockSpec((tm, d), lambda i, e: (i, 0)),
        out_shape=jax.ShapeDtypeStruct((m, d), F32),
        scratch_shapes=[pltpu.VMEM((tm, d), F32)],
        compiler_params=_cparams(("arbitrary", "arbitrary")),
        name="moe",
    )(h2, cmb, wg, wu, wd, x1, mod, g_final)


def _prep_weights(w_ada, b_ada, g_norm1, g_norm2, g_final, w_in, w_a2, b_a, b_f, g_gla_norm, w_up_a, w_up_b, w_out,
                  w_grp, b_grp, w_exp, b_exp, w_gate_e, w_up_e, w_down_e):
    wt = w_in.reshape(w_in.shape[1:]).T
    d = D_MODEL
    o_lra = 2 * GLA_KW + 2 * GLA_VW
    o_fox = o_lra + GLA_RANK
    o_fb = o_fox + 3 * FOX_W
    o_g = o_fb + FOX_HEADS
    pad = jnp.zeros((LANES - GLA_RANK - FOX_HEADS, d), F32)
    wall, mask = _gla_tables()
    wall_s, mask_s = _gla_tables(SAMPLE_LEVELS)
    bf_row =jnp.zeros((1, LANES), F32).at[0, FB_LANE:FB_LANE + FOX_HEADS].set(b_f[0])
    w_router = jnp.concatenate([w_exp[0], w_grp[0], jnp.zeros((d, LANES - N_EXPERTS - N_GROUPS), F32)], axis=1)
    b_router = jnp.concatenate([b_exp[0], b_grp[0], jnp.zeros((LANES - N_EXPERTS - N_GROUPS,), F32)])[None, :]
    return dict(
        w_ada=w_ada.reshape(w_ada.shape[1:]), b_ada=b_ada[0][None, :],
        g1=g_norm1[0][None, :], g2=g_norm2[0][None, :], gf=g_final[None, :],
        w_in_t=wt,
        w_small_t=jnp.concatenate([wt[o_lra:o_lra + GLA_RANK], wt[o_fb:o_fb + FOX_HEADS], pad], axis=0),
        wall=jnp.asarray(wall, BF16), mask=jnp.asarray(mask, F32),
        wall_s=jnp.asarray(wall_s, BF16), mask_s=jnp.asarray(mask_s, F32),
        w_a2=jnp.concatenate([w_a2[0], jnp.zeros((LANES - GLA_RANK, GLA_KW), F32)], axis=0), b_a=b_a[0][None, :], bf_row=bf_row, g_gla=g_gla_norm[0][None, :],
        w_up_a=w_up_a[0].astype(BF16), w_up_b=w_up_b[0].astype(BF16), w_out=w_out[0].astype(BF16),
        w_router=w_router, b_router=b_router,
        wg=w_gate_e.reshape(w_gate_e.shape[1:]), wu=w_up_e.reshape(w_up_e.shape[1:]),
        wd=w_down_e.reshape(w_down_e.shape[1:]),
    )


def _project(h, p, tm):
    tn = 1024
    wt = p["w_in_t"]
    n_gla = 2 * GLA_KW + 2 * GLA_VW
    c_q = n_gla + GLA_RANK
    c_g = c_q + 3 * FOX_W + FOX_HEADS
    (pg,) = _mm_wt(h, wt, 0, n_gla, tm, tn)
    (qb,) = _mm_wt(h, wt, c_q, FOX_W, tm, tn, (BF16,))
    kb, kb16 = _mm_heads(h, wt, c_q + FOX_W, min(tm, 512))
    vb, vb16 = _mm_heads(h, wt, c_q + 2 * FOX_W, min(tm, 512))
    (gates,) = _mm_wt(h, wt, c_g, 2 * D_MODEL, tm, tn, (BF16,))
    (small,) = _mm_wt(h, p["w_small_t"], 0, LANES, tm, LANES)
    return pg, qb, kb, kb16, vb, vb16, gates, small


def _tail(x, oa, ob, gates, mod, p, tm, tiles_per_batch, grouped):
    merged = _merge(oa, ob, p["w_up_a"], p["w_up_b"], gates, tm)
    tmo = min(tm, 256)
    x1, hx = _outproj(merged, p["w_out"], x, mod, p["g2"], p["w_router"], p["b_router"],
                      tmo, tiles_per_batch * (tm // tmo))
    if not grouped:
        tmm = min(tm, 512)
        return _moe(hx[:, 0:D_MODEL].astype(BF16), hx[:, D_MODEL:HX_W], p["wg"], p["wu"], p["wd"], x1, mod, p["gf"],
                    tmm, tiles_per_batch * (tm // tmm))
    rows_out = x.shape[0] + N_GROUPS * MOE_TILE
    pos, tg, nu = _moe_plan(hx, rows_out // MOE_TILE)
    ys = _moe_grouped(hx, pos, tg, nu, p["wg"], p["wu"], p["wd"], rows_out)
    return _moe_combine(ys, pos, x1, mod, p["gf"], tiles_per_batch * (tm // COMBINE_ROWS))


def kernel(x_prompt, x_sample, cache_k, cache_v, cache_logf, state_gla, page_table, c_prompt, c_sample, w_ada, b_ada,
           g_norm1, g_norm2, g_final, w_in, w_a2, b_a, b_f, g_gla_norm, w_up_a, w_up_b, w_out, w_grp, b_grp, w_exp,
           b_exp, w_gate_e, w_up_e, w_down_e):
    p = _prep_weights(w_ada, b_ada, g_norm1, g_norm2, g_final, w_in, w_a2, b_a, b_f, g_gla_norm, w_up_a, w_up_b,
                      w_out, w_grp, b_grp, w_exp, b_exp, w_gate_e, w_up_e, w_down_e)
    bp, seq, d = x_prompt.shape
    bs, t, _ = x_sample.shape
    assert t <= 2 ** SAMPLE_LEVELS and t <= ROWS8

    mod = _adaln(jnp.concatenate([c_prompt, c_sample], axis=0), p["w_ada"], p["b_ada"])
    mod_p = mod[:bp].reshape(bp, 6, 1, d).transpose(1, 0, 2, 3)
    mod_s = jnp.repeat(mod[bp:].reshape(bs, 6, d), t, axis=0).transpose(1, 0, 2)[:, None]

    tm = 1024
    tpb = seq // tm
    xp = x_prompt.reshape(bp * seq, d)
    hp = _normmod(xp, p["g1"], mod_p, tm, tpb)
    pg, qb, kb, kb16, vb, vb16, gates, small = _project(hp, p, tm)
    s0 = jnp.zeros((bp, GLA_HEADS, GLA_DK, GLA_DV), F32)
    oa, s_p = _gla_prompt(pg, small, s0, p["wall"], p["mask"], p["w_a2"], p["b_a"], p["g_gla"], bp, seq)
    lf_p, fcol, ft = _fox_bias_prompt(small, p["bf_row"], bp, seq)
    ob = _fox_prompt(qb, kb16, vb16, fcol, ft, bp, seq)
    y_p = _tail(xp, oa, ob, gates, mod_p, p, tm, tpb, grouped=True)

    rows = bs * t
    xs = x_sample.reshape(rows, d)
    hs = _normmod(xs, p["g1"], mod_s, rows, 1)
    pg_s, qs, ks, _, vs, _, gates_s, small_s = _project(hs, p, rows)
    oa_s, s_s = _gla_sample(pg_s.reshape(bs, t, -1), small_s.reshape(bs, t, LANES),
                            state_gla.reshape(state_gla.shape[1:]), p["wall_s"], p["mask_s"],
                            p["w_a2"], p["b_a"], p["g_gla"], bs, t)
    lf_s, fn_s = _fox_bias_sample(small_s, p["bf_row"], t)
    n_pool = cache_k.shape[1]
    ob_s = _fox_sample(page_table, qs.astype(F32).reshape(bs, t, FOX_W), ks.reshape(bs, t, FOX_HEADS, FOX_DH),
                       vs.reshape(bs, t, FOX_HEADS, FOX_DH), fn_s.reshape(bs, t, FOX_HEADS),
                       cache_k, cache_v, cache_logf.reshape(n_pool, PAGE, FOX_HEADS).transpose(0, 2, 1), t)
    y_s = _tail(xs, oa_s.reshape(rows, GLA_VW).astype(BF16), ob_s.reshape(rows, FOX_W).astype(BF16), gates_s, mod_s,
                p, rows, 1, grouped=False)

    return (y_p.reshape(bp, seq, d), y_s.reshape(bs, t, d),
            kb.reshape(1, bp, seq, FOX_HEADS, FOX_DH), vb.reshape(1, bp, seq, FOX_HEADS, FOX_DH),
            lf_p.reshape(1, bp, seq, FOX_HEADS), s_p[None],
            ks.reshape(1, bs, t, FOX_HEADS, FOX_DH), vs.reshape(1, bs, t, FOX_HEADS, FOX_DH),
            lf_s.reshape(1, bs, t, FOX_HEADS), s_s[None])
```

```python
import functools

import numpy as np
import jax
import jax.numpy as jnp
from jax import lax
from jax.experimental import pallas as pl
from jax.experimental.pallas import tpu as pltpu

F32 = jnp.float32
BF16 = jnp.bfloat16

D_MODEL = 2048
GLA_HEADS = 4
GLA_DK = 128
GLA_DV = 256
GLA_RANK = 16
GLA_TAU = 16.0
FOX_HEADS = 8
FOX_DH = 128
PAGE = 128
N_GROUPS = 4
EXP_PER_GROUP = 4
N_EXPERTS = 16
D_EXPERT = 512
RMS_EPS = 1e-6
GLA_KW = GLA_HEADS * GLA_DK
GLA_VW = GLA_HEADS * GLA_DV
FOX_W = FOX_HEADS * FOX_DH
GLA_CHUNK = 128
GLA_LEVELS = 7
SAMPLE_LEVELS = 2
LANES = 128
NEG = -1e30
VMEM_LIMIT = 56 * 1024 * 1024


def _cparams(sem):
    return pltpu.CompilerParams(dimension_semantics=sem, vmem_limit_bytes=VMEM_LIMIT)


def _dot(a, b):
    return jnp.dot(a, b, preferred_element_type=F32)


def _dot_nt(a, b):
    return lax.dot_general(a, b, (((1,), (1,)), ((), ())), preferred_element_type=F32)


def _dot_tn(a, b):
    return lax.dot_general(a, b, (((0,), (0,)), ((), ())), preferred_element_type=F32)


def _split2(x):
    hi = x.astype(BF16)
    lo = (x - hi.astype(F32)).astype(BF16)
    return hi, lo


def _split3(x):
    hi = x.astype(BF16)
    r = x - hi.astype(F32)
    mid = r.astype(BF16)
    lo = (r - mid.astype(F32)).astype(BF16)
    return hi, mid, lo


def _dot3(a, b):
    ah, al = _split2(a)
    bh, bl = _split2(b)
    return _dot(ah, bh) + _dot(ah, bl) + _dot(al, bh)


def _dot_sel(w01, x):
    hi, mid, lo = _split3(x)
    return _dot(w01, hi) + _dot(w01, mid) + _dot(w01, lo)


def _log_sigmoid(x):
    return jnp.minimum(x, 0.0) - jnp.log1p(jnp.exp(-jnp.abs(x)))


def _silu(x):
    return x * jax.nn.sigmoid(x)


def _adaln_kernel(c_ref, w_ref, b_ref, o_ref):
    o_ref[...] = _dot3(_silu(c_ref[...]), w_ref[...]) + b_ref[...]


def _adaln(c, w, b, tn=512):
    nb, d = c.shape
    n = w.shape[1]
    return pl.pallas_call(
        _adaln_kernel,
        grid=(n // tn,),
        in_specs=[pl.BlockSpec((nb, d), lambda j: (0, 0)),
                  pl.BlockSpec((d, tn), lambda j: (0, j)),
                  pl.BlockSpec((1, tn), lambda j: (0, j))],
        out_specs=pl.BlockSpec((nb, tn), lambda j: (0, j)),
        out_shape=jax.ShapeDtypeStruct((nb, n), F32),
        compiler_params=_cparams(("arbitrary",)),
        name="adaln",
    )(c, w, b)


def _rms(x):
    return x * lax.rsqrt(jnp.mean(x * x, axis=-1, keepdims=True) + RMS_EPS)


def _normmod_kernel(x_ref, g_ref, sh_ref, sc_ref, o_ref):
    y = _rms(x_ref[...]) * g_ref[...]
    o_ref[...] = (y * (1.0 + sc_ref[...]) + sh_ref[...]).astype(o_ref.dtype)


def _mod_spec(k, rb, d, tiles_per_batch):
    return pl.BlockSpec((None, None, rb, d), lambda i: (k, i // tiles_per_batch, 0, 0))


def _normmod(x, g, mod, tm, tiles_per_batch):
    m, d = x.shape
    rb = mod.shape[2]
    return pl.pallas_call(
        _normmod_kernel,
        grid=(m // tm,),
        in_specs=[pl.BlockSpec((tm, d), lambda i: (i, 0)),
                  pl.BlockSpec((1, d), lambda i: (0, 0)),
                  _mod_spec(0, rb, d, tiles_per_batch),
                  _mod_spec(1, rb, d, tiles_per_batch)],
        out_specs=pl.BlockSpec((tm, d), lambda i: (i, 0)),
        out_shape=jax.ShapeDtypeStruct((m, d), BF16),
        compiler_params=_cparams(("arbitrary",)),
        name="normmod",
    )(x, g, mod, mod)


def _wt_spec(k, row0, tn):
    assert row0 % 8 == 0 and tn % 8 == 0
    return pl.BlockSpec((pl.Element(tn), pl.Element(k)), lambda j, i: (pl.multiple_of(row0 + j * tn, 8), 0))


def _mm_wt_kernel(x_ref, wt_ref, *refs):
    o_refs, wb = refs[:-1], refs[-1]

    @pl.when(pl.program_id(1) == 0)
    def _():
        wb[...] = wt_ref[...].astype(BF16)

    r = _dot_nt(x_ref[...], wb[...])
    for o_ref in o_refs:
        o_ref[...] = r.astype(o_ref.dtype)


def _mm_wt(x, wt, row0, n, tm, tn, out_dtypes=(F32,)):
    m, k = x.shape
    return pl.pallas_call(
        _mm_wt_kernel,
        grid=(n // tn, m // tm),
        in_specs=[pl.BlockSpec((tm, k), lambda j, i: (i, 0)), _wt_spec(k, row0, tn)],
        out_specs=[pl.BlockSpec((tm, tn), lambda j, i: (i, j)) for _ in out_dtypes],
        out_shape=[jax.ShapeDtypeStruct((m, n), dt) for dt in out_dtypes],
        scratch_shapes=[pltpu.VMEM((tn, k), BF16)],
        compiler_params=_cparams(("arbitrary", "arbitrary")),
        name="proj_mm",
    )(x, wt)


def _mm_heads_kernel(x_ref, wt_ref, o_ref, o16_ref, wb):
    @pl.when(pl.program_id(1) == 0)
    def _():
        wb[...] = wt_ref[...].astype(BF16)

    r = _dot_nt(x_ref[...], wb[...])
    for h in range(FOX_HEADS):
        o_ref[:, h, :] = r[:, h * FOX_DH:(h + 1) * FOX_DH]
    o16_ref[...] = r.astype(BF16)


def _mm_heads(x, wt, row0, tm):
    m, k = x.shape
    return pl.pallas_call(
        _mm_heads_kernel,
        grid=(1, m // tm),
        in_specs=[pl.BlockSpec((tm, k), lambda j, i: (i, 0)), _wt_spec(k, row0, FOX_W)],
        out_specs=[pl.BlockSpec((tm, FOX_HEADS, FOX_DH), lambda j, i: (i, 0, 0)),
                   pl.BlockSpec((tm, FOX_W), lambda j, i: (i, 0))],
        out_shape=[jax.ShapeDtypeStruct((m, FOX_HEADS, FOX_DH), F32),
                   jax.ShapeDtypeStruct((m, FOX_W), BF16)],
        scratch_shapes=[pltpu.VMEM((FOX_W, k), BF16)],
        compiler_params=_cparams(("arbitrary", "arbitrary")),
        name="proj_heads",
    )(x, wt)


def _gla_tables(p=GLA_LEVELS):
    c = GLA_CHUNK
    t = np.arange(c)[:, None]
    m = np.arange(c)[None, :]
    wall = np.zeros((p + 2, c, c), np.float32)
    mask = np.zeros((p + 1, c, c), np.float32)
    for l in range(p):
        half = 1 << l
        pos = t % (2 * half)
        mid = t - pos + half
        right = pos >= half
        wall[l] = np.where(right, (m >= mid) & (m <= t), (m > t) & (m < mid))
        s = m
        mask[l] = ((t >> (l + 1)) == (s >> (l + 1))) & (((t >> l) & 1) == 1) & (((s >> l) & 1) == 0)
    wall[p] = m <= t
    wall[p + 1] = m > t
    mask[p] = t == m
    return wall.reshape((p + 2) * c, c), mask


def _gla_body(q_ref, k_ref, v_ref, ra_ref, sm_ref, wall_ref, mask_ref, wa2_ref, ba_ref, g_ref,
              og_ref, st_ref, *, nchunk, valid, store_rows):
    c, p = GLA_CHUNK, mask_ref.shape[0] - 1
    row = lax.broadcasted_iota(jnp.int32, (c, GLA_DK), 0)
    wall = wall_ref[...]
    for ci in range(nchunk):
        rows = pl.ds(ci * c, c)
        x = _dot3(sm_ref[rows, :], wa2_ref[...]) + ba_ref[...]
        la = _log_sigmoid(x) * (1.0 / GLA_TAU)
        if valid < c:
            rowh = lax.broadcasted_iota(jnp.int32, la.shape, 0)
            la = jnp.where(rowh < valid, la, 0.0)
        e_all = jnp.exp(_dot_sel(wall, la))
        for h in range(GLA_HEADS):
            ks = slice(h * GLA_DK, (h + 1) * GLA_DK)
            vs = slice(h * GLA_DV, (h + 1) * GLA_DV)
            q = q_ref[rows, ks] * (GLA_DK ** -0.5)
            k = k_ref[rows, ks]
            vb = v_ref[rows, vs].astype(BF16)
            a = mask_ref[p] * _dot_nt(q.astype(BF16), k.astype(BF16))
            for l in range(p):
                el = e_all[l * c:(l + 1) * c, ks]
                xl = (jnp.where(((row >> l) & 1) == 1, q, k) * el).astype(BF16)
                a = a + mask_ref[l] * _dot_nt(xl, xl)
            st = st_ref[h]
            qc = (q * e_all[p * c:(p + 1) * c, ks]).astype(BF16)
            o = _dot(a.astype(BF16), vb) + _dot_nt(qc, st.astype(BF16))
            kr = (k * e_all[(p + 1) * c:(p + 2) * c, ks]).astype(BF16)
            dec = e_all[(p + 1) * c - 1:(p + 1) * c, ks]
            st_ref[h] = dec * st + _dot_tn(vb, kr)
            og = _rms(o) * g_ref[...] * _silu(ra_ref[rows, vs])
            if store_rows < c:
                og_ref[:, vs] = og[:store_rows].astype(og_ref.dtype)
            else:
                og_ref[rows, vs] = og.astype(og_ref.dtype)


def _gla_prompt_kernel(q_ref, k_ref, v_ref, ra_ref, sm_ref, wall_ref, mask_ref, wa2_ref, ba_ref, g_ref, s0_ref,
                       og_ref, s_ref, st_ref, *, nchunk):
    n = pl.program_id(1)

    @pl.when(n == 0)
    def _():
        for h in range(GLA_HEADS):
            st_ref[h] = s0_ref[h].T

    _gla_body(q_ref, k_ref, v_ref, ra_ref, sm_ref, wall_ref, mask_ref, wa2_ref, ba_ref, g_ref, og_ref, st_ref,
              nchunk=nchunk, valid=GLA_CHUNK, store_rows=GLA_CHUNK)

    @pl.when(n == pl.num_programs(1) - 1)
    def _():
        for h in range(GLA_HEADS):
            s_ref[h] = st_ref[h].T


def _gla_prompt(pg, small, s0, wall, mask, wa2, ba, g, batch, seq, tb=256):
    nblk = seq // tb
    rowmap = lambda cb: (lambda b, n: (b * nblk + n, cb))
    const2 = lambda b, n: (0, 0)
    return pl.pallas_call(
        functools.partial(_gla_prompt_kernel, nchunk=tb // GLA_CHUNK),
        grid=(batch, nblk),
        in_specs=[pl.BlockSpec((tb, GLA_KW), rowmap(0)),
                  pl.BlockSpec((tb, GLA_KW), rowmap(1)),
                  pl.BlockSpec((tb, GLA_VW), rowmap(1)),
                  pl.BlockSpec((tb, GLA_VW), rowmap(2)),
                  pl.BlockSpec((tb, LANES), rowmap(0)),
                  pl.BlockSpec(wall.shape, const2),
                  pl.BlockSpec(mask.shape, lambda b, n: (0, 0, 0)),
                  pl.BlockSpec(wa2.shape, const2),
                  pl.BlockSpec(ba.shape, const2),
                  pl.BlockSpec(g.shape, const2),
                  pl.BlockSpec((None, GLA_HEADS, GLA_DK, GLA_DV), lambda b, n: (b, 0, 0, 0))],
        out_specs=[pl.BlockSpec((tb, GLA_VW), rowmap(0)),
                   pl.BlockSpec((None, GLA_HEADS, GLA_DK, GLA_DV), lambda b, n: (b, 0, 0, 0))],
        out_shape=[jax.ShapeDtypeStruct((batch * seq, GLA_VW), BF16),
                   jax.ShapeDtypeStruct((batch, GLA_HEADS, GLA_DK, GLA_DV), F32)],
        scratch_shapes=[pltpu.VMEM((GLA_HEADS, GLA_DV, GLA_DK), F32)],
        compiler_params=_cparams(("arbitrary", "arbitrary")),
        name="gla_prompt",
    )(pg, pg, pg, pg, small, wall, mask, wa2, ba, g, s0)


def _gla_sample_kernel(pg_ref, sm_ref, wall_ref, mask_ref, wa2_ref, ba_ref, g_ref, s0_ref,
                       og_ref, s_ref, pad_ref, smpad_ref, st_ref, *, t):
    @pl.when(pl.program_id(0) == 0)
    def _():
        pad_ref[...] = jnp.zeros(pad_ref.shape, F32)
        smpad_ref[...] = jnp.zeros(smpad_ref.shape, F32)

    pad_ref[0:t, :] = pg_ref[...]
    smpad_ref[0:t, :] = sm_ref[...]
    for h in range(GLA_HEADS):
        st_ref[h] = s0_ref[h].T
    q_ref = pad_ref.at[:, 0:GLA_KW]
    k_ref = pad_ref.at[:, GLA_KW:2 * GLA_KW]
    v_ref = pad_ref.at[:, 2 * GLA_KW:2 * GLA_KW + GLA_VW]
    ra_ref = pad_ref.at[:, 2 * GLA_KW + GLA_VW:2 * GLA_KW + 2 * GLA_VW]
    _gla_body(q_ref, k_ref, v_ref, ra_ref, smpad_ref, wall_ref, mask_ref, wa2_ref, ba_ref, g_ref, og_ref, st_ref,
              nchunk=1, valid=t, store_rows=t)
    for h in range(GLA_HEADS):
        s_ref[h] = st_ref[h].T


def _gla_sample(pg, small, s0, wall, mask, wa2, ba, g, batch, t):
    width = pg.shape[-1]
    const2 = lambda b: (0, 0)
    return pl.pallas_call(
        functools.partial(_gla_sample_kernel, t=t),
        grid=(batch,),
        in_specs=[pl.BlockSpec((None, t, width), lambda b: (b, 0, 0)),
                  pl.BlockSpec((None, t, LANES), lambda b: (b, 0, 0)),
                  pl.BlockSpec(wall.shape, const2),
                  pl.BlockSpec(mask.shape, lambda b: (0, 0, 0)),
                  pl.BlockSpec(wa2.shape, const2),
                  pl.BlockSpec(ba.shape, const2),
                  pl.BlockSpec(g.shape, const2),
                  pl.BlockSpec((None, GLA_HEADS, GLA_DK, GLA_DV), lambda b: (b, 0, 0, 0))],
        out_specs=[pl.BlockSpec((None, t, GLA_VW), lambda b: (b, 0, 0)),
                   pl.BlockSpec((None, GLA_HEADS, GLA_DK, GLA_DV), lambda b: (b, 0, 0, 0))],
        out_shape=[jax.ShapeDtypeStruct((batch, t, GLA_VW), F32),
                   jax.ShapeDtypeStruct((batch, GLA_HEADS, GLA_DK, GLA_DV), F32)],
        scratch_shapes=[pltpu.VMEM((GLA_CHUNK, width), F32),
                        pltpu.VMEM((GLA_CHUNK, LANES), F32),
                        pltpu.VMEM((GLA_HEADS, GLA_DV, GLA_DK), F32)],
        compiler_params=_cparams(("arbitrary",)),
        name="gla_sample",
    )(pg, small, wall, mask, wa2, ba, g, s0)


FB_LANE = GLA_RANK


LOG2E = 1.4426950408889634


def _fox_bias_prompt_kernel(sm_ref, bf_ref, tri_ref, lf_ref, fq_ref, fk_ref, carry_ref):
    @pl.when(pl.program_id(1) == 0)
    def _():
        carry_ref[...] = jnp.zeros(carry_ref.shape, F32)

    lf = _log_sigmoid(sm_ref[...] + bf_ref[...])
    lf_ref[...] = lf[:, FB_LANE:FB_LANE + FOX_HEADS]
    cum = _dot_sel(tri_ref[...], lf) + carry_ref[...]
    carry_ref[...] = cum[cum.shape[0] - 1:, :]
    lane = lax.broadcasted_iota(jnp.int32, cum.shape, 1)
    ones = jnp.where(lane < 6, 1.0, 0.0)
    for h in range(FOX_HEADS):
        f = jnp.broadcast_to(cum[:, FB_LANE + h:FB_LANE + h + 1] * LOG2E, cum.shape)
        hi = f.astype(BF16).astype(F32)
        r = f - hi
        mid = r.astype(BF16).astype(F32)
        lo = r - mid
        fq = jnp.where(lane == 0, hi, jnp.where(lane == 1, mid, jnp.where(lane == 2, lo, ones)))
        fk = jnp.where(lane == 3, -hi, jnp.where(lane == 4, -mid, jnp.where(lane == 5, -lo, ones)))
        fq_ref[:, h * LANES:(h + 1) * LANES] = fq.astype(BF16)
        fk_ref[:, h * LANES:(h + 1) * LANES] = fk.astype(BF16)


def _fox_bias_prompt(small, bf_row, batch, seq, tb=256):
    nblk = seq // tb
    tri = jnp.asarray(np.tril(np.ones((tb, tb), np.float32)), BF16)
    wide = FOX_HEADS * LANES
    return pl.pallas_call(
        _fox_bias_prompt_kernel,
        grid=(batch, nblk),
        in_specs=[pl.BlockSpec((tb, LANES), lambda b, n: (b * nblk + n, 0)),
                  pl.BlockSpec((1, LANES), lambda b, n: (0, 0)),
                  pl.BlockSpec((tb, tb), lambda b, n: (0, 0))],
        out_specs=[pl.BlockSpec((tb, FOX_HEADS), lambda b, n: (b * nblk + n, 0)),
                   pl.BlockSpec((tb, wide), lambda b, n: (b * nblk + n, 0)),
                   pl.BlockSpec((tb, wide), lambda b, n: (b * nblk + n, 0))],
        out_shape=[jax.ShapeDtypeStruct((batch * seq, FOX_HEADS), F32),
                   jax.ShapeDtypeStruct((batch * seq, wide), BF16),
                   jax.ShapeDtypeStruct((batch * seq, wide), BF16)],
        scratch_shapes=[pltpu.VMEM((1, LANES), F32)],
        compiler_params=_cparams(("arbitrary", "arbitrary")),
        name="fox_bias_prompt",
    )(small, bf_row, tri)


def _fox_bias_sample_kernel(sm_ref, bf_ref, sel_ref, lf_ref, fn_ref):
    lf = _log_sigmoid(sm_ref[...] + bf_ref[...])
    lf_ref[...] = lf[:, FB_LANE:FB_LANE + FOX_HEADS]
    cum = _dot_sel(sel_ref[...], lf)
    fn_ref[...] = cum[:, FB_LANE:FB_LANE + FOX_HEADS]


def _fox_bias_sample(small, bf_row, t):
    rows = small.shape[0]
    r = np.arange(rows)
    sel = ((r[:, None] // t) == (r[None, :] // t)) & (r[None, :] <= r[:, None])
    sel = jnp.asarray(sel.astype(np.float32), BF16)
    full = lambda shape: pl.BlockSpec(shape, lambda i: tuple(0 for _ in shape))
    return pl.pallas_call(
        _fox_bias_sample_kernel,
        grid=(1,),
        in_specs=[full((rows, LANES)), full((1, LANES)), full((rows, rows))],
        out_specs=[full((rows, FOX_HEADS)), full((rows, FOX_HEADS))],
        out_shape=[jax.ShapeDtypeStruct((rows, FOX_HEADS), F32),
                   jax.ShapeDtypeStruct((rows, FOX_HEADS), F32)],
        compiler_params=_cparams(("arbitrary",)),
        name="fox_bias_sample",
    )(small, bf_row, sel)


FOX_HEADS_PER_STEP = 8


def _fox_prompt_kernel(q_ref, k_ref, v_ref, fq_ref, fk_ref, o_ref, *, tq, tk):
    hb = FOX_HEADS_PER_STEP
    i = pl.program_id(2)
    rowi = lax.broadcasted_iota(jnp.int32, (tq, tk), 0)
    coli = lax.broadcasted_iota(jnp.int32, (tq, tk), 1)
    qs = []
    for hh in range(hb):
        hs = slice(hh * FOX_DH, (hh + 1) * FOX_DH)
        q = (q_ref[:, hs].astype(F32) * (FOX_DH ** -0.5 * LOG2E)).astype(BF16)
        qs.append(jnp.concatenate([q, fq_ref[:, hs]], axis=1))

    def step(j, carry, masked):
        ks = pl.ds(pl.multiple_of(j * tk, tk), tk)
        out = []
        for hh in range(hb):
            hs = slice(hh * FOX_DH, (hh + 1) * FOX_DH)
            m, l, acc = carry[hh]
            s = _dot_nt(qs[hh], jnp.concatenate([k_ref[ks, hs], fk_ref[ks, hs]], axis=1))
            if masked:
                s = jnp.where(rowi + i * tq >= coli + j * tk, s, NEG)
            m_new = jnp.maximum(m, jnp.max(s, axis=-1, keepdims=True))
            p = jnp.exp2(s - m_new)
            alpha = jnp.exp2(m - m_new)
            l = alpha * l + jnp.sum(p, axis=-1, keepdims=True)
            acc = alpha * acc + _dot(p.astype(BF16), v_ref[ks, hs])
            out.append((m_new, l, acc))
        return tuple(out)

    init = tuple((jnp.full((tq, 1), NEG, F32), jnp.zeros((tq, 1), F32), jnp.zeros((tq, FOX_DH), F32))
                 for _ in range(hb))
    nfull = (i * tq) // tk
    carry = lax.fori_loop(0, nfull, lambda j, c: step(j, c, False), init)
    for d in range(pl.cdiv(tq, tk)):
        carry = step(nfull + d, carry, True)
    for hh in range(hb):
        m, l, acc = carry[hh]
        o_ref[:, hh * FOX_DH:(hh + 1) * FOX_DH] = (acc / l).astype(o_ref.dtype)


def _fox_prompt(qb, kb, vb, fq, fk, batch, seq, tq=512, tk=256):
    nq = seq // tq
    hb = FOX_HEADS_PER_STEP
    wb = hb * FOX_DH
    return pl.pallas_call(
        functools.partial(_fox_prompt_kernel, tq=tq, tk=tk),
        grid=(batch, FOX_HEADS // hb, nq),
        in_specs=[pl.BlockSpec((tq, wb), lambda b, h, i: (b * nq + i, h)),
                  pl.BlockSpec((seq, wb), lambda b, h, i: (b, h)),
                  pl.BlockSpec((seq, wb), lambda b, h, i: (b, h)),
                  pl.BlockSpec((tq, wb), lambda b, h, i: (b * nq + i, h)),
                  pl.BlockSpec((seq, wb), lambda b, h, i: (b, h))],
        out_specs=pl.BlockSpec((tq, wb), lambda b, h, i: (b * nq + i, h)),
        out_shape=jax.ShapeDtypeStruct((batch * seq, FOX_W), BF16),
        compiler_params=_cparams(("arbitrary", "arbitrary", "arbitrary")),
        name="fox_prompt",
    )(qb, kb, vb, fq, fk)


PAGES_PER_STEP = 16
ROWS8 = 8
QROWS = FOX_HEADS * ROWS8
PAGE_FLAT = PAGE * FOX_HEADS


def _dot_sel_rhs(x, w01):
    hi, mid, lo = _split3(x)
    m = x.shape[0]
    stacked = jnp.concatenate([hi.astype(F32), mid.astype(F32), lo.astype(F32)], axis=0).astype(BF16)
    r = _dot(stacked, w01)
    return r[0:m] + r[m:2 * m] + r[2 * m:3 * m]


def _fox_sample_kernel(pt_ref, q_ref, kn_ref, vn_ref, fn_ref, fnrow_ref, madd_ref, maddn_ref, usuf_ref, pfx_ref,
                       hsum_ref, pexp_ref, hmask_ref, *refs, t):
    g = PAGES_PER_STEP
    k_refs, v_refs, lf_refs = refs[0:g], refs[g:2 * g], refs[2 * g:3 * g]
    o_ref = refs[3 * g]
    q_sc, m_sc, l_sc, acc_sc, carry_sc = refs[3 * g + 1:]
    j = pl.program_id(1)
    fn_t = fn_ref[...][:, 0:1]

    @pl.when(j == 0)
    def _():
        q_sc[...] = jnp.zeros(q_sc.shape, F32)
        for h in range(FOX_HEADS):
            q_sc[h * ROWS8:h * ROWS8 + t, :] = q_ref[:, h * FOX_DH:(h + 1) * FOX_DH] * (FOX_DH ** -0.5)
        carry_sc[...] = jnp.zeros(carry_sc.shape, F32)
        pad = jnp.zeros((LANES - t * FOX_HEADS, FOX_DH), F32)
        kn = jnp.concatenate([kn_ref[...].reshape(t * FOX_HEADS, FOX_DH), pad], axis=0)
        vn = jnp.concatenate([vn_ref[...].reshape(t * FOX_HEADS, FOX_DH), pad], axis=0)
        s = _dot_nt(q_sc[...], kn) + fn_t - fnrow_ref[...] + maddn_ref[...]
        m = jnp.max(s, axis=-1, keepdims=True)
        p = jnp.exp(s - m)
        m_sc[...] = jnp.broadcast_to(m, m_sc.shape)
        l_sc[...] = jnp.broadcast_to(jnp.sum(p, axis=-1, keepdims=True), l_sc.shape)
        acc_sc[...] = _dot(p, vn)

    lf = jnp.concatenate([lf_refs[gi][...] for gi in range(g)], axis=0)
    tot = jnp.broadcast_to(jnp.sum(lf, axis=-1, keepdims=True), lf.shape)
    carry = carry_sc[...]
    r_hk = (_dot_sel_rhs(lf, usuf_ref[...]) + _dot_sel(pfx_ref[...], tot)
            + jnp.concatenate([carry] * g, axis=0))
    carry_sc[...] = carry + _dot_sel(hsum_ref[...], tot)
    spread = _dot_sel_rhs(r_hk, pexp_ref[...]) * hmask_ref[...]
    r_flat = jnp.sum(spread.reshape(g, FOX_HEADS, PAGE_FLAT), axis=1)
    q = q_sc[...]
    madd = madd_ref[...] + fn_t
    m_old = m_sc[...]
    m_new = m_old
    s_list = []
    for gi in range(g):
        bias = madd + r_flat[gi:gi + 1, :]
        s = _dot_nt(q, k_refs[gi][...].reshape(PAGE_FLAT, FOX_DH)) + bias
        s_list.append(s)
        m_new = jnp.maximum(m_new, jnp.max(s, axis=-1, keepdims=True))
    alpha = jnp.exp(m_old - m_new)
    l = alpha * l_sc[...]
    acc = alpha * acc_sc[...]
    m_col = m_new[:, 0:1]
    for gi in range(g):
        p = jnp.exp(s_list[gi] - m_col)
        l = l + jnp.sum(p, axis=-1, keepdims=True)
        acc = acc + _dot(p, v_refs[gi][...].reshape(PAGE_FLAT, FOX_DH))
    m_sc[...] = m_new
    l_sc[...] = l
    acc_sc[...] = acc

    @pl.when(j == pl.num_programs(1) - 1)
    def _():
        o = acc / l
        for h in range(FOX_HEADS):
            o_ref[:, h * FOX_DH:(h + 1) * FOX_DH] = o[h * ROWS8:h * ROWS8 + t, :].astype(o_ref.dtype)


def _fox_sample_tables(t):
    row = np.arange(QROWS)[:, None]
    col = np.arange(PAGE_FLAT)[None, :]
    madd = np.where((row // ROWS8) == (col % FOX_HEADS), 0.0, NEG).astype(np.float32)
    coln = np.arange(LANES)[None, :]
    ok = (coln < t * FOX_HEADS) & ((row // ROWS8) == (coln % FOX_HEADS)) & ((coln // FOX_HEADS) <= (row % ROWS8))
    maddn = np.where(ok, 0.0, NEG).astype(np.float32)
    key = np.arange(PAGE)
    usuf = (key[:, None] > key[None, :]).astype(np.float32)
    gh = np.arange(PAGES_PER_STEP * FOX_HEADS)
    same_head = (gh[:, None] % FOX_HEADS) == (gh[None, :] % FOX_HEADS)
    pfx = (same_head & ((gh[None, :] // FOX_HEADS) < (gh[:, None] // FOX_HEADS))).astype(np.float32)
    hsum = (np.arange(FOX_HEADS)[:, None] == (gh[None, :] % FOX_HEADS)).astype(np.float32)
    pexp = (key[:, None] == (col // FOX_HEADS)).astype(np.float32)
    hmask = ((gh[:, None] % FOX_HEADS) == (col % FOX_HEADS)).astype(np.float32)
    return (jnp.asarray(madd), jnp.asarray(maddn), jnp.asarray(usuf, BF16), jnp.asarray(pfx, BF16),
            jnp.asarray(hsum, BF16), jnp.asarray(pexp, BF16), jnp.asarray(hmask))


def _fox_sample(page_table, q, kn, vn, fn, cache_k, cache_v, cache_lf, t):
    batch, npages = page_table.shape
    g = PAGES_PER_STEP
    nsteps = npages // g
    tables = _fox_sample_tables(t)
    fn_rows = jnp.pad(fn.transpose(0, 2, 1), ((0, 0), (0, 0), (0, ROWS8 - t))).reshape(batch, QROWS, 1)
    fn_rows = jnp.broadcast_to(fn_rows, (batch, QROWS, LANES))
    fn_cols = jnp.pad(fn.reshape(batch, 1, t * FOX_HEADS), ((0, 0), (0, 0), (0, LANES - t * FOX_HEADS)))

    def page_map5(gi):
        return lambda b, j, pt: (0, pt[b, npages - 1 - (j * g + gi)], 0, 0, 0)

    def page_map3(gi):
        return lambda b, j, pt: (pt[b, npages - 1 - (j * g + gi)], 0, 0)

    seq3 = lambda b, j, pt: (b, 0, 0)
    seq4 = lambda b, j, pt: (b, 0, 0, 0)
    const2 = lambda b, j, pt: (0, 0)
    in_specs = [pl.BlockSpec((None, t, FOX_W), seq3),
                pl.BlockSpec((None, t, FOX_HEADS, FOX_DH), seq4),
                pl.BlockSpec((None, t, FOX_HEADS, FOX_DH), seq4),
                pl.BlockSpec((None, QROWS, LANES), seq3),
                pl.BlockSpec((None, 1, LANES), seq3)]
    in_specs += [pl.BlockSpec(tb.shape, const2) for tb in tables]
    in_specs += [pl.BlockSpec((None, None, PAGE, FOX_HEADS, FOX_DH), page_map5(gi)) for gi in range(g)]
    in_specs += [pl.BlockSpec((None, None, PAGE, FOX_HEADS, FOX_DH), page_map5(gi)) for gi in range(g)]
    in_specs += [pl.BlockSpec((None, FOX_HEADS, PAGE), page_map3(gi)) for gi in range(g)]
    grid_spec = pltpu.PrefetchScalarGridSpec(
        num_scalar_prefetch=1,
        grid=(batch, nsteps),
        in_specs=in_specs,
        out_specs=pl.BlockSpec((None, t, FOX_W), seq3),
        scratch_shapes=[pltpu.VMEM((QROWS, FOX_DH), F32),
                        pltpu.VMEM((QROWS, LANES), F32),
                        pltpu.VMEM((QROWS, LANES), F32),
                        pltpu.VMEM((QROWS, FOX_DH), F32),
                        pltpu.VMEM((FOX_HEADS, PAGE), F32)],
    )
    return pl.pallas_call(
        functools.partial(_fox_sample_kernel, t=t),
        grid_spec=grid_spec,
        out_shape=jax.ShapeDtypeStruct((batch, t, FOX_W), F32),
        compiler_params=_cparams(("arbitrary", "arbitrary")),
        name="fox_sample",
    )(page_table, q, kn, vn, fn_rows, fn_cols, *tables,
      *([cache_k] * g), *([cache_v] * g), *([cache_lf] * g))


def _merge_kernel(oa_ref, ob_ref, wa_ref, wb_ref, ga_ref, gb_ref, o_ref):
    ua = _dot(oa_ref[...], wa_ref[...])
    ub = _dot(ob_ref[...], wb_ref[...])
    ga = jax.nn.sigmoid(ga_ref[...].astype(F32))
    gb = jax.nn.sigmoid(gb_ref[...].astype(F32))
    o_ref[...] = (ga * ua + gb * ub).astype(o_ref.dtype)


def _merge(oa, ob, wa, wb, gates, tm, tn=1024):
    m = oa.shape[0]
    d = wa.shape[1]
    nj = d // tn
    return pl.pallas_call(
        _merge_kernel,
        grid=(nj, m // tm),
        in_specs=[pl.BlockSpec((tm, GLA_VW), lambda j, i: (i, 0)),
                  pl.BlockSpec((tm, FOX_W), lambda j, i: (i, 0)),
                  pl.BlockSpec((GLA_VW, tn), lambda j, i: (0, j)),
                  pl.BlockSpec((FOX_W, tn), lambda j, i: (0, j)),
                  pl.BlockSpec((tm, tn), lambda j, i: (i, j)),
                  pl.BlockSpec((tm, tn), lambda j, i: (i, nj + j))],
        out_specs=pl.BlockSpec((tm, tn), lambda j, i: (i, j)),
        out_shape=jax.ShapeDtypeStruct((m, d), BF16),
        compiler_params=_cparams(("arbitrary", "arbitrary")),
        name="merge",
    )(oa, ob, wa, wb, gates, gates)


ROUTER_GROUP_LANE = N_EXPERTS


def _route(logits):
    lane_i = lax.broadcasted_iota(jnp.int32, logits.shape, 1)
    lane = lane_i.astype(F32)
    grp_of_lane = (lane_i >> 2).astype(F32)
    big = float(LANES)
    is_grp = (lane_i >= ROUTER_GROUP_LANE) & (lane_i < ROUTER_GROUP_LANE + N_GROUPS)
    gl = jnp.where(is_grp, logits, NEG)
    gmax = jnp.max(gl, axis=-1, keepdims=True)
    g_idx = jnp.min(jnp.where(is_grp & (gl == gmax), lane - ROUTER_GROUP_LANE, big), axis=-1, keepdims=True)
    g_w = 1.0 / jnp.sum(jnp.where(is_grp, jnp.exp(gl - gmax), 0.0), axis=-1, keepdims=True)
    in_grp = (lane_i < N_EXPERTS) & (grp_of_lane == g_idx)
    e1 = jnp.where(in_grp, logits, NEG)
    v1 = jnp.max(e1, axis=-1, keepdims=True)
    i1 = jnp.min(jnp.where(in_grp & (e1 == v1), lane, big), axis=-1, keepdims=True)
    rest = in_grp & (lane != i1)
    e2 = jnp.where(rest, logits, NEG)
    v2 = jnp.max(e2, axis=-1, keepdims=True)
    i2 = jnp.min(jnp.where(rest & (e2 == v2), lane, big), axis=-1, keepdims=True)
    r = jnp.exp(v2 - v1)
    w1 = g_w / (1.0 + r)
    w2 = g_w * r / (1.0 + r)
    grp_onehot = jnp.where(is_grp & (lane - ROUTER_GROUP_LANE == g_idx), 1.0, 0.0)
    return jnp.where(lane == i1, w1, 0.0) + jnp.where(lane == i2, w2, 0.0) + grp_onehot


HX_W = D_MODEL + LANES


def _outproj_kernel(mg_ref, w_ref, x_ref, gt_ref, g2_ref, sh_ref, sc_ref, wr_ref, br_ref, x1_ref, hx_ref):
    x1 = x_ref[...] + gt_ref[...] * _dot(mg_ref[...], w_ref[...])
    x1_ref[...] = x1
    h2 = _rms(x1) * g2_ref[...] * (1.0 + sc_ref[...]) + sh_ref[...]
    hx_ref[:, 0:D_MODEL] = h2
    hx_ref[:, D_MODEL:HX_W] = _route(_dot3(h2, wr_ref[...]) + br_ref[...])


def _outproj(merged, w_out, x, mod, g2, w_router, b_router, tm, tiles_per_batch):
    m, d = x.shape
    rb = mod.shape[2]
    const2 = lambda i: (0, 0)
    return pl.pallas_call(
        _outproj_kernel,
        grid=(m // tm,),
        in_specs=[pl.BlockSpec((tm, d), lambda i: (i, 0)),
                  pl.BlockSpec((d, d), const2),
                  pl.BlockSpec((tm, d), lambda i: (i, 0)),
                  _mod_spec(2, rb, d, tiles_per_batch),
                  pl.BlockSpec((1, d), const2),
                  _mod_spec(3, rb, d, tiles_per_batch),
                  _mod_spec(4, rb, d, tiles_per_batch),
                  pl.BlockSpec((d, LANES), const2),
                  pl.BlockSpec((1, LANES), const2)],
        out_specs=[pl.BlockSpec((tm, d), lambda i: (i, 0)),
                   pl.BlockSpec((tm, HX_W), lambda i: (i, 0))],
        out_shape=[jax.ShapeDtypeStruct((m, d), F32),
                   jax.ShapeDtypeStruct((m, HX_W), F32)],
        compiler_params=_cparams(("arbitrary",)),
        name="outproj",
    )(merged, w_out, x, mod, g2, mod, mod, w_router, b_router)


MOE_TILE = 512
PLAN_TILE = 512


def _moe_plan_kernel(r_ref, tri_ref, excl_ref, pos_ref, tg_ref, nu_ref, cnt_sc, off_sc, run_sc):
    ph = pl.program_id(0)
    n = pl.program_id(1)
    lane = lax.broadcasted_iota(jnp.int32, (1, LANES), 1)
    lane_t = lax.broadcasted_iota(jnp.int32, r_ref.shape, 1)
    is_grp = (lane_t >= ROUTER_GROUP_LANE) & (lane_t < ROUTER_GROUP_LANE + N_GROUPS)
    g4 = jnp.where(is_grp, r_ref[...], 0.0)

    @pl.when((ph == 0) & (n == 0))
    def _():
        cnt_sc[...] = jnp.zeros(cnt_sc.shape, F32)

    @pl.when(ph == 0)
    def _():
        cnt_sc[...] += jnp.sum(g4, axis=0, keepdims=True)

    @pl.when((ph == 1) & (n == 0))
    def _():
        padded = jnp.floor((cnt_sc[...] + (MOE_TILE - 1)) * (1.0 / MOE_TILE)) * MOE_TILE
        off = _dot_sel_rhs(jnp.broadcast_to(padded, (8, LANES)), excl_ref[...])[0:1]
        off_sc[...] = off
        run_sc[...] = jnp.zeros(run_sc.shape, F32)
        end = off + padded
        tile_start = lane.astype(F32) * MOE_TILE
        tg = jnp.zeros((1, LANES), F32)
        for g in range(N_GROUPS):
            end_g = jnp.sum(jnp.where(lane == ROUTER_GROUP_LANE + g, end, 0.0), axis=-1, keepdims=True)
            tg = tg + jnp.where(end_g <= tile_start, 1.0, 0.0)
        tg_ref[...] = jnp.minimum(tg, N_GROUPS - 1.0).astype(jnp.int32)
        total = jnp.sum(jnp.where(lane == ROUTER_GROUP_LANE + N_GROUPS - 1, end, 0.0), axis=-1, keepdims=True)
        nu_ref[...] = jnp.broadcast_to(total * (1.0 / MOE_TILE), (1, LANES)).astype(jnp.int32)

    @pl.when(ph == 1)
    def _():
        rank = _dot(tri_ref[...], g4.astype(BF16)) + run_sc[...]
        run_sc[...] += jnp.sum(g4, axis=0, keepdims=True)
        posv = g4 * (off_sc[...] + rank)
        hi, mid, lo = _split3(posv)
        ones = jnp.ones((8, LANES), BF16)
        row = _dot_nt(ones, hi) + _dot_nt(ones, mid) + _dot_nt(ones, lo)
        pos_ref[...] = row[0:1].astype(jnp.int32)


def _moe_plan(hx, ntiles):
    t = hx.shape[0]
    nblk = t // PLAN_TILE
    tri = jnp.asarray(np.tril(np.ones((PLAN_TILE, PLAN_TILE), np.float32), -1), BF16)
    excl = jnp.asarray(np.triu(np.ones((LANES, LANES), np.float32), 1), BF16)
    assert ntiles <= LANES
    pos, tg, nu = pl.pallas_call(
        _moe_plan_kernel,
        grid=(2, nblk),
        in_specs=[pl.BlockSpec((PLAN_TILE, LANES), lambda ph, n: (n, D_MODEL // LANES)),
                  pl.BlockSpec((PLAN_TILE, PLAN_TILE), lambda ph, n: (0, 0)),
                  pl.BlockSpec((LANES, LANES), lambda ph, n: (0, 0))],
        out_specs=[pl.BlockSpec((1, PLAN_TILE), lambda ph, n: (0, n * ph)),
                   pl.BlockSpec((1, LANES), lambda ph, n: (0, 0)),
                   pl.BlockSpec((1, LANES), lambda ph, n: (0, 0))],
        out_shape=[jax.ShapeDtypeStruct((1, t), jnp.int32),
                   jax.ShapeDtypeStruct((1, LANES), jnp.int32),
                   jax.ShapeDtypeStruct((1, LANES), jnp.int32)],
        scratch_shapes=[pltpu.VMEM((1, LANES), F32), pltpu.VMEM((1, LANES), F32), pltpu.VMEM((1, LANES), F32)],
        compiler_params=_cparams(("arbitrary", "arbitrary")),
        name="moe_plan",
    )(hx, tri, excl)
    return pos.reshape(t), tg.reshape(LANES), nu.reshape(LANES)[0:1]


def _row_copy(src_ref, src_row, dst_ref, dst_row, sem):
    return pltpu.make_async_copy(src_ref.at[pl.ds(src_row, 1)], dst_ref.at[pl.ds(dst_row, 1)], sem)


def _moe_grouped_kernel(pos_ref, tg_ref, nu_ref, hx_ref, wg_ref, wu_ref, wd_ref, ys_ref,
                        buf, sem, src_sc, xb_sc, *, t):
    i = pl.program_id(0)
    ei = pl.program_id(1)
    n_used = nu_ref[0]
    used = i < n_used
    slot = i % 2

    def fetch(tile, sl):
        def body(r, c):
            _row_copy(hx_ref, src_sc[tile * MOE_TILE + r], buf.at[sl], r, sem.at[sl]).start()
            return c
        lax.fori_loop(0, MOE_TILE, body, 0, unroll=8)

    @pl.when((i == 0) & (ei == 0))
    def _():
        def clear(r, c):
            src_sc[r] = 0
            return c

        def invert(tk, c):
            src_sc[pos_ref[tk]] = tk
            return c

        lax.fori_loop(0, src_sc.shape[0], clear, 0, unroll=8)
        lax.fori_loop(0, t, invert, 0, unroll=8)
        fetch(0, 0)

    @pl.when((ei == 0) & (i + 1 < n_used))
    def _():
        fetch(i + 1, (i + 1) % 2)

    @pl.when(used & (ei == 0))
    def _():
        def wait(r, c):
            _row_copy(hx_ref, 0, buf.at[slot], r, sem.at[slot]).wait()
            return c
        lax.fori_loop(0, MOE_TILE, wait, 0, unroll=8)
        xb_sc[...] = buf[slot, :, 0:D_MODEL].astype(BF16)

    @pl.when(ei == 0)
    def _():
        ys_ref[...] = jnp.zeros(ys_ref.shape, F32)

    @pl.when(used)
    def _():
        h = xb_sc[...]
        a = _dot(h, wg_ref[...].astype(BF16))
        u = _dot(h, wu_ref[...].astype(BF16))
        lane = lax.broadcasted_iota(jnp.int32, (MOE_TILE, LANES), 1)
        e = tg_ref[i] * EXP_PER_GROUP + ei
        cw = jnp.sum(jnp.where(lane == e, buf[slot, :, D_MODEL:HX_W], 0.0), axis=-1, keepdims=True)
        hid = (_silu(a) * u * cw).astype(BF16)
        ys_ref[...] += _dot(hid, wd_ref[...].astype(BF16))


def _moe_grouped(hx, pos, tg, nu, wg, wu, wd, rows_out):
    t = hx.shape[0]
    ne, d, de = wg.shape
    ntiles = rows_out // MOE_TILE

    def wmap(i, ei, pos_ref, tg_ref, nu_ref):
        e = jnp.where(i < nu_ref[0], tg_ref[i] * EXP_PER_GROUP + ei, ne - 1)
        return (e, 0, 0)

    grid_spec = pltpu.PrefetchScalarGridSpec(
        num_scalar_prefetch=3,
        grid=(ntiles, EXP_PER_GROUP),
        in_specs=[pl.BlockSpec(memory_space=pl.ANY),
                  pl.BlockSpec((None, d, de), wmap),
                  pl.BlockSpec((None, d, de), wmap),
                  pl.BlockSpec((None, de, d), wmap)],
        out_specs=pl.BlockSpec((MOE_TILE, d), lambda i, ei, pos_ref, tg_ref, nu_ref: (i, 0)),
        scratch_shapes=[pltpu.VMEM((2, MOE_TILE, HX_W), F32), pltpu.SemaphoreType.DMA((2,)),
                        pltpu.SMEM((rows_out,), jnp.int32),
                        pltpu.VMEM((MOE_TILE, d), BF16)],
    )
    return pl.pallas_call(
        functools.partial(_moe_grouped_kernel, t=t),
        grid_spec=grid_spec,
        out_shape=jax.ShapeDtypeStruct((rows_out, d), F32),
        compiler_params=_cparams(("arbitrary", "arbitrary")),
        name="moe_grouped",
    )(pos, tg, nu, hx, wg, wu, wd)


COMBINE_ROWS = 256


def _moe_combine_kernel(pos_ref, ys_ref, x1_ref, gt_ref, gf_ref, y_ref, buf, sem):
    i = pl.program_id(0)
    n = pl.num_programs(0)

    def fetch(tile, slot):
        def body(r, c):
            _row_copy(ys_ref, pos_ref[tile * COMBINE_ROWS + r], buf.at[slot], r, sem.at[slot]).start()
            return c
        lax.fori_loop(0, COMBINE_ROWS, body, 0, unroll=8)

    @pl.when(i == 0)
    def _():
        fetch(0, 0)

    @pl.when(i + 1 < n)
    def _():
        fetch(i + 1, (i + 1) % 2)

    slot = i % 2

    def wait(r, c):
        _row_copy(ys_ref, pos_ref[i * COMBINE_ROWS + r], buf.at[slot], r, sem.at[slot]).wait()
        return c

    lax.fori_loop(0, COMBINE_ROWS, wait, 0, unroll=8)
    x2 = x1_ref[...] + gt_ref[...] * buf[slot]
    y_ref[...] = _rms(x2) * gf_ref[...]


def _moe_combine(ys, pos, x1, mod, g_final, tiles_per_batch):
    m, d = x1.shape
    rb = mod.shape[2]
    grid_spec = pltpu.PrefetchScalarGridSpec(
        num_scalar_prefetch=1,
        grid=(m // COMBINE_ROWS,),
        in_specs=[pl.BlockSpec(memory_space=pl.ANY),
                  pl.BlockSpec((COMBINE_ROWS, d), lambda i, pos_ref: (i, 0)),
                  pl.BlockSpec((None, None, rb, d), lambda i, pos_ref: (5, i // tiles_per_batch, 0, 0)),
                  pl.BlockSpec((1, d), lambda i, pos_ref: (0, 0))],
        out_specs=pl.BlockSpec((COMBINE_ROWS, d), lambda i, pos_ref: (i, 0)),
        scratch_shapes=[pltpu.VMEM((2, COMBINE_ROWS, d), F32), pltpu.SemaphoreType.DMA((2,))],
    )
    return pl.pallas_call(
        _moe_combine_kernel,
        grid_spec=grid_spec,
        out_shape=jax.ShapeDtypeStruct((m, d), F32),
        compiler_params=_cparams(("arbitrary",)),
        name="moe_combine",
    )(pos, ys, x1, mod, g_final)


def _moe_kernel(h_ref, cmb_ref, wg_ref, wu_ref, wd_ref, x1_ref, gt_ref, gf_ref, y_ref, acc_ref):
    e = pl.program_id(1)

    @pl.when(e == 0)
    def _():
        acc_ref[...] = jnp.zeros(acc_ref.shape, F32)

    h = h_ref[...]
    a = _dot(h, wg_ref[...].astype(BF16))
    u = _dot(h, wu_ref[...].astype(BF16))
    lane = lax.broadcasted_iota(jnp.int32, cmb_ref.shape, 1)
    cw = jnp.sum(jnp.where(lane == e, cmb_ref[...], 0.0), axis=-1, keepdims=True)
    hid = (_silu(a) * u * cw).astype(BF16)
    acc_ref[...] += _dot(hid, wd_ref[...].astype(BF16))

    @pl.when(e == pl.num_programs(1) - 1)
    def _():
        x2 = x1_ref[...] + gt_ref[...] * acc_ref[...]
        y_ref[...] = _rms(x2) * gf_ref[...]


def _moe(h2, cmb, wg, wu, wd, x1, mod, g_final, tm, tiles_per_batch):
    m, d = x1.shape
    rb = mod.shape[2]
    ne, _, de = wg.shape
    return pl.pallas_call(
        _moe_kernel,
        grid=(m // tm, ne),
        in_specs=[pl.BlockSpec((tm, d), lambda i, e: (i, 0)),
                  pl.BlockSpec((tm, LANES), lambda i, e: (i, 0)),
                  pl.BlockSpec((None, d, de), lambda i, e: (e, 0, 0)),
                  pl.BlockSpec((None, d, de), lambda i, e: (e, 0, 0)),
                  pl.BlockSpec((None, de, d), lambda i, e: (e, 0, 0)),
                  pl.BlockSpec((tm, d), lambda i, e: (i, 0)),
                  pl.BlockSpec((None, None, rb, d), lambda i, e: (5, i // tiles_per_batch, 0, 0)),
                  pl.BlockSpec((1, d), lambda i, e: (0, 0))],
        out_specs=pl.BlockSpec((tm, d), lambda i, e: (i, 0)),
        out_shape=jax.ShapeDtypeStruct((m, d), F32),
        scratch_shapes=[pltpu.VMEM((tm, d), F32)],
        compiler_params=_cparams(("arbitrary", "arbitrary")),
        name="moe",
    )(h2, cmb, wg, wu, wd, x1, mod, g_final)


def _prep_weights(w_ada, b_ada, g_norm1, g_norm2, g_final, w_in, w_a2, b_a, b_f, g_gla_norm, w_up_a, w_up_b, w_out,
                  w_grp, b_grp, w_exp, b_exp, w_gate_e, w_up_e, w_down_e):
    wt = w_in.reshape(w_in.shape[1:]).T
    d = D_MODEL
    o_lra = 2 * GLA_KW + 2 * GLA_VW
    o_fox = o_lra + GLA_RANK
    o_fb = o_fox + 3 * FOX_W
    o_g = o_fb + FOX_HEADS
    pad = jnp.zeros((LANES - GLA_RANK - FOX_HEADS, d), F32)
    wall, mask = _gla_tables()
    wall_s, mask_s = _gla_tables(SAMPLE_LEVELS)
    bf_row =jnp.zeros((1, LANES), F32).at[0, FB_LANE:FB_LANE + FOX_HEADS].set(b_f[0])
    w_router = jnp.concatenate([w_exp[0], w_grp[0], jnp.zeros((d, LANES - N_EXPERTS - N_GROUPS), F32)], axis=1)
    b_router = jnp.concatenate([b_exp[0], b_grp[0], jnp.zeros((LANES - N_EXPERTS - N_GROUPS,), F32)])[None, :]
    return dict(
        w_ada=w_ada.reshape(w_ada.shape[1:]), b_ada=b_ada[0][None, :],
        g1=g_norm1[0][None, :], g2=g_norm2[0][None, :], gf=g_final[None, :],
        w_in_t=wt,
        w_small_t=jnp.concatenate([wt[o_lra:o_lra + GLA_RANK], wt[o_fb:o_fb + FOX_HEADS], pad], axis=0),
        wall=jnp.asarray(wall, BF16), mask=jnp.asarray(mask, F32),
        wall_s=jnp.asarray(wall_s, BF16), mask_s=jnp.asarray(mask_s, F32),
        w_a2=jnp.concatenate([w_a2[0], jnp.zeros((LANES - GLA_RANK, GLA_KW), F32)], axis=0), b_a=b_a[0][None, :], bf_row=bf_row, g_gla=g_gla_norm[0][None, :],
        w_up_a=w_up_a[0].astype(BF16), w_up_b=w_up_b[0].astype(BF16), w_out=w_out[0].astype(BF16),
        w_router=w_router, b_router=b_router,
        wg=w_gate_e.reshape(w_gate_e.shape[1:]), wu=w_up_e.reshape(w_up_e.shape[1:]),
        wd=w_down_e.reshape(w_down_e.shape[1:]),
    )


def _project(h, p, tm):
    tn = 1024
    wt = p["w_in_t"]
    n_gla = 2 * GLA_KW + 2 * GLA_VW
    c_q = n_gla + GLA_RANK
    c_g = c_q + 3 * FOX_W + FOX_HEADS
    (pg,) = _mm_wt(h, wt, 0, n_gla, tm, tn)
    (qb,) = _mm_wt(h, wt, c_q, FOX_W, tm, tn, (BF16,))
    kb, kb16 = _mm_heads(h, wt, c_q + FOX_W, min(tm, 512))
    vb, vb16 = _mm_heads(h, wt, c_q + 2 * FOX_W, min(tm, 512))
    (gates,) = _mm_wt(h, wt, c_g, 2 * D_MODEL, tm, tn, (BF16,))
    (small,) = _mm_wt(h, p["w_small_t"], 0, LANES, tm, LANES)
    return pg, qb, kb, kb16, vb, vb16, gates, small


def _tail(x, oa, ob, gates, mod, p, tm, tiles_per_batch, grouped):
    merged = _merge(oa, ob, p["w_up_a"], p["w_up_b"], gates, tm)
    tmo = min(tm, 256)
    x1, hx = _outproj(merged, p["w_out"], x, mod, p["g2"], p["w_router"], p["b_router"],
                      tmo, tiles_per_batch * (tm // tmo))
    if not grouped:
        tmm = min(tm, 512)
        return _moe(hx[:, 0:D_MODEL].astype(BF16), hx[:, D_MODEL:HX_W], p["wg"], p["wu"], p["wd"], x1, mod, p["gf"],
                    tmm, tiles_per_batch * (tm // tmm))
    rows_out = x.shape[0] + N_GROUPS * MOE_TILE
    pos, tg, nu = _moe_plan(hx, rows_out // MOE_TILE)
    ys = _moe_grouped(hx, pos, tg, nu, p["wg"], p["wu"], p["wd"], rows_out)
    return _moe_combine(ys, pos, x1, mod, p["gf"], tiles_per_batch * (tm // COMBINE_ROWS))


def kernel(x_prompt, x_sample, cache_k, cache_v, cache_logf, state_gla, page_table, c_prompt, c_sample, w_ada, b_ada,
           g_norm1, g_norm2, g_final, w_in, w_a2, b_a, b_f, g_gla_norm, w_up_a, w_up_b, w_out, w_grp, b_grp, w_exp,
           b_exp, w_gate_e, w_up_e, w_down_e):
    p = _prep_weights(w_ada, b_ada, g_norm1, g_norm2, g_final, w_in, w_a2, b_a, b_f, g_gla_norm, w_up_a, w_up_b,
                      w_out, w_grp, b_grp, w_exp, b_exp, w_gate_e, w_up_e, w_down_e)
    bp, seq, d = x_prompt.shape
    bs, t, _ = x_sample.shape
    assert t <= 2 ** SAMPLE_LEVELS and t <= ROWS8

    mod = _adaln(jnp.concatenate([c_prompt, c_sample], axis=0), p["w_ada"], p["b_ada"])
    mod_p = mod[:bp].reshape(bp, 6, 1, d).transpose(1, 0, 2, 3)
    mod_s = jnp.repeat(mod[bp:].reshape(bs, 6, d), t, axis=0).transpose(1, 0, 2)[:, None]

    tm = 1024
    tpb = seq // tm
    xp = x_prompt.reshape(bp * seq, d)
    hp = _normmod(xp, p["g1"], mod_p, tm, tpb)
    pg, qb, kb, kb16, vb, vb16, gates, small = _project(hp, p, tm)
    s0 = jnp.zeros((bp, GLA_HEADS, GLA_DK, GLA_DV), F32)
    oa, s_p = _gla_prompt(pg, small, s0, p["wall"], p["mask"], p["w_a2"], p["b_a"], p["g_gla"], bp, seq)
    lf_p, fq, fk = _fox_bias_prompt(small, p["bf_row"], bp, seq)
    ob = _fox_prompt(qb, kb16, vb16, fq, fk, bp, seq)
    y_p = _tail(xp, oa, ob, gates, mod_p, p, tm, tpb, grouped=True)

    rows = bs * t
    xs = x_sample.reshape(rows, d)
    hs = _normmod(xs, p["g1"], mod_s, rows, 1)
    pg_s, qs, ks, _, vs, _, gates_s, small_s = _project(hs, p, rows)
    oa_s, s_s = _gla_sample(pg_s.reshape(bs, t, -1), small_s.reshape(bs, t, LANES),
                            state_gla.reshape(state_gla.shape[1:]), p["wall_s"], p["mask_s"],
                            p["w_a2"], p["b_a"], p["g_gla"], bs, t)
    lf_s, fn_s = _fox_bias_sample(small_s, p["bf_row"], t)
    n_pool = cache_k.shape[1]
    ob_s = _fox_sample(page_table, qs.astype(F32).reshape(bs, t, FOX_W), ks.reshape(bs, t, FOX_HEADS, FOX_DH),
                       vs.reshape(bs, t, FOX_HEADS, FOX_DH), fn_s.reshape(bs, t, FOX_HEADS),
                       cache_k, cache_v, cache_logf.reshape(n_pool, PAGE, FOX_HEADS).transpose(0, 2, 1), t)
    y_s = _tail(xs, oa_s.reshape(rows, GLA_VW).astype(BF16), ob_s.reshape(rows, FOX_W).astype(BF16), gates_s, mod_s,
                p, rows, 1, grouped=False)

    return (y_p.reshape(bp, seq, d), y_s.reshape(bs, t, d),
            kb.reshape(1, bp, seq, FOX_HEADS, FOX_DH), vb.reshape(1, bp, seq, FOX_HEADS, FOX_DH),
            lf_p.reshape(1, bp, seq, FOX_HEADS), s_p[None],
            ks.reshape(1, bs, t, FOX_HEADS, FOX_DH), vs.reshape(1, bs, t, FOX_HEADS, FOX_DH),
            lf_s.reshape(1, bs, t, FOX_HEADS), s_s[None])
```

```python
import functools

import numpy as np
import jax
import jax.numpy as jnp
from jax import lax
from jax.experimental import pallas as pl
from jax.experimental.pallas import tpu as pltpu

F32 = jnp.float32
BF16 = jnp.bfloat16

D_MODEL = 2048
GLA_HEADS = 4
GLA_DK = 128
GLA_DV = 256
GLA_RANK = 16
GLA_TAU = 16.0
FOX_HEADS = 8
FOX_DH = 128
PAGE = 128
N_GROUPS = 4
EXP_PER_GROUP = 4
N_EXPERTS = 16
D_EXPERT = 512
RMS_EPS = 1e-6
GLA_KW = GLA_HEADS * GLA_DK
GLA_VW = GLA_HEADS * GLA_DV
FOX_W = FOX_HEADS * FOX_DH
GLA_CHUNK = 128
GLA_LEVELS = 7
GLA_MM_LEVELS = 4
SAMPLE_LEVELS = 2
LANES = 128
NEG = -1e30
VMEM_LIMIT = 56 * 1024 * 1024


def _cparams(sem):
    return pltpu.CompilerParams(dimension_semantics=sem, vmem_limit_bytes=VMEM_LIMIT)


def _dot(a, b):
    return jnp.dot(a, b, preferred_element_type=F32)


def _dot_nt(a, b):
    return lax.dot_general(a, b, (((1,), (1,)), ((), ())), preferred_element_type=F32)


def _dot_tn(a, b):
    return lax.dot_general(a, b, (((0,), (0,)), ((), ())), preferred_element_type=F32)


def _split2(x):
    hi = x.astype(BF16)
    lo = (x - hi.astype(F32)).astype(BF16)
    return hi, lo


def _split3(x):
    hi = x.astype(BF16)
    r = x - hi.astype(F32)
    mid = r.astype(BF16)
    lo = (r - mid.astype(F32)).astype(BF16)
    return hi, mid, lo


def _dot3(a, b):
    ah, al = _split2(a)
    bh, bl = _split2(b)
    return _dot(ah, bh) + _dot(ah, bl) + _dot(al, bh)


def _dot_sel(w01, x):
    hi, mid, lo = _split3(x)
    return _dot(w01, hi) + _dot(w01, mid) + _dot(w01, lo)


def _log_sigmoid(x):
    return jnp.minimum(x, 0.0) - jnp.log1p(jnp.exp(-jnp.abs(x)))


def _silu(x):
    return x * jax.nn.sigmoid(x)


def _adaln_kernel(c_ref, w_ref, b_ref, o_ref):
    o_ref[...] = _dot3(_silu(c_ref[...]), w_ref[...]) + b_ref[...]


def _adaln(c, w, b, tn=512):
    nb, d = c.shape
    n = w.shape[1]
    return pl.pallas_call(
        _adaln_kernel,
        grid=(n // tn,),
        in_specs=[pl.BlockSpec((nb, d), lambda j: (0, 0)),
                  pl.BlockSpec((d, tn), lambda j: (0, j)),
                  pl.BlockSpec((1, tn), lambda j: (0, j))],
        out_specs=pl.BlockSpec((nb, tn), lambda j: (0, j)),
        out_shape=jax.ShapeDtypeStruct((nb, n), F32),
        compiler_params=_cparams(("arbitrary",)),
        name="adaln",
    )(c, w, b)


def _rms(x):
    return x * lax.rsqrt(jnp.mean(x * x, axis=-1, keepdims=True) + RMS_EPS)


def _normmod_kernel(x_ref, g_ref, sh_ref, sc_ref, o_ref):
    y = _rms(x_ref[...]) * g_ref[...]
    o_ref[...] = (y * (1.0 + sc_ref[...]) + sh_ref[...]).astype(o_ref.dtype)


def _mod_spec(k, rb, d, tiles_per_batch):
    return pl.BlockSpec((None, None, rb, d), lambda i: (k, i // tiles_per_batch, 0, 0))


def _normmod(x, g, mod, tm, tiles_per_batch):
    m, d = x.shape
    rb = mod.shape[2]
    return pl.pallas_call(
        _normmod_kernel,
        grid=(m // tm,),
        in_specs=[pl.BlockSpec((tm, d), lambda i: (i, 0)),
                  pl.BlockSpec((1, d), lambda i: (0, 0)),
                  _mod_spec(0, rb, d, tiles_per_batch),
                  _mod_spec(1, rb, d, tiles_per_batch)],
        out_specs=pl.BlockSpec((tm, d), lambda i: (i, 0)),
        out_shape=jax.ShapeDtypeStruct((m, d), BF16),
        compiler_params=_cparams(("arbitrary",)),
        name="normmod",
    )(x, g, mod, mod)


def _wt_spec(k, row0, tn):
    assert row0 % 8 == 0 and tn % 8 == 0
    return pl.BlockSpec((pl.Element(tn), pl.Element(k)), lambda j, i: (pl.multiple_of(row0 + j * tn, 8), 0))


def _mm_wt_kernel(x_ref, wt_ref, *refs):
    o_refs, wb = refs[:-1], refs[-1]

    @pl.when(pl.program_id(1) == 0)
    def _():
        wb[...] = wt_ref[...].astype(BF16)

    r = _dot_nt(x_ref[...], wb[...])
    for o_ref in o_refs:
        o_ref[...] = r.astype(o_ref.dtype)


def _mm_wt(x, wt, row0, n, tm, tn, out_dtypes=(F32,)):
    m, k = x.shape
    return pl.pallas_call(
        _mm_wt_kernel,
        grid=(n // tn, m // tm),
        in_specs=[pl.BlockSpec((tm, k), lambda j, i: (i, 0)), _wt_spec(k, row0, tn)],
        out_specs=[pl.BlockSpec((tm, tn), lambda j, i: (i, j)) for _ in out_dtypes],
        out_shape=[jax.ShapeDtypeStruct((m, n), dt) for dt in out_dtypes],
        scratch_shapes=[pltpu.VMEM((tn, k), BF16)],
        compiler_params=_cparams(("arbitrary", "arbitrary")),
        name="proj_mm",
    )(x, wt)


def _mm_heads_kernel(x_ref, wt_ref, o_ref, o16_ref, wb):
    @pl.when(pl.program_id(1) == 0)
    def _():
        wb[...] = wt_ref[...].astype(BF16)

    r = _dot_nt(x_ref[...], wb[...])
    for h in range(FOX_HEADS):
        o_ref[:, h, :] = r[:, h * FOX_DH:(h + 1) * FOX_DH]
    o16_ref[...] = r.astype(BF16)


def _mm_heads(x, wt, row0, tm):
    m, k = x.shape
    return pl.pallas_call(
        _mm_heads_kernel,
        grid=(1, m // tm),
        in_specs=[pl.BlockSpec((tm, k), lambda j, i: (i, 0)), _wt_spec(k, row0, FOX_W)],
        out_specs=[pl.BlockSpec((tm, FOX_HEADS, FOX_DH), lambda j, i: (i, 0, 0)),
                   pl.BlockSpec((tm, FOX_W), lambda j, i: (i, 0))],
        out_shape=[jax.ShapeDtypeStruct((m, FOX_HEADS, FOX_DH), F32),
                   jax.ShapeDtypeStruct((m, FOX_W), BF16)],
        scratch_shapes=[pltpu.VMEM((FOX_W, k), BF16)],
        compiler_params=_cparams(("arbitrary", "arbitrary")),
        name="proj_heads",
    )(x, wt)


def _gla_tables(p=GLA_LEVELS):
    c = GLA_CHUNK
    nmm = min(p, GLA_MM_LEVELS)
    t = np.arange(c)[:, None]
    m = np.arange(c)[None, :]
    wall = np.zeros((nmm + 1, c, c), np.float32)
    mask = np.zeros((p + 1, c, c), np.float32)
    for l in range(p):
        half = 1 << l
        pos = t % (2 * half)
        mid = t - pos + half
        right = pos >= half
        if l < nmm:
            wall[l] = np.where(right, (m >= mid) & (m <= t), (m > t) & (m < mid))
        s = m
        mask[l] = ((t >> (l + 1)) == (s >> (l + 1))) & (((t >> l) & 1) == 1) & (((s >> l) & 1) == 0)
    wall[nmm] = m <= t
    mask[p] = t == m
    return wall.reshape((nmm + 1) * c, c), mask


def _gla_level_exponent(cum, row, l):
    half = 1 << l
    c = cum.shape[0]
    bound = None
    for start in range(0, c, 2 * half):
        b = cum[start + half - 1:start + half, :]
        bound = b if bound is None else jnp.where(row >= start, b, bound)
    return jnp.where(((row >> l) & 1) == 1, cum - bound, bound - cum)


def _gla_body(q_ref, k_ref, v_ref, ra_ref, sm_ref, wall_ref, mask_ref, wa2_ref, ba_ref, g_ref,
              og_ref, st_ref, *, nchunk, valid, store_rows):
    c, p = GLA_CHUNK, mask_ref.shape[0] - 1
    nmm = min(p, GLA_MM_LEVELS)
    row = lax.broadcasted_iota(jnp.int32, (c, GLA_DK), 0)
    rowh = lax.broadcasted_iota(jnp.int32, (c, GLA_KW), 0)
    wall = wall_ref[...]
    for ci in range(nchunk):
        rows = pl.ds(ci * c, c)
        x = _dot3(sm_ref[rows, :], wa2_ref[...]) + ba_ref[...]
        la = _log_sigmoid(x) * (1.0 / GLA_TAU)
        if valid < c:
            la = jnp.where(rowh < valid, la, 0.0)
        args = _dot_sel(wall, la)
        cum = args[nmm * c:(nmm + 1) * c]
        e_lvl = [jnp.exp(args[l * c:(l + 1) * c]) for l in range(nmm)]
        e_lvl += [jnp.exp(_gla_level_exponent(cum, rowh, l)) for l in range(nmm, p)]
        last = cum[c - 1:c, :]
        e_cum = jnp.exp(cum)
        e_rev = jnp.exp(last - cum)
        e_last = jnp.exp(last)
        for h in range(GLA_HEADS):
            ks = slice(h * GLA_DK, (h + 1) * GLA_DK)
            vs = slice(h * GLA_DV, (h + 1) * GLA_DV)
            q = q_ref[rows, ks] * (GLA_DK ** -0.5)
            k = k_ref[rows, ks]
            vb = v_ref[rows, vs].astype(BF16)
            a = mask_ref[p] * _dot_nt(q.astype(BF16), k.astype(BF16))
            for l in range(p):
                xl = (jnp.where(((row >> l) & 1) == 1, q, k) * e_lvl[l][:, ks]).astype(BF16)
                a = a + mask_ref[l] * _dot_nt(xl, xl)
            st = st_ref[h]
            qc = (q * e_cum[:, ks]).astype(BF16)
            o = _dot(a.astype(BF16), vb) + _dot_nt(qc, st.astype(BF16))
            kr = (k * e_rev[:, ks]).astype(BF16)
            st_ref[h] = e_last[:, ks] * st + _dot_tn(vb, kr)
            og = _rms(o) * g_ref[...] * _silu(ra_ref[rows, vs])
            if store_rows < c:
                og_ref[:, vs] = og[:store_rows].astype(og_ref.dtype)
            else:
                og_ref[rows, vs] = og.astype(og_ref.dtype)


def _gla_prompt_kernel(q_ref, k_ref, v_ref, ra_ref, sm_ref, wall_ref, mask_ref, wa2_ref, ba_ref, g_ref, s0_ref,
                       og_ref, s_ref, st_ref, *, nchunk):
    n = pl.program_id(1)

    @pl.when(n == 0)
    def _():
        for h in range(GLA_HEADS):
            st_ref[h] = s0_ref[h].T

    _gla_body(q_ref, k_ref, v_ref, ra_ref, sm_ref, wall_ref, mask_ref, wa2_ref, ba_ref, g_ref, og_ref, st_ref,
              nchunk=nchunk, valid=GLA_CHUNK, store_rows=GLA_CHUNK)

    @pl.when(n == pl.num_programs(1) - 1)
    def _():
        for h in range(GLA_HEADS):
            s_ref[h] = st_ref[h].T


def _gla_prompt(pg, small, s0, wall, mask, wa2, ba, g, batch, seq, tb=256):
    nblk = seq // tb
    rowmap = lambda cb: (lambda b, n: (b * nblk + n, cb))
    const2 = lambda b, n: (0, 0)
    return pl.pallas_call(
        functools.partial(_gla_prompt_kernel, nchunk=tb // GLA_CHUNK),
        grid=(batch, nblk),
        in_specs=[pl.BlockSpec((tb, GLA_KW), rowmap(0)),
                  pl.BlockSpec((tb, GLA_KW), rowmap(1)),
                  pl.BlockSpec((tb, GLA_VW), rowmap(1)),
                  pl.BlockSpec((tb, GLA_VW), rowmap(2)),
                  pl.BlockSpec((tb, LANES), rowmap(0)),
                  pl.BlockSpec(wall.shape, const2),
                  pl.BlockSpec(mask.shape, lambda b, n: (0, 0, 0)),
                  pl.BlockSpec(wa2.shape, const2),
                  pl.BlockSpec(ba.shape, const2),
                  pl.BlockSpec(g.shape, const2),
                  pl.BlockSpec((None, GLA_HEADS, GLA_DK, GLA_DV), lambda b, n: (b, 0, 0, 0))],
        out_specs=[pl.BlockSpec((tb, GLA_VW), rowmap(0)),
                   pl.BlockSpec((None, GLA_HEADS, GLA_DK, GLA_DV), lambda b, n: (b, 0, 0, 0))],
        out_shape=[jax.ShapeDtypeStruct((batch * seq, GLA_VW), BF16),
                   jax.ShapeDtypeStruct((batch, GLA_HEADS, GLA_DK, GLA_DV), F32)],
        scratch_shapes=[pltpu.VMEM((GLA_HEADS, GLA_DV, GLA_DK), F32)],
        compiler_params=_cparams(("arbitrary", "arbitrary")),
        name="gla_prompt",
    )(pg, pg, pg, pg, small, wall, mask, wa2, ba, g, s0)


def _gla_sample_kernel(pg_ref, sm_ref, wall_ref, mask_ref, wa2_ref, ba_ref, g_ref, s0_ref,
                       og_ref, s_ref, pad_ref, smpad_ref, st_ref, *, t):
    @pl.when(pl.program_id(0) == 0)
    def _():
        pad_ref[...] = jnp.zeros(pad_ref.shape, F32)
        smpad_ref[...] = jnp.zeros(smpad_ref.shape, F32)

    pad_ref[0:t, :] = pg_ref[...]
    smpad_ref[0:t, :] = sm_ref[...]
    for h in range(GLA_HEADS):
        st_ref[h] = s0_ref[h].T
    q_ref = pad_ref.at[:, 0:GLA_KW]
    k_ref = pad_ref.at[:, GLA_KW:2 * GLA_KW]
    v_ref = pad_ref.at[:, 2 * GLA_KW:2 * GLA_KW + GLA_VW]
    ra_ref = pad_ref.at[:, 2 * GLA_KW + GLA_VW:2 * GLA_KW + 2 * GLA_VW]
    _gla_body(q_ref, k_ref, v_ref, ra_ref, smpad_ref, wall_ref, mask_ref, wa2_ref, ba_ref, g_ref, og_ref, st_ref,
              nchunk=1, valid=t, store_rows=t)
    for h in range(GLA_HEADS):
        s_ref[h] = st_ref[h].T


def _gla_sample(pg, small, s0, wall, mask, wa2, ba, g, batch, t):
    width = pg.shape[-1]
    const2 = lambda b: (0, 0)
    return pl.pallas_call(
        functools.partial(_gla_sample_kernel, t=t),
        grid=(batch,),
        in_specs=[pl.BlockSpec((None, t, width), lambda b: (b, 0, 0)),
                  pl.BlockSpec((None, t, LANES), lambda b: (b, 0, 0)),
                  pl.BlockSpec(wall.shape, const2),
                  pl.BlockSpec(mask.shape, lambda b: (0, 0, 0)),
                  pl.BlockSpec(wa2.shape, const2),
                  pl.BlockSpec(ba.shape, const2),
                  pl.BlockSpec(g.shape, const2),
                  pl.BlockSpec((None, GLA_HEADS, GLA_DK, GLA_DV), lambda b: (b, 0, 0, 0))],
        out_specs=[pl.BlockSpec((None, t, GLA_VW), lambda b: (b, 0, 0)),
                   pl.BlockSpec((None, GLA_HEADS, GLA_DK, GLA_DV), lambda b: (b, 0, 0, 0))],
        out_shape=[jax.ShapeDtypeStruct((batch, t, GLA_VW), F32),
                   jax.ShapeDtypeStruct((batch, GLA_HEADS, GLA_DK, GLA_DV), F32)],
        scratch_shapes=[pltpu.VMEM((GLA_CHUNK, width), F32),
                        pltpu.VMEM((GLA_CHUNK, LANES), F32),
                        pltpu.VMEM((GLA_HEADS, GLA_DV, GLA_DK), F32)],
        compiler_params=_cparams(("arbitrary",)),
        name="gla_sample",
    )(pg, small, wall, mask, wa2, ba, g, s0)


FB_LANE = GLA_RANK


LOG2E = 1.4426950408889634


def _fox_bias_prompt_kernel(sm_ref, bf_ref, tri_ref, lf_ref, fq_ref, fk_ref, carry_ref):
    @pl.when(pl.program_id(1) == 0)
    def _():
        carry_ref[...] = jnp.zeros(carry_ref.shape, F32)

    lf = _log_sigmoid(sm_ref[...] + bf_ref[...])
    lf_ref[...] = lf[:, FB_LANE:FB_LANE + FOX_HEADS]
    cum = _dot_sel(tri_ref[...], lf) + carry_ref[...]
    carry_ref[...] = cum[cum.shape[0] - 1:, :]
    lane = lax.broadcasted_iota(jnp.int32, cum.shape, 1)
    ones = jnp.where(lane < 6, 1.0, 0.0)
    for h in range(FOX_HEADS):
        f = jnp.broadcast_to(cum[:, FB_LANE + h:FB_LANE + h + 1] * LOG2E, cum.shape)
        hi = f.astype(BF16).astype(F32)
        r = f - hi
        mid = r.astype(BF16).astype(F32)
        lo = r - mid
        fq = jnp.where(lane == 0, hi, jnp.where(lane == 1, mid, jnp.where(lane == 2, lo, ones)))
        fk = jnp.where(lane == 3, -hi, jnp.where(lane == 4, -mid, jnp.where(lane == 5, -lo, ones)))
        fq_ref[:, h * LANES:(h + 1) * LANES] = fq.astype(BF16)
        fk_ref[:, h * LANES:(h + 1) * LANES] = fk.astype(BF16)


def _fox_bias_prompt(small, bf_row, batch, seq, tb=256):
    nblk = seq // tb
    tri = jnp.asarray(np.tril(np.ones((tb, tb), np.float32)), BF16)
    wide = FOX_HEADS * LANES
    return pl.pallas_call(
        _fox_bias_prompt_kernel,
        grid=(batch, nblk),
        in_specs=[pl.BlockSpec((tb, LANES), lambda b, n: (b * nblk + n, 0)),
                  pl.BlockSpec((1, LANES), lambda b, n: (0, 0)),
                  pl.BlockSpec((tb, tb), lambda b, n: (0, 0))],
        out_specs=[pl.BlockSpec((tb, FOX_HEADS), lambda b, n: (b * nblk + n, 0)),
                   pl.BlockSpec((tb, wide), lambda b, n: (b * nblk + n, 0)),
                   pl.BlockSpec((tb, wide), lambda b, n: (b * nblk + n, 0))],
        out_shape=[jax.ShapeDtypeStruct((batch * seq, FOX_HEADS), F32),
                   jax.ShapeDtypeStruct((batch * seq, wide), BF16),
                   jax.ShapeDtypeStruct((batch * seq, wide), BF16)],
        scratch_shapes=[pltpu.VMEM((1, LANES), F32)],
        compiler_params=_cparams(("arbitrary", "arbitrary")),
        name="fox_bias_prompt",
    )(small, bf_row, tri)


def _fox_bias_sample_kernel(sm_ref, bf_ref, sel_ref, lf_ref, fn_ref):
    lf = _log_sigmoid(sm_ref[...] + bf_ref[...])
    lf_ref[...] = lf[:, FB_LANE:FB_LANE + FOX_HEADS]
    cum = _dot_sel(sel_ref[...], lf)
    fn_ref[...] = cum[:, FB_LANE:FB_LANE + FOX_HEADS]


def _fox_bias_sample(small, bf_row, t):
    rows = small.shape[0]
    r = np.arange(rows)
    sel = ((r[:, None] // t) == (r[None, :] // t)) & (r[None, :] <= r[:, None])
    sel = jnp.asarray(sel.astype(np.float32), BF16)
    full = lambda shape: pl.BlockSpec(shape, lambda i: tuple(0 for _ in shape))
    return pl.pallas_call(
        _fox_bias_sample_kernel,
        grid=(1,),
        in_specs=[full((rows, LANES)), full((1, LANES)), full((rows, rows))],
        out_specs=[full((rows, FOX_HEADS)), full((rows, FOX_HEADS))],
        out_shape=[jax.ShapeDtypeStruct((rows, FOX_HEADS), F32),
                   jax.ShapeDtypeStruct((rows, FOX_HEADS), F32)],
        compiler_params=_cparams(("arbitrary",)),
        name="fox_bias_sample",
    )(small, bf_row, sel)


FOX_HEADS_PER_STEP = 8


def _fox_prompt_kernel(q_ref, k_ref, v_ref, fq_ref, fk_ref, o_ref, *, tq, tk):
    hb = FOX_HEADS_PER_STEP
    i = pl.program_id(2)
    rowi = lax.broadcasted_iota(jnp.int32, (tq, tk), 0)
    coli = lax.broadcasted_iota(jnp.int32, (tq, tk), 1)
    qs = []
    for hh in range(hb):
        hs = slice(hh * FOX_DH, (hh + 1) * FOX_DH)
        q = (q_ref[:, hs].astype(F32) * (FOX_DH ** -0.5 * LOG2E)).astype(BF16)
        qs.append(jnp.concatenate([q, fq_ref[:, hs]], axis=1))

    def step(j, carry, masked):
        ks = pl.ds(pl.multiple_of(j * tk, tk), tk)
        out = []
        for hh in range(hb):
            hs = slice(hh * FOX_DH, (hh + 1) * FOX_DH)
            m, l, acc = carry[hh]
            s = _dot_nt(qs[hh], jnp.concatenate([k_ref[ks, hs], fk_ref[ks, hs]], axis=1))
            if masked:
                s = jnp.where(rowi + i * tq >= coli + j * tk, s, NEG)
            m_new = jnp.maximum(m, jnp.max(s, axis=-1, keepdims=True))
            p = jnp.exp2(s - m_new)
            alpha = jnp.exp2(m - m_new)
            l = alpha * l + jnp.sum(p, axis=-1, keepdims=True)
            acc = alpha * acc + _dot(p.astype(BF16), v_ref[ks, hs])
            out.append((m_new, l, acc))
        return tuple(out)

    init = tuple((jnp.full((tq, 1), NEG, F32), jnp.zeros((tq, 1), F32), jnp.zeros((tq, FOX_DH), F32))
                 for _ in range(hb))
    nfull = (i * tq) // tk
    carry = lax.fori_loop(0, nfull, lambda j, c: step(j, c, False), init)
    for d in range(pl.cdiv(tq, tk)):
        carry = step(nfull + d, carry, True)
    for hh in range(hb):
        m, l, acc = carry[hh]
        o_ref[:, hh * FOX_DH:(hh + 1) * FOX_DH] = (acc / l).astype(o_ref.dtype)


def _fox_prompt(qb, kb, vb, fq, fk, batch, seq, tq=512, tk=256):
    nq = seq // tq
    hb = FOX_HEADS_PER_STEP
    wb = hb * FOX_DH
    return pl.pallas_call(
        functools.partial(_fox_prompt_kernel, tq=tq, tk=tk),
        grid=(batch, FOX_HEADS // hb, nq),
        in_specs=[pl.BlockSpec((tq, wb), lambda b, h, i: (b * nq + i, h)),
                  pl.BlockSpec((seq, wb), lambda b, h, i: (b, h)),
                  pl.BlockSpec((seq, wb), lambda b, h, i: (b, h)),
                  pl.BlockSpec((tq, wb), lambda b, h, i: (b * nq + i, h)),
                  pl.BlockSpec((seq, wb), lambda b, h, i: (b, h))],
        out_specs=pl.BlockSpec((tq, wb), lambda b, h, i: (b * nq + i, h)),
        out_shape=jax.ShapeDtypeStruct((batch * seq, FOX_W), BF16),
        compiler_params=_cparams(("arbitrary", "arbitrary", "arbitrary")),
        name="fox_prompt",
    )(qb, kb, vb, fq, fk)


PAGES_PER_STEP = 16
ROWS8 = 8
QROWS = FOX_HEADS * ROWS8
PAGE_FLAT = PAGE * FOX_HEADS


def _dot_sel_rhs(x, w01):
    hi, mid, lo = _split3(x)
    m = x.shape[0]
    stacked = jnp.concatenate([hi.astype(F32), mid.astype(F32), lo.astype(F32)], axis=0).astype(BF16)
    r = _dot(stacked, w01)
    return r[0:m] + r[m:2 * m] + r[2 * m:3 * m]


def _fox_sample_kernel(pt_ref, q_ref, kn_ref, vn_ref, fn_ref, fnrow_ref, madd_ref, maddn_ref, usuf_ref, pfx_ref,
                       hsum_ref, pexp_ref, hmask_ref, *refs, t):
    g = PAGES_PER_STEP
    k_refs, v_refs, lf_refs = refs[0:g], refs[g:2 * g], refs[2 * g:3 * g]
    o_ref = refs[3 * g]
    q_sc, m_sc, l_sc, acc_sc, carry_sc = refs[3 * g + 1:]
    j = pl.program_id(1)
    fn_t = fn_ref[...][:, 0:1]

    @pl.when(j == 0)
    def _():
        q_sc[...] = jnp.zeros(q_sc.shape, F32)
        for h in range(FOX_HEADS):
            q_sc[h * ROWS8:h * ROWS8 + t, :] = q_ref[:, h * FOX_DH:(h + 1) * FOX_DH] * (FOX_DH ** -0.5)
        carry_sc[...] = jnp.zeros(carry_sc.shape, F32)
        pad = jnp.zeros((LANES - t * FOX_HEADS, FOX_DH), F32)
        kn = jnp.concatenate([kn_ref[...].reshape(t * FOX_HEADS, FOX_DH), pad], axis=0)
        vn = jnp.concatenate([vn_ref[...].reshape(t * FOX_HEADS, FOX_DH), pad], axis=0)
        s = _dot_nt(q_sc[...], kn) + fn_t - fnrow_ref[...] + maddn_ref[...]
        m = jnp.max(s, axis=-1, keepdims=True)
        p = jnp.exp(s - m)
        m_sc[...] = jnp.broadcast_to(m, m_sc.shape)
        l_sc[...] = jnp.broadcast_to(jnp.sum(p, axis=-1, keepdims=True), l_sc.shape)
        acc_sc[...] = _dot(p, vn)

    lf = jnp.concatenate([lf_refs[gi][...] for gi in range(g)], axis=0)
    tot = jnp.broadcast_to(jnp.sum(lf, axis=-1, keepdims=True), lf.shape)
    carry = carry_sc[...]
    r_hk = (_dot_sel_rhs(lf, usuf_ref[...]) + _dot_sel(pfx_ref[...], tot)
            + jnp.concatenate([carry] * g, axis=0))
    carry_sc[...] = carry + _dot_sel(hsum_ref[...], tot)
    spread = _dot_sel_rhs(r_hk, pexp_ref[...]) * hmask_ref[...]
    r_flat = jnp.sum(spread.reshape(g, FOX_HEADS, PAGE_FLAT), axis=1)
    q = q_sc[...]
    madd = madd_ref[...] + fn_t
    m_old = m_sc[...]
    m_new = m_old
    s_list = []
    for gi in range(g):
        bias = madd + r_flat[gi:gi + 1, :]
        s = _dot_nt(q, k_refs[gi][...].reshape(PAGE_FLAT, FOX_DH)) + bias
        s_list.append(s)
        m_new = jnp.maximum(m_new, jnp.max(s, axis=-1, keepdims=True))
    alpha = jnp.exp(m_old - m_new)
    l = alpha * l_sc[...]
    acc = alpha * acc_sc[...]
    m_col = m_new[:, 0:1]
    for gi in range(g):
        p = jnp.exp(s_list[gi] - m_col)
        l = l + jnp.sum(p, axis=-1, keepdims=True)
        acc = acc + _dot(p, v_refs[gi][...].reshape(PAGE_FLAT, FOX_DH))
    m_sc[...] = m_new
    l_sc[...] = l
    acc_sc[...] = acc

    @pl.when(j == pl.num_programs(1) - 1)
    def _():
        o = acc / l
        for h in range(FOX_HEADS):
            o_ref[:, h * FOX_DH:(h + 1) * FOX_DH] = o[h * ROWS8:h * ROWS8 + t, :].astype(o_ref.dtype)


def _fox_sample_tables(t):
    row = np.arange(QROWS)[:, None]
    col = np.arange(PAGE_FLAT)[None, :]
    madd = np.where((row // ROWS8) == (col % FOX_HEADS), 0.0, NEG).astype(np.float32)
    coln = np.arange(LANES)[None, :]
    ok = (coln < t * FOX_HEADS) & ((row // ROWS8) == (coln % FOX_HEADS)) & ((coln // FOX_HEADS) <= (row % ROWS8))
    maddn = np.where(ok, 0.0, NEG).astype(np.float32)
    key = np.arange(PAGE)
    usuf = (key[:, None] > key[None, :]).astype(np.float32)
    gh = np.arange(PAGES_PER_STEP * FOX_HEADS)
    same_head = (gh[:, None] % FOX_HEADS) == (gh[None, :] % FOX_HEADS)
    pfx = (same_head & ((gh[None, :] // FOX_HEADS) < (gh[:, None] // FOX_HEADS))).astype(np.float32)
    hsum = (np.arange(FOX_HEADS)[:, None] == (gh[None, :] % FOX_HEADS)).astype(np.float32)
    pexp = (key[:, None] == (col // FOX_HEADS)).astype(np.float32)
    hmask = ((gh[:, None] % FOX_HEADS) == (col % FOX_HEADS)).astype(np.float32)
    return (jnp.asarray(madd), jnp.asarray(maddn), jnp.asarray(usuf, BF16), jnp.asarray(pfx, BF16),
            jnp.asarray(hsum, BF16), jnp.asarray(pexp, BF16), jnp.asarray(hmask))


def _fox_sample(page_table, q, kn, vn, fn, cache_k, cache_v, cache_lf, t):
    batch, npages = page_table.shape
    g = PAGES_PER_STEP
    nsteps = npages // g
    tables = _fox_sample_tables(t)
    fn_rows = jnp.pad(fn.transpose(0, 2, 1), ((0, 0), (0, 0), (0, ROWS8 - t))).reshape(batch, QROWS, 1)
    fn_rows = jnp.broadcast_to(fn_rows, (batch, QROWS, LANES))
    fn_cols = jnp.pad(fn.reshape(batch, 1, t * FOX_HEADS), ((0, 0), (0, 0), (0, LANES - t * FOX_HEADS)))

    def page_map5(gi):
        return lambda b, j, pt: (0, pt[b, npages - 1 - (j * g + gi)], 0, 0, 0)

    def page_map3(gi):
        return lambda b, j, pt: (pt[b, npages - 1 - (j * g + gi)], 0, 0)

    seq3 = lambda b, j, pt: (b, 0, 0)
    seq4 = lambda b, j, pt: (b, 0, 0, 0)
    const2 = lambda b, j, pt: (0, 0)
    in_specs = [pl.BlockSpec((None, t, FOX_W), seq3),
                pl.BlockSpec((None, t, FOX_HEADS, FOX_DH), seq4),
                pl.BlockSpec((None, t, FOX_HEADS, FOX_DH), seq4),
                pl.BlockSpec((None, QROWS, LANES), seq3),
                pl.BlockSpec((None, 1, LANES), seq3)]
    in_specs += [pl.BlockSpec(tb.shape, const2) for tb in tables]
    in_specs += [pl.BlockSpec((None, None, PAGE, FOX_HEADS, FOX_DH), page_map5(gi)) for gi in range(g)]
    in_specs += [pl.BlockSpec((None, None, PAGE, FOX_HEADS, FOX_DH), page_map5(gi)) for gi in range(g)]
    in_specs += [pl.BlockSpec((None, FOX_HEADS, PAGE), page_map3(gi)) for gi in range(g)]
    grid_spec = pltpu.PrefetchScalarGridSpec(
        num_scalar_prefetch=1,
        grid=(batch, nsteps),
        in_specs=in_specs,
        out_specs=pl.BlockSpec((None, t, FOX_W), seq3),
        scratch_shapes=[pltpu.VMEM((QROWS, FOX_DH), F32),
                        pltpu.VMEM((QROWS, LANES), F32),
                        pltpu.VMEM((QROWS, LANES), F32),
                        pltpu.VMEM((QROWS, FOX_DH), F32),
                        pltpu.VMEM((FOX_HEADS, PAGE), F32)],
    )
    return pl.pallas_call(
        functools.partial(_fox_sample_kernel, t=t),
        grid_spec=grid_spec,
        out_shape=jax.ShapeDtypeStruct((batch, t, FOX_W), F32),
        compiler_params=_cparams(("arbitrary", "arbitrary")),
        name="fox_sample",
    )(page_table, q, kn, vn, fn_rows, fn_cols, *tables,
      *([cache_k] * g), *([cache_v] * g), *([cache_lf] * g))


def _merge_kernel(oa_ref, ob_ref, wa_ref, wb_ref, ga_ref, gb_ref, o_ref):
    ua = _dot(oa_ref[...], wa_ref[...])
    ub = _dot(ob_ref[...], wb_ref[...])
    ga = jax.nn.sigmoid(ga_ref[...].astype(F32))
    gb = jax.nn.sigmoid(gb_ref[...].astype(F32))
    o_ref[...] = (ga * ua + gb * ub).astype(o_ref.dtype)


def _merge(oa, ob, wa, wb, gates, tm, tn=1024):
    m = oa.shape[0]
    d = wa.shape[1]
    nj = d // tn
    return pl.pallas_call(
        _merge_kernel,
        grid=(nj, m // tm),
        in_specs=[pl.BlockSpec((tm, GLA_VW), lambda j, i: (i, 0)),
                  pl.BlockSpec((tm, FOX_W), lambda j, i: (i, 0)),
                  pl.BlockSpec((GLA_VW, tn), lambda j, i: (0, j)),
                  pl.BlockSpec((FOX_W, tn), lambda j, i: (0, j)),
                  pl.BlockSpec((tm, tn), lambda j, i: (i, j)),
                  pl.BlockSpec((tm, tn), lambda j, i: (i, nj + j))],
        out_specs=pl.BlockSpec((tm, tn), lambda j, i: (i, j)),
        out_shape=jax.ShapeDtypeStruct((m, d), BF16),
        compiler_params=_cparams(("arbitrary", "arbitrary")),
        name="merge",
    )(oa, ob, wa, wb, gates, gates)


ROUTER_GROUP_LANE = N_EXPERTS


def _route(logits):
    lane_i = lax.broadcasted_iota(jnp.int32, logits.shape, 1)
    lane = lane_i.astype(F32)
    grp_of_lane = (lane_i >> 2).astype(F32)
    big = float(LANES)
    is_grp = (lane_i >= ROUTER_GROUP_LANE) & (lane_i < ROUTER_GROUP_LANE + N_GROUPS)
    gl = jnp.where(is_grp, logits, NEG)
    gmax = jnp.max(gl, axis=-1, keepdims=True)
    g_idx = jnp.min(jnp.where(is_grp & (gl == gmax), lane - ROUTER_GROUP_LANE, big), axis=-1, keepdims=True)
    g_w = 1.0 / jnp.sum(jnp.where(is_grp, jnp.exp(gl - gmax), 0.0), axis=-1, keepdims=True)
    in_grp = (lane_i < N_EXPERTS) & (grp_of_lane == g_idx)
    e1 = jnp.where(in_grp, logits, NEG)
    v1 = jnp.max(e1, axis=-1, keepdims=True)
    i1 = jnp.min(jnp.where(in_grp & (e1 == v1), lane, big), axis=-1, keepdims=True)
    rest = in_grp & (lane != i1)
    e2 = jnp.where(rest, logits, NEG)
    v2 = jnp.max(e2, axis=-1, keepdims=True)
    i2 = jnp.min(jnp.where(rest & (e2 == v2), lane, big), axis=-1, keepdims=True)
    r = jnp.exp(v2 - v1)
    w1 = g_w / (1.0 + r)
    w2 = g_w * r / (1.0 + r)
    grp_onehot = jnp.where(is_grp & (lane - ROUTER_GROUP_LANE == g_idx), 1.0, 0.0)
    return jnp.where(lane == i1, w1, 0.0) + jnp.where(lane == i2, w2, 0.0) + grp_onehot


HX_W = D_MODEL + LANES


def _outproj_kernel(mg_ref, w_ref, x_ref, gt_ref, g2_ref, sh_ref, sc_ref, wr_ref, br_ref, x1_ref, hx_ref):
    x1 = x_ref[...] + gt_ref[...] * _dot(mg_ref[...], w_ref[...])
    x1_ref[...] = x1
    h2 = _rms(x1) * g2_ref[...] * (1.0 + sc_ref[...]) + sh_ref[...]
    hx_ref[:, 0:D_MODEL] = h2
    hx_ref[:, D_MODEL:HX_W] = _route(_dot3(h2, wr_ref[...]) + br_ref[...])


def _outproj(merged, w_out, x, mod, g2, w_router, b_router, tm, tiles_per_batch):
    m, d = x.shape
    rb = mod.shape[2]
    const2 = lambda i: (0, 0)
    return pl.pallas_call(
        _outproj_kernel,
        grid=(m // tm,),
        in_specs=[pl.BlockSpec((tm, d), lambda i: (i, 0)),
                  pl.BlockSpec((d, d), const2),
                  pl.BlockSpec((tm, d), lambda i: (i, 0)),
                  _mod_spec(2, rb, d, tiles_per_batch),
                  pl.BlockSpec((1, d), const2),
                  _mod_spec(3, rb, d, tiles_per_batch),
                  _mod_spec(4, rb, d, tiles_per_batch),
                  pl.BlockSpec((d, LANES), const2),
                  pl.BlockSpec((1, LANES), const2)],
        out_specs=[pl.BlockSpec((tm, d), lambda i: (i, 0)),
                   pl.BlockSpec((tm, HX_W), lambda i: (i, 0))],
        out_shape=[jax.ShapeDtypeStruct((m, d), F32),
                   jax.ShapeDtypeStruct((m, HX_W), F32)],
        compiler_params=_cparams(("arbitrary",)),
        name="outproj",
    )(merged, w_out, x, mod, g2, mod, mod, w_router, b_router)


MOE_TILE = 512
PLAN_TILE = 512


def _moe_plan_kernel(r_ref, tri_ref, excl_ref, pos_ref, tg_ref, nu_ref, cnt_sc, off_sc, run_sc):
    ph = pl.program_id(0)
    n = pl.program_id(1)
    lane = lax.broadcasted_iota(jnp.int32, (1, LANES), 1)
    lane_t = lax.broadcasted_iota(jnp.int32, r_ref.shape, 1)
    is_grp = (lane_t >= ROUTER_GROUP_LANE) & (lane_t < ROUTER_GROUP_LANE + N_GROUPS)
    g4 = jnp.where(is_grp, r_ref[...], 0.0)

    @pl.when((ph == 0) & (n == 0))
    def _():
        cnt_sc[...] = jnp.zeros(cnt_sc.shape, F32)

    @pl.when(ph == 0)
    def _():
        cnt_sc[...] += jnp.sum(g4, axis=0, keepdims=True)

    @pl.when((ph == 1) & (n == 0))
    def _():
        padded = jnp.floor((cnt_sc[...] + (MOE_TILE - 1)) * (1.0 / MOE_TILE)) * MOE_TILE
        off = _dot_sel_rhs(jnp.broadcast_to(padded, (8, LANES)), excl_ref[...])[0:1]
        off_sc[...] = off
        run_sc[...] = jnp.zeros(run_sc.shape, F32)
        end = off + padded
        tile_start = lane.astype(F32) * MOE_TILE
        tg = jnp.zeros((1, LANES), F32)
        for g in range(N_GROUPS):
            end_g = jnp.sum(jnp.where(lane == ROUTER_GROUP_LANE + g, end, 0.0), axis=-1, keepdims=True)
            tg = tg + jnp.where(end_g <= tile_start, 1.0, 0.0)
        tg_ref[...] = jnp.minimum(tg, N_GROUPS - 1.0).astype(jnp.int32)
        total = jnp.sum(jnp.where(lane == ROUTER_GROUP_LANE + N_GROUPS - 1, end, 0.0), axis=-1, keepdims=True)
        nu_ref[...] = jnp.broadcast_to(total * (1.0 / MOE_TILE), (1, LANES)).astype(jnp.int32)

    @pl.when(ph == 1)
    def _():
        rank = _dot(tri_ref[...], g4.astype(BF16)) + run_sc[...]
        run_sc[...] += jnp.sum(g4, axis=0, keepdims=True)
        posv = g4 * (off_sc[...] + rank)
        hi, mid, lo = _split3(posv)
        ones = jnp.ones((8, LANES), BF16)
        row = _dot_nt(ones, hi) + _dot_nt(ones, mid) + _dot_nt(ones, lo)
        pos_ref[...] = row[0:1].astype(jnp.int32)


def _moe_plan(hx, ntiles):
    t = hx.shape[0]
    nblk = t // PLAN_TILE
    tri = jnp.asarray(np.tril(np.ones((PLAN_TILE, PLAN_TILE), np.float32), -1), BF16)
    excl = jnp.asarray(np.triu(np.ones((LANES, LANES), np.float32), 1), BF16)
    assert ntiles <= LANES
    pos, tg, nu = pl.pallas_call(
        _moe_plan_kernel,
        grid=(2, nblk),
        in_specs=[pl.BlockSpec((PLAN_TILE, LANES), lambda ph, n: (n, D_MODEL // LANES)),
                  pl.BlockSpec((PLAN_TILE, PLAN_TILE), lambda ph, n: (0, 0)),
                  pl.BlockSpec((LANES, LANES), lambda ph, n: (0, 0))],
        out_specs=[pl.BlockSpec((1, PLAN_TILE), lambda ph, n: (0, n * ph)),
                   pl.BlockSpec((1, LANES), lambda ph, n: (0, 0)),
                   pl.BlockSpec((1, LANES), lambda ph, n: (0, 0))],
        out_shape=[jax.ShapeDtypeStruct((1, t), jnp.int32),
                   jax.ShapeDtypeStruct((1, LANES), jnp.int32),
                   jax.ShapeDtypeStruct((1, LANES), jnp.int32)],
        scratch_shapes=[pltpu.VMEM((1, LANES), F32), pltpu.VMEM((1, LANES), F32), pltpu.VMEM((1, LANES), F32)],
        compiler_params=_cparams(("arbitrary", "arbitrary")),
        name="moe_plan",
    )(hx, tri, excl)
    return pos.reshape(t), tg.reshape(LANES), nu.reshape(LANES)[0:1]


def _row_copy(src_ref, src_row, dst_ref, dst_row, sem):
    return pltpu.make_async_copy(src_ref.at[pl.ds(src_row, 1)], dst_ref.at[pl.ds(dst_row, 1)], sem)


def _moe_grouped_kernel(pos_ref, tg_ref, nu_ref, hx_ref, wg_ref, wu_ref, wd_ref, ys_ref,
                        buf, sem, src_sc, xb_sc, *, t):
    i = pl.program_id(0)
    ei = pl.program_id(1)
    n_used = nu_ref[0]
    used = i < n_used
    slot = i % 2

    def fetch(tile, sl):
        def body(r, c):
            _row_copy(hx_ref, src_sc[tile * MOE_TILE + r], buf.at[sl], r, sem.at[sl]).start()
            return c
        lax.fori_loop(0, MOE_TILE, body, 0, unroll=8)

    @pl.when((i == 0) & (ei == 0))
    def _():
        def clear(r, c):
            src_sc[r] = 0
            return c

        def invert(tk, c):
            src_sc[pos_ref[tk]] = tk
            return c

        lax.fori_loop(0, src_sc.shape[0], clear, 0, unroll=8)
        lax.fori_loop(0, t, invert, 0, unroll=8)
        fetch(0, 0)

    @pl.when(used & (ei == 0))
    def _():
        def wait(r, c):
            _row_copy(hx_ref, 0, buf.at[slot], r, sem.at[slot]).wait()
            return c
        lax.fori_loop(0, MOE_TILE, wait, 0, unroll=8)
        xb_sc[...] = buf[slot, :, 0:D_MODEL].astype(BF16)

    @pl.when(ei == 0)
    def _():
        ys_ref[...] = jnp.zeros(ys_ref.shape, F32)

    def expert_step():
        h = xb_sc[...]
        a = _dot(h, wg_ref[...].astype(BF16))
        u = _dot(h, wu_ref[...].astype(BF16))
        lane = lax.broadcasted_iota(jnp.int32, (MOE_TILE, LANES), 1)
        e = tg_ref[i] * EXP_PER_GROUP + ei
        cw = jnp.sum(jnp.where(lane == e, buf[slot, :, D_MODEL:HX_W], 0.0), axis=-1, keepdims=True)
        hid = (_silu(a) * u * cw).astype(BF16)
        ys_ref[...] += _dot(hid, wd_ref[...].astype(BF16))

    has_next = i + 1 < n_used

    @pl.when(used & has_next)
    def _():
        per_step = MOE_TILE // EXP_PER_GROUP
        nslot = (i + 1) % 2
        for r in range(per_step):
            row = ei * per_step + r
            _row_copy(hx_ref, src_sc[(i + 1) * MOE_TILE + row], buf.at[nslot], row, sem.at[nslot]).start()
        expert_step()

    @pl.when(used & jnp.logical_not(has_next))
    def _():
        expert_step()


def _moe_grouped(hx, pos, tg, nu, wg, wu, wd, rows_out):
    t = hx.shape[0]
    ne, d, de = wg.shape
    ntiles = rows_out // MOE_TILE

    def wmap(i, ei, pos_ref, tg_ref, nu_ref):
        e = jnp.where(i < nu_ref[0], tg_ref[i] * EXP_PER_GROUP + ei, ne - 1)
        return (e, 0, 0)

    grid_spec = pltpu.PrefetchScalarGridSpec(
        num_scalar_prefetch=3,
        grid=(ntiles, EXP_PER_GROUP),
        in_specs=[pl.BlockSpec(memory_space=pl.ANY),
                  pl.BlockSpec((None, d, de), wmap),
                  pl.BlockSpec((None, d, de), wmap),
                  pl.BlockSpec((None, de, d), wmap)],
        out_specs=pl.BlockSpec((MOE_TILE, d), lambda i, ei, pos_ref, tg_ref, nu_ref: (i, 0)),
        scratch_shapes=[pltpu.VMEM((2, MOE_TILE, HX_W), F32), pltpu.SemaphoreType.DMA((2,)),
                        pltpu.SMEM((rows_out,), jnp.int32),
                        pltpu.VMEM((MOE_TILE, d), BF16)],
    )
    return pl.pallas_call(
        functools.partial(_moe_grouped_kernel, t=t),
        grid_spec=grid_spec,
        out_shape=jax.ShapeDtypeStruct((rows_out, d), F32),
        compiler_params=_cparams(("arbitrary", "arbitrary")),
        name="moe_grouped",
    )(pos, tg, nu, hx, wg, wu, wd)


COMBINE_ROWS = 256


def _moe_combine_kernel(pos_ref, ys_ref, x1_ref, gt_ref, gf_ref, y_ref, buf, sem):
    i = pl.program_id(0)
    n = pl.num_programs(0)

    def fetch(tile, slot):
        def body(r, c):
            _row_copy(ys_ref, pos_ref[tile * COMBINE_ROWS + r], buf.at[slot], r, sem.at[slot]).start()
            return c
        lax.fori_loop(0, COMBINE_ROWS, body, 0, unroll=8)

    @pl.when(i == 0)
    def _():
        fetch(0, 0)

    @pl.when(i + 1 < n)
    def _():
        fetch(i + 1, (i + 1) % 2)

    slot = i % 2

    def wait(r, c):
        _row_copy(ys_ref, pos_ref[i * COMBINE_ROWS + r], buf.at[slot], r, sem.at[slot]).wait()
        return c

    lax.fori_loop(0, COMBINE_ROWS, wait, 0, unroll=8)
    x2 = x1_ref[...] + gt_ref[...] * buf[slot]
    y_ref[...] = _rms(x2) * gf_ref[...]


def _moe_combine(ys, pos, x1, mod, g_final, tiles_per_batch):
    m, d = x1.shape
    rb = mod.shape[2]
    grid_spec = pltpu.PrefetchScalarGridSpec(
        num_scalar_prefetch=1,
        grid=(m // COMBINE_ROWS,),
        in_specs=[pl.BlockSpec(memory_space=pl.ANY),
                  pl.BlockSpec((COMBINE_ROWS, d), lambda i, pos_ref: (i, 0)),
                  pl.BlockSpec((None, None, rb, d), lambda i, pos_ref: (5, i // tiles_per_batch, 0, 0)),
                  pl.BlockSpec((1, d), lambda i, pos_ref: (0, 0))],
        out_specs=pl.BlockSpec((COMBINE_ROWS, d), lambda i, pos_ref: (i, 0)),
        scratch_shapes=[pltpu.VMEM((2, COMBINE_ROWS, d), F32), pltpu.SemaphoreType.DMA((2,))],
    )
    return pl.pallas_call(
        _moe_combine_kernel,
        grid_spec=grid_spec,
        out_shape=jax.ShapeDtypeStruct((m, d), F32),
        compiler_params=_cparams(("arbitrary",)),
        name="moe_combine",
    )(pos, ys, x1, mod, g_final)


def _moe_kernel(h_ref, cmb_ref, wg_ref, wu_ref, wd_ref, x1_ref, gt_ref, gf_ref, y_ref, acc_ref):
    e = pl.program_id(1)

    @pl.when(e == 0)
    def _():
        acc_ref[...] = jnp.zeros(acc_ref.shape, F32)

    h = h_ref[...]
    a = _dot(h, wg_ref[...].astype(BF16))
    u = _dot(h, wu_ref[...].astype(BF16))
    lane = lax.broadcasted_iota(jnp.int32, cmb_ref.shape, 1)
    cw = jnp.sum(jnp.where(lane == e, cmb_ref[...], 0.0), axis=-1, keepdims=True)
    hid = (_silu(a) * u * cw).astype(BF16)
    acc_ref[...] += _dot(hid, wd_ref[...].astype(BF16))

    @pl.when(e == pl.num_programs(1) - 1)
    def _():
        x2 = x1_ref[...] + gt_ref[...] * acc_ref[...]
        y_ref[...] = _rms(x2) * gf_ref[...]


def _moe(h2, cmb, wg, wu, wd, x1, mod, g_final, tm, tiles_per_batch):
    m, d = x1.shape
    rb = mod.shape[2]
    ne, _, de = wg.shape
    return pl.pallas_call(
        _moe_kernel,
        grid=(m // tm, ne),
        in_specs=[pl.BlockSpec((tm, d), lambda i, e: (i, 0)),
                  pl.BlockSpec((tm, LANES), lambda i, e: (i, 0)),
                  pl.BlockSpec((None, d, de), lambda i, e: (e, 0, 0)),
                  pl.BlockSpec((None, d, de), lambda i, e: (e, 0, 0)),
                  pl.BlockSpec((None, de, d), lambda i, e: (e, 0, 0)),
                  pl.BlockSpec((tm, d), lambda i, e: (i, 0)),
                  pl.BlockSpec((None, None, rb, d), lambda i, e: (5, i // tiles_per_batch, 0, 0)),
                  pl.BlockSpec((1, d), lambda i, e: (0, 0))],
        out_specs=pl.BlockSpec((tm, d), lambda i, e: (i, 0)),
        out_shape=jax.ShapeDtypeStruct((m, d), F32),
        scratch_shapes=[pltpu.VMEM((tm, d), F32)],
        compiler_params=_cparams(("arbitrary", "arbitrary")),
        name="moe",
    )(h2, cmb, wg, wu, wd, x1, mod, g_final)


def _prep_weights(w_ada, b_ada, g_norm1, g_norm2, g_final, w_in, w_a2, b_a, b_f, g_gla_norm, w_up_a, w_up_b, w_out,
                  w_grp, b_grp, w_exp, b_exp, w_gate_e, w_up_e, w_down_e):
    wt = w_in.reshape(w_in.shape[1:]).T
    d = D_MODEL
    o_lra = 2 * GLA_KW + 2 * GLA_VW
    o_fox = o_lra + GLA_RANK
    o_fb = o_fox + 3 * FOX_W
    o_g = o_fb + FOX_HEADS
    pad = jnp.zeros((LANES - GLA_RANK - FOX_HEADS, d), F32)
    wall, mask = _gla_tables()
    wall_s, mask_s = _gla_tables(SAMPLE_LEVELS)
    bf_row =jnp.zeros((1, LANES), F32).at[0, FB_LANE:FB_LANE + FOX_HEADS].set(b_f[0])
    w_router = jnp.concatenate([w_exp[0], w_grp[0], jnp.zeros((d, LANES - N_EXPERTS - N_GROUPS), F32)], axis=1)
    b_router = jnp.concatenate([b_exp[0], b_grp[0], jnp.zeros((LANES - N_EXPERTS - N_GROUPS,), F32)])[None, :]
    return dict(
        w_ada=w_ada.reshape(w_ada.shape[1:]), b_ada=b_ada[0][None, :],
        g1=g_norm1[0][None, :], g2=g_norm2[0][None, :], gf=g_final[None, :],
        w_in_t=wt,
        w_small_t=jnp.concatenate([wt[o_lra:o_lra + GLA_RANK], wt[o_fb:o_fb + FOX_HEADS], pad], axis=0),
        wall=jnp.asarray(wall, BF16), mask=jnp.asarray(mask, F32),
        wall_s=jnp.asarray(wall_s, BF16), mask_s=jnp.asarray(mask_s, F32),
        w_a2=jnp.concatenate([w_a2[0], jnp.zeros((LANES - GLA_RANK, GLA_KW), F32)], axis=0), b_a=b_a[0][None, :], bf_row=bf_row, g_gla=g_gla_norm[0][None, :],
        w_up_a=w_up_a[0].astype(BF16), w_up_b=w_up_b[0].astype(BF16), w_out=w_out[0].astype(BF16),
        w_router=w_router, b_router=b_router,
        wg=w_gate_e.reshape(w_gate_e.shape[1:]), wu=w_up_e.reshape(w_up_e.shape[1:]),
        wd=w_down_e.reshape(w_down_e.shape[1:]),
    )


def _project(h, p, tm):
    tn = 1024
    wt = p["w_in_t"]
    n_gla = 2 * GLA_KW + 2 * GLA_VW
    c_q = n_gla + GLA_RANK
    c_g = c_q + 3 * FOX_W + FOX_HEADS
    (pg,) = _mm_wt(h, wt, 0, n_gla, tm, tn)
    (qb,) = _mm_wt(h, wt, c_q, FOX_W, tm, tn, (BF16,))
    kb, kb16 = _mm_heads(h, wt, c_q + FOX_W, min(tm, 512))
    vb, vb16 = _mm_heads(h, wt, c_q + 2 * FOX_W, min(tm, 512))
    (gates,) = _mm_wt(h, wt, c_g, 2 * D_MODEL, tm, tn, (BF16,))
    (small,) = _mm_wt(h, p["w_small_t"], 0, LANES, tm, LANES)
    return pg, qb, kb, kb16, vb, vb16, gates, small


def _tail(x, oa, ob, gates, mod, p, tm, tiles_per_batch, grouped):
    merged = _merge(oa, ob, p["w_up_a"], p["w_up_b"], gates, tm)
    tmo = min(tm, 256)
    x1, hx = _outproj(merged, p["w_out"], x, mod, p["g2"], p["w_router"], p["b_router"],
                      tmo, tiles_per_batch * (tm // tmo))
    if not grouped:
        tmm = min(tm, 512)
        return _moe(hx[:, 0:D_MODEL].astype(BF16), hx[:, D_MODEL:HX_W], p["wg"], p["wu"], p["wd"], x1, mod, p["gf"],
                    tmm, tiles_per_batch * (tm // tmm))
    rows_out = x.shape[0] + N_GROUPS * MOE_TILE
    pos, tg, nu = _moe_plan(hx, rows_out // MOE_TILE)
    ys = _moe_grouped(hx, pos, tg, nu, p["wg"], p["wu"], p["wd"], rows_out)
    return _moe_combine(ys, pos, x1, mod, p["gf"], tiles_per_batch * (tm // COMBINE_ROWS))


def kernel(x_prompt, x_sample, cache_k, cache_v, cache_logf, state_gla, page_table, c_prompt, c_sample, w_ada, b_ada,
           g_norm1, g_norm2, g_final, w_in, w_a2, b_a, b_f, g_gla_norm, w_up_a, w_up_b, w_out, w_grp, b_grp, w_exp,
           b_exp, w_gate_e, w_up_e, w_down_e):
    p = _prep_weights(w_ada, b_ada, g_norm1, g_norm2, g_final, w_in, w_a2, b_a, b_f, g_gla_norm, w_up_a, w_up_b,
                      w_out, w_grp, b_grp, w_exp, b_exp, w_gate_e, w_up_e, w_down_e)
    bp, seq, d = x_prompt.shape
    bs, t, _ = x_sample.shape
    assert t <= 2 ** SAMPLE_LEVELS and t <= ROWS8

    mod = _adaln(jnp.concatenate([c_prompt, c_sample], axis=0), p["w_ada"], p["b_ada"])
    mod_p = mod[:bp].reshape(bp, 6, 1, d).transpose(1, 0, 2, 3)
    mod_s = jnp.repeat(mod[bp:].reshape(bs, 6, d), t, axis=0).transpose(1, 0, 2)[:, None]

    tm = 1024
    tpb = seq // tm
    xp = x_prompt.reshape(bp * seq, d)
    hp = _normmod(xp, p["g1"], mod_p, tm, tpb)
    pg, qb, kb, kb16, vb, vb16, gates, small = _project(hp, p, tm)
    s0 = jnp.zeros((bp, GLA_HEADS, GLA_DK, GLA_DV), F32)
    oa, s_p = _gla_prompt(pg, small, s0, p["wall"], p["mask"], p["w_a2"], p["b_a"], p["g_gla"], bp, seq)
    lf_p, fq, fk = _fox_bias_prompt(small, p["bf_row"], bp, seq)
    ob = _fox_prompt(qb, kb16, vb16, fq, fk, bp, seq)
    y_p = _tail(xp, oa, ob, gates, mod_p, p, tm, tpb, grouped=True)

    rows = bs * t
    xs = x_sample.reshape(rows, d)
    hs = _normmod(xs, p["g1"], mod_s, rows, 1)
    pg_s, qs, ks, _, vs, _, gates_s, small_s = _project(hs, p, rows)
    oa_s, s_s = _gla_sample(pg_s.reshape(bs, t, -1), small_s.reshape(bs, t, LANES),
                            state_gla.reshape(state_gla.shape[1:]), p["wall_s"], p["mask_s"],
                            p["w_a2"], p["b_a"], p["g_gla"], bs, t)
    lf_s, fn_s = _fox_bias_sample(small_s, p["bf_row"], t)
    n_pool = cache_k.shape[1]
    ob_s = _fox_sample(page_table, qs.astype(F32).reshape(bs, t, FOX_W), ks.reshape(bs, t, FOX_HEADS, FOX_DH),
                       vs.reshape(bs, t, FOX_HEADS, FOX_DH), fn_s.reshape(bs, t, FOX_HEADS),
                       cache_k, cache_v, cache_logf.reshape(n_pool, PAGE, FOX_HEADS).transpose(0, 2, 1), t)
    y_s = _tail(xs, oa_s.reshape(rows, GLA_VW).astype(BF16), ob_s.reshape(rows, FOX_W).astype(BF16), gates_s, mod_s,
                p, rows, 1, grouped=False)

    return (y_p.reshape(bp, seq, d), y_s.reshape(bs, t, d),
            kb.reshape(1, bp, seq, FOX_HEADS, FOX_DH), vb.reshape(1, bp, seq, FOX_HEADS, FOX_DH),
            lf_p.reshape(1, bp, seq, FOX_HEADS), s_p[None],
            ks.reshape(1, bs, t, FOX_HEADS, FOX_DH), vs.reshape(1, bs, t, FOX_HEADS, FOX_DH),
            lf_s.reshape(1, bs, t, FOX_HEADS), s_s[None])
```

```python
import functools

import numpy as np
import jax
import jax.numpy as jnp
from jax import lax
from jax.experimental import pallas as pl
from jax.experimental.pallas import tpu as pltpu

F32 = jnp.float32
BF16 = jnp.bfloat16

D_MODEL = 2048
GLA_HEADS = 4
GLA_DK = 128
GLA_DV = 256
GLA_RANK = 16
GLA_TAU = 16.0
FOX_HEADS = 8
FOX_DH = 128
PAGE = 128
N_GROUPS = 4
EXP_PER_GROUP = 4
N_EXPERTS = 16
D_EXPERT = 512
RMS_EPS = 1e-6
GLA_KW = GLA_HEADS * GLA_DK
GLA_VW = GLA_HEADS * GLA_DV
FOX_W = FOX_HEADS * FOX_DH
GLA_CHUNK = 128
GLA_LEVELS = 7
GLA_MM_LEVELS = 4
SAMPLE_LEVELS = 2
LANES = 128
NEG = -1e30
VMEM_LIMIT = 56 * 1024 * 1024

ROW_TILE = 1024
PROJ_COL_TILE = 1024
HEADS_ROW_TILE = 512
OUTPROJ_ROW_TILE = 256
DENSE_MOE_ROW_TILE = 512
GLA_ROWS_PER_STEP = 512
FOX_BIAS_ROWS = 512
FOX_Q_TILE = 512
FOX_K_TILE = 256
ADALN_COL_TILE = 512


def _cparams(sem):
    return pltpu.CompilerParams(dimension_semantics=sem, vmem_limit_bytes=VMEM_LIMIT)


def _dot(a, b):
    return jnp.dot(a, b, preferred_element_type=F32)


def _dot_nt(a, b):
    return lax.dot_general(a, b, (((1,), (1,)), ((), ())), preferred_element_type=F32)


def _dot_tn(a, b):
    return lax.dot_general(a, b, (((0,), (0,)), ((), ())), preferred_element_type=F32)


def _split2(x):
    hi = x.astype(BF16)
    lo = (x - hi.astype(F32)).astype(BF16)
    return hi, lo


def _split3(x):
    hi = x.astype(BF16)
    r = x - hi.astype(F32)
    mid = r.astype(BF16)
    lo = (r - mid.astype(F32)).astype(BF16)
    return hi, mid, lo


def _dot3(a, b):
    ah, al = _split2(a)
    bh, bl = _split2(b)
    return _dot(ah, bh) + _dot(ah, bl) + _dot(al, bh)


def _dot_sel(w01, x):
    hi, mid, lo = _split3(x)
    return _dot(w01, hi) + _dot(w01, mid) + _dot(w01, lo)


def _log_sigmoid(x):
    return jnp.minimum(x, 0.0) - jnp.log1p(jnp.exp(-jnp.abs(x)))


def _silu(x):
    return x * jax.nn.sigmoid(x)


def _adaln_kernel(c_ref, w_ref, b_ref, o_ref):
    o_ref[...] = _dot3(_silu(c_ref[...]), w_ref[...]) + b_ref[...]


def _adaln(c, w, b, tn=ADALN_COL_TILE):
    nb, d = c.shape
    n = w.shape[1]
    return pl.pallas_call(
        _adaln_kernel,
        grid=(n // tn,),
        in_specs=[pl.BlockSpec((nb, d), lambda j: (0, 0)),
                  pl.BlockSpec((d, tn), lambda j: (0, j)),
                  pl.BlockSpec((1, tn), lambda j: (0, j))],
        out_specs=pl.BlockSpec((nb, tn), lambda j: (0, j)),
        out_shape=jax.ShapeDtypeStruct((nb, n), F32),
        compiler_params=_cparams(("arbitrary",)),
        name="adaln",
    )(c, w, b)


def _rms(x):
    return x * lax.rsqrt(jnp.mean(x * x, axis=-1, keepdims=True) + RMS_EPS)


def _normmod_kernel(x_ref, g_ref, sh_ref, sc_ref, o_ref):
    y = _rms(x_ref[...]) * g_ref[...]
    o_ref[...] = (y * (1.0 + sc_ref[...]) + sh_ref[...]).astype(o_ref.dtype)


def _mod_spec(k, rb, d, tiles_per_batch):
    return pl.BlockSpec((None, None, rb, d), lambda i: (k, i // tiles_per_batch, 0, 0))


def _normmod(x, g, mod, tm, tiles_per_batch):
    m, d = x.shape
    rb = mod.shape[2]
    return pl.pallas_call(
        _normmod_kernel,
        grid=(m // tm,),
        in_specs=[pl.BlockSpec((tm, d), lambda i: (i, 0)),
                  pl.BlockSpec((1, d), lambda i: (0, 0)),
                  _mod_spec(0, rb, d, tiles_per_batch),
                  _mod_spec(1, rb, d, tiles_per_batch)],
        out_specs=pl.BlockSpec((tm, d), lambda i: (i, 0)),
        out_shape=jax.ShapeDtypeStruct((m, d), BF16),
        compiler_params=_cparams(("arbitrary",)),
        name="normmod",
    )(x, g, mod, mod)


def _wt_spec(k, row0, tn):
    assert row0 % 8 == 0 and tn % 8 == 0
    return pl.BlockSpec((pl.Element(tn), pl.Element(k)), lambda j, i: (pl.multiple_of(row0 + j * tn, 8), 0))


def _mm_wt_kernel(x_ref, wt_ref, *refs):
    o_refs, wb = refs[:-1], refs[-1]

    @pl.when(pl.program_id(1) == 0)
    def _():
        wb[...] = wt_ref[...].astype(BF16)

    r = _dot_nt(x_ref[...], wb[...])
    for o_ref in o_refs:
        o_ref[...] = r.astype(o_ref.dtype)


def _mm_wt(x, wt, row0, n, tm, tn, out_dtypes=(F32,)):
    m, k = x.shape
    return pl.pallas_call(
        _mm_wt_kernel,
        grid=(n // tn, m // tm),
        in_specs=[pl.BlockSpec((tm, k), lambda j, i: (i, 0)), _wt_spec(k, row0, tn)],
        out_specs=[pl.BlockSpec((tm, tn), lambda j, i: (i, j)) for _ in out_dtypes],
        out_shape=[jax.ShapeDtypeStruct((m, n), dt) for dt in out_dtypes],
        scratch_shapes=[pltpu.VMEM((tn, k), BF16)],
        compiler_params=_cparams(("arbitrary", "arbitrary")),
        name="proj_mm",
    )(x, wt)


def _mm_heads_kernel(x_ref, wt_ref, o_ref, o16_ref, wb):
    @pl.when(pl.program_id(1) == 0)
    def _():
        wb[...] = wt_ref[...].astype(BF16)

    r = _dot_nt(x_ref[...], wb[...])
    for h in range(FOX_HEADS):
        o_ref[:, h, :] = r[:, h * FOX_DH:(h + 1) * FOX_DH]
    o16_ref[...] = r.astype(BF16)


def _mm_heads(x, wt, row0, tm):
    m, k = x.shape
    return pl.pallas_call(
        _mm_heads_kernel,
        grid=(1, m // tm),
        in_specs=[pl.BlockSpec((tm, k), lambda j, i: (i, 0)), _wt_spec(k, row0, FOX_W)],
        out_specs=[pl.BlockSpec((tm, FOX_HEADS, FOX_DH), lambda j, i: (i, 0, 0)),
                   pl.BlockSpec((tm, FOX_W), lambda j, i: (i, 0))],
        out_shape=[jax.ShapeDtypeStruct((m, FOX_HEADS, FOX_DH), F32),
                   jax.ShapeDtypeStruct((m, FOX_W), BF16)],
        scratch_shapes=[pltpu.VMEM((FOX_W, k), BF16)],
        compiler_params=_cparams(("arbitrary", "arbitrary")),
        name="proj_heads",
    )(x, wt)


def _gla_tables(p=GLA_LEVELS):
    c = GLA_CHUNK
    nmm = min(p, GLA_MM_LEVELS)
    t = np.arange(c)[:, None]
    m = np.arange(c)[None, :]
    wall = np.zeros((nmm + 1, c, c), np.float32)
    mask = np.zeros((p + 1, c, c), np.float32)
    for l in range(p):
        half = 1 << l
        pos = t % (2 * half)
        mid = t - pos + half
        right = pos >= half
        if l < nmm:
            wall[l] = np.where(right, (m >= mid) & (m <= t), (m > t) & (m < mid))
        s = m
        mask[l] = ((t >> (l + 1)) == (s >> (l + 1))) & (((t >> l) & 1) == 1) & (((s >> l) & 1) == 0)
    wall[nmm] = m <= t
    mask[p] = t == m
    return wall.reshape((nmm + 1) * c, c), mask


def _gla_level_exponent(cum, row, l):
    half = 1 << l
    c = cum.shape[0]
    bound = None
    for start in range(0, c, 2 * half):
        b = cum[start + half - 1:start + half, :]
        bound = b if bound is None else jnp.where(row >= start, b, bound)
    return jnp.where(((row >> l) & 1) == 1, cum - bound, bound - cum)


def _gla_body(q_ref, k_ref, v_ref, ra_ref, sm_ref, wall_ref, mask_ref, wa2_ref, ba_ref, g_ref,
              og_ref, st_ref, *, nchunk, valid, store_rows):
    c, p = GLA_CHUNK, mask_ref.shape[0] - 1
    nmm = min(p, GLA_MM_LEVELS)
    row = lax.broadcasted_iota(jnp.int32, (c, GLA_DK), 0)
    rowh = lax.broadcasted_iota(jnp.int32, (c, GLA_KW), 0)
    wall = wall_ref[...]
    for ci in range(nchunk):
        rows = pl.ds(ci * c, c)
        x = _dot3(sm_ref[rows, :], wa2_ref[...]) + ba_ref[...]
        la = _log_sigmoid(x) * (1.0 / GLA_TAU)
        if valid < c:
            la = jnp.where(rowh < valid, la, 0.0)
        args = _dot_sel(wall, la)
        cum = args[nmm * c:(nmm + 1) * c]
        e_lvl = [jnp.exp(args[l * c:(l + 1) * c]) for l in range(nmm)]
        e_lvl += [jnp.exp(_gla_level_exponent(cum, rowh, l)) for l in range(nmm, p)]
        last = cum[c - 1:c, :]
        e_cum = jnp.exp(cum)
        e_rev = jnp.exp(last - cum)
        e_last = jnp.exp(last)
        for h in range(GLA_HEADS):
            ks = slice(h * GLA_DK, (h + 1) * GLA_DK)
            vs = slice(h * GLA_DV, (h + 1) * GLA_DV)
            q = q_ref[rows, ks] * (GLA_DK ** -0.5)
            k = k_ref[rows, ks]
            vb = v_ref[rows, vs].astype(BF16)
            a = mask_ref[p] * _dot_nt(q.astype(BF16), k.astype(BF16))
            for l in range(p):
                xl = (jnp.where(((row >> l) & 1) == 1, q, k) * e_lvl[l][:, ks]).astype(BF16)
                a = a + mask_ref[l] * _dot_nt(xl, xl)
            st = st_ref[h]
            qc = (q * e_cum[:, ks]).astype(BF16)
            o = _dot(a.astype(BF16), vb) + _dot_nt(qc, st.astype(BF16))
            kr = (k * e_rev[:, ks]).astype(BF16)
            st_ref[h] = e_last[:, ks] * st + _dot_tn(vb, kr)
            og = _rms(o) * g_ref[...] * _silu(ra_ref[rows, vs])
            if store_rows < c:
                og_ref[:, vs] = og[:store_rows].astype(og_ref.dtype)
            else:
                og_ref[rows, vs] = og.astype(og_ref.dtype)


def _gla_prompt_kernel(q_ref, k_ref, v_ref, ra_ref, sm_ref, wall_ref, mask_ref, wa2_ref, ba_ref, g_ref, s0_ref,
                       og_ref, s_ref, st_ref, *, nchunk):
    n = pl.program_id(1)

    @pl.when(n == 0)
    def _():
        for h in range(GLA_HEADS):
            st_ref[h] = s0_ref[h].T

    _gla_body(q_ref, k_ref, v_ref, ra_ref, sm_ref, wall_ref, mask_ref, wa2_ref, ba_ref, g_ref, og_ref, st_ref,
              nchunk=nchunk, valid=GLA_CHUNK, store_rows=GLA_CHUNK)

    @pl.when(n == pl.num_programs(1) - 1)
    def _():
        for h in range(GLA_HEADS):
            s_ref[h] = st_ref[h].T


def _gla_prompt(pg, small, s0, wall, mask, wa2, ba, g, batch, seq, tb=GLA_ROWS_PER_STEP):
    nblk = seq // tb
    rowmap = lambda cb: (lambda b, n: (b * nblk + n, cb))
    const2 = lambda b, n: (0, 0)
    return pl.pallas_call(
        functools.partial(_gla_prompt_kernel, nchunk=tb // GLA_CHUNK),
        grid=(batch, nblk),
        in_specs=[pl.BlockSpec((tb, GLA_KW), rowmap(0)),
                  pl.BlockSpec((tb, GLA_KW), rowmap(1)),
                  pl.BlockSpec((tb, GLA_VW), rowmap(1)),
                  pl.BlockSpec((tb, GLA_VW), rowmap(2)),
                  pl.BlockSpec((tb, LANES), rowmap(0)),
                  pl.BlockSpec(wall.shape, const2),
                  pl.BlockSpec(mask.shape, lambda b, n: (0, 0, 0)),
                  pl.BlockSpec(wa2.shape, const2),
                  pl.BlockSpec(ba.shape, const2),
                  pl.BlockSpec(g.shape, const2),
                  pl.BlockSpec((None, GLA_HEADS, GLA_DK, GLA_DV), lambda b, n: (b, 0, 0, 0))],
        out_specs=[pl.BlockSpec((tb, GLA_VW), rowmap(0)),
                   pl.BlockSpec((None, GLA_HEADS, GLA_DK, GLA_DV), lambda b, n: (b, 0, 0, 0))],
        out_shape=[jax.ShapeDtypeStruct((batch * seq, GLA_VW), BF16),
                   jax.ShapeDtypeStruct((batch, GLA_HEADS, GLA_DK, GLA_DV), F32)],
        scratch_shapes=[pltpu.VMEM((GLA_HEADS, GLA_DV, GLA_DK), F32)],
        compiler_params=_cparams(("arbitrary", "arbitrary")),
        name="gla_prompt",
    )(pg, pg, pg, pg, small, wall, mask, wa2, ba, g, s0)


def _gla_sample_kernel(pg_ref, sm_ref, wall_ref, mask_ref, wa2_ref, ba_ref, g_ref, s0_ref,
                       og_ref, s_ref, pad_ref, smpad_ref, st_ref, *, t):
    @pl.when(pl.program_id(0) == 0)
    def _():
        pad_ref[...] = jnp.zeros(pad_ref.shape, F32)
        smpad_ref[...] = jnp.zeros(smpad_ref.shape, F32)

    pad_ref[0:t, :] = pg_ref[...]
    smpad_ref[0:t, :] = sm_ref[...]
    for h in range(GLA_HEADS):
        st_ref[h] = s0_ref[h].T
    q_ref = pad_ref.at[:, 0:GLA_KW]
    k_ref = pad_ref.at[:, GLA_KW:2 * GLA_KW]
    v_ref = pad_ref.at[:, 2 * GLA_KW:2 * GLA_KW + GLA_VW]
    ra_ref = pad_ref.at[:, 2 * GLA_KW + GLA_VW:2 * GLA_KW + 2 * GLA_VW]
    _gla_body(q_ref, k_ref, v_ref, ra_ref, smpad_ref, wall_ref, mask_ref, wa2_ref, ba_ref, g_ref, og_ref, st_ref,
              nchunk=1, valid=t, store_rows=t)
    for h in range(GLA_HEADS):
        s_ref[h] = st_ref[h].T


def _gla_sample(pg, small, s0, wall, mask, wa2, ba, g, batch, t):
    width = pg.shape[-1]
    const2 = lambda b: (0, 0)
    return pl.pallas_call(
        functools.partial(_gla_sample_kernel, t=t),
        grid=(batch,),
        in_specs=[pl.BlockSpec((None, t, width), lambda b: (b, 0, 0)),
                  pl.BlockSpec((None, t, LANES), lambda b: (b, 0, 0)),
                  pl.BlockSpec(wall.shape, const2),
                  pl.BlockSpec(mask.shape, lambda b: (0, 0, 0)),
                  pl.BlockSpec(wa2.shape, const2),
                  pl.BlockSpec(ba.shape, const2),
                  pl.BlockSpec(g.shape, const2),
                  pl.BlockSpec((None, GLA_HEADS, GLA_DK, GLA_DV), lambda b: (b, 0, 0, 0))],
        out_specs=[pl.BlockSpec((None, t, GLA_VW), lambda b: (b, 0, 0)),
                   pl.BlockSpec((None, GLA_HEADS, GLA_DK, GLA_DV), lambda b: (b, 0, 0, 0))],
        out_shape=[jax.ShapeDtypeStruct((batch, t, GLA_VW), F32),
                   jax.ShapeDtypeStruct((batch, GLA_HEADS, GLA_DK, GLA_DV), F32)],
        scratch_shapes=[pltpu.VMEM((GLA_CHUNK, width), F32),
                        pltpu.VMEM((GLA_CHUNK, LANES), F32),
                        pltpu.VMEM((GLA_HEADS, GLA_DV, GLA_DK), F32)],
        compiler_params=_cparams(("arbitrary",)),
        name="gla_sample",
    )(pg, small, wall, mask, wa2, ba, g, s0)


FB_LANE = GLA_RANK


LOG2E = 1.4426950408889634


def _fox_bias_prompt_kernel(sm_ref, bf_ref, tri_ref, lf_ref, fq_ref, fk_ref, carry_ref):
    @pl.when(pl.program_id(1) == 0)
    def _():
        carry_ref[...] = jnp.zeros(carry_ref.shape, F32)

    lf = _log_sigmoid(sm_ref[...] + bf_ref[...])
    lf_ref[...] = lf[:, FB_LANE:FB_LANE + FOX_HEADS]
    cum = _dot_sel(tri_ref[...], lf) + carry_ref[...]
    carry_ref[...] = cum[cum.shape[0] - 1:, :]
    lane = lax.broadcasted_iota(jnp.int32, cum.shape, 1)
    ones = jnp.where(lane < 6, 1.0, 0.0)
    for h in range(FOX_HEADS):
        f = jnp.broadcast_to(cum[:, FB_LANE + h:FB_LANE + h + 1] * LOG2E, cum.shape)
        hi = f.astype(BF16).astype(F32)
        r = f - hi
        mid = r.astype(BF16).astype(F32)
        lo = r - mid
        fq = jnp.where(lane == 0, hi, jnp.where(lane == 1, mid, jnp.where(lane == 2, lo, ones)))
        fk = jnp.where(lane == 3, -hi, jnp.where(lane == 4, -mid, jnp.where(lane == 5, -lo, ones)))
        fq_ref[:, h * LANES:(h + 1) * LANES] = fq.astype(BF16)
        fk_ref[:, h * LANES:(h + 1) * LANES] = fk.astype(BF16)


def _fox_bias_prompt(small, bf_row, batch, seq, tb=FOX_BIAS_ROWS):
    nblk = seq // tb
    tri = jnp.asarray(np.tril(np.ones((tb, tb), np.float32)), BF16)
    wide = FOX_HEADS * LANES
    return pl.pallas_call(
        _fox_bias_prompt_kernel,
        grid=(batch, nblk),
        in_specs=[pl.BlockSpec((tb, LANES), lambda b, n: (b * nblk + n, 0)),
                  pl.BlockSpec((1, LANES), lambda b, n: (0, 0)),
                  pl.BlockSpec((tb, tb), lambda b, n: (0, 0))],
        out_specs=[pl.BlockSpec((tb, FOX_HEADS), lambda b, n: (b * nblk + n, 0)),
                   pl.BlockSpec((tb, wide), lambda b, n: (b * nblk + n, 0)),
                   pl.BlockSpec((tb, wide), lambda b, n: (b * nblk + n, 0))],
        out_shape=[jax.ShapeDtypeStruct((batch * seq, FOX_HEADS), F32),
                   jax.ShapeDtypeStruct((batch * seq, wide), BF16),
                   jax.ShapeDtypeStruct((batch * seq, wide), BF16)],
        scratch_shapes=[pltpu.VMEM((1, LANES), F32)],
        compiler_params=_cparams(("arbitrary", "arbitrary")),
        name="fox_bias_prompt",
    )(small, bf_row, tri)


def _fox_bias_sample_kernel(sm_ref, bf_ref, sel_ref, lf_ref, fn_ref):
    lf = _log_sigmoid(sm_ref[...] + bf_ref[...])
    lf_ref[...] = lf[:, FB_LANE:FB_LANE + FOX_HEADS]
    cum = _dot_sel(sel_ref[...], lf)
    fn_ref[...] = cum[:, FB_LANE:FB_LANE + FOX_HEADS]


def _fox_bias_sample(small, bf_row, t):
    rows = small.shape[0]
    r = np.arange(rows)
    sel = ((r[:, None] // t) == (r[None, :] // t)) & (r[None, :] <= r[:, None])
    sel = jnp.asarray(sel.astype(np.float32), BF16)
    full = lambda shape: pl.BlockSpec(shape, lambda i: tuple(0 for _ in shape))
    return pl.pallas_call(
        _fox_bias_sample_kernel,
        grid=(1,),
        in_specs=[full((rows, LANES)), full((1, LANES)), full((rows, rows))],
        out_specs=[full((rows, FOX_HEADS)), full((rows, FOX_HEADS))],
        out_shape=[jax.ShapeDtypeStruct((rows, FOX_HEADS), F32),
                   jax.ShapeDtypeStruct((rows, FOX_HEADS), F32)],
        compiler_params=_cparams(("arbitrary",)),
        name="fox_bias_sample",
    )(small, bf_row, sel)


FOX_HEADS_PER_STEP = 8


def _fox_prompt_kernel(q_ref, k_ref, v_ref, fq_ref, fk_ref, o_ref, *, tq, tk):
    hb = FOX_HEADS_PER_STEP
    i = pl.program_id(2)
    rowi = lax.broadcasted_iota(jnp.int32, (tq, tk), 0)
    coli = lax.broadcasted_iota(jnp.int32, (tq, tk), 1)
    qs = []
    for hh in range(hb):
        hs = slice(hh * FOX_DH, (hh + 1) * FOX_DH)
        q = (q_ref[:, hs].astype(F32) * (FOX_DH ** -0.5 * LOG2E)).astype(BF16)
        qs.append(jnp.concatenate([q, fq_ref[:, hs]], axis=1))

    def step(j, carry, masked):
        ks = pl.ds(pl.multiple_of(j * tk, tk), tk)
        out = []
        for hh in range(hb):
            hs = slice(hh * FOX_DH, (hh + 1) * FOX_DH)
            m, l, acc = carry[hh]
            s = _dot_nt(qs[hh], jnp.concatenate([k_ref[ks, hs], fk_ref[ks, hs]], axis=1))
            if masked:
                s = jnp.where(rowi + i * tq >= coli + j * tk, s, NEG)
            m_new = jnp.maximum(m, jnp.max(s, axis=-1, keepdims=True))
            p = jnp.exp2(s - m_new)
            alpha = jnp.exp2(m - m_new)
            l = alpha * l + jnp.sum(p, axis=-1, keepdims=True)
            acc = alpha * acc + _dot(p.astype(BF16), v_ref[ks, hs])
            out.append((m_new, l, acc))
        return tuple(out)

    init = tuple((jnp.full((tq, 1), NEG, F32), jnp.zeros((tq, 1), F32), jnp.zeros((tq, FOX_DH), F32))
                 for _ in range(hb))
    nfull = (i * tq) // tk
    carry = lax.fori_loop(0, nfull, lambda j, c: step(j, c, False), init)
    for d in range(pl.cdiv(tq, tk)):
        carry = step(nfull + d, carry, True)
    for hh in range(hb):
        m, l, acc = carry[hh]
        o_ref[:, hh * FOX_DH:(hh + 1) * FOX_DH] = (acc / l).astype(o_ref.dtype)


def _fox_prompt(qb, kb, vb, fq, fk, batch, seq, tq=FOX_Q_TILE, tk=FOX_K_TILE):
    nq = seq // tq
    hb = FOX_HEADS_PER_STEP
    wb = hb * FOX_DH
    return pl.pallas_call(
        functools.partial(_fox_prompt_kernel, tq=tq, tk=tk),
        grid=(batch, FOX_HEADS // hb, nq),
        in_specs=[pl.BlockSpec((tq, wb), lambda b, h, i: (b * nq + i, h)),
                  pl.BlockSpec((seq, wb), lambda b, h, i: (b, h)),
                  pl.BlockSpec((seq, wb), lambda b, h, i: (b, h)),
                  pl.BlockSpec((tq, wb), lambda b, h, i: (b * nq + i, h)),
                  pl.BlockSpec((seq, wb), lambda b, h, i: (b, h))],
        out_specs=pl.BlockSpec((tq, wb), lambda b, h, i: (b * nq + i, h)),
        out_shape=jax.ShapeDtypeStruct((batch * seq, FOX_W), BF16),
        compiler_params=_cparams(("arbitrary", "arbitrary", "arbitrary")),
        name="fox_prompt",
    )(qb, kb, vb, fq, fk)


PAGES_PER_STEP = 16
ROWS8 = 8
QROWS = FOX_HEADS * ROWS8
PAGE_FLAT = PAGE * FOX_HEADS


def _dot_sel_rhs(x, w01):
    hi, mid, lo = _split3(x)
    m = x.shape[0]
    stacked = jnp.concatenate([hi.astype(F32), mid.astype(F32), lo.astype(F32)], axis=0).astype(BF16)
    r = _dot(stacked, w01)
    return r[0:m] + r[m:2 * m] + r[2 * m:3 * m]


def _fox_sample_kernel(pt_ref, q_ref, kn_ref, vn_ref, fn_ref, fnrow_ref, madd_ref, maddn_ref, usuf_ref, pfx_ref,
                       hsum_ref, pexp_ref, hmask_ref, *refs, t):
    g = PAGES_PER_STEP
    k_refs, v_refs, lf_refs = refs[0:g], refs[g:2 * g], refs[2 * g:3 * g]
    o_ref = refs[3 * g]
    q_sc, m_sc, l_sc, acc_sc, carry_sc = refs[3 * g + 1:]
    j = pl.program_id(1)
    fn_t = fn_ref[...][:, 0:1]

    @pl.when(j == 0)
    def _():
        q_sc[...] = jnp.zeros(q_sc.shape, F32)
        for h in range(FOX_HEADS):
            q_sc[h * ROWS8:h * ROWS8 + t, :] = q_ref[:, h * FOX_DH:(h + 1) * FOX_DH] * (FOX_DH ** -0.5)
        carry_sc[...] = jnp.zeros(carry_sc.shape, F32)
        pad = jnp.zeros((LANES - t * FOX_HEADS, FOX_DH), F32)
        kn = jnp.concatenate([kn_ref[...].reshape(t * FOX_HEADS, FOX_DH), pad], axis=0)
        vn = jnp.concatenate([vn_ref[...].reshape(t * FOX_HEADS, FOX_DH), pad], axis=0)
        s = _dot_nt(q_sc[...], kn) + fn_t - fnrow_ref[...] + maddn_ref[...]
        m = jnp.max(s, axis=-1, keepdims=True)
        p = jnp.exp(s - m)
        m_sc[...] = jnp.broadcast_to(m, m_sc.shape)
        l_sc[...] = jnp.broadcast_to(jnp.sum(p, axis=-1, keepdims=True), l_sc.shape)
        acc_sc[...] = _dot(p, vn)

    lf = jnp.concatenate([lf_refs[gi][...] for gi in range(g)], axis=0)
    tot = jnp.broadcast_to(jnp.sum(lf, axis=-1, keepdims=True), lf.shape)
    carry = carry_sc[...]
    r_hk = (_dot_sel_rhs(lf, usuf_ref[...]) + _dot_sel(pfx_ref[...], tot)
            + jnp.concatenate([carry] * g, axis=0))
    carry_sc[...] = carry + _dot_sel(hsum_ref[...], tot)
    spread = _dot_sel_rhs(r_hk, pexp_ref[...]) * hmask_ref[...]
    r_flat = jnp.sum(spread.reshape(g, FOX_HEADS, PAGE_FLAT), axis=1)
    q = q_sc[...]
    madd = madd_ref[...] + fn_t
    m_old = m_sc[...]
    m_new = m_old
    s_list = []
    for gi in range(g):
        bias = madd + r_flat[gi:gi + 1, :]
        s = _dot_nt(q, k_refs[gi][...].reshape(PAGE_FLAT, FOX_DH)) + bias
        s_list.append(s)
        m_new = jnp.maximum(m_new, jnp.max(s, axis=-1, keepdims=True))
    alpha = jnp.exp(m_old - m_new)
    l = alpha * l_sc[...]
    acc = alpha * acc_sc[...]
    m_col = m_new[:, 0:1]
    for gi in range(g):
        p = jnp.exp(s_list[gi] - m_col)
        l = l + jnp.sum(p, axis=-1, keepdims=True)
        acc = acc + _dot(p, v_refs[gi][...].reshape(PAGE_FLAT, FOX_DH))
    m_sc[...] = m_new
    l_sc[...] = l
    acc_sc[...] = acc

    @pl.when(j == pl.num_programs(1) - 1)
    def _():
        o = acc / l
        for h in range(FOX_HEADS):
            o_ref[:, h * FOX_DH:(h + 1) * FOX_DH] = o[h * ROWS8:h * ROWS8 + t, :].astype(o_ref.dtype)


def _fox_sample_tables(t):
    row = np.arange(QROWS)[:, None]
    col = np.arange(PAGE_FLAT)[None, :]
    madd = np.where((row // ROWS8) == (col % FOX_HEADS), 0.0, NEG).astype(np.float32)
    coln = np.arange(LANES)[None, :]
    ok = (coln < t * FOX_HEADS) & ((row // ROWS8) == (coln % FOX_HEADS)) & ((coln // FOX_HEADS) <= (row % ROWS8))
    maddn = np.where(ok, 0.0, NEG).astype(np.float32)
    key = np.arange(PAGE)
    usuf = (key[:, None] > key[None, :]).astype(np.float32)
    gh = np.arange(PAGES_PER_STEP * FOX_HEADS)
    same_head = (gh[:, None] % FOX_HEADS) == (gh[None, :] % FOX_HEADS)
    pfx = (same_head & ((gh[None, :] // FOX_HEADS) < (gh[:, None] // FOX_HEADS))).astype(np.float32)
    hsum = (np.arange(FOX_HEADS)[:, None] == (gh[None, :] % FOX_HEADS)).astype(np.float32)
    pexp = (key[:, None] == (col // FOX_HEADS)).astype(np.float32)
    hmask = ((gh[:, None] % FOX_HEADS) == (col % FOX_HEADS)).astype(np.float32)
    return (jnp.asarray(madd), jnp.asarray(maddn), jnp.asarray(usuf, BF16), jnp.asarray(pfx, BF16),
            jnp.asarray(hsum, BF16), jnp.asarray(pexp, BF16), jnp.asarray(hmask))


def _fox_sample(page_table, q, kn, vn, fn, cache_k, cache_v, cache_lf, t):
    batch, npages = page_table.shape
    g = PAGES_PER_STEP
    nsteps = npages // g
    tables = _fox_sample_tables(t)
    fn_rows = jnp.pad(fn.transpose(0, 2, 1), ((0, 0), (0, 0), (0, ROWS8 - t))).reshape(batch, QROWS, 1)
    fn_rows = jnp.broadcast_to(fn_rows, (batch, QROWS, LANES))
    fn_cols = jnp.pad(fn.reshape(batch, 1, t * FOX_HEADS), ((0, 0), (0, 0), (0, LANES - t * FOX_HEADS)))

    def page_map5(gi):
        return lambda b, j, pt: (0, pt[b, npages - 1 - (j * g + gi)], 0, 0, 0)

    def page_map3(gi):
        return lambda b, j, pt: (pt[b, npages - 1 - (j * g + gi)], 0, 0)

    seq3 = lambda b, j, pt: (b, 0, 0)
    seq4 = lambda b, j, pt: (b, 0, 0, 0)
    const2 = lambda b, j, pt: (0, 0)
    in_specs = [pl.BlockSpec((None, t, FOX_W), seq3),
                pl.BlockSpec((None, t, FOX_HEADS, FOX_DH), seq4),
                pl.BlockSpec((None, t, FOX_HEADS, FOX_DH), seq4),
                pl.BlockSpec((None, QROWS, LANES), seq3),
                pl.BlockSpec((None, 1, LANES), seq3)]
    in_specs += [pl.BlockSpec(tb.shape, const2) for tb in tables]
    in_specs += [pl.BlockSpec((None, None, PAGE, FOX_HEADS, FOX_DH), page_map5(gi)) for gi in range(g)]
    in_specs += [pl.BlockSpec((None, None, PAGE, FOX_HEADS, FOX_DH), page_map5(gi)) for gi in range(g)]
    in_specs += [pl.BlockSpec((None, FOX_HEADS, PAGE), page_map3(gi)) for gi in range(g)]
    grid_spec = pltpu.PrefetchScalarGridSpec(
        num_scalar_prefetch=1,
        grid=(batch, nsteps),
        in_specs=in_specs,
        out_specs=pl.BlockSpec((None, t, FOX_W), seq3),
        scratch_shapes=[pltpu.VMEM((QROWS, FOX_DH), F32),
                        pltpu.VMEM((QROWS, LANES), F32),
                        pltpu.VMEM((QROWS, LANES), F32),
                        pltpu.VMEM((QROWS, FOX_DH), F32),
                        pltpu.VMEM((FOX_HEADS, PAGE), F32)],
    )
    return pl.pallas_call(
        functools.partial(_fox_sample_kernel, t=t),
        grid_spec=grid_spec,
        out_shape=jax.ShapeDtypeStruct((batch, t, FOX_W), F32),
        compiler_params=_cparams(("arbitrary", "arbitrary")),
        name="fox_sample",
    )(page_table, q, kn, vn, fn_rows, fn_cols, *tables,
      *([cache_k] * g), *([cache_v] * g), *([cache_lf] * g))


def _merge_kernel(oa_ref, ob_ref, wa_ref, wb_ref, ga_ref, gb_ref, o_ref):
    ua = _dot(oa_ref[...], wa_ref[...])
    ub = _dot(ob_ref[...], wb_ref[...])
    ga = jax.nn.sigmoid(ga_ref[...].astype(F32))
    gb = jax.nn.sigmoid(gb_ref[...].astype(F32))
    o_ref[...] = (ga * ua + gb * ub).astype(o_ref.dtype)


def _merge(oa, ob, wa, wb, gates, tm, tn=1024):
    m = oa.shape[0]
    d = wa.shape[1]
    nj = d // tn
    return pl.pallas_call(
        _merge_kernel,
        grid=(nj, m // tm),
        in_specs=[pl.BlockSpec((tm, GLA_VW), lambda j, i: (i, 0)),
                  pl.BlockSpec((tm, FOX_W), lambda j, i: (i, 0)),
                  pl.BlockSpec((GLA_VW, tn), lambda j, i: (0, j)),
                  pl.BlockSpec((FOX_W, tn), lambda j, i: (0, j)),
                  pl.BlockSpec((tm, tn), lambda j, i: (i, j)),
                  pl.BlockSpec((tm, tn), lambda j, i: (i, nj + j))],
        out_specs=pl.BlockSpec((tm, tn), lambda j, i: (i, j)),
        out_shape=jax.ShapeDtypeStruct((m, d), BF16),
        compiler_params=_cparams(("arbitrary", "arbitrary")),
        name="merge",
    )(oa, ob, wa, wb, gates, gates)


ROUTER_GROUP_LANE = N_EXPERTS


def _route(logits):
    lane_i = lax.broadcasted_iota(jnp.int32, logits.shape, 1)
    lane = lane_i.astype(F32)
    grp_of_lane = (lane_i >> 2).astype(F32)
    big = float(LANES)
    is_grp = (lane_i >= ROUTER_GROUP_LANE) & (lane_i < ROUTER_GROUP_LANE + N_GROUPS)
    gl = jnp.where(is_grp, logits, NEG)
    gmax = jnp.max(gl, axis=-1, keepdims=True)
    g_idx = jnp.min(jnp.where(is_grp & (gl == gmax), lane - ROUTER_GROUP_LANE, big), axis=-1, keepdims=True)
    g_w = 1.0 / jnp.sum(jnp.where(is_grp, jnp.exp(gl - gmax), 0.0), axis=-1, keepdims=True)
    in_grp = (lane_i < N_EXPERTS) & (grp_of_lane == g_idx)
    e1 = jnp.where(in_grp, logits, NEG)
    v1 = jnp.max(e1, axis=-1, keepdims=True)
    i1 = jnp.min(jnp.where(in_grp & (e1 == v1), lane, big), axis=-1, keepdims=True)
    rest = in_grp & (lane != i1)
    e2 = jnp.where(rest, logits, NEG)
    v2 = jnp.max(e2, axis=-1, keepdims=True)
    i2 = jnp.min(jnp.where(rest & (e2 == v2), lane, big), axis=-1, keepdims=True)
    r = jnp.exp(v2 - v1)
    w1 = g_w / (1.0 + r)
    w2 = g_w * r / (1.0 + r)
    grp_onehot = jnp.where(is_grp & (lane - ROUTER_GROUP_LANE == g_idx), 1.0, 0.0)
    return jnp.where(lane == i1, w1, 0.0) + jnp.where(lane == i2, w2, 0.0) + grp_onehot


HX_W = D_MODEL + LANES


def _outproj_kernel(mg_ref, w_ref, x_ref, gt_ref, g2_ref, sh_ref, sc_ref, wr_ref, br_ref, x1_ref, hx_ref):
    x1 = x_ref[...] + gt_ref[...] * _dot(mg_ref[...], w_ref[...])
    x1_ref[...] = x1
    h2 = _rms(x1) * g2_ref[...] * (1.0 + sc_ref[...]) + sh_ref[...]
    hx_ref[:, 0:D_MODEL] = h2
    hx_ref[:, D_MODEL:HX_W] = _route(_dot3(h2, wr_ref[...]) + br_ref[...])


def _outproj(merged, w_out, x, mod, g2, w_router, b_router, tm, tiles_per_batch):
    m, d = x.shape
    rb = mod.shape[2]
    const2 = lambda i: (0, 0)
    return pl.pallas_call(
        _outproj_kernel,
        grid=(m // tm,),
        in_specs=[pl.BlockSpec((tm, d), lambda i: (i, 0)),
                  pl.BlockSpec((d, d), const2),
                  pl.BlockSpec((tm, d), lambda i: (i, 0)),
                  _mod_spec(2, rb, d, tiles_per_batch),
                  pl.BlockSpec((1, d), const2),
                  _mod_spec(3, rb, d, tiles_per_batch),
                  _mod_spec(4, rb, d, tiles_per_batch),
                  pl.BlockSpec((d, LANES), const2),
                  pl.BlockSpec((1, LANES), const2)],
        out_specs=[pl.BlockSpec((tm, d), lambda i: (i, 0)),
                   pl.BlockSpec((tm, HX_W), lambda i: (i, 0))],
        out_shape=[jax.ShapeDtypeStruct((m, d), F32),
                   jax.ShapeDtypeStruct((m, HX_W), F32)],
        compiler_params=_cparams(("arbitrary",)),
        name="outproj",
    )(merged, w_out, x, mod, g2, mod, mod, w_router, b_router)


MOE_TILE = 512
PLAN_TILE = 512


def _moe_plan_kernel(r_ref, tri_ref, excl_ref, pos_ref, tg_ref, nu_ref, cnt_sc, off_sc, run_sc):
    ph = pl.program_id(0)
    n = pl.program_id(1)
    lane = lax.broadcasted_iota(jnp.int32, (1, LANES), 1)
    lane_t = lax.broadcasted_iota(jnp.int32, r_ref.shape, 1)
    is_grp = (lane_t >= ROUTER_GROUP_LANE) & (lane_t < ROUTER_GROUP_LANE + N_GROUPS)
    g4 = jnp.where(is_grp, r_ref[...], 0.0)

    @pl.when((ph == 0) & (n == 0))
    def _():
        cnt_sc[...] = jnp.zeros(cnt_sc.shape, F32)

    @pl.when(ph == 0)
    def _():
        cnt_sc[...] += jnp.sum(g4, axis=0, keepdims=True)

    @pl.when((ph == 1) & (n == 0))
    def _():
        padded = jnp.floor((cnt_sc[...] + (MOE_TILE - 1)) * (1.0 / MOE_TILE)) * MOE_TILE
        off = _dot_sel_rhs(jnp.broadcast_to(padded, (8, LANES)), excl_ref[...])[0:1]
        off_sc[...] = off
        run_sc[...] = jnp.zeros(run_sc.shape, F32)
        end = off + padded
        tile_start = lane.astype(F32) * MOE_TILE
        tg = jnp.zeros((1, LANES), F32)
        for g in range(N_GROUPS):
            end_g = jnp.sum(jnp.where(lane == ROUTER_GROUP_LANE + g, end, 0.0), axis=-1, keepdims=True)
            tg = tg + jnp.where(end_g <= tile_start, 1.0, 0.0)
        tg_ref[...] = jnp.minimum(tg, N_GROUPS - 1.0).astype(jnp.int32)
        total = jnp.sum(jnp.where(lane == ROUTER_GROUP_LANE + N_GROUPS - 1, end, 0.0), axis=-1, keepdims=True)
        nu_ref[...] = jnp.broadcast_to(total * (1.0 / MOE_TILE), (1, LANES)).astype(jnp.int32)

    @pl.when(ph == 1)
    def _():
        rank = _dot(tri_ref[...], g4.astype(BF16)) + run_sc[...]
        run_sc[...] += jnp.sum(g4, axis=0, keepdims=True)
        posv = g4 * (off_sc[...] + rank)
        hi, mid, lo = _split3(posv)
        ones = jnp.ones((8, LANES), BF16)
        row = _dot_nt(ones, hi) + _dot_nt(ones, mid) + _dot_nt(ones, lo)
        pos_ref[...] = row[0:1].astype(jnp.int32)


def _moe_plan(hx, ntiles):
    t = hx.shape[0]
    nblk = t // PLAN_TILE
    tri = jnp.asarray(np.tril(np.ones((PLAN_TILE, PLAN_TILE), np.float32), -1), BF16)
    excl = jnp.asarray(np.triu(np.ones((LANES, LANES), np.float32), 1), BF16)
    assert ntiles <= LANES
    pos, tg, nu = pl.pallas_call(
        _moe_plan_kernel,
        grid=(2, nblk),
        in_specs=[pl.BlockSpec((PLAN_TILE, LANES), lambda ph, n: (n, D_MODEL // LANES)),
                  pl.BlockSpec((PLAN_TILE, PLAN_TILE), lambda ph, n: (0, 0)),
                  pl.BlockSpec((LANES, LANES), lambda ph, n: (0, 0))],
        out_specs=[pl.BlockSpec((1, PLAN_TILE), lambda ph, n: (0, n * ph)),
                   pl.BlockSpec((1, LANES), lambda ph, n: (0, 0)),
                   pl.BlockSpec((1, LANES), lambda ph, n: (0, 0))],
        out_shape=[jax.ShapeDtypeStruct((1, t), jnp.int32),
                   jax.ShapeDtypeStruct((1, LANES), jnp.int32),
                   jax.ShapeDtypeStruct((1, LANES), jnp.int32)],
        scratch_shapes=[pltpu.VMEM((1, LANES), F32), pltpu.VMEM((1, LANES), F32), pltpu.VMEM((1, LANES), F32)],
        compiler_params=_cparams(("arbitrary", "arbitrary")),
        name="moe_plan",
    )(hx, tri, excl)
    return pos.reshape(t), tg.reshape(LANES), nu.reshape(LANES)[0:1]


def _row_copy(src_ref, src_row, dst_ref, dst_row, sem):
    return pltpu.make_async_copy(src_ref.at[pl.ds(src_row, 1)], dst_ref.at[pl.ds(dst_row, 1)], sem)


def _moe_grouped_kernel(pos_ref, tg_ref, nu_ref, hx_ref, wg_ref, wu_ref, wd_ref, ys_ref,
                        buf, sem, src_sc, xb_sc, *, t):
    i = pl.program_id(0)
    ei = pl.program_id(1)
    n_used = nu_ref[0]
    used = i < n_used
    slot = i % 2

    def fetch(tile, sl):
        def body(r, c):
            _row_copy(hx_ref, src_sc[tile * MOE_TILE + r], buf.at[sl], r, sem.at[sl]).start()
            return c
        lax.fori_loop(0, MOE_TILE, body, 0, unroll=8)

    @pl.when((i == 0) & (ei == 0))
    def _():
        def clear(r, c):
            src_sc[r] = 0
            return c

        def invert(tk, c):
            src_sc[pos_ref[tk]] = tk
            return c

        lax.fori_loop(0, src_sc.shape[0], clear, 0, unroll=8)
        lax.fori_loop(0, t, invert, 0, unroll=8)
        fetch(0, 0)

    @pl.when(used & (ei == 0))
    def _():
        def wait(r, c):
            _row_copy(hx_ref, 0, buf.at[slot], r, sem.at[slot]).wait()
            return c
        lax.fori_loop(0, MOE_TILE, wait, 0, unroll=8)
        xb_sc[...] = buf[slot, :, 0:D_MODEL].astype(BF16)

    @pl.when(ei == 0)
    def _():
        ys_ref[...] = jnp.zeros(ys_ref.shape, F32)

    def expert_step():
        h = xb_sc[...]
        a = _dot(h, wg_ref[...].astype(BF16))
        u = _dot(h, wu_ref[...].astype(BF16))
        lane = lax.broadcasted_iota(jnp.int32, (MOE_TILE, LANES), 1)
        e = tg_ref[i] * EXP_PER_GROUP + ei
        cw = jnp.sum(jnp.where(lane == e, buf[slot, :, D_MODEL:HX_W], 0.0), axis=-1, keepdims=True)
        hid = (_silu(a) * u * cw).astype(BF16)
        ys_ref[...] += _dot(hid, wd_ref[...].astype(BF16))

    has_next = i + 1 < n_used

    @pl.when(used & has_next)
    def _():
        per_step = MOE_TILE // EXP_PER_GROUP
        nslot = (i + 1) % 2
        for r in range(per_step):
            row = ei * per_step + r
            _row_copy(hx_ref, src_sc[(i + 1) * MOE_TILE + row], buf.at[nslot], row, sem.at[nslot]).start()
        expert_step()

    @pl.when(used & jnp.logical_not(has_next))
    def _():
        expert_step()


def _moe_grouped(hx, pos, tg, nu, wg, wu, wd, rows_out):
    t = hx.shape[0]
    ne, d, de = wg.shape
    ntiles = rows_out // MOE_TILE

    def wmap(i, ei, pos_ref, tg_ref, nu_ref):
        e = jnp.where(i < nu_ref[0], tg_ref[i] * EXP_PER_GROUP + ei, ne - 1)
        return (e, 0, 0)

    grid_spec = pltpu.PrefetchScalarGridSpec(
        num_scalar_prefetch=3,
        grid=(ntiles, EXP_PER_GROUP),
        in_specs=[pl.BlockSpec(memory_space=pl.ANY),
                  pl.BlockSpec((None, d, de), wmap),
                  pl.BlockSpec((None, d, de), wmap),
                  pl.BlockSpec((None, de, d), wmap)],
        out_specs=pl.BlockSpec((MOE_TILE, d), lambda i, ei, pos_ref, tg_ref, nu_ref: (i, 0)),
        scratch_shapes=[pltpu.VMEM((2, MOE_TILE, HX_W), F32), pltpu.SemaphoreType.DMA((2,)),
                        pltpu.SMEM((rows_out,), jnp.int32),
                        pltpu.VMEM((MOE_TILE, d), BF16)],
    )
    return pl.pallas_call(
        functools.partial(_moe_grouped_kernel, t=t),
        grid_spec=grid_spec,
        out_shape=jax.ShapeDtypeStruct((rows_out, d), F32),
        compiler_params=_cparams(("arbitrary", "arbitrary")),
        name="moe_grouped",
    )(pos, tg, nu, hx, wg, wu, wd)


COMBINE_ROWS = 512


def _moe_combine_kernel(pos_ref, ys_ref, x1_ref, gt_ref, gf_ref, y_ref, buf, sem):
    i = pl.program_id(0)
    n = pl.num_programs(0)

    def fetch(tile, slot):
        def body(r, c):
            _row_copy(ys_ref, pos_ref[tile * COMBINE_ROWS + r], buf.at[slot], r, sem.at[slot]).start()
            return c
        lax.fori_loop(0, COMBINE_ROWS, body, 0, unroll=8)

    @pl.when(i == 0)
    def _():
        fetch(0, 0)

    @pl.when(i + 1 < n)
    def _():
        fetch(i + 1, (i + 1) % 2)

    slot = i % 2

    def wait(r, c):
        _row_copy(ys_ref, pos_ref[i * COMBINE_ROWS + r], buf.at[slot], r, sem.at[slot]).wait()
        return c

    lax.fori_loop(0, COMBINE_ROWS, wait, 0, unroll=8)
    x2 = x1_ref[...] + gt_ref[...] * buf[slot]
    y_ref[...] = _rms(x2) * gf_ref[...]


def _moe_combine(ys, pos, x1, mod, g_final, tiles_per_batch):
    m, d = x1.shape
    rb = mod.shape[2]
    grid_spec = pltpu.PrefetchScalarGridSpec(
        num_scalar_prefetch=1,
        grid=(m // COMBINE_ROWS,),
        in_specs=[pl.BlockSpec(memory_space=pl.ANY),
                  pl.BlockSpec((COMBINE_ROWS, d), lambda i, pos_ref: (i, 0)),
                  pl.BlockSpec((None, None, rb, d), lambda i, pos_ref: (5, i // tiles_per_batch, 0, 0)),
                  pl.BlockSpec((1, d), lambda i, pos_ref: (0, 0))],
        out_specs=pl.BlockSpec((COMBINE_ROWS, d), lambda i, pos_ref: (i, 0)),
        scratch_shapes=[pltpu.VMEM((2, COMBINE_ROWS, d), F32), pltpu.SemaphoreType.DMA((2,))],
    )
    return pl.pallas_call(
        _moe_combine_kernel,
        grid_spec=grid_spec,
        out_shape=jax.ShapeDtypeStruct((m, d), F32),
        compiler_params=_cparams(("arbitrary",)),
        name="moe_combine",
    )(pos, ys, x1, mod, g_final)


def _moe_kernel(h_ref, cmb_ref, wg_ref, wu_ref, wd_ref, x1_ref, gt_ref, gf_ref, y_ref, acc_ref):
    e = pl.program_id(1)

    @pl.when(e == 0)
    def _():
        acc_ref[...] = jnp.zeros(acc_ref.shape, F32)

    h = h_ref[...]
    a = _dot(h, wg_ref[...].astype(BF16))
    u = _dot(h, wu_ref[...].astype(BF16))
    lane = lax.broadcasted_iota(jnp.int32, cmb_ref.shape, 1)
    cw = jnp.sum(jnp.where(lane == e, cmb_ref[...], 0.0), axis=-1, keepdims=True)
    hid = (_silu(a) * u * cw).astype(BF16)
    acc_ref[...] += _dot(hid, wd_ref[...].astype(BF16))

    @pl.when(e == pl.num_programs(1) - 1)
    def _():
        x2 = x1_ref[...] + gt_ref[...] * acc_ref[...]
        y_ref[...] = _rms(x2) * gf_ref[...]


def _moe(h2, cmb, wg, wu, wd, x1, mod, g_final, tm, tiles_per_batch):
    m, d = x1.shape
    rb = mod.shape[2]
    ne, _, de = wg.shape
    return pl.pallas_call(
        _moe_kernel,
        grid=(m // tm, ne),
        in_specs=[pl.BlockSpec((tm, d), lambda i, e: (i, 0)),
                  pl.BlockSpec((tm, LANES), lambda i, e: (i, 0)),
                  pl.BlockSpec((None, d, de), lambda i, e: (e, 0, 0)),
                  pl.BlockSpec((None, d, de), lambda i, e: (e, 0, 0)),
                  pl.BlockSpec((None, de, d), lambda i, e: (e, 0, 0)),
                  pl.BlockSpec((tm, d), lambda i, e: (i, 0)),
                  pl.BlockSpec((None, None, rb, d), lambda i, e: (5, i // tiles_per_batch, 0, 0)),
                  pl.BlockSpec((1, d), lambda i, e: (0, 0))],
        out_specs=pl.BlockSpec((tm, d), lambda i, e: (i, 0)),
        out_shape=jax.ShapeDtypeStruct((m, d), F32),
        scratch_shapes=[pltpu.VMEM((tm, d), F32)],
        compiler_params=_cparams(("arbitrary", "arbitrary")),
        name="moe",
    )(h2, cmb, wg, wu, wd, x1, mod, g_final)


def _prep_weights(w_ada, b_ada, g_norm1, g_norm2, g_final, w_in, w_a2, b_a, b_f, g_gla_norm, w_up_a, w_up_b, w_out,
                  w_grp, b_grp, w_exp, b_exp, w_gate_e, w_up_e, w_down_e):
    wt = w_in.reshape(w_in.shape[1:]).T
    d = D_MODEL
    o_lra = 2 * GLA_KW + 2 * GLA_VW
    o_fox = o_lra + GLA_RANK
    o_fb = o_fox + 3 * FOX_W
    o_g = o_fb + FOX_HEADS
    pad = jnp.zeros((LANES - GLA_RANK - FOX_HEADS, d), F32)
    wall, mask = _gla_tables()
    wall_s, mask_s = _gla_tables(SAMPLE_LEVELS)
    bf_row =jnp.zeros((1, LANES), F32).at[0, FB_LANE:FB_LANE + FOX_HEADS].set(b_f[0])
    w_router = jnp.concatenate([w_exp[0], w_grp[0], jnp.zeros((d, LANES - N_EXPERTS - N_GROUPS), F32)], axis=1)
    b_router = jnp.concatenate([b_exp[0], b_grp[0], jnp.zeros((LANES - N_EXPERTS - N_GROUPS,), F32)])[None, :]
    return dict(
        w_ada=w_ada.reshape(w_ada.shape[1:]), b_ada=b_ada[0][None, :],
        g1=g_norm1[0][None, :], g2=g_norm2[0][None, :], gf=g_final[None, :],
        w_in_t=wt,
        w_small_t=jnp.concatenate([wt[o_lra:o_lra + GLA_RANK], wt[o_fb:o_fb + FOX_HEADS], pad], axis=0),
        wall=jnp.asarray(wall, BF16), mask=jnp.asarray(mask, F32),
        wall_s=jnp.asarray(wall_s, BF16), mask_s=jnp.asarray(mask_s, F32),
        w_a2=jnp.concatenate([w_a2[0], jnp.zeros((LANES - GLA_RANK, GLA_KW), F32)], axis=0), b_a=b_a[0][None, :], bf_row=bf_row, g_gla=g_gla_norm[0][None, :],
        w_up_a=w_up_a[0].astype(BF16), w_up_b=w_up_b[0].astype(BF16), w_out=w_out[0].astype(BF16),
        w_router=w_router, b_router=b_router,
        wg=w_gate_e.reshape(w_gate_e.shape[1:]), wu=w_up_e.reshape(w_up_e.shape[1:]),
        wd=w_down_e.reshape(w_down_e.shape[1:]),
    )


def _project(h, p, tm):
    tn = PROJ_COL_TILE
    wt = p["w_in_t"]
    n_gla = 2 * GLA_KW + 2 * GLA_VW
    c_q = n_gla + GLA_RANK
    c_g = c_q + 3 * FOX_W + FOX_HEADS
    (pg,) = _mm_wt(h, wt, 0, n_gla, tm, tn)
    (qb,) = _mm_wt(h, wt, c_q, FOX_W, tm, tn, (BF16,))
    kb, kb16 = _mm_heads(h, wt, c_q + FOX_W, min(tm, HEADS_ROW_TILE))
    vb, vb16 = _mm_heads(h, wt, c_q + 2 * FOX_W, min(tm, HEADS_ROW_TILE))
    (gates,) = _mm_wt(h, wt, c_g, 2 * D_MODEL, tm, tn, (BF16,))
    (small,) = _mm_wt(h, p["w_small_t"], 0, LANES, tm, LANES)
    return pg, qb, kb, kb16, vb, vb16, gates, small


def _tail(x, oa, ob, gates, mod, p, tm, tiles_per_batch, grouped):
    merged = _merge(oa, ob, p["w_up_a"], p["w_up_b"], gates, tm)
    tmo = min(tm, OUTPROJ_ROW_TILE)
    x1, hx = _outproj(merged, p["w_out"], x, mod, p["g2"], p["w_router"], p["b_router"],
                      tmo, tiles_per_batch * (tm // tmo))
    if not grouped:
        tmm = min(tm, DENSE_MOE_ROW_TILE)
        return _moe(hx[:, 0:D_MODEL].astype(BF16), hx[:, D_MODEL:HX_W], p["wg"], p["wu"], p["wd"], x1, mod, p["gf"],
                    tmm, tiles_per_batch * (tm // tmm))
    rows_out = x.shape[0] + N_GROUPS * MOE_TILE
    pos, tg, nu = _moe_plan(hx, rows_out // MOE_TILE)
    ys = _moe_grouped(hx, pos, tg, nu, p["wg"], p["wu"], p["wd"], rows_out)
    return _moe_combine(ys, pos, x1, mod, p["gf"], tiles_per_batch * (tm // COMBINE_ROWS))


def kernel(x_prompt, x_sample, cache_k, cache_v, cache_logf, state_gla, page_table, c_prompt, c_sample, w_ada, b_ada,
           g_norm1, g_norm2, g_final, w_in, w_a2, b_a, b_f, g_gla_norm, w_up_a, w_up_b, w_out, w_grp, b_grp, w_exp,
           b_exp, w_gate_e, w_up_e, w_down_e):
    p = _prep_weights(w_ada, b_ada, g_norm1, g_norm2, g_final, w_in, w_a2, b_a, b_f, g_gla_norm, w_up_a, w_up_b,
                      w_out, w_grp, b_grp, w_exp, b_exp, w_gate_e, w_up_e, w_down_e)
    bp, seq, d = x_prompt.shape
    bs, t, _ = x_sample.shape
    assert t <= 2 ** SAMPLE_LEVELS and t <= ROWS8

    mod = _adaln(jnp.concatenate([c_prompt, c_sample], axis=0), p["w_ada"], p["b_ada"])
    mod_p = mod[:bp].reshape(bp, 6, 1, d).transpose(1, 0, 2, 3)
    mod_s = jnp.repeat(mod[bp:].reshape(bs, 6, d), t, axis=0).transpose(1, 0, 2)[:, None]

    tm = ROW_TILE
    tpb = seq // tm
    xp = x_prompt.reshape(bp * seq, d)
    hp = _normmod(xp, p["g1"], mod_p, tm, tpb)
    pg, qb, kb, kb16, vb, vb16, gates, small = _project(hp, p, tm)
    s0 = jnp.zeros((bp, GLA_HEADS, GLA_DK, GLA_DV), F32)
    oa, s_p = _gla_prompt(pg, small, s0, p["wall"], p["mask"], p["w_a2"], p["b_a"], p["g_gla"], bp, seq)
    lf_p, fq, fk = _fox_bias_prompt(small, p["bf_row"], bp, seq)
    ob = _fox_prompt(qb, kb16, vb16, fq, fk, bp, seq)
    y_p = _tail(xp, oa, ob, gates, mod_p, p, tm, tpb, grouped=True)

    rows = bs * t
    xs = x_sample.reshape(rows, d)
    hs = _normmod(xs, p["g1"], mod_s, rows, 1)
    pg_s, qs, ks, _, vs, _, gates_s, small_s = _project(hs, p, rows)
    oa_s, s_s = _gla_sample(pg_s.reshape(bs, t, -1), small_s.reshape(bs, t, LANES),
                            state_gla.reshape(state_gla.shape[1:]), p["wall_s"], p["mask_s"],
                            p["w_a2"], p["b_a"], p["g_gla"], bs, t)
    lf_s, fn_s = _fox_bias_sample(small_s, p["bf_row"], t)
    n_pool = cache_k.shape[1]
    ob_s = _fox_sample(page_table, qs.astype(F32).reshape(bs, t, FOX_W), ks.reshape(bs, t, FOX_HEADS, FOX_DH),
                       vs.reshape(bs, t, FOX_HEADS, FOX_DH), fn_s.reshape(bs, t, FOX_HEADS),
                       cache_k, cache_v, cache_logf.reshape(n_pool, PAGE, FOX_HEADS).transpose(0, 2, 1), t)
    y_s = _tail(xs, oa_s.reshape(rows, GLA_VW).astype(BF16), ob_s.reshape(rows, FOX_W).astype(BF16), gates_s, mod_s,
                p, rows, 1, grouped=False)

    return (y_p.reshape(bp, seq, d), y_s.reshape(bs, t, d),
            kb.reshape(1, bp, seq, FOX_HEADS, FOX_DH), vb.reshape(1, bp, seq, FOX_HEADS, FOX_DH),
            lf_p.reshape(1, bp, seq, FOX_HEADS), s_p[None],
            ks.reshape(1, bs, t, FOX_HEADS, FOX_DH), vs.reshape(1, bs, t, FOX_HEADS, FOX_DH),
            lf_s.reshape(1, bs, t, FOX_HEADS), s_s[None])
```

```python
import functools

import numpy as np
import jax
import jax.numpy as jnp
from jax import lax
from jax.experimental import pallas as pl
from jax.experimental.pallas import tpu as pltpu

F32 = jnp.float32
BF16 = jnp.bfloat16

D_MODEL = 2048
GLA_HEADS = 4
GLA_DK = 128
GLA_DV = 256
GLA_RANK = 16
GLA_TAU = 16.0
FOX_HEADS = 8
FOX_DH = 128
PAGE = 128
N_GROUPS = 4
EXP_PER_GROUP = 4
N_EXPERTS = 16
D_EXPERT = 512
RMS_EPS = 1e-6
GLA_KW = GLA_HEADS * GLA_DK
GLA_VW = GLA_HEADS * GLA_DV
FOX_W = FOX_HEADS * FOX_DH
GLA_CHUNK = 128
GLA_LEVELS = 7
GLA_MM_LEVELS = 4
SAMPLE_LEVELS = 2
LANES = 128
NEG = -1e30
VMEM_LIMIT = 56 * 1024 * 1024

ROW_TILE = 1024
PROJ_COL_TILE = 1024
HEADS_ROW_TILE = 512
OUTPROJ_ROW_TILE = 256
DENSE_MOE_ROW_TILE = 512
GLA_ROWS_PER_STEP = 512
FOX_BIAS_ROWS = 512
FOX_Q_TILE = 512
FOX_K_TILE = 256
ADALN_COL_TILE = 512


def _cparams(sem):
    return pltpu.CompilerParams(dimension_semantics=sem, vmem_limit_bytes=VMEM_LIMIT)


def _dot(a, b):
    return jnp.dot(a, b, preferred_element_type=F32)


def _dot_nt(a, b):
    return lax.dot_general(a, b, (((1,), (1,)), ((), ())), preferred_element_type=F32)


def _dot_tn(a, b):
    return lax.dot_general(a, b, (((0,), (0,)), ((), ())), preferred_element_type=F32)


def _split2(x):
    hi = x.astype(BF16)
    lo = (x - hi.astype(F32)).astype(BF16)
    return hi, lo


def _split3(x):
    hi = x.astype(BF16)
    r = x - hi.astype(F32)
    mid = r.astype(BF16)
    lo = (r - mid.astype(F32)).astype(BF16)
    return hi, mid, lo


def _dot3(a, b):
    ah, al = _split2(a)
    bh, bl = _split2(b)
    return _dot(ah, bh) + _dot(ah, bl) + _dot(al, bh)


def _dot_sel(w01, x):
    hi, mid, lo = _split3(x)
    return _dot(w01, hi) + _dot(w01, mid) + _dot(w01, lo)


def _log_sigmoid(x):
    return jnp.minimum(x, 0.0) - jnp.log1p(jnp.exp(-jnp.abs(x)))


def _silu(x):
    return x * jax.nn.sigmoid(x)


def _adaln_kernel(c_ref, w_ref, b_ref, o_ref):
    o_ref[...] = _dot3(_silu(c_ref[...]), w_ref[...]) + b_ref[...]


def _adaln(c, w, b, tn=ADALN_COL_TILE):
    nb, d = c.shape
    n = w.shape[1]
    return pl.pallas_call(
        _adaln_kernel,
        grid=(n // tn,),
        in_specs=[pl.BlockSpec((nb, d), lambda j: (0, 0)),
                  pl.BlockSpec((d, tn), lambda j: (0, j)),
                  pl.BlockSpec((1, tn), lambda j: (0, j))],
        out_specs=pl.BlockSpec((nb, tn), lambda j: (0, j)),
        out_shape=jax.ShapeDtypeStruct((nb, n), F32),
        compiler_params=_cparams(("arbitrary",)),
        name="adaln",
    )(c, w, b)


def _rms(x):
    return x * lax.rsqrt(jnp.mean(x * x, axis=-1, keepdims=True) + RMS_EPS)


def _normmod_kernel(x_ref, g_ref, sh_ref, sc_ref, o_ref):
    y = _rms(x_ref[...]) * g_ref[...]
    o_ref[...] = (y * (1.0 + sc_ref[...]) + sh_ref[...]).astype(o_ref.dtype)


def _mod_spec(k, rb, d, tiles_per_batch):
    return pl.BlockSpec((None, None, rb, d), lambda i: (k, i // tiles_per_batch, 0, 0))


def _normmod(x, g, mod, tm, tiles_per_batch):
    m, d = x.shape
    rb = mod.shape[2]
    return pl.pallas_call(
        _normmod_kernel,
        grid=(m // tm,),
        in_specs=[pl.BlockSpec((tm, d), lambda i: (i, 0)),
                  pl.BlockSpec((1, d), lambda i: (0, 0)),
                  _mod_spec(0, rb, d, tiles_per_batch),
                  _mod_spec(1, rb, d, tiles_per_batch)],
        out_specs=pl.BlockSpec((tm, d), lambda i: (i, 0)),
        out_shape=jax.ShapeDtypeStruct((m, d), BF16),
        compiler_params=_cparams(("arbitrary",)),
        name="normmod",
    )(x, g, mod, mod)


def _wt_spec(k, row0, tn):
    assert row0 % 8 == 0 and tn % 8 == 0
    return pl.BlockSpec((pl.Element(tn), pl.Element(k)), lambda j, i: (pl.multiple_of(row0 + j * tn, 8), 0))


def _mm_wt_kernel(x_ref, wt_ref, *refs):
    o_refs, wb = refs[:-1], refs[-1]

    @pl.when(pl.program_id(1) == 0)
    def _():
        wb[...] = wt_ref[...].astype(BF16)

    r = _dot_nt(x_ref[...], wb[...])
    for o_ref in o_refs:
        o_ref[...] = r.astype(o_ref.dtype)


def _mm_wt(x, wt, row0, n, tm, tn, out_dtypes=(F32,)):
    m, k = x.shape
    return pl.pallas_call(
        _mm_wt_kernel,
        grid=(n // tn, m // tm),
        in_specs=[pl.BlockSpec((tm, k), lambda j, i: (i, 0)), _wt_spec(k, row0, tn)],
        out_specs=[pl.BlockSpec((tm, tn), lambda j, i: (i, j)) for _ in out_dtypes],
        out_shape=[jax.ShapeDtypeStruct((m, n), dt) for dt in out_dtypes],
        scratch_shapes=[pltpu.VMEM((tn, k), BF16)],
        compiler_params=_cparams(("arbitrary", "arbitrary")),
        name="proj_mm",
    )(x, wt)


def _mm_heads_kernel(x_ref, wt_ref, o_ref, o16_ref, wb):
    @pl.when(pl.program_id(1) == 0)
    def _():
        wb[...] = wt_ref[...].astype(BF16)

    r = _dot_nt(x_ref[...], wb[...])
    for h in range(FOX_HEADS):
        o_ref[:, h, :] = r[:, h * FOX_DH:(h + 1) * FOX_DH]
    o16_ref[...] = r.astype(BF16)


def _mm_heads(x, wt, row0, tm):
    m, k = x.shape
    return pl.pallas_call(
        _mm_heads_kernel,
        grid=(1, m // tm),
        in_specs=[pl.BlockSpec((tm, k), lambda j, i: (i, 0)), _wt_spec(k, row0, FOX_W)],
        out_specs=[pl.BlockSpec((tm, FOX_HEADS, FOX_DH), lambda j, i: (i, 0, 0)),
                   pl.BlockSpec((tm, FOX_W), lambda j, i: (i, 0))],
        out_shape=[jax.ShapeDtypeStruct((m, FOX_HEADS, FOX_DH), F32),
                   jax.ShapeDtypeStruct((m, FOX_W), BF16)],
        scratch_shapes=[pltpu.VMEM((FOX_W, k), BF16)],
        compiler_params=_cparams(("arbitrary", "arbitrary")),
        name="proj_heads",
    )(x, wt)


def _gla_tables(p=GLA_LEVELS):
    c = GLA_CHUNK
    nmm = min(p, GLA_MM_LEVELS)
    t = np.arange(c)[:, None]
    m = np.arange(c)[None, :]
    wall = np.zeros((nmm + 1, c, c), np.float32)
    mask = np.zeros((p + 1, c, c), np.float32)
    for l in range(p):
        half = 1 << l
        pos = t % (2 * half)
        mid = t - pos + half
        right = pos >= half
        if l < nmm:
            wall[l] = np.where(right, (m >= mid) & (m <= t), (m > t) & (m < mid))
        s = m
        mask[l] = ((t >> (l + 1)) == (s >> (l + 1))) & (((t >> l) & 1) == 1) & (((s >> l) & 1) == 0)
    wall[nmm] = m <= t
    mask[p] = t == m
    return wall.reshape((nmm + 1) * c, c), mask


def _gla_level_exponent(cum, row, l):
    half = 1 << l
    c = cum.shape[0]
    bound = None
    for start in range(0, c, 2 * half):
        b = cum[start + half - 1:start + half, :]
        bound = b if bound is None else jnp.where(row >= start, b, bound)
    return jnp.where(((row >> l) & 1) == 1, cum - bound, bound - cum)


def _gla_body(q_ref, k_ref, v_ref, ra_ref, sm_ref, wall_ref, mask_ref, wa2_ref, ba_ref, g_ref,
              og_ref, st_ref, *, nchunk, valid, store_rows):
    c, p = GLA_CHUNK, mask_ref.shape[0] - 1
    nmm = min(p, GLA_MM_LEVELS)
    row = lax.broadcasted_iota(jnp.int32, (c, GLA_DK), 0)
    rowh = lax.broadcasted_iota(jnp.int32, (c, GLA_KW), 0)
    wall = wall_ref[...]
    for ci in range(nchunk):
        rows = pl.ds(ci * c, c)
        x = _dot3(sm_ref[rows, :], wa2_ref[...]) + ba_ref[...]
        la = _log_sigmoid(x) * (1.0 / GLA_TAU)
        if valid < c:
            la = jnp.where(rowh < valid, la, 0.0)
        args = _dot_sel(wall, la)
        cum = args[nmm * c:(nmm + 1) * c]
        e_lvl = [jnp.exp(args[l * c:(l + 1) * c]) for l in range(nmm)]
        e_lvl += [jnp.exp(_gla_level_exponent(cum, rowh, l)) for l in range(nmm, p)]
        last = cum[c - 1:c, :]
        e_cum = jnp.exp(cum)
        e_rev = jnp.exp(last - cum)
        e_last = jnp.exp(last)
        for h in range(GLA_HEADS):
            ks = slice(h * GLA_DK, (h + 1) * GLA_DK)
            vs = slice(h * GLA_DV, (h + 1) * GLA_DV)
            q = q_ref[rows, ks] * (GLA_DK ** -0.5)
            k = k_ref[rows, ks]
            vb = v_ref[rows, vs].astype(BF16)
            a = mask_ref[p] * _dot_nt(q.astype(BF16), k.astype(BF16))
            for l in range(p):
                xl = (jnp.where(((row >> l) & 1) == 1, q, k) * e_lvl[l][:, ks]).astype(BF16)
                a = a + mask_ref[l] * _dot_nt(xl, xl)
            st = st_ref[h]
            qc = (q * e_cum[:, ks]).astype(BF16)
            o = _dot(a.astype(BF16), vb) + _dot_nt(qc, st.astype(BF16))
            kr = (k * e_rev[:, ks]).astype(BF16)
            st_ref[h] = e_last[:, ks] * st + _dot_tn(vb, kr)
            og = _rms(o) * g_ref[...] * _silu(ra_ref[rows, vs])
            if store_rows < c:
                og_ref[:, vs] = og[:store_rows].astype(og_ref.dtype)
            else:
                og_ref[rows, vs] = og.astype(og_ref.dtype)


def _gla_prompt_kernel(q_ref, k_ref, v_ref, ra_ref, sm_ref, wall_ref, mask_ref, wa2_ref, ba_ref, g_ref, s0_ref,
                       og_ref, s_ref, st_ref, *, nchunk):
    n = pl.program_id(1)

    @pl.when(n == 0)
    def _():
        for h in range(GLA_HEADS):
            st_ref[h] = s0_ref[h].T

    _gla_body(q_ref, k_ref, v_ref, ra_ref, sm_ref, wall_ref, mask_ref, wa2_ref, ba_ref, g_ref, og_ref, st_ref,
              nchunk=nchunk, valid=GLA_CHUNK, store_rows=GLA_CHUNK)

    @pl.when(n == pl.num_programs(1) - 1)
    def _():
        for h in range(GLA_HEADS):
            s_ref[h] = st_ref[h].T


def _gla_prompt(pg, small, s0, wall, mask, wa2, ba, g, batch, seq, tb=GLA_ROWS_PER_STEP):
    nblk = seq // tb
    rowmap = lambda cb: (lambda b, n: (b * nblk + n, cb))
    const2 = lambda b, n: (0, 0)
    return pl.pallas_call(
        functools.partial(_gla_prompt_kernel, nchunk=tb // GLA_CHUNK),
        grid=(batch, nblk),
        in_specs=[pl.BlockSpec((tb, GLA_KW), rowmap(0)),
                  pl.BlockSpec((tb, GLA_KW), rowmap(1)),
                  pl.BlockSpec((tb, GLA_VW), rowmap(1)),
                  pl.BlockSpec((tb, GLA_VW), rowmap(2)),
                  pl.BlockSpec((tb, LANES), rowmap(0)),
                  pl.BlockSpec(wall.shape, const2),
                  pl.BlockSpec(mask.shape, lambda b, n: (0, 0, 0)),
                  pl.BlockSpec(wa2.shape, const2),
                  pl.BlockSpec(ba.shape, const2),
                  pl.BlockSpec(g.shape, const2),
                  pl.BlockSpec((None, GLA_HEADS, GLA_DK, GLA_DV), lambda b, n: (b, 0, 0, 0))],
        out_specs=[pl.BlockSpec((tb, GLA_VW), rowmap(0)),
                   pl.BlockSpec((None, GLA_HEADS, GLA_DK, GLA_DV), lambda b, n: (b, 0, 0, 0))],
        out_shape=[jax.ShapeDtypeStruct((batch * seq, GLA_VW), BF16),
                   jax.ShapeDtypeStruct((batch, GLA_HEADS, GLA_DK, GLA_DV), F32)],
        scratch_shapes=[pltpu.VMEM((GLA_HEADS, GLA_DV, GLA_DK), F32)],
        compiler_params=_cparams(("arbitrary", "arbitrary")),
        name="gla_prompt",
    )(pg, pg, pg, pg, small, wall, mask, wa2, ba, g, s0)


def _gla_sample_kernel(pg_ref, sm_ref, wall_ref, mask_ref, wa2_ref, ba_ref, g_ref, s0_ref,
                       og_ref, s_ref, pad_ref, smpad_ref, st_ref, *, t):
    @pl.when(pl.program_id(0) == 0)
    def _():
        pad_ref[...] = jnp.zeros(pad_ref.shape, F32)
        smpad_ref[...] = jnp.zeros(smpad_ref.shape, F32)

    pad_ref[0:t, :] = pg_ref[...]
    smpad_ref[0:t, :] = sm_ref[...]
    for h in range(GLA_HEADS):
        st_ref[h] = s0_ref[h].T
    q_ref = pad_ref.at[:, 0:GLA_KW]
    k_ref = pad_ref.at[:, GLA_KW:2 * GLA_KW]
    v_ref = pad_ref.at[:, 2 * GLA_KW:2 * GLA_KW + GLA_VW]
    ra_ref = pad_ref.at[:, 2 * GLA_KW + GLA_VW:2 * GLA_KW + 2 * GLA_VW]
    _gla_body(q_ref, k_ref, v_ref, ra_ref, smpad_ref, wall_ref, mask_ref, wa2_ref, ba_ref, g_ref, og_ref, st_ref,
              nchunk=1, valid=t, store_rows=t)
    for h in range(GLA_HEADS):
        s_ref[h] = st_ref[h].T


def _gla_sample(pg, small, s0, wall, mask, wa2, ba, g, batch, t):
    width = pg.shape[-1]
    const2 = lambda b: (0, 0)
    return pl.pallas_call(
        functools.partial(_gla_sample_kernel, t=t),
        grid=(batch,),
        in_specs=[pl.BlockSpec((None, t, width), lambda b: (b, 0, 0)),
                  pl.BlockSpec((None, t, LANES), lambda b: (b, 0, 0)),
                  pl.BlockSpec(wall.shape, const2),
                  pl.BlockSpec(mask.shape, lambda b: (0, 0, 0)),
                  pl.BlockSpec(wa2.shape, const2),
                  pl.BlockSpec(ba.shape, const2),
                  pl.BlockSpec(g.shape, const2),
                  pl.BlockSpec((None, GLA_HEADS, GLA_DK, GLA_DV), lambda b: (b, 0, 0, 0))],
        out_specs=[pl.BlockSpec((None, t, GLA_VW), lambda b: (b, 0, 0)),
                   pl.BlockSpec((None, GLA_HEADS, GLA_DK, GLA_DV), lambda b: (b, 0, 0, 0))],
        out_shape=[jax.ShapeDtypeStruct((batch, t, GLA_VW), F32),
                   jax.ShapeDtypeStruct((batch, GLA_HEADS, GLA_DK, GLA_DV), F32)],
        scratch_shapes=[pltpu.VMEM((GLA_CHUNK, width), F32),
                        pltpu.VMEM((GLA_CHUNK, LANES), F32),
                        pltpu.VMEM((GLA_HEADS, GLA_DV, GLA_DK), F32)],
        compiler_params=_cparams(("arbitrary",)),
        name="gla_sample",
    )(pg, small, wall, mask, wa2, ba, g, s0)


FB_LANE = GLA_RANK


LOG2E = 1.4426950408889634


def _fox_bias_prompt_kernel(sm_ref, bf_ref, tri_ref, lf_ref, fq_ref, fk_ref, carry_ref):
    @pl.when(pl.program_id(1) == 0)
    def _():
        carry_ref[...] = jnp.zeros(carry_ref.shape, F32)

    lf = _log_sigmoid(sm_ref[...] + bf_ref[...])
    lf_ref[...] = lf[:, FB_LANE:FB_LANE + FOX_HEADS]
    cum = _dot_sel(tri_ref[...], lf) + carry_ref[...]
    carry_ref[...] = cum[cum.shape[0] - 1:, :]
    lane = lax.broadcasted_iota(jnp.int32, cum.shape, 1)
    ones = jnp.where(lane < 6, 1.0, 0.0)
    for h in range(FOX_HEADS):
        f = jnp.broadcast_to(cum[:, FB_LANE + h:FB_LANE + h + 1] * LOG2E, cum.shape)
        hi = f.astype(BF16).astype(F32)
        r = f - hi
        mid = r.astype(BF16).astype(F32)
        lo = r - mid
        fq = jnp.where(lane == 0, hi, jnp.where(lane == 1, mid, jnp.where(lane == 2, lo, ones)))
        fk = jnp.where(lane == 3, -hi, jnp.where(lane == 4, -mid, jnp.where(lane == 5, -lo, ones)))
        fq_ref[:, h * LANES:(h + 1) * LANES] = fq.astype(BF16)
        fk_ref[:, h * LANES:(h + 1) * LANES] = fk.astype(BF16)


def _fox_bias_prompt(small, bf_row, batch, seq, tb=FOX_BIAS_ROWS):
    nblk = seq // tb
    tri = jnp.asarray(np.tril(np.ones((tb, tb), np.float32)), BF16)
    wide = FOX_HEADS * LANES
    return pl.pallas_call(
        _fox_bias_prompt_kernel,
        grid=(batch, nblk),
        in_specs=[pl.BlockSpec((tb, LANES), lambda b, n: (b * nblk + n, 0)),
                  pl.BlockSpec((1, LANES), lambda b, n: (0, 0)),
                  pl.BlockSpec((tb, tb), lambda b, n: (0, 0))],
        out_specs=[pl.BlockSpec((tb, FOX_HEADS), lambda b, n: (b * nblk + n, 0)),
                   pl.BlockSpec((tb, wide), lambda b, n: (b * nblk + n, 0)),
                   pl.BlockSpec((tb, wide), lambda b, n: (b * nblk + n, 0))],
        out_shape=[jax.ShapeDtypeStruct((batch * seq, FOX_HEADS), F32),
                   jax.ShapeDtypeStruct((batch * seq, wide), BF16),
                   jax.ShapeDtypeStruct((batch * seq, wide), BF16)],
        scratch_shapes=[pltpu.VMEM((1, LANES), F32)],
        compiler_params=_cparams(("arbitrary", "arbitrary")),
        name="fox_bias_prompt",
    )(small, bf_row, tri)


def _fox_bias_sample_kernel(sm_ref, bf_ref, sel_ref, lf_ref, fn_ref):
    lf = _log_sigmoid(sm_ref[...] + bf_ref[...])
    lf_ref[...] = lf[:, FB_LANE:FB_LANE + FOX_HEADS]
    cum = _dot_sel(sel_ref[...], lf)
    fn_ref[...] = cum[:, FB_LANE:FB_LANE + FOX_HEADS]


def _fox_bias_sample(small, bf_row, t):
    rows = small.shape[0]
    r = np.arange(rows)
    sel = ((r[:, None] // t) == (r[None, :] // t)) & (r[None, :] <= r[:, None])
    sel = jnp.asarray(sel.astype(np.float32), BF16)
    full = lambda shape: pl.BlockSpec(shape, lambda i: tuple(0 for _ in shape))
    return pl.pallas_call(
        _fox_bias_sample_kernel,
        grid=(1,),
        in_specs=[full((rows, LANES)), full((1, LANES)), full((rows, rows))],
        out_specs=[full((rows, FOX_HEADS)), full((rows, FOX_HEADS))],
        out_shape=[jax.ShapeDtypeStruct((rows, FOX_HEADS), F32),
                   jax.ShapeDtypeStruct((rows, FOX_HEADS), F32)],
        compiler_params=_cparams(("arbitrary",)),
        name="fox_bias_sample",
    )(small, bf_row, sel)


FOX_HEADS_PER_STEP = 8


def _fox_prompt_kernel(q_ref, k_ref, v_ref, fq_ref, fk_ref, o_ref, *, tq, tk):
    hb = FOX_HEADS_PER_STEP
    i = pl.program_id(2)
    rowi = lax.broadcasted_iota(jnp.int32, (tq, tk), 0)
    coli = lax.broadcasted_iota(jnp.int32, (tq, tk), 1)
    qs = []
    for hh in range(hb):
        hs = slice(hh * FOX_DH, (hh + 1) * FOX_DH)
        q = (q_ref[:, hs].astype(F32) * (FOX_DH ** -0.5 * LOG2E)).astype(BF16)
        qs.append(jnp.concatenate([q, fq_ref[:, hs]], axis=1))

    def step(j, carry, masked):
        ks = pl.ds(pl.multiple_of(j * tk, tk), tk)
        out = []
        for hh in range(hb):
            hs = slice(hh * FOX_DH, (hh + 1) * FOX_DH)
            m, l, acc = carry[hh]
            s = _dot_nt(qs[hh], jnp.concatenate([k_ref[ks, hs], fk_ref[ks, hs]], axis=1))
            if masked:
                s = jnp.where(rowi + i * tq >= coli + j * tk, s, NEG)
            m_new = jnp.maximum(m, jnp.max(s, axis=-1, keepdims=True))
            p = jnp.exp2(s - m_new)
            alpha = jnp.exp2(m - m_new)
            l = alpha * l + jnp.sum(p, axis=-1, keepdims=True)
            acc = alpha * acc + _dot(p.astype(BF16), v_ref[ks, hs])
            out.append((m_new, l, acc))
        return tuple(out)

    init = tuple((jnp.full((tq, 1), NEG, F32), jnp.zeros((tq, 1), F32), jnp.zeros((tq, FOX_DH), F32))
                 for _ in range(hb))
    nfull = (i * tq) // tk
    carry = lax.fori_loop(0, nfull, lambda j, c: step(j, c, False), init)
    for d in range(pl.cdiv(tq, tk)):
        carry = step(nfull + d, carry, True)
    for hh in range(hb):
        m, l, acc = carry[hh]
        o_ref[:, hh * FOX_DH:(hh + 1) * FOX_DH] = (acc / l).astype(o_ref.dtype)


def _fox_prompt(qb, kb, vb, fq, fk, batch, seq, tq=FOX_Q_TILE, tk=FOX_K_TILE):
    nq = seq // tq
    hb = FOX_HEADS_PER_STEP
    wb = hb * FOX_DH
    return pl.pallas_call(
        functools.partial(_fox_prompt_kernel, tq=tq, tk=tk),
        grid=(batch, FOX_HEADS // hb, nq),
        in_specs=[pl.BlockSpec((tq, wb), lambda b, h, i: (b * nq + i, h)),
                  pl.BlockSpec((seq, wb), lambda b, h, i: (b, h)),
                  pl.BlockSpec((seq, wb), lambda b, h, i: (b, h)),
                  pl.BlockSpec((tq, wb), lambda b, h, i: (b * nq + i, h)),
                  pl.BlockSpec((seq, wb), lambda b, h, i: (b, h))],
        out_specs=pl.BlockSpec((tq, wb), lambda b, h, i: (b * nq + i, h)),
        out_shape=jax.ShapeDtypeStruct((batch * seq, FOX_W), BF16),
        compiler_params=_cparams(("arbitrary", "arbitrary", "arbitrary")),
        name="fox_prompt",
    )(qb, kb, vb, fq, fk)


PAGES_PER_STEP = 16
ROWS8 = 8
QROWS = FOX_HEADS * ROWS8
PAGE_FLAT = PAGE * FOX_HEADS


def _dot_sel_rhs(x, w01):
    hi, mid, lo = _split3(x)
    m = x.shape[0]
    stacked = jnp.concatenate([hi.astype(F32), mid.astype(F32), lo.astype(F32)], axis=0).astype(BF16)
    r = _dot(stacked, w01)
    return r[0:m] + r[m:2 * m] + r[2 * m:3 * m]


def _fox_sample_kernel(pt_ref, q_ref, kn_ref, vn_ref, fn_ref, fnrow_ref, madd_ref, maddn_ref, usuf_ref, pfx_ref,
                       hsum_ref, pexp_ref, hmask_ref, *refs, t):
    g = PAGES_PER_STEP
    k_refs, v_refs, lf_refs = refs[0:g], refs[g:2 * g], refs[2 * g:3 * g]
    o_ref = refs[3 * g]
    q_sc, m_sc, l_sc, acc_sc, carry_sc = refs[3 * g + 1:]
    j = pl.program_id(1)
    fn_t = fn_ref[...][:, 0:1]

    @pl.when(j == 0)
    def _():
        q_sc[...] = jnp.zeros(q_sc.shape, F32)
        for h in range(FOX_HEADS):
            q_sc[h * ROWS8:h * ROWS8 + t, :] = q_ref[:, h * FOX_DH:(h + 1) * FOX_DH] * (FOX_DH ** -0.5)
        carry_sc[...] = jnp.zeros(carry_sc.shape, F32)
        pad = jnp.zeros((LANES - t * FOX_HEADS, FOX_DH), F32)
        kn = jnp.concatenate([kn_ref[...].reshape(t * FOX_HEADS, FOX_DH), pad], axis=0)
        vn = jnp.concatenate([vn_ref[...].reshape(t * FOX_HEADS, FOX_DH), pad], axis=0)
        s = _dot_nt(q_sc[...], kn) + fn_t - fnrow_ref[...] + maddn_ref[...]
        m = jnp.max(s, axis=-1, keepdims=True)
        p = jnp.exp(s - m)
        m_sc[...] = jnp.broadcast_to(m, m_sc.shape)
        l_sc[...] = jnp.broadcast_to(jnp.sum(p, axis=-1, keepdims=True), l_sc.shape)
        acc_sc[...] = _dot(p, vn)

    lf = jnp.concatenate([lf_refs[gi][...] for gi in range(g)], axis=0)
    tot = jnp.broadcast_to(jnp.sum(lf, axis=-1, keepdims=True), lf.shape)
    carry = carry_sc[...]
    r_hk = (_dot_sel_rhs(lf, usuf_ref[...]) + _dot_sel(pfx_ref[...], tot)
            + jnp.concatenate([carry] * g, axis=0))
    carry_sc[...] = carry + _dot_sel(hsum_ref[...], tot)
    spread = _dot_sel_rhs(r_hk, pexp_ref[...]) * hmask_ref[...]
    r_flat = jnp.sum(spread.reshape(g, FOX_HEADS, PAGE_FLAT), axis=1)
    q = q_sc[...]
    madd = madd_ref[...] + fn_t
    m_old = m_sc[...]
    m_new = m_old
    s_list = []
    for gi in range(g):
        bias = madd + r_flat[gi:gi + 1, :]
        s = _dot_nt(q, k_refs[gi][...].reshape(PAGE_FLAT, FOX_DH)) + bias
        s_list.append(s)
        m_new = jnp.maximum(m_new, jnp.max(s, axis=-1, keepdims=True))
    alpha = jnp.exp(m_old - m_new)
    l = alpha * l_sc[...]
    acc = alpha * acc_sc[...]
    m_col = m_new[:, 0:1]
    for gi in range(g):
        p = jnp.exp(s_list[gi] - m_col)
        l = l + jnp.sum(p, axis=-1, keepdims=True)
        acc = acc + _dot(p, v_refs[gi][...].reshape(PAGE_FLAT, FOX_DH))
    m_sc[...] = m_new
    l_sc[...] = l
    acc_sc[...] = acc

    @pl.when(j == pl.num_programs(1) - 1)
    def _():
        o = acc / l
        for h in range(FOX_HEADS):
            o_ref[:, h * FOX_DH:(h + 1) * FOX_DH] = o[h * ROWS8:h * ROWS8 + t, :].astype(o_ref.dtype)


def _fox_sample_tables(t):
    row = np.arange(QROWS)[:, None]
    col = np.arange(PAGE_FLAT)[None, :]
    madd = np.where((row // ROWS8) == (col % FOX_HEADS), 0.0, NEG).astype(np.float32)
    coln = np.arange(LANES)[None, :]
    ok = (coln < t * FOX_HEADS) & ((row // ROWS8) == (coln % FOX_HEADS)) & ((coln // FOX_HEADS) <= (row % ROWS8))
    maddn = np.where(ok, 0.0, NEG).astype(np.float32)
    key = np.arange(PAGE)
    usuf = (key[:, None] > key[None, :]).astype(np.float32)
    gh = np.arange(PAGES_PER_STEP * FOX_HEADS)
    same_head = (gh[:, None] % FOX_HEADS) == (gh[None, :] % FOX_HEADS)
    pfx = (same_head & ((gh[None, :] // FOX_HEADS) < (gh[:, None] // FOX_HEADS))).astype(np.float32)
    hsum = (np.arange(FOX_HEADS)[:, None] == (gh[None, :] % FOX_HEADS)).astype(np.float32)
    pexp = (key[:, None] == (col // FOX_HEADS)).astype(np.float32)
    hmask = ((gh[:, None] % FOX_HEADS) == (col % FOX_HEADS)).astype(np.float32)
    return (jnp.asarray(madd), jnp.asarray(maddn), jnp.asarray(usuf, BF16), jnp.asarray(pfx, BF16),
            jnp.asarray(hsum, BF16), jnp.asarray(pexp, BF16), jnp.asarray(hmask))


def _fox_sample(page_table, q, kn, vn, fn, cache_k, cache_v, cache_lf, t):
    batch, npages = page_table.shape
    g = PAGES_PER_STEP
    nsteps = npages // g
    tables = _fox_sample_tables(t)
    fn_rows = jnp.pad(fn.transpose(0, 2, 1), ((0, 0), (0, 0), (0, ROWS8 - t))).reshape(batch, QROWS, 1)
    fn_rows = jnp.broadcast_to(fn_rows, (batch, QROWS, LANES))
    fn_cols = jnp.pad(fn.reshape(batch, 1, t * FOX_HEADS), ((0, 0), (0, 0), (0, LANES - t * FOX_HEADS)))

    def page_map5(gi):
        return lambda b, j, pt: (0, pt[b, npages - 1 - (j * g + gi)], 0, 0, 0)

    def page_map3(gi):
        return lambda b, j, pt: (pt[b, npages - 1 - (j * g + gi)], 0, 0)

    seq3 = lambda b, j, pt: (b, 0, 0)
    seq4 = lambda b, j, pt: (b, 0, 0, 0)
    const2 = lambda b, j, pt: (0, 0)
    in_specs = [pl.BlockSpec((None, t, FOX_W), seq3),
                pl.BlockSpec((None, t, FOX_HEADS, FOX_DH), seq4),
                pl.BlockSpec((None, t, FOX_HEADS, FOX_DH), seq4),
                pl.BlockSpec((None, QROWS, LANES), seq3),
                pl.BlockSpec((None, 1, LANES), seq3)]
    in_specs += [pl.BlockSpec(tb.shape, const2) for tb in tables]
    in_specs += [pl.BlockSpec((None, None, PAGE, FOX_HEADS, FOX_DH), page_map5(gi)) for gi in range(g)]
    in_specs += [pl.BlockSpec((None, None, PAGE, FOX_HEADS, FOX_DH), page_map5(gi)) for gi in range(g)]
    in_specs += [pl.BlockSpec((None, FOX_HEADS, PAGE), page_map3(gi)) for gi in range(g)]
    grid_spec = pltpu.PrefetchScalarGridSpec(
        num_scalar_prefetch=1,
        grid=(batch, nsteps),
        in_specs=in_specs,
        out_specs=pl.BlockSpec((None, t, FOX_W), seq3),
        scratch_shapes=[pltpu.VMEM((QROWS, FOX_DH), F32),
                        pltpu.VMEM((QROWS, LANES), F32),
                        pltpu.VMEM((QROWS, LANES), F32),
                        pltpu.VMEM((QROWS, FOX_DH), F32),
                        pltpu.VMEM((FOX_HEADS, PAGE), F32)],
    )
    return pl.pallas_call(
        functools.partial(_fox_sample_kernel, t=t),
        grid_spec=grid_spec,
        out_shape=jax.ShapeDtypeStruct((batch, t, FOX_W), F32),
        compiler_params=_cparams(("arbitrary", "arbitrary")),
        name="fox_sample",
    )(page_table, q, kn, vn, fn_rows, fn_cols, *tables,
      *([cache_k] * g), *([cache_v] * g), *([cache_lf] * g))


def _merge_kernel(oa_ref, ob_ref, wa_ref, wb_ref, ga_ref, gb_ref, o_ref):
    ua = _dot(oa_ref[...], wa_ref[...])
    ub = _dot(ob_ref[...], wb_ref[...])
    ga = jax.nn.sigmoid(ga_ref[...].astype(F32))
    gb = jax.nn.sigmoid(gb_ref[...].astype(F32))
    o_ref[...] = (ga * ua + gb * ub).astype(o_ref.dtype)


def _merge(oa, ob, wa, wb, gates, tm, tn=1024):
    m = oa.shape[0]
    d = wa.shape[1]
    nj = d // tn
    return pl.pallas_call(
        _merge_kernel,
        grid=(nj, m // tm),
        in_specs=[pl.BlockSpec((tm, GLA_VW), lambda j, i: (i, 0)),
                  pl.BlockSpec((tm, FOX_W), lambda j, i: (i, 0)),
                  pl.BlockSpec((GLA_VW, tn), lambda j, i: (0, j)),
                  pl.BlockSpec((FOX_W, tn), lambda j, i: (0, j)),
                  pl.BlockSpec((tm, tn), lambda j, i: (i, j)),
                  pl.BlockSpec((tm, tn), lambda j, i: (i, nj + j))],
        out_specs=pl.BlockSpec((tm, tn), lambda j, i: (i, j)),
        out_shape=jax.ShapeDtypeStruct((m, d), BF16),
        compiler_params=_cparams(("arbitrary", "arbitrary")),
        name="merge",
    )(oa, ob, wa, wb, gates, gates)


ROUTER_GROUP_LANE = N_EXPERTS


def _route(logits):
    lane_i = lax.broadcasted_iota(jnp.int32, logits.shape, 1)
    lane = lane_i.astype(F32)
    grp_of_lane = (lane_i >> 2).astype(F32)
    big = float(LANES)
    is_grp = (lane_i >= ROUTER_GROUP_LANE) & (lane_i < ROUTER_GROUP_LANE + N_GROUPS)
    gl = jnp.where(is_grp, logits, NEG)
    gmax = jnp.max(gl, axis=-1, keepdims=True)
    g_idx = jnp.min(jnp.where(is_grp & (gl == gmax), lane - ROUTER_GROUP_LANE, big), axis=-1, keepdims=True)
    g_w = 1.0 / jnp.sum(jnp.where(is_grp, jnp.exp(gl - gmax), 0.0), axis=-1, keepdims=True)
    in_grp = (lane_i < N_EXPERTS) & (grp_of_lane == g_idx)
    e1 = jnp.where(in_grp, logits, NEG)
    v1 = jnp.max(e1, axis=-1, keepdims=True)
    i1 = jnp.min(jnp.where(in_grp & (e1 == v1), lane, big), axis=-1, keepdims=True)
    rest = in_grp & (lane != i1)
    e2 = jnp.where(rest, logits, NEG)
    v2 = jnp.max(e2, axis=-1, keepdims=True)
    i2 = jnp.min(jnp.where(rest & (e2 == v2), lane, big), axis=-1, keepdims=True)
    r = jnp.exp(v2 - v1)
    w1 = g_w / (1.0 + r)
    w2 = g_w * r / (1.0 + r)
    grp_onehot = jnp.where(is_grp & (lane - ROUTER_GROUP_LANE == g_idx), 1.0, 0.0)
    return jnp.where(lane == i1, w1, 0.0) + jnp.where(lane == i2, w2, 0.0) + grp_onehot


HX_W = D_MODEL + LANES


def _outproj_kernel(mg_ref, w_ref, x_ref, gt_ref, g2_ref, sh_ref, sc_ref, wr_ref, br_ref, x1_ref, hx_ref):
    x1 = x_ref[...] + gt_ref[...] * _dot(mg_ref[...], w_ref[...])
    x1_ref[...] = x1
    h2 = _rms(x1) * g2_ref[...] * (1.0 + sc_ref[...]) + sh_ref[...]
    hx_ref[:, 0:D_MODEL] = h2
    hx_ref[:, D_MODEL:HX_W] = _route(_dot3(h2, wr_ref[...]) + br_ref[...])


def _outproj(merged, w_out, x, mod, g2, w_router, b_router, tm, tiles_per_batch):
    m, d = x.shape
    rb = mod.shape[2]
    const2 = lambda i: (0, 0)
    return pl.pallas_call(
        _outproj_kernel,
        grid=(m // tm,),
        in_specs=[pl.BlockSpec((tm, d), lambda i: (i, 0)),
                  pl.BlockSpec((d, d), const2),
                  pl.BlockSpec((tm, d), lambda i: (i, 0)),
                  _mod_spec(2, rb, d, tiles_per_batch),
                  pl.BlockSpec((1, d), const2),
                  _mod_spec(3, rb, d, tiles_per_batch),
                  _mod_spec(4, rb, d, tiles_per_batch),
                  pl.BlockSpec((d, LANES), const2),
                  pl.BlockSpec((1, LANES), const2)],
        out_specs=[pl.BlockSpec((tm, d), lambda i: (i, 0)),
                   pl.BlockSpec((tm, HX_W), lambda i: (i, 0))],
        out_shape=[jax.ShapeDtypeStruct((m, d), F32),
                   jax.ShapeDtypeStruct((m, HX_W), F32)],
        compiler_params=_cparams(("arbitrary",)),
        name="outproj",
    )(merged, w_out, x, mod, g2, mod, mod, w_router, b_router)


MOE_TILE = 512
PLAN_TILE = 512


def _moe_plan_kernel(r_ref, tri_ref, excl_ref, pos_ref, tg_ref, nu_ref, cnt_sc, off_sc, run_sc):
    ph = pl.program_id(0)
    n = pl.program_id(1)
    lane = lax.broadcasted_iota(jnp.int32, (1, LANES), 1)
    lane_t = lax.broadcasted_iota(jnp.int32, r_ref.shape, 1)
    is_grp = (lane_t >= ROUTER_GROUP_LANE) & (lane_t < ROUTER_GROUP_LANE + N_GROUPS)
    g4 = jnp.where(is_grp, r_ref[...], 0.0)

    @pl.when((ph == 0) & (n == 0))
    def _():
        cnt_sc[...] = jnp.zeros(cnt_sc.shape, F32)

    @pl.when(ph == 0)
    def _():
        cnt_sc[...] += jnp.sum(g4, axis=0, keepdims=True)

    @pl.when((ph == 1) & (n == 0))
    def _():
        padded = jnp.floor((cnt_sc[...] + (MOE_TILE - 1)) * (1.0 / MOE_TILE)) * MOE_TILE
        off = _dot_sel_rhs(jnp.broadcast_to(padded, (8, LANES)), excl_ref[...])[0:1]
        off_sc[...] = off
        run_sc[...] = jnp.zeros(run_sc.shape, F32)
        end = off + padded
        tile_start = lane.astype(F32) * MOE_TILE
        tg = jnp.zeros((1, LANES), F32)
        for g in range(N_GROUPS):
            end_g = jnp.sum(jnp.where(lane == ROUTER_GROUP_LANE + g, end, 0.0), axis=-1, keepdims=True)
            tg = tg + jnp.where(end_g <= tile_start, 1.0, 0.0)
        tg_ref[...] = jnp.minimum(tg, N_GROUPS - 1.0).astype(jnp.int32)
        total = jnp.sum(jnp.where(lane == ROUTER_GROUP_LANE + N_GROUPS - 1, end, 0.0), axis=-1, keepdims=True)
        nu_ref[...] = jnp.broadcast_to(total * (1.0 / MOE_TILE), (1, LANES)).astype(jnp.int32)

    @pl.when(ph == 1)
    def _():
        rank = _dot(tri_ref[...], g4.astype(BF16)) + run_sc[...]
        run_sc[...] += jnp.sum(g4, axis=0, keepdims=True)
        posv = g4 * (off_sc[...] + rank)
        hi, mid, lo = _split3(posv)
        ones = jnp.ones((8, LANES), BF16)
        row = _dot_nt(ones, hi) + _dot_nt(ones, mid) + _dot_nt(ones, lo)
        pos_ref[...] = row[0:1].astype(jnp.int32)


def _moe_plan(hx, ntiles):
    t = hx.shape[0]
    nblk = t // PLAN_TILE
    tri = jnp.asarray(np.tril(np.ones((PLAN_TILE, PLAN_TILE), np.float32), -1), BF16)
    excl = jnp.asarray(np.triu(np.ones((LANES, LANES), np.float32), 1), BF16)
    assert ntiles <= LANES
    pos, tg, nu = pl.pallas_call(
        _moe_plan_kernel,
        grid=(2, nblk),
        in_specs=[pl.BlockSpec((PLAN_TILE, LANES), lambda ph, n: (n, D_MODEL // LANES)),
                  pl.BlockSpec((PLAN_TILE, PLAN_TILE), lambda ph, n: (0, 0)),
                  pl.BlockSpec((LANES, LANES), lambda ph, n: (0, 0))],
        out_specs=[pl.BlockSpec((1, PLAN_TILE), lambda ph, n: (0, n * ph)),
                   pl.BlockSpec((1, LANES), lambda ph, n: (0, 0)),
                   pl.BlockSpec((1, LANES), lambda ph, n: (0, 0))],
        out_shape=[jax.ShapeDtypeStruct((1, t), jnp.int32),
                   jax.ShapeDtypeStruct((1, LANES), jnp.int32),
                   jax.ShapeDtypeStruct((1, LANES), jnp.int32)],
        scratch_shapes=[pltpu.VMEM((1, LANES), F32), pltpu.VMEM((1, LANES), F32), pltpu.VMEM((1, LANES), F32)],
        compiler_params=_cparams(("arbitrary", "arbitrary")),
        name="moe_plan",
    )(hx, tri, excl)
    return pos.reshape(t), tg.reshape(LANES), nu.reshape(LANES)[0:1]


def _row_copy(src_ref, src_row, dst_ref, dst_row, sem):
    return pltpu.make_async_copy(src_ref.at[pl.ds(src_row, 1)], dst_ref.at[pl.ds(dst_row, 1)], sem)


def _moe_grouped_kernel(pos_ref, tg_ref, nu_ref, hx_ref, wg_ref, wu_ref, wd_ref, ys_ref,
                        buf, sem, src_sc, xb_sc, *, t):
    i = pl.program_id(0)
    ei = pl.program_id(1)
    n_used = nu_ref[0]
    used = i < n_used
    slot = i % 2

    def fetch(tile, sl):
        def body(r, c):
            _row_copy(hx_ref, src_sc[tile * MOE_TILE + r], buf.at[sl], r, sem.at[sl]).start()
            return c
        lax.fori_loop(0, MOE_TILE, body, 0, unroll=8)

    @pl.when((i == 0) & (ei == 0))
    def _():
        def clear(r, c):
            src_sc[r] = 0
            return c

        def invert(tk, c):
            src_sc[pos_ref[tk]] = tk
            return c

        lax.fori_loop(0, src_sc.shape[0], clear, 0, unroll=8)
        lax.fori_loop(0, t, invert, 0, unroll=8)
        fetch(0, 0)

    @pl.when(used & (ei == 0))
    def _():
        def wait(r, c):
            _row_copy(hx_ref, 0, buf.at[slot], r, sem.at[slot]).wait()
            return c
        lax.fori_loop(0, MOE_TILE, wait, 0, unroll=8)
        xb_sc[...] = buf[slot, :, 0:D_MODEL].astype(BF16)

    @pl.when(ei == 0)
    def _():
        ys_ref[...] = jnp.zeros(ys_ref.shape, F32)

    def expert_step():
        h = xb_sc[...]
        a = _dot(h, wg_ref[...].astype(BF16))
        u = _dot(h, wu_ref[...].astype(BF16))
        lane = lax.broadcasted_iota(jnp.int32, (MOE_TILE, LANES), 1)
        e = tg_ref[i] * EXP_PER_GROUP + ei
        cw = jnp.sum(jnp.where(lane == e, buf[slot, :, D_MODEL:HX_W], 0.0), axis=-1, keepdims=True)
        hid = (_silu(a) * u * cw).astype(BF16)
        ys_ref[...] += _dot(hid, wd_ref[...].astype(BF16))

    has_next = i + 1 < n_used

    @pl.when(used & has_next)
    def _():
        per_step = MOE_TILE // EXP_PER_GROUP
        nslot = (i + 1) % 2
        for r in range(per_step):
            row = ei * per_step + r
            _row_copy(hx_ref, src_sc[(i + 1) * MOE_TILE + row], buf.at[nslot], row, sem.at[nslot]).start()
        expert_step()

    @pl.when(used & jnp.logical_not(has_next))
    def _():
        expert_step()


def _moe_grouped(hx, pos, tg, nu, wg, wu, wd, rows_out):
    t = hx.shape[0]
    ne, d, de = wg.shape
    ntiles = rows_out // MOE_TILE

    def wmap(i, ei, pos_ref, tg_ref, nu_ref):
        e = jnp.where(i < nu_ref[0], tg_ref[i] * EXP_PER_GROUP + ei, ne - 1)
        return (e, 0, 0)

    grid_spec = pltpu.PrefetchScalarGridSpec(
        num_scalar_prefetch=3,
        grid=(ntiles, EXP_PER_GROUP),
        in_specs=[pl.BlockSpec(memory_space=pl.ANY),
                  pl.BlockSpec((None, d, de), wmap),
                  pl.BlockSpec((None, d, de), wmap),
                  pl.BlockSpec((None, de, d), wmap)],
        out_specs=pl.BlockSpec((MOE_TILE, d), lambda i, ei, pos_ref, tg_ref, nu_ref: (i, 0)),
        scratch_shapes=[pltpu.VMEM((2, MOE_TILE, HX_W), F32), pltpu.SemaphoreType.DMA((2,)),
                        pltpu.SMEM((rows_out,), jnp.int32),
                        pltpu.VMEM((MOE_TILE, d), BF16)],
    )
    return pl.pallas_call(
        functools.partial(_moe_grouped_kernel, t=t),
        grid_spec=grid_spec,
        out_shape=jax.ShapeDtypeStruct((rows_out, d), F32),
        compiler_params=_cparams(("arbitrary", "arbitrary")),
        name="moe_grouped",
    )(pos, tg, nu, hx, wg, wu, wd)


COMBINE_ROWS = 512


def _moe_combine_kernel(pos_ref, ys_ref, x1_ref, gt_ref, gf_ref, y_ref, buf, sem):
    i = pl.program_id(0)
    n = pl.num_programs(0)

    def fetch(tile, slot):
        def body(r, c):
            _row_copy(ys_ref, pos_ref[tile * COMBINE_ROWS + r], buf.at[slot], r, sem.at[slot]).start()
            return c
        lax.fori_loop(0, COMBINE_ROWS, body, 0, unroll=8)

    @pl.when(i == 0)
    def _():
        fetch(0, 0)

    @pl.when(i + 1 < n)
    def _():
        fetch(i + 1, (i + 1) % 2)

    slot = i % 2

    def wait(r, c):
        _row_copy(ys_ref, pos_ref[i * COMBINE_ROWS + r], buf.at[slot], r, sem.at[slot]).wait()
        return c

    lax.fori_loop(0, COMBINE_ROWS, wait, 0, unroll=8)
    x2 = x1_ref[...] + gt_ref[...] * buf[slot]
    y_ref[...] = _rms(x2) * gf_ref[...]


def _moe_combine(ys, pos, x1, mod, g_final, tiles_per_batch):
    m, d = x1.shape
    rb = mod.shape[2]
    grid_spec = pltpu.PrefetchScalarGridSpec(
        num_scalar_prefetch=1,
        grid=(m // COMBINE_ROWS,),
        in_specs=[pl.BlockSpec(memory_space=pl.ANY),
                  pl.BlockSpec((COMBINE_ROWS, d), lambda i, pos_ref: (i, 0)),
                  pl.BlockSpec((None, None, rb, d), lambda i, pos_ref: (5, i // tiles_per_batch, 0, 0)),
                  pl.BlockSpec((1, d), lambda i, pos_ref: (0, 0))],
        out_specs=pl.BlockSpec((COMBINE_ROWS, d), lambda i, pos_ref: (i, 0)),
        scratch_shapes=[pltpu.VMEM((2, COMBINE_ROWS, d), F32), pltpu.SemaphoreType.DMA((2,))],
    )
    return pl.pallas_call(
        _moe_combine_kernel,
        grid_spec=grid_spec,
        out_shape=jax.ShapeDtypeStruct((m, d), F32),
        compiler_params=_cparams(("arbitrary",)),
        name="moe_combine",
    )(pos, ys, x1, mod, g_final)


def _moe_kernel(h_ref, cmb_ref, wg_ref, wu_ref, wd_ref, x1_ref, gt_ref, gf_ref,
                y_ref, wg16_ref, wu16_ref, wd16_ref, acc_ref):
    e = pl.program_id(1)

    @pl.when(e == 0)
    def _():
        acc_ref[...] = jnp.zeros(acc_ref.shape, F32)

    wg = wg_ref[...].astype(BF16)
    wu = wu_ref[...].astype(BF16)
    wd = wd_ref[...].astype(BF16)
    wg16_ref[...] = wg
    wu16_ref[...] = wu
    wd16_ref[...] = wd
    h = h_ref[...]
    a = _dot(h, wg)
    u = _dot(h, wu)
    lane = lax.broadcasted_iota(jnp.int32, cmb_ref.shape, 1)
    cw = jnp.sum(jnp.where(lane == e, cmb_ref[...], 0.0), axis=-1, keepdims=True)
    hid = (_silu(a) * u * cw).astype(BF16)
    acc_ref[...] += _dot(hid, wd)

    @pl.when(e == pl.num_programs(1) - 1)
    def _():
        x2 = x1_ref[...] + gt_ref[...] * acc_ref[...]
        y_ref[...] = _rms(x2) * gf_ref[...]


def _moe(h2, cmb, wg, wu, wd, x1, mod, g_final, tm, tiles_per_batch):
    m, d = x1.shape
    rb = mod.shape[2]
    ne, _, de = wg.shape
    assert m == tm
    return pl.pallas_call(
        _moe_kernel,
        grid=(m // tm, ne),
        in_specs=[pl.BlockSpec((tm, d), lambda i, e: (i, 0)),
                  pl.BlockSpec((tm, LANES), lambda i, e: (i, 0)),
                  pl.BlockSpec((None, d, de), lambda i, e: (e, 0, 0)),
                  pl.BlockSpec((None, d, de), lambda i, e: (e, 0, 0)),
                  pl.BlockSpec((None, de, d), lambda i, e: (e, 0, 0)),
                  pl.BlockSpec((tm, d), lambda i, e: (i, 0)),
                  pl.BlockSpec((None, None, rb, d), lambda i, e: (5, i // tiles_per_batch, 0, 0)),
                  pl.BlockSpec((1, d), lambda i, e: (0, 0))],
        out_specs=[pl.BlockSpec((tm, d), lambda i, e: (i, 0)),
                   pl.BlockSpec((None, d, de), lambda i, e: (e, 0, 0)),
                   pl.BlockSpec((None, d, de), lambda i, e: (e, 0, 0)),
                   pl.BlockSpec((None, de, d), lambda i, e: (e, 0, 0))],
        out_shape=[jax.ShapeDtypeStruct((m, d), F32),
                   jax.ShapeDtypeStruct((ne, d, de), BF16),
                   jax.ShapeDtypeStruct((ne, d, de), BF16),
                   jax.ShapeDtypeStruct((ne, de, d), BF16)],
        scratch_shapes=[pltpu.VMEM((tm, d), F32)],
        compiler_params=_cparams(("arbitrary", "arbitrary")),
        name="moe",
    )(h2, cmb, wg, wu, wd, x1, mod, g_final)


def _prep_weights(w_ada, b_ada, g_norm1, g_norm2, g_final, w_in, w_a2, b_a, b_f, g_gla_norm, w_up_a, w_up_b, w_out,
                  w_grp, b_grp, w_exp, b_exp, w_gate_e, w_up_e, w_down_e):
    wt = w_in.reshape(w_in.shape[1:]).T
    d = D_MODEL
    o_lra = 2 * GLA_KW + 2 * GLA_VW
    o_fox = o_lra + GLA_RANK
    o_fb = o_fox + 3 * FOX_W
    o_g = o_fb + FOX_HEADS
    pad = jnp.zeros((LANES - GLA_RANK - FOX_HEADS, d), F32)
    wall, mask = _gla_tables()
    wall_s, mask_s = _gla_tables(SAMPLE_LEVELS)
    bf_row =jnp.zeros((1, LANES), F32).at[0, FB_LANE:FB_LANE + FOX_HEADS].set(b_f[0])
    w_router = jnp.concatenate([w_exp[0], w_grp[0], jnp.zeros((d, LANES - N_EXPERTS - N_GROUPS), F32)], axis=1)
    b_router = jnp.concatenate([b_exp[0], b_grp[0], jnp.zeros((LANES - N_EXPERTS - N_GROUPS,), F32)])[None, :]
    return dict(
        w_ada=w_ada.reshape(w_ada.shape[1:]), b_ada=b_ada[0][None, :],
        g1=g_norm1[0][None, :], g2=g_norm2[0][None, :], gf=g_final[None, :],
        w_in_t=wt,
        w_small_t=jnp.concatenate([wt[o_lra:o_lra + GLA_RANK], wt[o_fb:o_fb + FOX_HEADS], pad], axis=0),
        wall=jnp.asarray(wall, BF16), mask=jnp.asarray(mask, F32),
        wall_s=jnp.asarray(wall_s, BF16), mask_s=jnp.asarray(mask_s, F32),
        w_a2=jnp.concatenate([w_a2[0], jnp.zeros((LANES - GLA_RANK, GLA_KW), F32)], axis=0), b_a=b_a[0][None, :], bf_row=bf_row, g_gla=g_gla_norm[0][None, :],
        w_up_a=w_up_a[0].astype(BF16), w_up_b=w_up_b[0].astype(BF16), w_out=w_out[0].astype(BF16),
        w_router=w_router, b_router=b_router,
        wg=w_gate_e.reshape(w_gate_e.shape[1:]), wu=w_up_e.reshape(w_up_e.shape[1:]),
        wd=w_down_e.reshape(w_down_e.shape[1:]),
    )


def _project(h, p, tm):
    tn = PROJ_COL_TILE
    wt = p["w_in_t"]
    n_gla = 2 * GLA_KW + 2 * GLA_VW
    c_q = n_gla + GLA_RANK
    c_g = c_q + 3 * FOX_W + FOX_HEADS
    (pg,) = _mm_wt(h, wt, 0, n_gla, tm, tn)
    (qb,) = _mm_wt(h, wt, c_q, FOX_W, tm, tn, (BF16,))
    kb, kb16 = _mm_heads(h, wt, c_q + FOX_W, min(tm, HEADS_ROW_TILE))
    vb, vb16 = _mm_heads(h, wt, c_q + 2 * FOX_W, min(tm, HEADS_ROW_TILE))
    (gates,) = _mm_wt(h, wt, c_g, 2 * D_MODEL, tm, tn, (BF16,))
    (small,) = _mm_wt(h, p["w_small_t"], 0, LANES, tm, LANES)
    return pg, qb, kb, kb16, vb, vb16, gates, small


def _tail(x, oa, ob, gates, mod, p, tm, tiles_per_batch, expert_w16=None):
    merged = _merge(oa, ob, p["w_up_a"], p["w_up_b"], gates, tm)
    tmo = min(tm, OUTPROJ_ROW_TILE)
    x1, hx = _outproj(merged, p["w_out"], x, mod, p["g2"], p["w_router"], p["b_router"],
                      tmo, tiles_per_batch * (tm // tmo))
    if expert_w16 is None:
        assert tm <= DENSE_MOE_ROW_TILE
        y, wg16, wu16, wd16 = _moe(hx[:, 0:D_MODEL].astype(BF16), hx[:, D_MODEL:HX_W], p["wg"], p["wu"], p["wd"],
                                   x1, mod, p["gf"], tm, tiles_per_batch)
        return y, (wg16, wu16, wd16)
    rows_out = x.shape[0] + N_GROUPS * MOE_TILE
    pos, tg, nu = _moe_plan(hx, rows_out // MOE_TILE)
    ys = _moe_grouped(hx, pos, tg, nu, *expert_w16, rows_out)
    return _moe_combine(ys, pos, x1, mod, p["gf"], tiles_per_batch * (tm // COMBINE_ROWS))


def kernel(x_prompt, x_sample, cache_k, cache_v, cache_logf, state_gla, page_table, c_prompt, c_sample, w_ada, b_ada,
           g_norm1, g_norm2, g_final, w_in, w_a2, b_a, b_f, g_gla_norm, w_up_a, w_up_b, w_out, w_grp, b_grp, w_exp,
           b_exp, w_gate_e, w_up_e, w_down_e):
    p = _prep_weights(w_ada, b_ada, g_norm1, g_norm2, g_final, w_in, w_a2, b_a, b_f, g_gla_norm, w_up_a, w_up_b,
                      w_out, w_grp, b_grp, w_exp, b_exp, w_gate_e, w_up_e, w_down_e)
    bp, seq, d = x_prompt.shape
    bs, t, _ = x_sample.shape
    assert t <= 2 ** SAMPLE_LEVELS and t <= ROWS8

    mod = _adaln(jnp.concatenate([c_prompt, c_sample], axis=0), p["w_ada"], p["b_ada"])
    mod_p = mod[:bp].reshape(bp, 6, 1, d).transpose(1, 0, 2, 3)
    mod_s = jnp.repeat(mod[bp:].reshape(bs, 6, d), t, axis=0).transpose(1, 0, 2)[:, None]

    tm = ROW_TILE
    tpb = seq // tm
    xp = x_prompt.reshape(bp * seq, d)
    hp = _normmod(xp, p["g1"], mod_p, tm, tpb)
    pg, qb, kb, kb16, vb, vb16, gates, small = _project(hp, p, tm)
    s0 = jnp.zeros((bp, GLA_HEADS, GLA_DK, GLA_DV), F32)
    oa, s_p = _gla_prompt(pg, small, s0, p["wall"], p["mask"], p["w_a2"], p["b_a"], p["g_gla"], bp, seq)
    lf_p, fq, fk = _fox_bias_prompt(small, p["bf_row"], bp, seq)
    ob = _fox_prompt(qb, kb16, vb16, fq, fk, bp, seq)

    rows = bs * t
    xs = x_sample.reshape(rows, d)
    hs = _normmod(xs, p["g1"], mod_s, rows, 1)
    pg_s, qs, ks, _, vs, _, gates_s, small_s = _project(hs, p, rows)
    oa_s, s_s = _gla_sample(pg_s.reshape(bs, t, -1), small_s.reshape(bs, t, LANES),
                            state_gla.reshape(state_gla.shape[1:]), p["wall_s"], p["mask_s"],
                            p["w_a2"], p["b_a"], p["g_gla"], bs, t)
    lf_s, fn_s = _fox_bias_sample(small_s, p["bf_row"], t)
    n_pool = cache_k.shape[1]
    ob_s = _fox_sample(page_table, qs.astype(F32).reshape(bs, t, FOX_W), ks.reshape(bs, t, FOX_HEADS, FOX_DH),
                       vs.reshape(bs, t, FOX_HEADS, FOX_DH), fn_s.reshape(bs, t, FOX_HEADS),
                       cache_k, cache_v, cache_logf.reshape(n_pool, PAGE, FOX_HEADS).transpose(0, 2, 1), t)
    y_s, expert_w16 = _tail(xs, oa_s.reshape(rows, GLA_VW).astype(BF16), ob_s.reshape(rows, FOX_W).astype(BF16),
                            gates_s, mod_s, p, rows, 1)
    y_p = _tail(xp, oa, ob, gates, mod_p, p, tm, tpb, expert_w16)

    return (y_p.reshape(bp, seq, d), y_s.reshape(bs, t, d),
            kb.reshape(1, bp, seq, FOX_HEADS, FOX_DH), vb.reshape(1, bp, seq, FOX_HEADS, FOX_DH),
            lf_p.reshape(1, bp, seq, FOX_HEADS), s_p[None],
            ks.reshape(1, bs, t, FOX_HEADS, FOX_DH), vs.reshape(1, bs, t, FOX_HEADS, FOX_DH),
            lf_s.reshape(1, bs, t, FOX_HEADS), s_s[None])
```

```python
import functools

import numpy as np
import jax
import jax.numpy as jnp
from jax import lax
from jax.experimental import pallas as pl
from jax.experimental.pallas import tpu as pltpu

F32 = jnp.float32
BF16 = jnp.bfloat16

D_MODEL = 2048
GLA_HEADS = 4
GLA_DK = 128
GLA_DV = 256
GLA_RANK = 16
GLA_TAU = 16.0
FOX_HEADS = 8
FOX_DH = 128
PAGE = 128
N_GROUPS = 4
EXP_PER_GROUP = 4
N_EXPERTS = 16
D_EXPERT = 512
RMS_EPS = 1e-6
GLA_KW = GLA_HEADS * GLA_DK
GLA_VW = GLA_HEADS * GLA_DV
FOX_W = FOX_HEADS * FOX_DH
GLA_CHUNK = 128
GLA_LEVELS = 7
GLA_MM_LEVELS = 4
SAMPLE_LEVELS = 2
LANES = 128
NEG = -1e30
VMEM_LIMIT = 56 * 1024 * 1024

ROW_TILE = 1024
PROJ_COL_TILE = 1024
HEADS_ROW_TILE = 512
OUTPROJ_ROW_TILE = 256
DENSE_MOE_ROW_TILE = 512
GLA_ROWS_PER_STEP = 512
FOX_BIAS_ROWS = 512
FOX_Q_TILE = 512
FOX_K_TILE = 512
ADALN_COL_TILE = 512


def _cparams(sem):
    return pltpu.CompilerParams(dimension_semantics=sem, vmem_limit_bytes=VMEM_LIMIT)


def _dot(a, b):
    return jnp.dot(a, b, preferred_element_type=F32)


def _dot_nt(a, b):
    return lax.dot_general(a, b, (((1,), (1,)), ((), ())), preferred_element_type=F32)


def _dot_tn(a, b):
    return lax.dot_general(a, b, (((0,), (0,)), ((), ())), preferred_element_type=F32)


def _split2(x):
    hi = x.astype(BF16)
    lo = (x - hi.astype(F32)).astype(BF16)
    return hi, lo


def _split3(x):
    hi = x.astype(BF16)
    r = x - hi.astype(F32)
    mid = r.astype(BF16)
    lo = (r - mid.astype(F32)).astype(BF16)
    return hi, mid, lo


def _dot3(a, b):
    ah, al = _split2(a)
    bh, bl = _split2(b)
    return _dot(ah, bh) + _dot(ah, bl) + _dot(al, bh)


def _dot_sel(w01, x):
    hi, mid, lo = _split3(x)
    return _dot(w01, hi) + _dot(w01, mid) + _dot(w01, lo)


def _log_sigmoid(x):
    return jnp.minimum(x, 0.0) - jnp.log1p(jnp.exp(-jnp.abs(x)))


def _silu(x):
    return x * jax.nn.sigmoid(x)


def _adaln_kernel(c_ref, w_ref, b_ref, o_ref):
    o_ref[...] = _dot3(_silu(c_ref[...]), w_ref[...]) + b_ref[...]


def _adaln(c, w, b, tn=ADALN_COL_TILE):
    nb, d = c.shape
    n = w.shape[1]
    return pl.pallas_call(
        _adaln_kernel,
        grid=(n // tn,),
        in_specs=[pl.BlockSpec((nb, d), lambda j: (0, 0)),
                  pl.BlockSpec((d, tn), lambda j: (0, j)),
                  pl.BlockSpec((1, tn), lambda j: (0, j))],
        out_specs=pl.BlockSpec((nb, tn), lambda j: (0, j)),
        out_shape=jax.ShapeDtypeStruct((nb, n), F32),
        compiler_params=_cparams(("arbitrary",)),
        name="adaln",
    )(c, w, b)


def _rms(x):
    return x * lax.rsqrt(jnp.mean(x * x, axis=-1, keepdims=True) + RMS_EPS)


def _normmod_kernel(x_ref, g_ref, sh_ref, sc_ref, o_ref):
    y = _rms(x_ref[...]) * g_ref[...]
    o_ref[...] = (y * (1.0 + sc_ref[...]) + sh_ref[...]).astype(o_ref.dtype)


def _mod_spec(k, rb, d, tiles_per_batch):
    return pl.BlockSpec((None, None, rb, d), lambda i: (k, i // tiles_per_batch, 0, 0))


def _normmod(x, g, mod, tm, tiles_per_batch):
    m, d = x.shape
    rb = mod.shape[2]
    return pl.pallas_call(
        _normmod_kernel,
        grid=(m // tm,),
        in_specs=[pl.BlockSpec((tm, d), lambda i: (i, 0)),
                  pl.BlockSpec((1, d), lambda i: (0, 0)),
                  _mod_spec(0, rb, d, tiles_per_batch),
                  _mod_spec(1, rb, d, tiles_per_batch)],
        out_specs=pl.BlockSpec((tm, d), lambda i: (i, 0)),
        out_shape=jax.ShapeDtypeStruct((m, d), BF16),
        compiler_params=_cparams(("arbitrary",)),
        name="normmod",
    )(x, g, mod, mod)


def _wt_spec(k, row0, tn):
    assert row0 % 8 == 0 and tn % 8 == 0
    return pl.BlockSpec((pl.Element(tn), pl.Element(k)), lambda j, i: (pl.multiple_of(row0 + j * tn, 8), 0))


def _mm_wt_kernel(x_ref, wt_ref, *refs):
    o_refs, wb = refs[:-1], refs[-1]

    @pl.when(pl.program_id(1) == 0)
    def _():
        wb[...] = wt_ref[...].astype(BF16)

    r = _dot_nt(x_ref[...], wb[...])
    for o_ref in o_refs:
        o_ref[...] = r.astype(o_ref.dtype)


def _mm_wt(x, wt, row0, n, tm, tn, out_dtypes=(F32,)):
    m, k = x.shape
    return pl.pallas_call(
        _mm_wt_kernel,
        grid=(n // tn, m // tm),
        in_specs=[pl.BlockSpec((tm, k), lambda j, i: (i, 0)), _wt_spec(k, row0, tn)],
        out_specs=[pl.BlockSpec((tm, tn), lambda j, i: (i, j)) for _ in out_dtypes],
        out_shape=[jax.ShapeDtypeStruct((m, n), dt) for dt in out_dtypes],
        scratch_shapes=[pltpu.VMEM((tn, k), BF16)],
        compiler_params=_cparams(("arbitrary", "arbitrary")),
        name="proj_mm",
    )(x, wt)


def _mm_heads_kernel(x_ref, wt_ref, o_ref, o16_ref, wb):
    @pl.when(pl.program_id(1) == 0)
    def _():
        wb[...] = wt_ref[...].astype(BF16)

    r = _dot_nt(x_ref[...], wb[...])
    for h in range(FOX_HEADS):
        o_ref[:, h, :] = r[:, h * FOX_DH:(h + 1) * FOX_DH]
    o16_ref[...] = r.astype(BF16)


def _mm_heads(x, wt, row0, tm):
    m, k = x.shape
    return pl.pallas_call(
        _mm_heads_kernel,
        grid=(1, m // tm),
        in_specs=[pl.BlockSpec((tm, k), lambda j, i: (i, 0)), _wt_spec(k, row0, FOX_W)],
        out_specs=[pl.BlockSpec((tm, FOX_HEADS, FOX_DH), lambda j, i: (i, 0, 0)),
                   pl.BlockSpec((tm, FOX_W), lambda j, i: (i, 0))],
        out_shape=[jax.ShapeDtypeStruct((m, FOX_HEADS, FOX_DH), F32),
                   jax.ShapeDtypeStruct((m, FOX_W), BF16)],
        scratch_shapes=[pltpu.VMEM((FOX_W, k), BF16)],
        compiler_params=_cparams(("arbitrary", "arbitrary")),
        name="proj_heads",
    )(x, wt)


def _gla_tables(p=GLA_LEVELS):
    c = GLA_CHUNK
    nmm = min(p, GLA_MM_LEVELS)
    t = np.arange(c)[:, None]
    m = np.arange(c)[None, :]
    wall = np.zeros((nmm + 1, c, c), np.float32)
    mask = np.zeros((p + 1, c, c), np.float32)
    for l in range(p):
        half = 1 << l
        pos = t % (2 * half)
        mid = t - pos + half
        right = pos >= half
        if l < nmm:
            wall[l] = np.where(right, (m >= mid) & (m <= t), (m > t) & (m < mid))
        s = m
        mask[l] = ((t >> (l + 1)) == (s >> (l + 1))) & (((t >> l) & 1) == 1) & (((s >> l) & 1) == 0)
    wall[nmm] = m <= t
    mask[p] = t == m
    return wall.reshape((nmm + 1) * c, c), mask


def _gla_level_exponent(cum, row, l):
    half = 1 << l
    c = cum.shape[0]
    bound = None
    for start in range(0, c, 2 * half):
        b = cum[start + half - 1:start + half, :]
        bound = b if bound is None else jnp.where(row >= start, b, bound)
    return jnp.where(((row >> l) & 1) == 1, cum - bound, bound - cum)


def _gla_body(q_ref, k_ref, v_ref, ra_ref, sm_ref, wall_ref, mask_ref, wa2_ref, ba_ref, g_ref,
              og_ref, st_ref, *, nchunk, valid, store_rows):
    c, p = GLA_CHUNK, mask_ref.shape[0] - 1
    nmm = min(p, GLA_MM_LEVELS)
    row = lax.broadcasted_iota(jnp.int32, (c, GLA_DK), 0)
    rowh = lax.broadcasted_iota(jnp.int32, (c, GLA_KW), 0)
    wall = wall_ref[...]
    for ci in range(nchunk):
        rows = pl.ds(ci * c, c)
        x = _dot3(sm_ref[rows, :], wa2_ref[...]) + ba_ref[...]
        la = _log_sigmoid(x) * (1.0 / GLA_TAU)
        if valid < c:
            la = jnp.where(rowh < valid, la, 0.0)
        args = _dot_sel(wall, la)
        cum = args[nmm * c:(nmm + 1) * c]
        e_lvl = [jnp.exp(args[l * c:(l + 1) * c]) for l in range(nmm)]
        e_lvl += [jnp.exp(_gla_level_exponent(cum, rowh, l)) for l in range(nmm, p)]
        last = cum[c - 1:c, :]
        e_cum = jnp.exp(cum)
        e_rev = jnp.exp(last - cum)
        e_last = jnp.exp(last)
        for h in range(GLA_HEADS):
            ks = slice(h * GLA_DK, (h + 1) * GLA_DK)
            vs = slice(h * GLA_DV, (h + 1) * GLA_DV)
            q = q_ref[rows, ks] * (GLA_DK ** -0.5)
            k = k_ref[rows, ks]
            vb = v_ref[rows, vs].astype(BF16)
            a = mask_ref[p] * _dot_nt(q.astype(BF16), k.astype(BF16))
            for l in range(p):
                xl = (jnp.where(((row >> l) & 1) == 1, q, k) * e_lvl[l][:, ks]).astype(BF16)
                a = a + mask_ref[l] * _dot_nt(xl, xl)
            st = st_ref[h]
            qc = (q * e_cum[:, ks]).astype(BF16)
            o = _dot(a.astype(BF16), vb) + _dot_nt(qc, st.astype(BF16))
            kr = (k * e_rev[:, ks]).astype(BF16)
            st_ref[h] = e_last[:, ks] * st + _dot_tn(vb, kr)
            og = _rms(o) * g_ref[...] * _silu(ra_ref[rows, vs])
            if store_rows < c:
                og_ref[:, vs] = og[:store_rows].astype(og_ref.dtype)
            else:
                og_ref[rows, vs] = og.astype(og_ref.dtype)


def _gla_prompt_kernel(q_ref, k_ref, v_ref, ra_ref, sm_ref, wall_ref, mask_ref, wa2_ref, ba_ref, g_ref, s0_ref,
                       og_ref, s_ref, st_ref, *, nchunk):
    n = pl.program_id(1)

    @pl.when(n == 0)
    def _():
        for h in range(GLA_HEADS):
            st_ref[h] = s0_ref[h].T

    _gla_body(q_ref, k_ref, v_ref, ra_ref, sm_ref, wall_ref, mask_ref, wa2_ref, ba_ref, g_ref, og_ref, st_ref,
              nchunk=nchunk, valid=GLA_CHUNK, store_rows=GLA_CHUNK)

    @pl.when(n == pl.num_programs(1) - 1)
    def _():
        for h in range(GLA_HEADS):
            s_ref[h] = st_ref[h].T


def _gla_prompt(pg, small, s0, wall, mask, wa2, ba, g, batch, seq, tb=GLA_ROWS_PER_STEP):
    nblk = seq // tb
    rowmap = lambda cb: (lambda b, n: (b * nblk + n, cb))
    const2 = lambda b, n: (0, 0)
    return pl.pallas_call(
        functools.partial(_gla_prompt_kernel, nchunk=tb // GLA_CHUNK),
        grid=(batch, nblk),
        in_specs=[pl.BlockSpec((tb, GLA_KW), rowmap(0)),
                  pl.BlockSpec((tb, GLA_KW), rowmap(1)),
                  pl.BlockSpec((tb, GLA_VW), rowmap(1)),
                  pl.BlockSpec((tb, GLA_VW), rowmap(2)),
                  pl.BlockSpec((tb, LANES), rowmap(0)),
                  pl.BlockSpec(wall.shape, const2),
                  pl.BlockSpec(mask.shape, lambda b, n: (0, 0, 0)),
                  pl.BlockSpec(wa2.shape, const2),
                  pl.BlockSpec(ba.shape, const2),
                  pl.BlockSpec(g.shape, const2),
                  pl.BlockSpec((None, GLA_HEADS, GLA_DK, GLA_DV), lambda b, n: (b, 0, 0, 0))],
        out_specs=[pl.BlockSpec((tb, GLA_VW), rowmap(0)),
                   pl.BlockSpec((None, GLA_HEADS, GLA_DK, GLA_DV), lambda b, n: (b, 0, 0, 0))],
        out_shape=[jax.ShapeDtypeStruct((batch * seq, GLA_VW), BF16),
                   jax.ShapeDtypeStruct((batch, GLA_HEADS, GLA_DK, GLA_DV), F32)],
        scratch_shapes=[pltpu.VMEM((GLA_HEADS, GLA_DV, GLA_DK), F32)],
        compiler_params=_cparams(("arbitrary", "arbitrary")),
        name="gla_prompt",
    )(pg, pg, pg, pg, small, wall, mask, wa2, ba, g, s0)


def _gla_sample_kernel(pg_ref, sm_ref, wall_ref, mask_ref, wa2_ref, ba_ref, g_ref, s0_ref,
                       og_ref, s_ref, pad_ref, smpad_ref, st_ref, *, t):
    @pl.when(pl.program_id(0) == 0)
    def _():
        pad_ref[...] = jnp.zeros(pad_ref.shape, F32)
        smpad_ref[...] = jnp.zeros(smpad_ref.shape, F32)

    pad_ref[0:t, :] = pg_ref[...]
    smpad_ref[0:t, :] = sm_ref[...]
    for h in range(GLA_HEADS):
        st_ref[h] = s0_ref[h].T
    q_ref = pad_ref.at[:, 0:GLA_KW]
    k_ref = pad_ref.at[:, GLA_KW:2 * GLA_KW]
    v_ref = pad_ref.at[:, 2 * GLA_KW:2 * GLA_KW + GLA_VW]
    ra_ref = pad_ref.at[:, 2 * GLA_KW + GLA_VW:2 * GLA_KW + 2 * GLA_VW]
    _gla_body(q_ref, k_ref, v_ref, ra_ref, smpad_ref, wall_ref, mask_ref, wa2_ref, ba_ref, g_ref, og_ref, st_ref,
              nchunk=1, valid=t, store_rows=t)
    for h in range(GLA_HEADS):
        s_ref[h] = st_ref[h].T


def _gla_sample(pg, small, s0, wall, mask, wa2, ba, g, batch, t):
    width = pg.shape[-1]
    const2 = lambda b: (0, 0)
    return pl.pallas_call(
        functools.partial(_gla_sample_kernel, t=t),
        grid=(batch,),
        in_specs=[pl.BlockSpec((None, t, width), lambda b: (b, 0, 0)),
                  pl.BlockSpec((None, t, LANES), lambda b: (b, 0, 0)),
                  pl.BlockSpec(wall.shape, const2),
                  pl.BlockSpec(mask.shape, lambda b: (0, 0, 0)),
                  pl.BlockSpec(wa2.shape, const2),
                  pl.BlockSpec(ba.shape, const2),
                  pl.BlockSpec(g.shape, const2),
                  pl.BlockSpec((None, GLA_HEADS, GLA_DK, GLA_DV), lambda b: (b, 0, 0, 0))],
        out_specs=[pl.BlockSpec((None, t, GLA_VW), lambda b: (b, 0, 0)),
                   pl.BlockSpec((None, GLA_HEADS, GLA_DK, GLA_DV), lambda b: (b, 0, 0, 0))],
        out_shape=[jax.ShapeDtypeStruct((batch, t, GLA_VW), F32),
                   jax.ShapeDtypeStruct((batch, GLA_HEADS, GLA_DK, GLA_DV), F32)],
        scratch_shapes=[pltpu.VMEM((GLA_CHUNK, width), F32),
                        pltpu.VMEM((GLA_CHUNK, LANES), F32),
                        pltpu.VMEM((GLA_HEADS, GLA_DV, GLA_DK), F32)],
        compiler_params=_cparams(("arbitrary",)),
        name="gla_sample",
    )(pg, small, wall, mask, wa2, ba, g, s0)


FB_LANE = GLA_RANK


LOG2E = 1.4426950408889634


def _fox_bias_prompt_kernel(sm_ref, bf_ref, tri_ref, lf_ref, fq_ref, fk_ref, carry_ref):
    @pl.when(pl.program_id(1) == 0)
    def _():
        carry_ref[...] = jnp.zeros(carry_ref.shape, F32)

    lf = _log_sigmoid(sm_ref[...] + bf_ref[...])
    lf_ref[...] = lf[:, FB_LANE:FB_LANE + FOX_HEADS]
    cum = _dot_sel(tri_ref[...], lf) + carry_ref[...]
    carry_ref[...] = cum[cum.shape[0] - 1:, :]
    lane = lax.broadcasted_iota(jnp.int32, cum.shape, 1)
    ones = jnp.where(lane < 6, 1.0, 0.0)
    for h in range(FOX_HEADS):
        f = jnp.broadcast_to(cum[:, FB_LANE + h:FB_LANE + h + 1] * LOG2E, cum.shape)
        hi = f.astype(BF16).astype(F32)
        r = f - hi
        mid = r.astype(BF16).astype(F32)
        lo = r - mid
        fq = jnp.where(lane == 0, hi, jnp.where(lane == 1, mid, jnp.where(lane == 2, lo, ones)))
        fk = jnp.where(lane == 3, -hi, jnp.where(lane == 4, -mid, jnp.where(lane == 5, -lo, ones)))
        fq_ref[:, h * LANES:(h + 1) * LANES] = fq.astype(BF16)
        fk_ref[:, h * LANES:(h + 1) * LANES] = fk.astype(BF16)


def _fox_bias_prompt(small, bf_row, batch, seq, tb=FOX_BIAS_ROWS):
    nblk = seq // tb
    tri = jnp.asarray(np.tril(np.ones((tb, tb), np.float32)), BF16)
    wide = FOX_HEADS * LANES
    return pl.pallas_call(
        _fox_bias_prompt_kernel,
        grid=(batch, nblk),
        in_specs=[pl.BlockSpec((tb, LANES), lambda b, n: (b * nblk + n, 0)),
                  pl.BlockSpec((1, LANES), lambda b, n: (0, 0)),
                  pl.BlockSpec((tb, tb), lambda b, n: (0, 0))],
        out_specs=[pl.BlockSpec((tb, FOX_HEADS), lambda b, n: (b * nblk + n, 0)),
                   pl.BlockSpec((tb, wide), lambda b, n: (b * nblk + n, 0)),
                   pl.BlockSpec((tb, wide), lambda b, n: (b * nblk + n, 0))],
        out_shape=[jax.ShapeDtypeStruct((batch * seq, FOX_HEADS), F32),
                   jax.ShapeDtypeStruct((batch * seq, wide), BF16),
                   jax.ShapeDtypeStruct((batch * seq, wide), BF16)],
        scratch_shapes=[pltpu.VMEM((1, LANES), F32)],
        compiler_params=_cparams(("arbitrary", "arbitrary")),
        name="fox_bias_prompt",
    )(small, bf_row, tri)


def _fox_bias_sample_kernel(sm_ref, bf_ref, sel_ref, lf_ref, fn_ref):
    lf = _log_sigmoid(sm_ref[...] + bf_ref[...])
    lf_ref[...] = lf[:, FB_LANE:FB_LANE + FOX_HEADS]
    cum = _dot_sel(sel_ref[...], lf)
    fn_ref[...] = cum[:, FB_LANE:FB_LANE + FOX_HEADS]


def _fox_bias_sample(small, bf_row, t):
    rows = small.shape[0]
    r = np.arange(rows)
    sel = ((r[:, None] // t) == (r[None, :] // t)) & (r[None, :] <= r[:, None])
    sel = jnp.asarray(sel.astype(np.float32), BF16)
    full = lambda shape: pl.BlockSpec(shape, lambda i: tuple(0 for _ in shape))
    return pl.pallas_call(
        _fox_bias_sample_kernel,
        grid=(1,),
        in_specs=[full((rows, LANES)), full((1, LANES)), full((rows, rows))],
        out_specs=[full((rows, FOX_HEADS)), full((rows, FOX_HEADS))],
        out_shape=[jax.ShapeDtypeStruct((rows, FOX_HEADS), F32),
                   jax.ShapeDtypeStruct((rows, FOX_HEADS), F32)],
        compiler_params=_cparams(("arbitrary",)),
        name="fox_bias_sample",
    )(small, bf_row, sel)


FOX_HEADS_PER_STEP = 8


def _fox_prompt_kernel(q_ref, k_ref, v_ref, fq_ref, fk_ref, o_ref, *, tq, tk):
    hb = FOX_HEADS_PER_STEP
    i = pl.program_id(2)
    rowi = lax.broadcasted_iota(jnp.int32, (tq, tk), 0)
    coli = lax.broadcasted_iota(jnp.int32, (tq, tk), 1)
    qs = []
    for hh in range(hb):
        hs = slice(hh * FOX_DH, (hh + 1) * FOX_DH)
        q = (q_ref[:, hs].astype(F32) * (FOX_DH ** -0.5 * LOG2E)).astype(BF16)
        qs.append(jnp.concatenate([q, fq_ref[:, hs]], axis=1))

    def step(j, carry, masked):
        ks = pl.ds(pl.multiple_of(j * tk, tk), tk)
        out = []
        for hh in range(hb):
            hs = slice(hh * FOX_DH, (hh + 1) * FOX_DH)
            m, l, acc = carry[hh]
            s = _dot_nt(qs[hh], jnp.concatenate([k_ref[ks, hs], fk_ref[ks, hs]], axis=1))
            if masked:
                s = jnp.where(rowi + i * tq >= coli + j * tk, s, NEG)
            m_new = jnp.maximum(m, jnp.max(s, axis=-1, keepdims=True))
            p = jnp.exp2(s - m_new)
            alpha = jnp.exp2(m - m_new)
            l = alpha * l + jnp.sum(p, axis=-1, keepdims=True)
            acc = alpha * acc + _dot(p.astype(BF16), v_ref[ks, hs])
            out.append((m_new, l, acc))
        return tuple(out)

    init = tuple((jnp.full((tq, 1), NEG, F32), jnp.zeros((tq, 1), F32), jnp.zeros((tq, FOX_DH), F32))
                 for _ in range(hb))
    nfull = (i * tq) // tk
    carry = lax.fori_loop(0, nfull, lambda j, c: step(j, c, False), init)
    for d in range(pl.cdiv(tq, tk)):
        carry = step(nfull + d, carry, True)
    for hh in range(hb):
        m, l, acc = carry[hh]
        o_ref[:, hh * FOX_DH:(hh + 1) * FOX_DH] = (acc / l).astype(o_ref.dtype)


def _fox_prompt(qb, kb, vb, fq, fk, batch, seq, tq=FOX_Q_TILE, tk=FOX_K_TILE):
    nq = seq // tq
    hb = FOX_HEADS_PER_STEP
    wb = hb * FOX_DH
    return pl.pallas_call(
        functools.partial(_fox_prompt_kernel, tq=tq, tk=tk),
        grid=(batch, FOX_HEADS // hb, nq),
        in_specs=[pl.BlockSpec((tq, wb), lambda b, h, i: (b * nq + i, h)),
                  pl.BlockSpec((seq, wb), lambda b, h, i: (b, h)),
                  pl.BlockSpec((seq, wb), lambda b, h, i: (b, h)),
                  pl.BlockSpec((tq, wb), lambda b, h, i: (b * nq + i, h)),
                  pl.BlockSpec((seq, wb), lambda b, h, i: (b, h))],
        out_specs=pl.BlockSpec((tq, wb), lambda b, h, i: (b * nq + i, h)),
        out_shape=jax.ShapeDtypeStruct((batch * seq, FOX_W), BF16),
        compiler_params=_cparams(("arbitrary", "arbitrary", "arbitrary")),
        name="fox_prompt",
    )(qb, kb, vb, fq, fk)


PAGES_PER_STEP = 16
ROWS8 = 8
QROWS = FOX_HEADS * ROWS8
PAGE_FLAT = PAGE * FOX_HEADS


def _dot_sel_rhs(x, w01):
    hi, mid, lo = _split3(x)
    m = x.shape[0]
    stacked = jnp.concatenate([hi.astype(F32), mid.astype(F32), lo.astype(F32)], axis=0).astype(BF16)
    r = _dot(stacked, w01)
    return r[0:m] + r[m:2 * m] + r[2 * m:3 * m]


def _fox_sample_kernel(pt_ref, q_ref, kn_ref, vn_ref, fn_ref, fnrow_ref, madd_ref, maddn_ref, usuf_ref, pfx_ref,
                       hsum_ref, pexp_ref, hmask_ref, *refs, t):
    g = PAGES_PER_STEP
    k_refs, v_refs, lf_refs = refs[0:g], refs[g:2 * g], refs[2 * g:3 * g]
    o_ref = refs[3 * g]
    q_sc, m_sc, l_sc, acc_sc, carry_sc = refs[3 * g + 1:]
    j = pl.program_id(1)
    fn_t = fn_ref[...][:, 0:1]

    @pl.when(j == 0)
    def _():
        q_sc[...] = jnp.zeros(q_sc.shape, F32)
        for h in range(FOX_HEADS):
            q_sc[h * ROWS8:h * ROWS8 + t, :] = q_ref[:, h * FOX_DH:(h + 1) * FOX_DH] * (FOX_DH ** -0.5)
        carry_sc[...] = jnp.zeros(carry_sc.shape, F32)
        pad = jnp.zeros((LANES - t * FOX_HEADS, FOX_DH), F32)
        kn = jnp.concatenate([kn_ref[...].reshape(t * FOX_HEADS, FOX_DH), pad], axis=0)
        vn = jnp.concatenate([vn_ref[...].reshape(t * FOX_HEADS, FOX_DH), pad], axis=0)
        s = _dot_nt(q_sc[...], kn) + fn_t - fnrow_ref[...] + maddn_ref[...]
        m = jnp.max(s, axis=-1, keepdims=True)
        p = jnp.exp(s - m)
        m_sc[...] = jnp.broadcast_to(m, m_sc.shape)
        l_sc[...] = jnp.broadcast_to(jnp.sum(p, axis=-1, keepdims=True), l_sc.shape)
        acc_sc[...] = _dot(p, vn)

    lf = jnp.concatenate([lf_refs[gi][...] for gi in range(g)], axis=0)
    tot = jnp.broadcast_to(jnp.sum(lf, axis=-1, keepdims=True), lf.shape)
    carry = carry_sc[...]
    r_hk = (_dot_sel_rhs(lf, usuf_ref[...]) + _dot_sel(pfx_ref[...], tot)
            + jnp.concatenate([carry] * g, axis=0))
    carry_sc[...] = carry + _dot_sel(hsum_ref[...], tot)
    spread = _dot_sel_rhs(r_hk, pexp_ref[...]) * hmask_ref[...]
    r_flat = jnp.sum(spread.reshape(g, FOX_HEADS, PAGE_FLAT), axis=1)
    q = q_sc[...]
    madd = madd_ref[...] + fn_t
    m_old = m_sc[...]
    m_new = m_old
    s_list = []
    for gi in range(g):
        bias = madd + r_flat[gi:gi + 1, :]
        s = _dot_nt(q, k_refs[gi][...].reshape(PAGE_FLAT, FOX_DH)) + bias
        s_list.append(s)
        m_new = jnp.maximum(m_new, jnp.max(s, axis=-1, keepdims=True))
    alpha = jnp.exp(m_old - m_new)
    l = alpha * l_sc[...]
    acc = alpha * acc_sc[...]
    m_col = m_new[:, 0:1]
    for gi in range(g):
        p = jnp.exp(s_list[gi] - m_col)
        l = l + jnp.sum(p, axis=-1, keepdims=True)
        acc = acc + _dot(p, v_refs[gi][...].reshape(PAGE_FLAT, FOX_DH))
    m_sc[...] = m_new
    l_sc[...] = l
    acc_sc[...] = acc

    @pl.when(j == pl.num_programs(1) - 1)
    def _():
        o = acc / l
        for h in range(FOX_HEADS):
            o_ref[:, h * FOX_DH:(h + 1) * FOX_DH] = o[h * ROWS8:h * ROWS8 + t, :].astype(o_ref.dtype)


def _fox_sample_tables(t):
    row = np.arange(QROWS)[:, None]
    col = np.arange(PAGE_FLAT)[None, :]
    madd = np.where((row // ROWS8) == (col % FOX_HEADS), 0.0, NEG).astype(np.float32)
    coln = np.arange(LANES)[None, :]
    ok = (coln < t * FOX_HEADS) & ((row // ROWS8) == (coln % FOX_HEADS)) & ((coln // FOX_HEADS) <= (row % ROWS8))
    maddn = np.where(ok, 0.0, NEG).astype(np.float32)
    key = np.arange(PAGE)
    usuf = (key[:, None] > key[None, :]).astype(np.float32)
    gh = np.arange(PAGES_PER_STEP * FOX_HEADS)
    same_head = (gh[:, None] % FOX_HEADS) == (gh[None, :] % FOX_HEADS)
    pfx = (same_head & ((gh[None, :] // FOX_HEADS) < (gh[:, None] // FOX_HEADS))).astype(np.float32)
    hsum = (np.arange(FOX_HEADS)[:, None] == (gh[None, :] % FOX_HEADS)).astype(np.float32)
    pexp = (key[:, None] == (col // FOX_HEADS)).astype(np.float32)
    hmask = ((gh[:, None] % FOX_HEADS) == (col % FOX_HEADS)).astype(np.float32)
    return (jnp.asarray(madd), jnp.asarray(maddn), jnp.asarray(usuf, BF16), jnp.asarray(pfx, BF16),
            jnp.asarray(hsum, BF16), jnp.asarray(pexp, BF16), jnp.asarray(hmask))


def _fox_sample(page_table, q, kn, vn, fn, cache_k, cache_v, cache_lf, t):
    batch, npages = page_table.shape
    g = PAGES_PER_STEP
    nsteps = npages // g
    tables = _fox_sample_tables(t)
    fn_rows = jnp.pad(fn.transpose(0, 2, 1), ((0, 0), (0, 0), (0, ROWS8 - t))).reshape(batch, QROWS, 1)
    fn_rows = jnp.broadcast_to(fn_rows, (batch, QROWS, LANES))
    fn_cols = jnp.pad(fn.reshape(batch, 1, t * FOX_HEADS), ((0, 0), (0, 0), (0, LANES - t * FOX_HEADS)))

    def page_map5(gi):
        return lambda b, j, pt: (0, pt[b, npages - 1 - (j * g + gi)], 0, 0, 0)

    def page_map3(gi):
        return lambda b, j, pt: (pt[b, npages - 1 - (j * g + gi)], 0, 0)

    seq3 = lambda b, j, pt: (b, 0, 0)
    seq4 = lambda b, j, pt: (b, 0, 0, 0)
    const2 = lambda b, j, pt: (0, 0)
    in_specs = [pl.BlockSpec((None, t, FOX_W), seq3),
                pl.BlockSpec((None, t, FOX_HEADS, FOX_DH), seq4),
                pl.BlockSpec((None, t, FOX_HEADS, FOX_DH), seq4),
                pl.BlockSpec((None, QROWS, LANES), seq3),
                pl.BlockSpec((None, 1, LANES), seq3)]
    in_specs += [pl.BlockSpec(tb.shape, const2) for tb in tables]
    in_specs += [pl.BlockSpec((None, None, PAGE, FOX_HEADS, FOX_DH), page_map5(gi)) for gi in range(g)]
    in_specs += [pl.BlockSpec((None, None, PAGE, FOX_HEADS, FOX_DH), page_map5(gi)) for gi in range(g)]
    in_specs += [pl.BlockSpec((None, FOX_HEADS, PAGE), page_map3(gi)) for gi in range(g)]
    grid_spec = pltpu.PrefetchScalarGridSpec(
        num_scalar_prefetch=1,
        grid=(batch, nsteps),
        in_specs=in_specs,
        out_specs=pl.BlockSpec((None, t, FOX_W), seq3),
        scratch_shapes=[pltpu.VMEM((QROWS, FOX_DH), F32),
                        pltpu.VMEM((QROWS, LANES), F32),
                        pltpu.VMEM((QROWS, LANES), F32),
                        pltpu.VMEM((QROWS, FOX_DH), F32),
                        pltpu.VMEM((FOX_HEADS, PAGE), F32)],
    )
    return pl.pallas_call(
        functools.partial(_fox_sample_kernel, t=t),
        grid_spec=grid_spec,
        out_shape=jax.ShapeDtypeStruct((batch, t, FOX_W), F32),
        compiler_params=_cparams(("arbitrary", "arbitrary")),
        name="fox_sample",
    )(page_table, q, kn, vn, fn_rows, fn_cols, *tables,
      *([cache_k] * g), *([cache_v] * g), *([cache_lf] * g))


def _merge_kernel(oa_ref, ob_ref, wa_ref, wb_ref, ga_ref, gb_ref, o_ref):
    ua = _dot(oa_ref[...], wa_ref[...])
    ub = _dot(ob_ref[...], wb_ref[...])
    ga = jax.nn.sigmoid(ga_ref[...].astype(F32))
    gb = jax.nn.sigmoid(gb_ref[...].astype(F32))
    o_ref[...] = (ga * ua + gb * ub).astype(o_ref.dtype)


def _merge(oa, ob, wa, wb, gates, tm, tn=1024):
    m = oa.shape[0]
    d = wa.shape[1]
    nj = d // tn
    return pl.pallas_call(
        _merge_kernel,
        grid=(nj, m // tm),
        in_specs=[pl.BlockSpec((tm, GLA_VW), lambda j, i: (i, 0)),
                  pl.BlockSpec((tm, FOX_W), lambda j, i: (i, 0)),
                  pl.BlockSpec((GLA_VW, tn), lambda j, i: (0, j)),
                  pl.BlockSpec((FOX_W, tn), lambda j, i: (0, j)),
                  pl.BlockSpec((tm, tn), lambda j, i: (i, j)),
                  pl.BlockSpec((tm, tn), lambda j, i: (i, nj + j))],
        out_specs=pl.BlockSpec((tm, tn), lambda j, i: (i, j)),
        out_shape=jax.ShapeDtypeStruct((m, d), BF16),
        compiler_params=_cparams(("arbitrary", "arbitrary")),
        name="merge",
    )(oa, ob, wa, wb, gates, gates)


ROUTER_GROUP_LANE = N_EXPERTS


def _route(logits):
    lane_i = lax.broadcasted_iota(jnp.int32, logits.shape, 1)
    lane = lane_i.astype(F32)
    grp_of_lane = (lane_i >> 2).astype(F32)
    big = float(LANES)
    is_grp = (lane_i >= ROUTER_GROUP_LANE) & (lane_i < ROUTER_GROUP_LANE + N_GROUPS)
    gl = jnp.where(is_grp, logits, NEG)
    gmax = jnp.max(gl, axis=-1, keepdims=True)
    g_idx = jnp.min(jnp.where(is_grp & (gl == gmax), lane - ROUTER_GROUP_LANE, big), axis=-1, keepdims=True)
    g_w = 1.0 / jnp.sum(jnp.where(is_grp, jnp.exp(gl - gmax), 0.0), axis=-1, keepdims=True)
    in_grp = (lane_i < N_EXPERTS) & (grp_of_lane == g_idx)
    e1 = jnp.where(in_grp, logits, NEG)
    v1 = jnp.max(e1, axis=-1, keepdims=True)
    i1 = jnp.min(jnp.where(in_grp & (e1 == v1), lane, big), axis=-1, keepdims=True)
    rest = in_grp & (lane != i1)
    e2 = jnp.where(rest, logits, NEG)
    v2 = jnp.max(e2, axis=-1, keepdims=True)
    i2 = jnp.min(jnp.where(rest & (e2 == v2), lane, big), axis=-1, keepdims=True)
    r = jnp.exp(v2 - v1)
    w1 = g_w / (1.0 + r)
    w2 = g_w * r / (1.0 + r)
    grp_onehot = jnp.where(is_grp & (lane - ROUTER_GROUP_LANE == g_idx), 1.0, 0.0)
    return jnp.where(lane == i1, w1, 0.0) + jnp.where(lane == i2, w2, 0.0) + grp_onehot


HX_W = D_MODEL + LANES


def _outproj_kernel(mg_ref, w_ref, x_ref, gt_ref, g2_ref, sh_ref, sc_ref, wr_ref, br_ref, x1_ref, hx_ref):
    x1 = x_ref[...] + gt_ref[...] * _dot(mg_ref[...], w_ref[...])
    x1_ref[...] = x1
    h2 = _rms(x1) * g2_ref[...] * (1.0 + sc_ref[...]) + sh_ref[...]
    hx_ref[:, 0:D_MODEL] = h2
    hx_ref[:, D_MODEL:HX_W] = _route(_dot3(h2, wr_ref[...]) + br_ref[...])


def _outproj(merged, w_out, x, mod, g2, w_router, b_router, tm, tiles_per_batch):
    m, d = x.shape
    rb = mod.shape[2]
    const2 = lambda i: (0, 0)
    return pl.pallas_call(
        _outproj_kernel,
        grid=(m // tm,),
        in_specs=[pl.BlockSpec((tm, d), lambda i: (i, 0)),
                  pl.BlockSpec((d, d), const2),
                  pl.BlockSpec((tm, d), lambda i: (i, 0)),
                  _mod_spec(2, rb, d, tiles_per_batch),
                  pl.BlockSpec((1, d), const2),
                  _mod_spec(3, rb, d, tiles_per_batch),
                  _mod_spec(4, rb, d, tiles_per_batch),
                  pl.BlockSpec((d, LANES), const2),
                  pl.BlockSpec((1, LANES), const2)],
        out_specs=[pl.BlockSpec((tm, d), lambda i: (i, 0)),
                   pl.BlockSpec((tm, HX_W), lambda i: (i, 0))],
        out_shape=[jax.ShapeDtypeStruct((m, d), F32),
                   jax.ShapeDtypeStruct((m, HX_W), F32)],
        compiler_params=_cparams(("arbitrary",)),
        name="outproj",
    )(merged, w_out, x, mod, g2, mod, mod, w_router, b_router)


MOE_TILE = 512
PLAN_TILE = 512


def _moe_plan_kernel(r_ref, tri_ref, excl_ref, pos_ref, tg_ref, nu_ref, cnt_sc, off_sc, run_sc):
    ph = pl.program_id(0)
    n = pl.program_id(1)
    lane = lax.broadcasted_iota(jnp.int32, (1, LANES), 1)
    lane_t = lax.broadcasted_iota(jnp.int32, r_ref.shape, 1)
    is_grp = (lane_t >= ROUTER_GROUP_LANE) & (lane_t < ROUTER_GROUP_LANE + N_GROUPS)
    g4 = jnp.where(is_grp, r_ref[...], 0.0)

    @pl.when((ph == 0) & (n == 0))
    def _():
        cnt_sc[...] = jnp.zeros(cnt_sc.shape, F32)

    @pl.when(ph == 0)
    def _():
        cnt_sc[...] += jnp.sum(g4, axis=0, keepdims=True)

    @pl.when((ph == 1) & (n == 0))
    def _():
        padded = jnp.floor((cnt_sc[...] + (MOE_TILE - 1)) * (1.0 / MOE_TILE)) * MOE_TILE
        off = _dot_sel_rhs(jnp.broadcast_to(padded, (8, LANES)), excl_ref[...])[0:1]
        off_sc[...] = off
        run_sc[...] = jnp.zeros(run_sc.shape, F32)
        end = off + padded
        tile_start = lane.astype(F32) * MOE_TILE
        tg = jnp.zeros((1, LANES), F32)
        for g in range(N_GROUPS):
            end_g = jnp.sum(jnp.where(lane == ROUTER_GROUP_LANE + g, end, 0.0), axis=-1, keepdims=True)
            tg = tg + jnp.where(end_g <= tile_start, 1.0, 0.0)
        tg_ref[...] = jnp.minimum(tg, N_GROUPS - 1.0).astype(jnp.int32)
        total = jnp.sum(jnp.where(lane == ROUTER_GROUP_LANE + N_GROUPS - 1, end, 0.0), axis=-1, keepdims=True)
        nu_ref[...] = jnp.broadcast_to(total * (1.0 / MOE_TILE), (1, LANES)).astype(jnp.int32)

    @pl.when(ph == 1)
    def _():
        rank = _dot(tri_ref[...], g4.astype(BF16)) + run_sc[...]
        run_sc[...] += jnp.sum(g4, axis=0, keepdims=True)
        posv = g4 * (off_sc[...] + rank)
        hi, mid, lo = _split3(posv)
        ones = jnp.ones((8, LANES), BF16)
        row = _dot_nt(ones, hi) + _dot_nt(ones, mid) + _dot_nt(ones, lo)
        pos_ref[...] = row[0:1].astype(jnp.int32)


def _moe_plan(hx, ntiles):
    t = hx.shape[0]
    nblk = t // PLAN_TILE
    tri = jnp.asarray(np.tril(np.ones((PLAN_TILE, PLAN_TILE), np.float32), -1), BF16)
    excl = jnp.asarray(np.triu(np.ones((LANES, LANES), np.float32), 1), BF16)
    assert ntiles <= LANES
    pos, tg, nu = pl.pallas_call(
        _moe_plan_kernel,
        grid=(2, nblk),
        in_specs=[pl.BlockSpec((PLAN_TILE, LANES), lambda ph, n: (n, D_MODEL // LANES)),
                  pl.BlockSpec((PLAN_TILE, PLAN_TILE), lambda ph, n: (0, 0)),
                  pl.BlockSpec((LANES, LANES), lambda ph, n: (0, 0))],
        out_specs=[pl.BlockSpec((1, PLAN_TILE), lambda ph, n: (0, n * ph)),
                   pl.BlockSpec((1, LANES), lambda ph, n: (0, 0)),
                   pl.BlockSpec((1, LANES), lambda ph, n: (0, 0))],
        out_shape=[jax.ShapeDtypeStruct((1, t), jnp.int32),
                   jax.ShapeDtypeStruct((1, LANES), jnp.int32),
                   jax.ShapeDtypeStruct((1, LANES), jnp.int32)],
        scratch_shapes=[pltpu.VMEM((1, LANES), F32), pltpu.VMEM((1, LANES), F32), pltpu.VMEM((1, LANES), F32)],
        compiler_params=_cparams(("arbitrary", "arbitrary")),
        name="moe_plan",
    )(hx, tri, excl)
    return pos.reshape(t), tg.reshape(LANES), nu.reshape(LANES)[0:1]


def _row_copy(src_ref, src_row, dst_ref, dst_row, sem):
    return pltpu.make_async_copy(src_ref.at[pl.ds(src_row, 1)], dst_ref.at[pl.ds(dst_row, 1)], sem)


def _moe_grouped_kernel(pos_ref, tg_ref, nu_ref, hx_ref, wg_ref, wu_ref, wd_ref, ys_ref,
                        buf, sem, src_sc, xb_sc, *, t):
    i = pl.program_id(0)
    ei = pl.program_id(1)
    n_used = nu_ref[0]
    used = i < n_used
    slot = i % 2

    def fetch(tile, sl):
        def body(r2, c):
            for pr in range(2):
                r = 2 * r2 + pr
                _row_copy(hx_ref, src_sc[tile * MOE_TILE + r], buf.at[sl], r, sem.at[sl]).start(priority=pr)
            return c
        lax.fori_loop(0, MOE_TILE // 2, body, 0, unroll=4)

    @pl.when((i == 0) & (ei == 0))
    def _():
        def clear(r, c):
            src_sc[r] = 0
            return c

        def invert(tk, c):
            src_sc[pos_ref[tk]] = tk
            return c

        lax.fori_loop(0, src_sc.shape[0], clear, 0, unroll=8)
        lax.fori_loop(0, t, invert, 0, unroll=8)
        fetch(0, 0)

    @pl.when(used & (ei == 0))
    def _():
        def wait(r, c):
            _row_copy(hx_ref, 0, buf.at[slot], r, sem.at[slot]).wait()
            return c
        lax.fori_loop(0, MOE_TILE, wait, 0, unroll=8)
        xb_sc[...] = buf[slot, :, 0:D_MODEL].astype(BF16)

    @pl.when(ei == 0)
    def _():
        ys_ref[...] = jnp.zeros(ys_ref.shape, F32)

    def expert_step():
        h = xb_sc[...]
        a = _dot(h, wg_ref[...].astype(BF16))
        u = _dot(h, wu_ref[...].astype(BF16))
        lane = lax.broadcasted_iota(jnp.int32, (MOE_TILE, LANES), 1)
        e = tg_ref[i] * EXP_PER_GROUP + ei
        cw = jnp.sum(jnp.where(lane == e, buf[slot, :, D_MODEL:HX_W], 0.0), axis=-1, keepdims=True)
        hid = (_silu(a) * u * cw).astype(BF16)
        ys_ref[...] += _dot(hid, wd_ref[...].astype(BF16))

    has_next = i + 1 < n_used

    @pl.when(used & has_next)
    def _():
        per_step = MOE_TILE // EXP_PER_GROUP
        nslot = (i + 1) % 2
        for r in range(per_step):
            row = ei * per_step + r
            _row_copy(hx_ref, src_sc[(i + 1) * MOE_TILE + row], buf.at[nslot], row,
                      sem.at[nslot]).start(priority=r % 2)
        expert_step()

    @pl.when(used & jnp.logical_not(has_next))
    def _():
        expert_step()


def _moe_grouped(hx, pos, tg, nu, wg, wu, wd, rows_out):
    t = hx.shape[0]
    ne, d, de = wg.shape
    ntiles = rows_out // MOE_TILE

    def wmap(i, ei, pos_ref, tg_ref, nu_ref):
        e = jnp.where(i < nu_ref[0], tg_ref[i] * EXP_PER_GROUP + ei, ne - 1)
        return (e, 0, 0)

    grid_spec = pltpu.PrefetchScalarGridSpec(
        num_scalar_prefetch=3,
        grid=(ntiles, EXP_PER_GROUP),
        in_specs=[pl.BlockSpec(memory_space=pl.ANY),
                  pl.BlockSpec((None, d, de), wmap),
                  pl.BlockSpec((None, d, de), wmap),
                  pl.BlockSpec((None, de, d), wmap)],
        out_specs=pl.BlockSpec((MOE_TILE, d), lambda i, ei, pos_ref, tg_ref, nu_ref: (i, 0)),
        scratch_shapes=[pltpu.VMEM((2, MOE_TILE, HX_W), F32), pltpu.SemaphoreType.DMA((2,)),
                        pltpu.SMEM((rows_out,), jnp.int32),
                        pltpu.VMEM((MOE_TILE, d), BF16)],
    )
    return pl.pallas_call(
        functools.partial(_moe_grouped_kernel, t=t),
        grid_spec=grid_spec,
        out_shape=jax.ShapeDtypeStruct((rows_out, d), F32),
        compiler_params=_cparams(("arbitrary", "arbitrary")),
        name="moe_grouped",
    )(pos, tg, nu, hx, wg, wu, wd)


COMBINE_ROWS = 512


def _moe_combine_kernel(pos_ref, ys_ref, x1_ref, gt_ref, gf_ref, y_ref, buf, sem):
    i = pl.program_id(0)
    n = pl.num_programs(0)

    def fetch(tile, slot):
        def body(r2, c):
            for pr in range(2):
                r = 2 * r2 + pr
                _row_copy(ys_ref, pos_ref[tile * COMBINE_ROWS + r], buf.at[slot], r, sem.at[slot]).start(priority=pr)
            return c
        lax.fori_loop(0, COMBINE_ROWS // 2, body, 0, unroll=4)

    @pl.when(i == 0)
    def _():
        fetch(0, 0)

    @pl.when(i + 1 < n)
    def _():
        fetch(i + 1, (i + 1) % 2)

    slot = i % 2

    def wait(r, c):
        _row_copy(ys_ref, pos_ref[i * COMBINE_ROWS + r], buf.at[slot], r, sem.at[slot]).wait()
        return c

    lax.fori_loop(0, COMBINE_ROWS, wait, 0, unroll=8)
    x2 = x1_ref[...] + gt_ref[...] * buf[slot]
    y_ref[...] = _rms(x2) * gf_ref[...]


def _moe_combine(ys, pos, x1, mod, g_final, tiles_per_batch):
    m, d = x1.shape
    rb = mod.shape[2]
    grid_spec = pltpu.PrefetchScalarGridSpec(
        num_scalar_prefetch=1,
        grid=(m // COMBINE_ROWS,),
        in_specs=[pl.BlockSpec(memory_space=pl.ANY),
                  pl.BlockSpec((COMBINE_ROWS, d), lambda i, pos_ref: (i, 0)),
                  pl.BlockSpec((None, None, rb, d), lambda i, pos_ref: (5, i // tiles_per_batch, 0, 0)),
                  pl.BlockSpec((1, d), lambda i, pos_ref: (0, 0))],
        out_specs=pl.BlockSpec((COMBINE_ROWS, d), lambda i, pos_ref: (i, 0)),
        scratch_shapes=[pltpu.VMEM((2, COMBINE_ROWS, d), F32), pltpu.SemaphoreType.DMA((2,))],
    )
    return pl.pallas_call(
        _moe_combine_kernel,
        grid_spec=grid_spec,
        out_shape=jax.ShapeDtypeStruct((m, d), F32),
        compiler_params=_cparams(("arbitrary",)),
        name="moe_combine",
    )(pos, ys, x1, mod, g_final)


def _moe_kernel(h_ref, cmb_ref, wg_ref, wu_ref, wd_ref, x1_ref, gt_ref, gf_ref,
                y_ref, wg16_ref, wu16_ref, wd16_ref, acc_ref):
    e = pl.program_id(1)

    @pl.when(e == 0)
    def _():
        acc_ref[...] = jnp.zeros(acc_ref.shape, F32)

    wg = wg_ref[...].astype(BF16)
    wu = wu_ref[...].astype(BF16)
    wd = wd_ref[...].astype(BF16)
    wg16_ref[...] = wg
    wu16_ref[...] = wu
    wd16_ref[...] = wd
    h = h_ref[...]
    a = _dot(h, wg)
    u = _dot(h, wu)
    lane = lax.broadcasted_iota(jnp.int32, cmb_ref.shape, 1)
    cw = jnp.sum(jnp.where(lane == e, cmb_ref[...], 0.0), axis=-1, keepdims=True)
    hid = (_silu(a) * u * cw).astype(BF16)
    acc_ref[...] += _dot(hid, wd)

    @pl.when(e == pl.num_programs(1) - 1)
    def _():
        x2 = x1_ref[...] + gt_ref[...] * acc_ref[...]
        y_ref[...] = _rms(x2) * gf_ref[...]


def _moe(h2, cmb, wg, wu, wd, x1, mod, g_final, tm, tiles_per_batch):
    m, d = x1.shape
    rb = mod.shape[2]
    ne, _, de = wg.shape
    assert m == tm
    return pl.pallas_call(
        _moe_kernel,
        grid=(m // tm, ne),
        in_specs=[pl.BlockSpec((tm, d), lambda i, e: (i, 0)),
                  pl.BlockSpec((tm, LANES), lambda i, e: (i, 0)),
                  pl.BlockSpec((None, d, de), lambda i, e: (e, 0, 0)),
                  pl.BlockSpec((None, d, de), lambda i, e: (e, 0, 0)),
                  pl.BlockSpec((None, de, d), lambda i, e: (e, 0, 0)),
                  pl.BlockSpec((tm, d), lambda i, e: (i, 0)),
                  pl.BlockSpec((None, None, rb, d), lambda i, e: (5, i // tiles_per_batch, 0, 0)),
                  pl.BlockSpec((1, d), lambda i, e: (0, 0))],
        out_specs=[pl.BlockSpec((tm, d), lambda i, e: (i, 0)),
                   pl.BlockSpec((None, d, de), lambda i, e: (e, 0, 0)),
                   pl.BlockSpec((None, d, de), lambda i, e: (e, 0, 0)),
                   pl.BlockSpec((None, de, d), lambda i, e: (e, 0, 0))],
        out_shape=[jax.ShapeDtypeStruct((m, d), F32),
                   jax.ShapeDtypeStruct((ne, d, de), BF16),
                   jax.ShapeDtypeStruct((ne, d, de), BF16),
                   jax.ShapeDtypeStruct((ne, de, d), BF16)],
        scratch_shapes=[pltpu.VMEM((tm, d), F32)],
        compiler_params=_cparams(("arbitrary", "arbitrary")),
        name="moe",
    )(h2, cmb, wg, wu, wd, x1, mod, g_final)


def _prep_weights(w_ada, b_ada, g_norm1, g_norm2, g_final, w_in, w_a2, b_a, b_f, g_gla_norm, w_up_a, w_up_b, w_out,
                  w_grp, b_grp, w_exp, b_exp, w_gate_e, w_up_e, w_down_e):
    wt = w_in.reshape(w_in.shape[1:]).T
    d = D_MODEL
    o_lra = 2 * GLA_KW + 2 * GLA_VW
    o_fox = o_lra + GLA_RANK
    o_fb = o_fox + 3 * FOX_W
    o_g = o_fb + FOX_HEADS
    pad = jnp.zeros((LANES - GLA_RANK - FOX_HEADS, d), F32)
    wall, mask = _gla_tables()
    wall_s, mask_s = _gla_tables(SAMPLE_LEVELS)
    bf_row =jnp.zeros((1, LANES), F32).at[0, FB_LANE:FB_LANE + FOX_HEADS].set(b_f[0])
    w_router = jnp.concatenate([w_exp[0], w_grp[0], jnp.zeros((d, LANES - N_EXPERTS - N_GROUPS), F32)], axis=1)
    b_router = jnp.concatenate([b_exp[0], b_grp[0], jnp.zeros((LANES - N_EXPERTS - N_GROUPS,), F32)])[None, :]
    return dict(
        w_ada=w_ada.reshape(w_ada.shape[1:]), b_ada=b_ada[0][None, :],
        g1=g_norm1[0][None, :], g2=g_norm2[0][None, :], gf=g_final[None, :],
        w_in_t=wt,
        w_small_t=jnp.concatenate([wt[o_lra:o_lra + GLA_RANK], wt[o_fb:o_fb + FOX_HEADS], pad], axis=0),
        wall=jnp.asarray(wall, BF16), mask=jnp.asarray(mask, F32),
        wall_s=jnp.asarray(wall_s, BF16), mask_s=jnp.asarray(mask_s, F32),
        w_a2=jnp.concatenate([w_a2[0], jnp.zeros((LANES - GLA_RANK, GLA_KW), F32)], axis=0), b_a=b_a[0][None, :], bf_row=bf_row, g_gla=g_gla_norm[0][None, :],
        w_up_a=w_up_a[0].astype(BF16), w_up_b=w_up_b[0].astype(BF16), w_out=w_out[0].astype(BF16),
        w_router=w_router, b_router=b_router,
        wg=w_gate_e.reshape(w_gate_e.shape[1:]), wu=w_up_e.reshape(w_up_e.shape[1:]),
        wd=w_down_e.reshape(w_down_e.shape[1:]),
    )


def _project(h, p, tm):
    tn = PROJ_COL_TILE
    wt = p["w_in_t"]
    n_gla = 2 * GLA_KW + 2 * GLA_VW
    c_q = n_gla + GLA_RANK
    c_g = c_q + 3 * FOX_W + FOX_HEADS
    (pg,) = _mm_wt(h, wt, 0, n_gla, tm, tn)
    (qb,) = _mm_wt(h, wt, c_q, FOX_W, tm, tn, (BF16,))
    kb, kb16 = _mm_heads(h, wt, c_q + FOX_W, min(tm, HEADS_ROW_TILE))
    vb, vb16 = _mm_heads(h, wt, c_q + 2 * FOX_W, min(tm, HEADS_ROW_TILE))
    (gates,) = _mm_wt(h, wt, c_g, 2 * D_MODEL, tm, tn, (BF16,))
    (small,) = _mm_wt(h, p["w_small_t"], 0, LANES, tm, LANES)
    return pg, qb, kb, kb16, vb, vb16, gates, small


def _tail(x, oa, ob, gates, mod, p, tm, tiles_per_batch, expert_w16=None):
    merged = _merge(oa, ob, p["w_up_a"], p["w_up_b"], gates, tm)
    tmo = min(tm, OUTPROJ_ROW_TILE)
    x1, hx = _outproj(merged, p["w_out"], x, mod, p["g2"], p["w_router"], p["b_router"],
                      tmo, tiles_per_batch * (tm // tmo))
    if expert_w16 is None:
        assert tm <= DENSE_MOE_ROW_TILE
        y, wg16, wu16, wd16 = _moe(hx[:, 0:D_MODEL].astype(BF16), hx[:, D_MODEL:HX_W], p["wg"], p["wu"], p["wd"],
                                   x1, mod, p["gf"], tm, tiles_per_batch)
        return y, (wg16, wu16, wd16)
    rows_out = x.shape[0] + N_GROUPS * MOE_TILE
    pos, tg, nu = _moe_plan(hx, rows_out // MOE_TILE)
    ys = _moe_grouped(hx, pos, tg, nu, *expert_w16, rows_out)
    return _moe_combine(ys, pos, x1, mod, p["gf"], tiles_per_batch * (tm // COMBINE_ROWS))


def kernel(x_prompt, x_sample, cache_k, cache_v, cache_logf, state_gla, page_table, c_prompt, c_sample, w_ada, b_ada,
           g_norm1, g_norm2, g_final, w_in, w_a2, b_a, b_f, g_gla_norm, w_up_a, w_up_b, w_out, w_grp, b_grp, w_exp,
           b_exp, w_gate_e, w_up_e, w_down_e):
    p = _prep_weights(w_ada, b_ada, g_norm1, g_norm2, g_final, w_in, w_a2, b_a, b_f, g_gla_norm, w_up_a, w_up_b,
                      w_out, w_grp, b_grp, w_exp, b_exp, w_gate_e, w_up_e, w_down_e)
    bp, seq, d = x_prompt.shape
    bs, t, _ = x_sample.shape
    assert t <= 2 ** SAMPLE_LEVELS and t <= ROWS8

    mod = _adaln(jnp.concatenate([c_prompt, c_sample], axis=0), p["w_ada"], p["b_ada"])
    mod_p = mod[:bp].reshape(bp, 6, 1, d).transpose(1, 0, 2, 3)
    mod_s = jnp.repeat(mod[bp:].reshape(bs, 6, d), t, axis=0).transpose(1, 0, 2)[:, None]

    tm = ROW_TILE
    tpb = seq // tm
    xp = x_prompt.reshape(bp * seq, d)
    hp = _normmod(xp, p["g1"], mod_p, tm, tpb)
    pg, qb, kb, kb16, vb, vb16, gates, small = _project(hp, p, tm)
    s0 = jnp.zeros((bp, GLA_HEADS, GLA_DK, GLA_DV), F32)
    oa, s_p = _gla_prompt(pg, small, s0, p["wall"], p["mask"], p["w_a2"], p["b_a"], p["g_gla"], bp, seq)
    lf_p, fq, fk = _fox_bias_prompt(small, p["bf_row"], bp, seq)
    ob = _fox_prompt(qb, kb16, vb16, fq, fk, bp, seq)

    rows = bs * t
    xs = x_sample.reshape(rows, d)
    hs = _normmod(xs, p["g1"], mod_s, rows, 1)
    pg_s, qs, ks, _, vs, _, gates_s, small_s = _project(hs, p, rows)
    oa_s, s_s = _gla_sample(pg_s.reshape(bs, t, -1), small_s.reshape(bs, t, LANES),
                            state_gla.reshape(state_gla.shape[1:]), p["wall_s"], p["mask_s"],
                            p["w_a2"], p["b_a"], p["g_gla"], bs, t)
    lf_s, fn_s = _fox_bias_sample(small_s, p["bf_row"], t)
    n_pool = cache_k.shape[1]
    ob_s = _fox_sample(page_table, qs.astype(F32).reshape(bs, t, FOX_W), ks.reshape(bs, t, FOX_HEADS, FOX_DH),
                       vs.reshape(bs, t, FOX_HEADS, FOX_DH), fn_s.reshape(bs, t, FOX_HEADS),
                       cache_k, cache_v, cache_logf.reshape(n_pool, PAGE, FOX_HEADS).transpose(0, 2, 1), t)
    y_s, expert_w16 = _tail(xs, oa_s.reshape(rows, GLA_VW).astype(BF16), ob_s.reshape(rows, FOX_W).astype(BF16),
                            gates_s, mod_s, p, rows, 1)
    y_p = _tail(xp, oa, ob, gates, mod_p, p, tm, tpb, expert_w16)

    return (y_p.reshape(bp, seq, d), y_s.reshape(bs, t, d),
            kb.reshape(1, bp, seq, FOX_HEADS, FOX_DH), vb.reshape(1, bp, seq, FOX_HEADS, FOX_DH),
            lf_p.reshape(1, bp, seq, FOX_HEADS), s_p[None],
            ks.reshape(1, bs, t, FOX_HEADS, FOX_DH), vs.reshape(1, bs, t, FOX_HEADS, FOX_DH),
            lf_s.reshape(1, bs, t, FOX_HEADS), s_s[None])
```
